```python
import math
import jax, jax.numpy as jnp
from jax import lax
import numpy as np

D_MODEL = 2048
BATCH = 4
SEQ = 2048
DEPTH = 2
DEC_BATCH = 128
DEC_SEQ = 1
PAST_LEN = 16384
PAGE_SIZE = 128

F32 = jnp.float32
CHUNK = 128
N_BRANCH = 4
BRANCH_W = D_MODEL // 2
RET_H = 4
RET_DK = D_MODEL // 16
RET_DV = D_MODEL // 8
ROPE_BASE = 10000.0
ML_H = 4
ML_DK = D_MODEL // 16
ML_DV = D_MODEL // 8
S5_WIDTH = D_MODEL // 2
S5_GROUP = 16
S5_G = S5_WIDTH // S5_GROUP
S5_N = 64
HG_H = 4
HG_DK = D_MODEL // 16
HG_DV = D_MODEL // 8
D_FF = 11 * D_MODEL // 4
MEM_LEN = 256
XA_H = 4
XA_DH = D_MODEL // XA_H
N_NORMS = 8
SEG_SIZES = (RET_H * RET_DK, RET_H * RET_DK, RET_H * RET_DV, BRANCH_W,
             ML_H * ML_DK, ML_H * ML_DK, ML_H * ML_DV, ML_H, ML_H, BRANCH_W,
             S5_WIDTH,
             HG_H * HG_DK, HG_H * HG_DK, HG_H * HG_DV, BRANCH_W,
             N_BRANCH * D_MODEL)
N_IN = sum(SEG_SIZES)

kernel_name = "hybrid_ret_mlstm_s5_hgrn2_decoder_step"


def rms_norm(x, w, eps=1e-6):
    xf = x.astype(F32)
    y = xf * lax.rsqrt(jnp.mean(xf * xf, axis=-1, keepdims=True) + eps) * w.astype(F32)
    return y.astype(x.dtype)


def head_rms(x, w, eps=1e-6):
    y = x * lax.rsqrt(jnp.mean(x * x, axis=-1, keepdims=True) + eps)
    return y.reshape(*x.shape[:-2], -1) * w.astype(F32)


def group_norm(x, eps=1e-5):
    mu = jnp.mean(x, axis=-1, keepdims=True)
    xc = x - mu
    y = xc * lax.rsqrt(jnp.mean(xc * xc, axis=-1, keepdims=True) + eps)
    return y.reshape(*x.shape[:-2], -1)


def swiglu(x, w_in, w_out):
    g, u = jnp.split(x @ w_in, 2, axis=-1)
    return (jax.nn.silu(g) * u) @ w_out


def rope(x, pos):
    half = x.shape[-1] // 2
    inv = ROPE_BASE ** (-jnp.arange(half, dtype=F32) / half)
    ang = pos.astype(F32)[:, None] * inv[None, :]
    cos = jnp.cos(ang)[None, :, None, :]
    sin = jnp.sin(ang)[None, :, None, :]
    x1, x2 = x[..., :half], x[..., half:]
    return jnp.concatenate([x1 * cos - x2 * sin, x1 * sin + x2 * cos], axis=-1)


def _to_chunks(a, n, c):
    return a.reshape(a.shape[0], n, c, *a.shape[2:]).swapaxes(0, 1)


def _from_chunks(a):
    a = a.swapaxes(0, 1)
    return a.reshape(a.shape[0], a.shape[1] * a.shape[2], *a.shape[3:])


def chunk_scan(step, state, xs):
    L = xs[0].shape[1]
    c = CHUNK if L % CHUNK == 0 else L
    n = L // c
    xs_c = tuple(_to_chunks(a, n, c) for a in xs)
    state, ys = lax.scan(step, state, xs_c)
    return _from_chunks(ys), state


def retention(q, k, v, s0):
    lg = jnp.log1p(-jnp.exp2(-5.0 - jnp.arange(RET_H, dtype=F32)))

    def step(s, inp):
        qc, kc, vc = inp
        c = qc.shape[1]
        idx = jnp.arange(c, dtype=F32)
        diff = idx[:, None] - idx[None, :]
        dmat = jnp.where(diff[None] >= 0, jnp.exp(jnp.maximum(diff, 0.0)[None] * lg[:, None, None]), 0.0)
        sc = jnp.einsum('bihd,bjhd->bhij', qc, kc) * dmat[None]
        o = jnp.einsum('bhij,bjhe->bihe', sc, vc)
        o = o + jnp.einsum('bihd,bhde->bihe', qc, s) * jnp.exp((idx[:, None] + 1.0) * lg[None, :])[None, :, :, None]
        kd = kc * jnp.exp((c - 1.0 - idx)[:, None] * lg[None, :])[None, :, :, None]
        s = jnp.exp(c * lg)[None, :, None, None] * s + jnp.einsum('bjhd,bjhe->bhde', kd, vc)
        return s, o

    return chunk_scan(step, s0, (q, k, v))


def mlstm(q, k, v, ig, lf, c0, n0, m0):
    def step(carry, inp):
        cm, nv, m = carry
        qc, kc, vc, ic, fc = inp
        c = qc.shape[1]
        b = jnp.cumsum(fc, axis=1)
        causal = jnp.tril(jnp.ones((c, c), dtype=bool))
        logd = jnp.where(causal[None, :, :, None],
                         b[:, :, None, :] - b[:, None, :, :] + ic[:, None, :, :], -jnp.inf)
        inter = m[:, None, :] + b
        mi = jnp.maximum(inter, jnp.max(logd, axis=2))
        w = jnp.exp(logd - mi[:, :, None, :])
        wi = jnp.exp(inter - mi)
        a = jnp.einsum('bihd,bjhd->bijh', qc, kc) * w
        num = jnp.einsum('bijh,bjhe->bihe', a, vc) + wi[..., None] * jnp.einsum('bihd,bhde->bihe', qc, cm)
        nq = jnp.sum(a, axis=2) + wi * jnp.einsum('bihd,bhd->bih', qc, nv)
        h = num / jnp.maximum(jnp.abs(nq), jnp.exp(-mi))[..., None]
        m_new = mi[:, -1]
        wl = jnp.exp(b[:, -1:] - b + ic - m_new[:, None])
        dp = jnp.exp(m + b[:, -1] - m_new)
        cm = dp[..., None, None] * cm + jnp.einsum('bjh,bjhd,bjhe->bhde', wl, kc, vc)
        nv = dp[..., None] * nv + jnp.einsum('bjh,bjhd->bhd', wl, kc)
        return (cm, nv, m_new), h

    return chunk_scan(step, (c0, n0, m0), (q, k, v, ig, lf))


def hgrn2(q, k, lf, iv, s0):
    def step(s, inp):
        qc, kc, fc, ic = inp
        c = qc.shape[1]
        bc = jnp.cumsum(fc, axis=1)
        causal = jnp.tril(jnp.ones((c, c), dtype=bool))
        dec = jnp.exp(jnp.where(causal[None, :, :, None, None], bc[:, :, None] - bc[:, None, :], -jnp.inf))
        a = jnp.einsum('bihd,bjhd,bijhd->bijh', qc, kc, dec)
        o = jnp.einsum('bijh,bjhe->bihe', a, ic) + jnp.einsum('bihd,bhde->bihe', qc * jnp.exp(bc), s)
        s = jnp.exp(bc[:, -1])[..., None] * s + jnp.einsum('bjhd,bjhe->bhde', kc * jnp.exp(bc[:, -1:] - bc), ic)
        return s, o

    return chunk_scan(step, s0, (q, k, lf, iv))


def _cmul_combine(e1, e2):
    a1r, a1i, b1r, b1i = e1
    a2r, a2i, b2r, b2i = e2
    return (a2r * a1r - a2i * a1i, a2r * a1i + a2i * a1r,
            a2r * b1r - a2i * b1i + b2r, a2r * b1i + a2i * b1r + b2i)


def s5(u, h0r, h0i, lam_re, lam_im, log_step, b_re, b_im, c_re, c_im, d, w_glu):
    bsz, L, _ = u.shape
    ug = u.reshape(bsz, L, S5_G, S5_GROUP)
    lr, li = lam_re.astype(F32), lam_im.astype(F32)
    dt = jnp.exp(log_step.astype(F32))[:, None]
    mag = jnp.exp(lr * dt)
    ar, ai = mag * jnp.cos(li * dt), mag * jnp.sin(li * dt)
    den = lr * lr + li * li
    cr = ((ar - 1.0) * lr + ai * li) / den
    ci = (ai * lr - (ar - 1.0) * li) / den
    bbr = cr[..., None] * b_re - ci[..., None] * b_im
    bbi = cr[..., None] * b_im + ci[..., None] * b_re
    bur = jnp.einsum('blgp,gnp->lbgn', ug, bbr)
    bui = jnp.einsum('blgp,gnp->lbgn', ug, bbi)
    a_r = jnp.broadcast_to(ar[None, None], (L, 1, S5_G, S5_N))
    a_i = jnp.broadcast_to(ai[None, None], (L, 1, S5_G, S5_N))
    pr, pim, xr, xi = lax.associative_scan(_cmul_combine, (a_r, a_i, bur, bui), axis=0)
    xr = xr + pr * h0r[None] - pim * h0i[None]
    xi = xi + pr * h0i[None] + pim * h0r[None]
    y = jnp.einsum('lbgn,gpn->blgp', xr, c_re) - jnp.einsum('lbgn,gpn->blgp', xi, c_im)
    y = y.reshape(bsz, L, S5_WIDTH) + d.astype(F32) * u
    z = jax.nn.gelu(y)
    return z * jax.nn.sigmoid(z @ w_glu), (xr[-1], xi[-1])


def _split_in(u):
    offs = [int(o) for o in np.cumsum(SEG_SIZES)[:-1]]
    return jnp.split(u, offs, axis=-1)


def token_mixing(xn, pos, st, lp, lb):
    bsz, L, _ = xn.shape
    st = {name: val.astype(F32) for name, val in st.items()}
    (rq, rk, rv, rg, mq, mk, mv, mi, mf, mo, su, hq, hf, hi, hg, gates) = _split_in((xn @ lp['w_in']).astype(F32))
    rq = rope(rq.reshape(bsz, L, RET_H, RET_DK), pos)
    rk = rope(rk.reshape(bsz, L, RET_H, RET_DK), pos) * (RET_DK ** -0.5)
    ro, s_ret = retention(rq, rk, rv.reshape(bsz, L, RET_H, RET_DV), st['ret'])
    h_ret = group_norm(ro) * jax.nn.silu(rg)
    gb = lp['ml_gate_bias'].astype(F32)
    ig = mi + gb[0]
    lf = jax.nn.log_sigmoid(mf + gb[1])
    mh, (s_c, s_n, s_m) = mlstm(mq.reshape(bsz, L, ML_H, ML_DK), mk.reshape(bsz, L, ML_H, ML_DK) * (ML_DK ** -0.5),
                                mv.reshape(bsz, L, ML_H, ML_DV), ig, lf, st['ml_c'], st['ml_n'], st['ml_m'])
    h_ml = head_rms(mh, lp['ml_norm_w']) * jax.nn.sigmoid(mo)
    h_s5, (s5r, s5i) = s5(su, st['s5_re'], st['s5_im'], lp['s5_lambda_re'], lp['s5_lambda_im'], lp['s5_log_step'],
                          lp['s5_b_re'], lp['s5_b_im'], lp['s5_c_re'], lp['s5_c_im'], lp['s5_d'], lp['s5_w_glu'])
    lbh = lb.reshape(HG_H, HG_DK)
    fpre = hf.reshape(bsz, L, HG_H, HG_DK)
    hlf = jnp.logaddexp(jnp.log(lbh), jnp.log1p(-lbh) + jax.nn.log_sigmoid(fpre))
    hk = (1.0 - lbh) * jax.nn.sigmoid(-fpre)
    ho, s_hg = hgrn2(jax.nn.silu(hq).reshape(bsz, L, HG_H, HG_DK), hk, hlf,
                     hi.reshape(bsz, L, HG_H, HG_DV), st['hg'])
    h_hg = head_rms(ho, lp['hg_norm_w']) * jax.nn.silu(hg)
    hb = jnp.stack([h_ret, h_ml, h_s5, h_hg], axis=2)
    proj = jnp.einsum('blnw,nwd->blnd', hb, lp['w_branch'])
    merged = jnp.sum(jax.nn.sigmoid(gates.reshape(bsz, L, N_BRANCH, D_MODEL)) * proj, axis=2)
    out = merged @ lp['w_out']
    new_st = dict(ret=s_ret, ml_c=s_c, ml_n=s_n, ml_m=s_m, s5_re=s5r, s5_im=s5i, hg=s_hg)
    return out, new_st


def memory_kv(mem, w_norm, w_kv):
    bsz = mem.shape[0]
    k, v = jnp.split(rms_norm(mem, w_norm) @ w_kv, 2, axis=-1)
    return k.reshape(bsz, MEM_LEN, XA_H, XA_DH), v.reshape(bsz, MEM_LEN, XA_H, XA_DH)


def cross_attn(xn, k, v, wq, wo):
    bsz, L, _ = xn.shape
    q = (xn @ wq).reshape(bsz, L, XA_H, XA_DH).astype(F32)
    s = jnp.einsum('blhd,bmhd->bhlm', q, k.astype(F32)) * (XA_DH ** -0.5)
    p = jax.nn.softmax(s, axis=-1)
    o = jnp.einsum('bhlm,bmhd->blhd', p, v.astype(F32)).reshape(bsz, L, D_MODEL)
    return o @ wo


def decoder_layer(x, pos, st, mem_k, mem_v, lp, lb):
    dt = x.dtype
    nw = lp['norm_w']
    h = swiglu(rms_norm(x, nw[0]), lp['ffn_w_in'][0], lp['ffn_w_out'][0])
    x = x + (0.5 * rms_norm(h, nw[1])).astype(dt)
    h, new_st = token_mixing(rms_norm(x, nw[2]), pos, st, lp, lb)
    x = x + rms_norm(h, nw[3]).astype(dt)
    h = cross_attn(rms_norm(x, nw[4]), mem_k, mem_v, lp['xa_wq'], lp['xa_wo'])
    x = x + rms_norm(h, nw[5]).astype(dt)
    h = swiglu(rms_norm(x, nw[6]), lp['ffn_w_in'][1], lp['ffn_w_out'][1])
    x = x + (0.5 * rms_norm(h, nw[7])).astype(dt)
    return x, new_st


def hgrn_lower_bounds(logits):
    p = jax.nn.softmax(logits.astype(F32), axis=0)
    c = jnp.cumsum(p, axis=0)
    return c - c[:1]


def zero_states(bsz):
    return dict(ret=jnp.zeros((bsz, RET_H, RET_DK, RET_DV), F32),
                ml_c=jnp.zeros((bsz, ML_H, ML_DK, ML_DV), F32),
                ml_n=jnp.zeros((bsz, ML_H, ML_DK), F32),
                ml_m=jnp.zeros((bsz, ML_H), F32),
                s5_re=jnp.zeros((bsz, S5_G, S5_N), F32),
                s5_im=jnp.zeros((bsz, S5_G, S5_N), F32),
                hg=jnp.zeros((bsz, HG_H, HG_DK, HG_DV), F32))


def setup_inputs(seed: int = 0) -> dict:
    key = jax.random.key(seed)
    ks = jax.random.split(key, 40)

    def nrm(k, shape, scale):
        return jax.random.normal(k, shape, F32) * scale

    gate_bias = jnp.stack([nrm(ks[20], (DEPTH, ML_H), 0.1),
                           jnp.linspace(3.0, 6.0, ML_H, dtype=F32)[None, :] + nrm(ks[21], (DEPTH, ML_H), 0.1)], axis=1)
    return {
        'x_prompt': nrm(ks[0], (BATCH, SEQ, D_MODEL), 1.0),
        'x_sample': nrm(ks[1], (DEC_BATCH, DEC_SEQ, D_MODEL), 1.0),
        'mem_prompt': nrm(ks[2], (BATCH, MEM_LEN, D_MODEL), 1.0),
        'state_ret': nrm(ks[3], (DEPTH, DEC_BATCH, RET_H, RET_DK, RET_DV), 0.5),
        'state_mlstm_c': nrm(ks[4], (DEPTH, DEC_BATCH, ML_H, ML_DK, ML_DV), 0.5),
        'state_mlstm_n': nrm(ks[5], (DEPTH, DEC_BATCH, ML_H, ML_DK), 0.5),
        'state_mlstm_m': nrm(ks[6], (DEPTH, DEC_BATCH, ML_H), 1.0),
        'state_s5_re': nrm(ks[7], (DEPTH, DEC_BATCH, S5_G, S5_N), 0.5),
        'state_s5_im': nrm(ks[8], (DEPTH, DEC_BATCH, S5_G, S5_N), 0.5),
        'state_hgrn': nrm(ks[9], (DEPTH, DEC_BATCH, HG_H, HG_DK, HG_DV), 0.5),
        'cache_mem_k': nrm(ks[10], (DEPTH, DEC_BATCH, MEM_LEN, XA_H, XA_DH), 1.0),
        'cache_mem_v': nrm(ks[11], (DEPTH, DEC_BATCH, MEM_LEN, XA_H, XA_DH), 1.0),
        'norm_w': 1.0 + nrm(ks[12], (DEPTH, N_NORMS, D_MODEL), 0.05),
        'ffn_w_in': nrm(ks[13], (DEPTH, 2, D_MODEL, 2 * D_FF), D_MODEL ** -0.5),
        'ffn_w_out': nrm(ks[14], (DEPTH, 2, D_FF, D_MODEL), D_FF ** -0.5),
        'w_in': nrm(ks[15], (DEPTH, D_MODEL, N_IN), D_MODEL ** -0.5),
        'ml_gate_bias': gate_bias,
        'ml_norm_w': 1.0 + nrm(ks[16], (DEPTH, BRANCH_W), 0.05),
        's5_lambda_re': -0.5 + nrm(ks[17], (DEPTH, S5_G, S5_N), 0.01),
        's5_lambda_im': math.pi * jnp.arange(S5_N, dtype=F32)[None, None, :] + nrm(ks[18], (DEPTH, S5_G, S5_N), 0.01),
        's5_log_step': jax.random.uniform(ks[19], (DEPTH, S5_G), F32, math.log(1e-3), math.log(1e-1)),
        's5_b_re': nrm(ks[22], (DEPTH, S5_G, S5_N, S5_GROUP), (2 * S5_GROUP) ** -0.5),
        's5_b_im': nrm(ks[23], (DEPTH, S5_G, S5_N, S5_GROUP), (2 * S5_GROUP) ** -0.5),
        's5_c_re': nrm(ks[24], (DEPTH, S5_G, S5_GROUP, S5_N), S5_N ** -0.5),
        's5_c_im': nrm(ks[25], (DEPTH, S5_G, S5_GROUP, S5_N), S5_N ** -0.5),
        's5_d': nrm(ks[26], (DEPTH, S5_WIDTH), 0.5),
        's5_w_glu': nrm(ks[27], (DEPTH, S5_WIDTH, S5_WIDTH), S5_WIDTH ** -0.5),
        'hg_lb_logits': nrm(ks[28], (DEPTH, HG_H * HG_DK), 0.5),
        'hg_norm_w': 1.0 + nrm(ks[29], (DEPTH, BRANCH_W), 0.05),
        'w_branch': nrm(ks[30], (DEPTH, N_BRANCH, BRANCH_W, D_MODEL), BRANCH_W ** -0.5),
        'w_out': nrm(ks[31], (DEPTH, D_MODEL, D_MODEL), D_MODEL ** -0.5),
        'xa_mem_norm': 1.0 + nrm(ks[32], (DEPTH, D_MODEL), 0.05),
        'xa_wq': nrm(ks[33], (DEPTH, D_MODEL, D_MODEL), D_MODEL ** -0.5),
        'xa_wkv': nrm(ks[34], (DEPTH, D_MODEL, 2 * D_MODEL), D_MODEL ** -0.5),
        'xa_wo': nrm(ks[35], (DEPTH, D_MODEL, D_MODEL), D_MODEL ** -0.5),
    }


def reference(x_prompt, x_sample, mem_prompt, state_ret, state_mlstm_c, state_mlstm_n, state_mlstm_m,
              state_s5_re, state_s5_im, state_hgrn, cache_mem_k, cache_mem_v,
              norm_w, ffn_w_in, ffn_w_out, w_in, ml_gate_bias, ml_norm_w,
              s5_lambda_re, s5_lambda_im, s5_log_step, s5_b_re, s5_b_im, s5_c_re, s5_c_im, s5_d, s5_w_glu,
              hg_lb_logits, hg_norm_w, w_branch, w_out, xa_mem_norm, xa_wq, xa_wkv, xa_wo):
    lbs = hgrn_lower_bounds(hg_lb_logits)
    pos_p = jnp.arange(x_prompt.shape[1], dtype=jnp.int32)
    pos_s = PAST_LEN + jnp.arange(x_sample.shape[1], dtype=jnp.int32)
    names = ('ret', 'ml_c', 'ml_n', 'ml_m', 's5_re', 's5_im', 'hg')
    p_new = {n: [] for n in names}
    s_new = {n: [] for n in names}
    p_mk, p_mv = [], []
    yp, ys = x_prompt, x_sample
    for l in range(DEPTH):
        lp = dict(norm_w=norm_w[l], ffn_w_in=ffn_w_in[l], ffn_w_out=ffn_w_out[l], w_in=w_in[l],
                  ml_gate_bias=ml_gate_bias[l], ml_norm_w=ml_norm_w[l],
                  s5_lambda_re=s5_lambda_re[l], s5_lambda_im=s5_lambda_im[l], s5_log_step=s5_log_step[l],
                  s5_b_re=s5_b_re[l], s5_b_im=s5_b_im[l], s5_c_re=s5_c_re[l], s5_c_im=s5_c_im[l],
                  s5_d=s5_d[l], s5_w_glu=s5_w_glu[l], hg_norm_w=hg_norm_w[l],
                  w_branch=w_branch[l], w_out=w_out[l], xa_wq=xa_wq[l], xa_wo=xa_wo[l])
        mk, mv = memory_kv(mem_prompt, xa_mem_norm[l], xa_wkv[l])
        yp, stp = decoder_layer(yp, pos_p, zero_states(yp.shape[0]), mk, mv, lp, lbs[l])
        st_in = dict(ret=state_ret[l], ml_c=state_mlstm_c[l], ml_n=state_mlstm_n[l], ml_m=state_mlstm_m[l],
                     s5_re=state_s5_re[l], s5_im=state_s5_im[l], hg=state_hgrn[l])
        ys, sts = decoder_layer(ys, pos_s, st_in, cache_mem_k[l], cache_mem_v[l], lp, lbs[l])
        for n in names:
            p_new[n].append(stp[n])
            s_new[n].append(sts[n])
        p_mk.append(mk)
        p_mv.append(mv)
    return (yp, ys,
            jnp.stack(p_new['ret']), jnp.stack(p_new['ml_c']), jnp.stack(p_new['ml_n']), jnp.stack(p_new['ml_m']),
            jnp.stack(p_new['s5_re']), jnp.stack(p_new['s5_im']), jnp.stack(p_new['hg']),
            jnp.stack(p_mk), jnp.stack(p_mv),
            jnp.stack(s_new['ret']), jnp.stack(s_new['ml_c']), jnp.stack(s_new['ml_n']), jnp.stack(s_new['ml_m']),
            jnp.stack(s_new['s5_re']), jnp.stack(s_new['s5_im']), jnp.stack(s_new['hg']))
```

```python
import functools
import math

import jax
import jax.numpy as jnp
from jax import lax
from jax.experimental import pallas as pl
from jax.experimental.pallas import tpu as pltpu

F32 = jnp.float32
BF16 = jnp.bfloat16

CHUNK = 128
N_BRANCH = 4
N_HEADS = 4
DK = 128
DV = 256
S5_GROUP = 16
S5_N = 64
ROPE_BASE = 10000.0
PAST_LEN = 16384

V7X_VMEM_BYTES = 64 * 1024 * 1024
MXU_K = 256
MIB = 1024 * 1024

OFF_RQ, OFF_RK, OFF_RV, OFF_RG = 0, 512, 1024, 2048
OFF_MQ, OFF_MK, OFF_MV, OFF_MO = 3072, 3584, 4096, 5120
OFF_SU = 6144
OFF_HQ, OFF_HF, OFF_HI, OFF_HG = 7168, 7680, 8192, 9216
OFF_GATES = 10240
N_MAIN = 18432
GATE_PAD = 128


def _cparams(sem, vmem_mib):
    return pltpu.CompilerParams(dimension_semantics=sem, vmem_limit_bytes=vmem_mib * MIB)


def _tile(n, cap, mult=16):
    best = None
    for t in range(mult, min(n, cap) + 1, mult):
        if n % t == 0:
            best = t
    return best if best is not None else n


def _dot(a, b):
    return jnp.dot(a, b, preferred_element_type=F32)


def _dot_nt(a, b):
    return lax.dot_general(a, b, (((1,), (1,)), ((), ())), preferred_element_type=F32)


def _rms(x, w, eps=1e-6):
    return x * lax.rsqrt(jnp.mean(x * x, axis=-1, keepdims=True) + eps) * w


def _sigmoid(x):
    return 1.0 / (1.0 + jnp.exp(-x))


def _silu(x):
    return x * _sigmoid(x)


def _log_sigmoid(x):
    return jnp.minimum(x, 0.0) - jnp.log1p(jnp.exp(-jnp.abs(x)))


def _logaddexp(a, b):
    return jnp.maximum(a, b) + jnp.log1p(jnp.exp(-jnp.abs(a - b)))


def _gelu_tanh(x):
    return 0.5 * x * (1.0 + jnp.tanh(math.sqrt(2.0 / math.pi) * (x + 0.044715 * (x * x * x))))


def _row_to_col(row):
    n = row.shape[1]
    r = lax.broadcasted_iota(jnp.int32, (n, n), 0)
    c = lax.broadcasted_iota(jnp.int32, (n, n), 1)
    return jnp.sum(jnp.where(r == c, row, 0.0), axis=1, keepdims=True)


def _cumsum_rows(x):
    c = x.shape[0]
    r = lax.broadcasted_iota(jnp.int32, (c, c), 0)
    k = lax.broadcasted_iota(jnp.int32, (c, c), 1)
    tri = jnp.where(r >= k, 1.0, 0.0).astype(BF16)
    hi = x.astype(BF16)
    r1 = x - hi.astype(F32)
    mid = r1.astype(BF16)
    lo = (r1 - mid.astype(F32)).astype(BF16)
    return _dot(tri, hi) + _dot(tri, mid) + _dot(tri, lo)


def _ffn_kernel(x_ref, nw_ref, wg_ref, wu_ref, wo_ref, o_ref, xn_ref, *, nf):
    f = pl.program_id(1)

    @pl.when(f == 0)
    def _():
        xn_ref[...] = _rms(x_ref[...], nw_ref[0:1, :]).astype(BF16)

    xn = xn_ref[...]
    g = _dot(xn, wg_ref[...])
    u = _dot(xn, wu_ref[...])
    part = _dot((_silu(g) * u).astype(BF16), wo_ref[...])

    @pl.when(f == 0)
    def _():
        o_ref[...] = part

    @pl.when(f > 0)
    def _():
        o_ref[...] += part

    @pl.when(f == nf - 1)
    def _():
        o_ref[...] = x_ref[...] + 0.5 * _rms(o_ref[...], nw_ref[1:2, :])


def _ffn(x, nw2, w_in, w_out):
    t, d = x.shape
    dff = w_out.shape[0]
    tm = _tile(t, 640)
    tf = _tile(dff, 512, 128)
    nf = dff // tf
    return pl.pallas_call(
        functools.partial(_ffn_kernel, nf=nf),
        grid=(t // tm, nf),
        in_specs=[pl.BlockSpec((tm, d), lambda i, f: (i, 0)),
                  pl.BlockSpec((2, d), lambda i, f: (0, 0)),
                  pl.BlockSpec((d, tf), lambda i, f: (0, f)),
                  pl.BlockSpec((d, tf), lambda i, f: (0, nf + f)),
                  pl.BlockSpec((tf, d), lambda i, f: (f, 0))],
        out_specs=pl.BlockSpec((tm, d), lambda i, f: (i, 0)),
        out_shape=jax.ShapeDtypeStruct((t, d), F32),
        scratch_shapes=[pltpu.VMEM((tm, d), BF16)],
        compiler_params=_cparams(("parallel", "arbitrary"), 56),
        name="ffn",
    )(x, nw2, w_in, w_in, w_out)


def _norm_matmul_kernel(x_ref, nw_ref, w_ref, o_ref, xn_ref):
    @pl.when(pl.program_id(1) == 0)
    def _():
        xn_ref[...] = _rms(x_ref[...], nw_ref[...]).astype(BF16)

    o_ref[...] = _dot(xn_ref[...], w_ref[...]).astype(o_ref.dtype)


def _norm_matmul(x, nw, w, name):
    t, d = x.shape
    n = w.shape[1]
    tm = _tile(t, 640)
    tn = _tile(n, 1024, 128)
    return pl.pallas_call(
        _norm_matmul_kernel,
        grid=(t // tm, n // tn),
        in_specs=[pl.BlockSpec((tm, d), lambda i, j: (i, 0)),
                  pl.BlockSpec((1, d), lambda i, j: (0, 0)),
                  pl.BlockSpec((d, tn), lambda i, j: (0, j))],
        out_specs=pl.BlockSpec((tm, tn), lambda i, j: (i, j)),
        out_shape=jax.ShapeDtypeStruct((t, n), F32),
        scratch_shapes=[pltpu.VMEM((tm, d), BF16)],
        compiler_params=_cparams(("parallel", "arbitrary"), 48),
        name=name,
    )(x, nw, w)


def _in_proj_kernel(x_ref, nw_ref, w_ref, wg_ref, o_ref, og_ref, xn_ref):
    @pl.when(pl.program_id(1) == 0)
    def _():
        xn = _rms(x_ref[...], nw_ref[...]).astype(BF16)
        xn_ref[...] = xn
        og_ref[...] = _dot(xn, wg_ref[...])

    o_ref[...] = _dot(xn_ref[...], w_ref[...])


def _in_proj(x, nw, w_main, w_gate):
    t, d = x.shape
    n = w_main.shape[1]
    tm = _tile(t, 640)
    tn = _tile(n, 1024, 128)
    return pl.pallas_call(
        _in_proj_kernel,
        grid=(t // tm, n // tn),
        in_specs=[pl.BlockSpec((tm, d), lambda i, j: (i, 0)),
                  pl.BlockSpec((1, d), lambda i, j: (0, 0)),
                  pl.BlockSpec((d, tn), lambda i, j: (0, j)),
                  pl.BlockSpec((d, GATE_PAD), lambda i, j: (0, 0))],
        out_specs=[pl.BlockSpec((tm, tn), lambda i, j: (i, j)),
                   pl.BlockSpec((tm, GATE_PAD), lambda i, j: (i, 0))],
        out_shape=[jax.ShapeDtypeStruct((t, n), F32), jax.ShapeDtypeStruct((t, GATE_PAD), F32)],
        scratch_shapes=[pltpu.VMEM((tm, d), BF16)],
        compiler_params=_cparams(("parallel", "arbitrary"), 48),
        name="in_proj",
    )(x, nw, w_main, w_gate)


def _merge_kernel(h0_ref, h1_ref, h2_ref, h3_ref, g_ref, w_ref, x_ref, nw_ref, o_ref, acc_ref, mb_ref):
    n = pl.program_id(1)
    half = mb_ref.shape[1] // 2
    for k, h_ref in enumerate((h0_ref, h1_ref, h2_ref, h3_ref)):
        @pl.when(n == k)
        def _(k=k, h_ref=h_ref):
            part = _sigmoid(g_ref[...]) * _dot(h_ref[...], w_ref[0])
            if k == 0:
                acc_ref[...] = part
            else:
                acc_ref[...] += part

    @pl.when(n == N_BRANCH)
    def _():
        mb = acc_ref[...].astype(BF16)
        mb_ref[...] = mb
        o_ref[...] = _dot(mb[:, :half], w_ref[0])

    @pl.when(n == N_BRANCH + 1)
    def _():
        out = o_ref[...] + _dot(mb_ref[:, half:], w_ref[0])
        o_ref[...] = x_ref[...] + _rms(out, nw_ref[...])


def _merge(hs, u, w6, x, nw):
    t, d = x.shape
    bw = hs[0].shape[1]
    tm = _tile(t, 416)
    gate_blk = OFF_GATES // d
    return pl.pallas_call(
        _merge_kernel,
        grid=(t // tm, N_BRANCH + 2),
        in_specs=[pl.BlockSpec((tm, bw), lambda i, n: (i, 0))] * N_BRANCH + [
            pl.BlockSpec((tm, d), lambda i, n: (i, gate_blk + jnp.minimum(n, N_BRANCH - 1))),
            pl.BlockSpec((1, bw, d), lambda i, n: (n, 0, 0)),
            pl.BlockSpec((tm, d), lambda i, n: (i, 0)),
            pl.BlockSpec((1, d), lambda i, n: (0, 0))],
        out_specs=pl.BlockSpec((tm, d), lambda i, n: (i, 0)),
        out_shape=jax.ShapeDtypeStruct((t, d), F32),
        scratch_shapes=[pltpu.VMEM((tm, d), F32), pltpu.VMEM((tm, d), BF16)],
        compiler_params=_cparams(("parallel", "arbitrary"), 56),
        name="merge",
    )(*hs, u, w6, x, nw)


def _proj_res_kernel(a_ref, w_ref, x_ref, nw_ref, o_ref):
    o_ref[...] = x_ref[...] + _rms(_dot(a_ref[...], w_ref[...]), nw_ref[...])


def _proj_res(a, w, x, nw):
    t, d = x.shape
    tm = _tile(t, 640)
    return pl.pallas_call(
        _proj_res_kernel,
        grid=(t // tm,),
        in_specs=[pl.BlockSpec((tm, d), lambda i: (i, 0)),
                  pl.BlockSpec((d, d), lambda i: (0, 0)),
                  pl.BlockSpec((tm, d), lambda i: (i, 0)),
                  pl.BlockSpec((1, d), lambda i: (0, 0))],
        out_specs=pl.BlockSpec((tm, d), lambda i: (i, 0)),
        out_shape=jax.ShapeDtypeStruct((t, d), F32),
        compiler_params=_cparams(("parallel",), 56),
        name="xa_out",
    )(a, w, x, nw)


def _rope(x, cs, sn):
    return x * cs + pltpu.roll(x, x.shape[1] // 2, 1) * sn


def _group_norm(o, eps=1e-5):
    oc = o - jnp.mean(o, axis=-1, keepdims=True)
    return oc * lax.rsqrt(jnp.mean(oc * oc, axis=-1, keepdims=True) + eps)


def _ret_kernel(q_ref, k_ref, v_ref, g_ref, cos_ref, sin_ref, dm_ref, dec_ref, h_ref, st_ref, s_ref, *, nc):
    c = pl.program_id(2)

    @pl.when(c == 0)
    def _():
        s_ref[...] = jnp.zeros_like(s_ref)

    cs, sn = cos_ref[...], sin_ref[...]
    qr = _rope(q_ref[...], cs, sn)
    kr = _rope(k_ref[...], cs, sn) * (DK ** -0.5)
    dec = dec_ref[0]
    e1, e2, e3 = dec[:, 0:1], dec[:, 1:2], dec[0:1, 2:3]
    vb = v_ref[...].astype(BF16)
    qb = qr.astype(BF16)
    s = s_ref[...]
    sc = _dot_nt(qb, kr.astype(BF16)) * dm_ref[0]
    o = _dot(sc.astype(BF16), vb) + _dot(qb, s.astype(BF16)) * e1
    s_new = e3 * s + _dot((kr * e2).T.astype(BF16), vb)
    s_ref[...] = s_new
    h_ref[...] = (_group_norm(o) * _silu(g_ref[...])).astype(h_ref.dtype)

    @pl.when(c == nc - 1)
    def _():
        st_ref[0, 0] = s_new


def _ret_prompt(u, bsz, seqlen, cos2, sin2, dmat, dec):
    nc = seqlen // CHUNK
    tp = bsz * seqlen
    row = lambda b, h, c: b * nc + c
    return pl.pallas_call(
        functools.partial(_ret_kernel, nc=nc),
        grid=(bsz, N_HEADS, nc),
        in_specs=[pl.BlockSpec((CHUNK, DK), lambda b, h, c: (row(b, h, c), OFF_RQ // DK + h)),
                  pl.BlockSpec((CHUNK, DK), lambda b, h, c: (row(b, h, c), OFF_RK // DK + h)),
                  pl.BlockSpec((CHUNK, DV), lambda b, h, c: (row(b, h, c), OFF_RV // DV + h)),
                  pl.BlockSpec((CHUNK, DV), lambda b, h, c: (row(b, h, c), OFF_RG // DV + h)),
                  pl.BlockSpec((CHUNK, DK), lambda b, h, c: (c, 0)),
                  pl.BlockSpec((CHUNK, DK), lambda b, h, c: (c, 0)),
                  pl.BlockSpec((1, CHUNK, CHUNK), lambda b, h, c: (h, 0, 0)),
                  pl.BlockSpec((1, CHUNK, 8), lambda b, h, c: (h, 0, 0))],
        out_specs=[pl.BlockSpec((CHUNK, DV), lambda b, h, c: (row(b, h, c), h)),
                   pl.BlockSpec((1, 1, DK, DV), lambda b, h, c: (b, h, 0, 0))],
        out_shape=[jax.ShapeDtypeStruct((tp, N_HEADS * DV), BF16),
                   jax.ShapeDtypeStruct((bsz, N_HEADS, DK, DV), F32)],
        scratch_shapes=[pltpu.VMEM((DK, DV), F32)],
        compiler_params=_cparams(("parallel", "parallel", "arbitrary"), 32),
        name="ret_prompt",
    )(u, u, u, u, cos2, sin2, dmat, dec)


def _mlstm_kernel(gb_ref, q_ref, k_ref, v_ref, mo_ref, ug_ref, ugt_ref, nw_ref,
                  h_ref, c_out, n_out, m_out, cm_ref, nv_ref, m_ref, *, nc):
    h = pl.program_id(1)
    c = pl.program_id(2)

    @pl.when(c == 0)
    def _():
        cm_ref[...] = jnp.zeros_like(cm_ref)
        nv_ref[...] = jnp.zeros_like(nv_ref)
        m_ref[...] = jnp.zeros_like(m_ref)

    n = CHUNK
    row = lax.broadcasted_iota(jnp.int32, (n, n), 0)
    col = lax.broadcasted_iota(jnp.int32, (n, n), 1)
    causal = row >= col
    bias_i = gb_ref[0, h]
    bias_f = gb_ref[1, h]
    ig_row = ugt_ref[pl.ds(h, 1), :] + bias_i
    lf_row = _log_sigmoid(ugt_ref[pl.ds(N_HEADS + h, 1), :] + bias_f)
    ug = ug_ref[...]
    lane = lax.broadcasted_iota(jnp.int32, ug.shape, 1)
    ig_col = jnp.sum(jnp.where(lane == h, ug, 0.0), axis=1, keepdims=True) + bias_i
    lf_col = _log_sigmoid(jnp.sum(jnp.where(lane == N_HEADS + h, ug, 0.0), axis=1, keepdims=True) + bias_f)
    b_col = jnp.sum(jnp.where(causal, lf_row, 0.0), axis=1, keepdims=True)
    b_row = jnp.sum(jnp.where(row <= col, lf_col, 0.0), axis=0, keepdims=True)
    b_last = b_col[n - 1:n, :]
    m = m_ref[...]
    logd = jnp.where(causal, b_col - b_row + ig_row, -jnp.inf)
    inter = m + b_col
    mi = jnp.maximum(inter, jnp.max(logd, axis=1, keepdims=True))
    w = jnp.exp(logd - mi)
    wi = jnp.exp(inter - mi)
    q = q_ref[...]
    k = k_ref[...] * (DK ** -0.5)
    qb = q.astype(BF16)
    vb = v_ref[...].astype(BF16)
    cm = cm_ref[...]
    nv = nv_ref[...]
    a = _dot_nt(qb, k.astype(BF16)) * w
    num = _dot(a.astype(BF16), vb) + wi * _dot(qb, cm.astype(BF16))
    nq = jnp.sum(a, axis=1, keepdims=True) + wi * jnp.sum(q * nv, axis=1, keepdims=True)
    hh = num / jnp.maximum(jnp.abs(nq), jnp.exp(-mi))
    m_new = mi[n - 1:n, :]
    wl = jnp.exp(b_last - b_col + ig_col - m_new)
    dp = jnp.exp(m + b_last - m_new)
    kw = k * wl
    cm_new = dp * cm + _dot(kw.T.astype(BF16), vb)
    nv_new = dp * nv + jnp.sum(kw, axis=0, keepdims=True)
    cm_ref[...] = cm_new
    nv_ref[...] = nv_new
    m_ref[...] = m_new
    h_ref[...] = (_rms(hh, nw_ref[...]) * _sigmoid(mo_ref[...])).astype(h_ref.dtype)

    @pl.when(c == nc - 1)
    def _():
        c_out[0, 0] = cm_new
        n_out[0, 0] = nv_new
        m_out[0, 0] = m_new


def _mlstm_prompt(u, ug, ugt, gate_bias, norm_w, bsz, seqlen):
    nc = seqlen // CHUNK
    tp = bsz * seqlen
    row = lambda b, h, c: b * nc + c
    return pl.pallas_call(
        functools.partial(_mlstm_kernel, nc=nc),
        grid=(bsz, N_HEADS, nc),
        in_specs=[pl.BlockSpec(memory_space=pltpu.SMEM),
                  pl.BlockSpec((CHUNK, DK), lambda b, h, c: (row(b, h, c), OFF_MQ // DK + h)),
                  pl.BlockSpec((CHUNK, DK), lambda b, h, c: (row(b, h, c), OFF_MK // DK + h)),
                  pl.BlockSpec((CHUNK, DV), lambda b, h, c: (row(b, h, c), OFF_MV // DV + h)),
                  pl.BlockSpec((CHUNK, DV), lambda b, h, c: (row(b, h, c), OFF_MO // DV + h)),
                  pl.BlockSpec((CHUNK, GATE_PAD), lambda b, h, c: (row(b, h, c), 0)),
                  pl.BlockSpec((2 * N_HEADS, CHUNK), lambda b, h, c: (0, row(b, h, c))),
                  pl.BlockSpec((1, DV), lambda b, h, c: (0, h))],
        out_specs=[pl.BlockSpec((CHUNK, DV), lambda b, h, c: (row(b, h, c), h)),
                   pl.BlockSpec((1, 1, DK, DV), lambda b, h, c: (b, h, 0, 0)),
                   pl.BlockSpec((1, 1, 1, DK), lambda b, h, c: (b, h, 0, 0)),
                   pl.BlockSpec((1, 1, 1, 1), lambda b, h, c: (b, h, 0, 0))],
        out_shape=[jax.ShapeDtypeStruct((tp, N_HEADS * DV), BF16),
                   jax.ShapeDtypeStruct((bsz, N_HEADS, DK, DV), F32),
                   jax.ShapeDtypeStruct((bsz, N_HEADS, 1, DK), F32),
                   jax.ShapeDtypeStruct((bsz, N_HEADS, 1, 1), F32)],
        scratch_shapes=[pltpu.VMEM((DK, DV), F32), pltpu.VMEM((1, DK), F32), pltpu.VMEM((1, 1), F32)],
        compiler_params=_cparams(("parallel", "parallel", "arbitrary"), 32),
        name="mlstm_prompt",
    )(gate_bias, u, u, u, u, ug, ugt, norm_w)


def _hgrn_lower_bound(logits, layer):
    e = jnp.exp(logits - jnp.max(logits, axis=0, keepdims=True))
    p = e / jnp.sum(e, axis=0, keepdims=True)
    lb = jnp.zeros_like(p[0:1, :])
    for r in range(1, layer + 1):
        lb = lb + p[r:r + 1, :]
    return lb


def _hgrn_gates(fpre, lb):
    hlf = _logaddexp(jnp.log(lb), jnp.log1p(-lb) + _log_sigmoid(fpre))
    hk = (1.0 - lb) * _sigmoid(-fpre)
    return hlf, hk


def _hgrn_kernel(lg_ref, q_ref, f_ref, i_ref, g_ref, nw_ref, h_ref, st_ref, s_ref, *, nc, layer):
    c = pl.program_id(2)

    @pl.when(c == 0)
    def _():
        s_ref[...] = jnp.zeros_like(s_ref)

    n = CHUNK
    lb = _hgrn_lower_bound(lg_ref[...], layer)
    hlf, hk = _hgrn_gates(f_ref[...], lb)
    q = _silu(q_ref[...])
    ib = i_ref[...].astype(BF16)
    bc = _cumsum_rows(hlf)
    r2 = lax.broadcasted_iota(jnp.int32, (n, n), 0)
    c2 = lax.broadcasted_iota(jnp.int32, (n, n), 1)
    rowv = lax.broadcasted_iota(jnp.int32, (n, DK), 0)
    a = jnp.where(r2 == c2, _dot_nt(q.astype(BF16), hk.astype(BF16)), 0.0)
    p = bc
    s = 1
    lev = 0
    while s < n:
        right = (rowv & s) != 0
        nxt = pltpu.roll(p, n - s, 0)
        e = jnp.exp(jnp.where(right, bc - p, nxt - bc))
        qs = jnp.where(right, q * e, 0.0).astype(BF16)
        ks = jnp.where(right, 0.0, hk * e).astype(BF16)
        a = a + jnp.where((r2 >> (lev + 1)) == (c2 >> (lev + 1)), _dot_nt(qs, ks), 0.0)
        if 2 * s < n:
            p = jnp.where(right, pltpu.roll(p, s, 0), p)
        s *= 2
        lev += 1
    st = s_ref[...]
    o = _dot(a.astype(BF16), ib) + _dot((q * jnp.exp(bc)).astype(BF16), st.astype(BF16))
    bl = bc[n - 1:n, :]
    s_new = _row_to_col(jnp.exp(bl)) * st + _dot((hk * jnp.exp(bl - bc)).T.astype(BF16), ib)
    s_ref[...] = s_new
    h_ref[...] = (_rms(o, nw_ref[...]) * _silu(g_ref[...])).astype(h_ref.dtype)

    @pl.when(c == nc - 1)
    def _():
        st_ref[0, 0] = s_new


def _hgrn_prompt(u, logits, norm_w, bsz, seqlen, layer):
    nc = seqlen // CHUNK
    tp = bsz * seqlen
    depth = logits.shape[0]
    row = lambda b, h, c: b * nc + c
    return pl.pallas_call(
        functools.partial(_hgrn_kernel, nc=nc, layer=layer),
        grid=(bsz, N_HEADS, nc),
        in_specs=[pl.BlockSpec((depth, DK), lambda b, h, c: (0, h)),
                  pl.BlockSpec((CHUNK, DK), lambda b, h, c: (row(b, h, c), OFF_HQ // DK + h)),
                  pl.BlockSpec((CHUNK, DK), lambda b, h, c: (row(b, h, c), OFF_HF // DK + h)),
                  pl.BlockSpec((CHUNK, DV), lambda b, h, c: (row(b, h, c), OFF_HI // DV + h)),
                  pl.BlockSpec((CHUNK, DV), lambda b, h, c: (row(b, h, c), OFF_HG // DV + h)),
                  pl.BlockSpec((1, DV), lambda b, h, c: (0, h))],
        out_specs=[pl.BlockSpec((CHUNK, DV), lambda b, h, c: (row(b, h, c), h)),
                   pl.BlockSpec((1, 1, DK, DV), lambda b, h, c: (b, h, 0, 0))],
        out_shape=[jax.ShapeDtypeStruct((tp, N_HEADS * DV), BF16),
                   jax.ShapeDtypeStruct((bsz, N_HEADS, DK, DV), F32)],
        scratch_shapes=[pltpu.VMEM((DK, DV), F32)],
        compiler_params=_cparams(("parallel", "parallel", "arbitrary"), 32),
        name="hgrn_prompt",
    )(logits, u, u, u, u, norm_w)


def _s5_prep_kernel(lr_ref, li_ref, dt_ref, bre_ref, bim_ref, ar_ref, ai_ref, bbr_ref, bbi_ref):
    lr, li, dt = lr_ref[...], li_ref[...], dt_ref[...]
    mag = jnp.exp(lr * dt)
    ar = mag * jnp.cos(li * dt)
    ai = mag * jnp.sin(li * dt)
    den = lr * lr + li * li
    cr = ((ar - 1.0) * lr + ai * li) / den
    ci = (ai * lr - (ar - 1.0) * li) / den
    ar_ref[...] = ar
    ai_ref[...] = ai
    bbr_ref[...] = cr * bre_ref[...] - ci * bim_ref[...]
    bbi_ref[...] = cr * bim_ref[...] + ci * bre_ref[...]


def _s5_prep(lam_re, lam_im, log_step, b_re, b_im):
    g, n = lam_re.shape
    p = b_re.shape[-1]
    gn = g * n
    dt = jnp.repeat(jnp.exp(log_step.astype(F32)), n).reshape(gn, 1)
    col = lambda a: a.astype(F32).reshape(gn, 1)
    return pl.pallas_call(
        _s5_prep_kernel,
        out_shape=[jax.ShapeDtypeStruct((gn, 1), F32), jax.ShapeDtypeStruct((gn, 1), F32),
                   jax.ShapeDtypeStruct((gn, p), F32), jax.ShapeDtypeStruct((gn, p), F32)],
        name="s5_prep",
    )(col(lam_re), col(lam_im), dt, b_re.reshape(gn, p), b_im.reshape(gn, p))


def _s5_kernel(*refs, seq, nc, n_kt, sw):
    if seq:
        (u_ref, ar_ref, ai_ref, bblk_ref, cblk_ref, d_ref, wglu_ref,
         h_ref, sr_ref, si_ref, hr_scr, hi_scr) = refs
    else:
        (u_ref, ar_ref, ai_ref, bblk_ref, cblk_ref, d_ref, wglu_ref, h0r_ref, h0i_ref,
         h_ref, sr_ref, si_ref) = refs
    u = u_ref[...]
    ub = u.astype(BF16)
    rows = u.shape[0]
    if seq:
        c = pl.program_id(1)

        @pl.when(c == 0)
        def _():
            hr_scr[...] = jnp.zeros_like(hr_scr)
            hi_scr[...] = jnp.zeros_like(hi_scr)

        rowi = lax.broadcasted_iota(jnp.int32, (rows, sw), 0)
    ys = []
    for kt in range(n_kt):
        lanes = slice(kt * sw, (kt + 1) * sw)
        bu = _dot(ub[:, kt * MXU_K:(kt + 1) * MXU_K], bblk_ref[kt])
        xr, xi = bu[:, :sw], bu[:, sw:]
        ar, ai = ar_ref[:, lanes], ai_ref[:, lanes]
        if seq:
            h0r, h0i = hr_scr[:, lanes], hi_scr[:, lanes]
            first = rowi == 0
            xr = xr + jnp.where(first, ar * h0r - ai * h0i, 0.0)
            xi = xi + jnp.where(first, ar * h0i + ai * h0r, 0.0)
            pr, pi = ar, ai
            sft = 1
            while sft < rows:
                live = rowi >= sft
                sr = jnp.where(live, pltpu.roll(xr, sft, 0), 0.0)
                si = jnp.where(live, pltpu.roll(xi, sft, 0), 0.0)
                xr, xi = xr + pr * sr - pi * si, xi + pr * si + pi * sr
                pr, pi = pr * pr - pi * pi, 2.0 * pr * pi
                sft *= 2
            hr_scr[:, lanes] = xr[rows - 1:rows, :]
            hi_scr[:, lanes] = xi[rows - 1:rows, :]
        else:
            h0r, h0i = h0r_ref[:, lanes], h0i_ref[:, lanes]
            xr, xi = xr + ar * h0r - ai * h0i, xi + ar * h0i + ai * h0r
            sr_ref[:, lanes] = xr
            si_ref[:, lanes] = xi
        ys.append(_dot(jnp.concatenate([xr, xi], axis=1).astype(BF16), cblk_ref[kt]))
    y = jnp.concatenate(ys, axis=1) + d_ref[...] * u
    z = _gelu_tanh(y)
    h_ref[...] = (z * _sigmoid(_dot(z.astype(BF16), wglu_ref[...]))).astype(h_ref.dtype)
    if seq:
        @pl.when(c == nc - 1)
        def _():
            sr_ref[0] = hr_scr[...]
            si_ref[0] = hi_scr[...]


def _s5_weights(bbr, bbi, c_re, c_im):
    gn, p = bbr.shape
    g = gn // S5_N
    gpt = MXU_K // p
    n_kt = g // gpt
    eye = jnp.eye(gpt, dtype=F32)

    def b_blk(bb):
        return jnp.einsum('kgnp,gh->kgphn', bb.reshape(n_kt, gpt, S5_N, p), eye).reshape(n_kt, gpt * p, gpt * S5_N)

    def c_blk(cc):
        return jnp.einsum('kgpn,gh->kgnhp', cc.astype(F32).reshape(n_kt, gpt, p, S5_N), eye).reshape(
            n_kt, gpt * S5_N, gpt * p)

    bblk = jnp.concatenate([b_blk(bbr), b_blk(bbi)], axis=2).astype(BF16)
    cblk = jnp.concatenate([c_blk(c_re), -c_blk(c_im)], axis=1).astype(BF16)
    return bblk, cblk


def _s5_prompt(u, ar, ai, bblk, cblk, d, wglu, bsz, seqlen):
    width = wglu.shape[0]
    n_kt = bblk.shape[0]
    sw = bblk.shape[2] // 2
    ct = _tile(seqlen, 256)
    nc = seqlen // ct
    tp = bsz * seqlen
    gn = ar.shape[1]
    full = lambda shape: pl.BlockSpec(shape, lambda b, c: (0,) * len(shape))
    return pl.pallas_call(
        functools.partial(_s5_kernel, seq=True, nc=nc, n_kt=n_kt, sw=sw),
        grid=(bsz, nc),
        in_specs=[pl.BlockSpec((ct, width), lambda b, c: (b * nc + c, OFF_SU // width)),
                  full((1, gn)), full((1, gn)), full(bblk.shape), full(cblk.shape),
                  full((1, width)), full(wglu.shape)],
        out_specs=[pl.BlockSpec((ct, width), lambda b, c: (b * nc + c, 0)),
                   pl.BlockSpec((1, 1, gn), lambda b, c: (b, 0, 0)),
                   pl.BlockSpec((1, 1, gn), lambda b, c: (b, 0, 0))],
        out_shape=[jax.ShapeDtypeStruct((tp, width), BF16),
                   jax.ShapeDtypeStruct((bsz, 1, gn), F32), jax.ShapeDtypeStruct((bsz, 1, gn), F32)],
        scratch_shapes=[pltpu.VMEM((1, gn), F32), pltpu.VMEM((1, gn), F32)],
        compiler_params=_cparams(("parallel", "arbitrary"), 56),
        name="s5_prompt",
    )(u, ar, ai, bblk, cblk, d, wglu)


def _s5_sample(u, ar, ai, bblk, cblk, d, wglu, h0r, h0i):
    width = wglu.shape[0]
    n_kt = bblk.shape[0]
    sw = bblk.shape[2] // 2
    bs, gn = h0r.shape
    tb = _tile(bs, 128, 8)
    full = lambda shape: pl.BlockSpec(shape, lambda i: (0,) * len(shape))
    return pl.pallas_call(
        functools.partial(_s5_kernel, seq=False, nc=1, n_kt=n_kt, sw=sw),
        grid=(bs // tb,),
        in_specs=[pl.BlockSpec((tb, width), lambda i: (i, OFF_SU // width)),
                  full((1, gn)), full((1, gn)), full(bblk.shape), full(cblk.shape),
                  full((1, width)), full(wglu.shape),
                  pl.BlockSpec((tb, gn), lambda i: (i, 0)), pl.BlockSpec((tb, gn), lambda i: (i, 0))],
        out_specs=[pl.BlockSpec((tb, width), lambda i: (i, 0)),
                   pl.BlockSpec((tb, gn), lambda i: (i, 0)), pl.BlockSpec((tb, gn), lambda i: (i, 0))],
        out_shape=[jax.ShapeDtypeStruct((bs, width), BF16),
                   jax.ShapeDtypeStruct((bs, gn), F32), jax.ShapeDtypeStruct((bs, gn), F32)],
        compiler_params=_cparams(("parallel",), 56),
        name="s5_sample",
    )(u, ar, ai, bblk, cblk, d, wglu, h0r, h0i)


SAMPLE_BLOCK = 8


def _state_step(s, dcol, kcol, vrow, qcol):
    s_new = dcol * s + kcol * vrow
    return s_new, jnp.sum(qcol * s_new, axis=0, keepdims=True)


def _ret_step_kernel(gam_ref, q_ref, k_ref, v_ref, g_ref, cos_ref, sin_ref, st_ref, h_ref, so_ref, o_scr):
    cs, sn = cos_ref[...], sin_ref[...]
    for h in range(N_HEADS):
        qr = _rope(q_ref[:, h * DK:(h + 1) * DK], cs, sn)
        kr = _rope(k_ref[:, h * DK:(h + 1) * DK], cs, sn) * (DK ** -0.5)
        for i in range(SAMPLE_BLOCK):
            s_new, o = _state_step(st_ref[0, i, h], gam_ref[h], _row_to_col(kr[i:i + 1, :]),
                                   v_ref[i:i + 1, h * DV:(h + 1) * DV], _row_to_col(qr[i:i + 1, :]))
            so_ref[0, i, h] = s_new
            o_scr[i:i + 1, h * DV:(h + 1) * DV] = o
    for h in range(N_HEADS):
        cols = slice(h * DV, (h + 1) * DV)
        h_ref[:, cols] = (_group_norm(o_scr[:, cols]) * _silu(g_ref[:, cols])).astype(h_ref.dtype)


def _mlstm_step_kernel(gb_ref, q_ref, k_ref, v_ref, mo_ref, ug_ref, nw_ref, c_ref, n_ref, m_ref,
                       h_ref, co_ref, no_ref, mo_out_ref, o_scr):
    ug = ug_ref[...]
    m_all = m_ref[0]
    m_new_cols = []
    for h in range(N_HEADS):
        ig = ug[:, h:h + 1] + gb_ref[0, h]
        lf = _log_sigmoid(ug[:, N_HEADS + h:N_HEADS + h + 1] + gb_ref[1, h])
        m_old = m_all[:, h:h + 1]
        inter = m_old + lf
        mi = jnp.maximum(inter, ig)
        w = jnp.exp(ig - mi)
        wi = jnp.exp(inter - mi)
        m_new_cols.append(mi)
        lim = jnp.exp(-mi)
        qh = q_ref[:, h * DK:(h + 1) * DK]
        kh = k_ref[:, h * DK:(h + 1) * DK] * (DK ** -0.5)
        for i in range(SAMPLE_BLOCK):
            wk = w[i:i + 1, :] * kh[i:i + 1, :]
            c_new, num = _state_step(c_ref[0, i, h], wi[i:i + 1, :], _row_to_col(wk),
                                     v_ref[i:i + 1, h * DV:(h + 1) * DV], _row_to_col(qh[i:i + 1, :]))
            n_new = wi[i:i + 1, :] * n_ref[0, i, h:h + 1, :] + wk
            nq = jnp.sum(qh[i:i + 1, :] * n_new, axis=1, keepdims=True)
            co_ref[0, i, h] = c_new
            no_ref[0, i, h:h + 1, :] = n_new
            o_scr[i:i + 1, h * DV:(h + 1) * DV] = num / jnp.maximum(jnp.abs(nq), lim[i:i + 1, :])
    mo_out_ref[0] = jnp.concatenate(m_new_cols, axis=1)
    for h in range(N_HEADS):
        cols = slice(h * DV, (h + 1) * DV)
        h_ref[:, cols] = (_rms(o_scr[:, cols], nw_ref[:, cols]) * _sigmoid(mo_ref[:, cols])).astype(h_ref.dtype)


def _hgrn_step_kernel(lg_ref, q_ref, f_ref, i_ref, g_ref, nw_ref, st_ref, h_ref, so_ref, o_scr, *, layer):
    lb_all = _hgrn_lower_bound(lg_ref[...], layer)
    for h in range(N_HEADS):
        hlf, hk = _hgrn_gates(f_ref[:, h * DK:(h + 1) * DK], lb_all[:, h * DK:(h + 1) * DK])
        dec = jnp.exp(hlf)
        qh = _silu(q_ref[:, h * DK:(h + 1) * DK])
        for i in range(SAMPLE_BLOCK):
            s_new, o = _state_step(st_ref[0, i, h], _row_to_col(dec[i:i + 1, :]), _row_to_col(hk[i:i + 1, :]),
                                   i_ref[i:i + 1, h * DV:(h + 1) * DV], _row_to_col(qh[i:i + 1, :]))
            so_ref[0, i, h] = s_new
            o_scr[i:i + 1, h * DV:(h + 1) * DV] = o
    for h in range(N_HEADS):
        cols = slice(h * DV, (h + 1) * DV)
        h_ref[:, cols] = (_rms(o_scr[:, cols], nw_ref[:, cols]) * _silu(g_ref[:, cols])).astype(h_ref.dtype)


def _seg(width, off):
    return pl.BlockSpec((SAMPLE_BLOCK, width), lambda i: (i, off // width))


def _state_spec(layer, tail):
    return pl.BlockSpec((1, SAMPLE_BLOCK) + tail, lambda i: (layer, i) + (0,) * len(tail))


def _new_state_spec(tail):
    return pl.BlockSpec((1, SAMPLE_BLOCK) + tail, lambda i: (0, i) + (0,) * len(tail))


_MAT = (N_HEADS, DK, DV)
_SMEM = pl.BlockSpec(memory_space=pltpu.SMEM)


def _whole(shape):
    return pl.BlockSpec(shape, lambda i: (0,) * len(shape))


def _ret_sample(us, gamma, cos2, sin2, state, layer):
    bs = us.shape[0]
    hw, kw = N_HEADS * DV, N_HEADS * DK
    return pl.pallas_call(
        _ret_step_kernel,
        grid=(bs // SAMPLE_BLOCK,),
        in_specs=[_SMEM, _seg(kw, OFF_RQ), _seg(kw, OFF_RK), _seg(hw, OFF_RV), _seg(hw, OFF_RG),
                  _whole((1, DK)), _whole((1, DK)), _state_spec(layer, _MAT)],
        out_specs=[pl.BlockSpec((SAMPLE_BLOCK, hw), lambda i: (i, 0)), _new_state_spec(_MAT)],
        out_shape=[jax.ShapeDtypeStruct((bs, hw), BF16), jax.ShapeDtypeStruct((1, bs) + _MAT, F32)],
        scratch_shapes=[pltpu.VMEM((SAMPLE_BLOCK, hw), F32)],
        compiler_params=_cparams(("parallel",), 40),
        name="ret_sample",
    )(gamma, us, us, us, us, cos2, sin2, state)


def _mlstm_sample(us, ugs, gate_bias, norm_w, st_c, st_n, st_m, layer):
    bs = us.shape[0]
    hw, kw = N_HEADS * DV, N_HEADS * DK
    return pl.pallas_call(
        _mlstm_step_kernel,
        grid=(bs // SAMPLE_BLOCK,),
        in_specs=[_SMEM, _seg(kw, OFF_MQ), _seg(kw, OFF_MK), _seg(hw, OFF_MV), _seg(hw, OFF_MO),
                  pl.BlockSpec((SAMPLE_BLOCK, GATE_PAD), lambda i: (i, 0)), _whole((1, hw)),
                  _state_spec(layer, _MAT), _state_spec(layer, (N_HEADS, DK)), _state_spec(layer, (N_HEADS,))],
        out_specs=[pl.BlockSpec((SAMPLE_BLOCK, hw), lambda i: (i, 0)), _new_state_spec(_MAT),
                   _new_state_spec((N_HEADS, DK)), _new_state_spec((N_HEADS,))],
        out_shape=[jax.ShapeDtypeStruct((bs, hw), BF16), jax.ShapeDtypeStruct((1, bs) + _MAT, F32),
                   jax.ShapeDtypeStruct((1, bs, N_HEADS, DK), F32), jax.ShapeDtypeStruct((1, bs, N_HEADS), F32)],
        scratch_shapes=[pltpu.VMEM((SAMPLE_BLOCK, hw), F32)],
        compiler_params=_cparams(("parallel",), 40),
        name="mlstm_sample",
    )(gate_bias, us, us, us, us, ugs, norm_w, st_c, st_n, st_m)


def _hgrn_sample(us, logits, norm_w, state, layer):
    bs = us.shape[0]
    hw, kw = N_HEADS * DV, N_HEADS * DK
    return pl.pallas_call(
        functools.partial(_hgrn_step_kernel, layer=layer),
        grid=(bs // SAMPLE_BLOCK,),
        in_specs=[_whole(logits.shape), _seg(kw, OFF_HQ), _seg(kw, OFF_HF), _seg(hw, OFF_HI), _seg(hw, OFF_HG),
                  _whole((1, hw)), _state_spec(layer, _MAT)],
        out_specs=[pl.BlockSpec((SAMPLE_BLOCK, hw), lambda i: (i, 0)), _new_state_spec(_MAT)],
        out_shape=[jax.ShapeDtypeStruct((bs, hw), BF16), jax.ShapeDtypeStruct((1, bs) + _MAT, F32)],
        scratch_shapes=[pltpu.VMEM((SAMPLE_BLOCK, hw), F32)],
        compiler_params=_cparams(("parallel",), 40),
        name="hgrn_sample",
    )(logits, us, us, us, us, norm_w, state)


def _attn_prompt_kernel(q_ref, k_ref, v_ref, o_ref, *, scale):
    s = _dot_nt(q_ref[...].astype(BF16), k_ref[...].astype(BF16)) * scale
    e = jnp.exp(s - jnp.max(s, axis=1, keepdims=True))
    p = e / jnp.sum(e, axis=1, keepdims=True)
    o_ref[...] = _dot(p.astype(BF16), v_ref[...].astype(BF16)).astype(o_ref.dtype)


def _attn_prompt(q, kv, bsz, seqlen, mem_len):
    d = q.shape[1]
    dh = d // N_HEADS
    tq = _tile(seqlen, 512)
    nq = seqlen // tq
    return pl.pallas_call(
        functools.partial(_attn_prompt_kernel, scale=dh ** -0.5),
        grid=(bsz, N_HEADS, nq),
        in_specs=[pl.BlockSpec((tq, dh), lambda b, h, i: (b * nq + i, h)),
                  pl.BlockSpec((mem_len, dh), lambda b, h, i: (b, h)),
                  pl.BlockSpec((mem_len, dh), lambda b, h, i: (b, N_HEADS + h))],
        out_specs=pl.BlockSpec((tq, dh), lambda b, h, i: (b * nq + i, h)),
        out_shape=jax.ShapeDtypeStruct((bsz * seqlen, d), BF16),
        compiler_params=_cparams(("parallel", "parallel", "arbitrary"), 32),
        name="attn_prompt",
    )(q, kv, kv)


ATTN_SAMPLE_BLOCK = 2


def _attn_sample_kernel(q_ref, k_ref, v_ref, o_ref, *, scale, dh):
    for i in range(ATTN_SAMPLE_BLOCK):
        prod = k_ref[0, i] * q_ref[i]
        outs = []
        for h in range(N_HEADS):
            cols = slice(h * dh, (h + 1) * dh)
            s = jnp.sum(prod[:, cols], axis=1, keepdims=True) * scale
            e = jnp.exp(s - jnp.max(s, axis=0, keepdims=True))
            p = e / jnp.sum(e, axis=0, keepdims=True)
            outs.append(jnp.sum(p * v_ref[0, i, :, cols], axis=0, keepdims=True))
        o_ref[i] = jnp.concatenate(outs, axis=1)


def _attn_sample(q3, cache_k, cache_v, layer):
    bs, _, d = q3.shape
    mem_len = cache_k.shape[2]
    blk = ATTN_SAMPLE_BLOCK
    kv_spec = pl.BlockSpec((1, blk, mem_len, d), lambda i: (layer, i, 0, 0))
    return pl.pallas_call(
        functools.partial(_attn_sample_kernel, scale=(d // N_HEADS) ** -0.5, dh=d // N_HEADS),
        grid=(bs // blk,),
        in_specs=[pl.BlockSpec((blk, 1, d), lambda i: (i, 0, 0)), kv_spec, kv_spec],
        out_specs=pl.BlockSpec((blk, 1, d), lambda i: (i, 0, 0)),
        out_shape=jax.ShapeDtypeStruct((bs, 1, d), F32),
        compiler_params=_cparams(("parallel",), 40),
        name="attn_sample",
    )(q3, cache_k, cache_v)


def _rope_tables(pos):
    half = DK // 2
    inv = ROPE_BASE ** (-jnp.arange(half, dtype=F32) / half)
    ang = pos.astype(F32)[:, None] * inv[None, :]
    cos, sin = jnp.cos(ang), jnp.sin(ang)
    return jnp.concatenate([cos, cos], axis=1), jnp.concatenate([-sin, sin], axis=1)


def _retention_tables(c):
    lg = jnp.log1p(-jnp.exp2(-5.0 - jnp.arange(N_HEADS, dtype=F32)))
    idx = jnp.arange(c, dtype=F32)
    diff = idx[:, None] - idx[None, :]
    dmat = jnp.where(diff[None] >= 0, jnp.exp(jnp.maximum(diff, 0.0)[None] * lg[:, None, None]), 0.0)
    e1 = jnp.exp((idx[None, :] + 1.0) * lg[:, None])
    e2 = jnp.exp((c - 1.0 - idx)[None, :] * lg[:, None])
    e3 = jnp.broadcast_to(jnp.exp(c * lg)[:, None], (N_HEADS, c))
    dec = jnp.stack([e1, e2, e3] + [jnp.zeros_like(e1)] * 5, axis=-1)
    return dmat, dec


def kernel(x_prompt, x_sample, mem_prompt, state_ret, state_mlstm_c, state_mlstm_n, state_mlstm_m, state_s5_re, state_s5_im, state_hgrn, cache_mem_k, cache_mem_v, norm_w, ffn_w_in, ffn_w_out, w_in, ml_gate_bias, ml_norm_w, s5_lambda_re, s5_lambda_im, s5_log_step, s5_b_re, s5_b_im, s5_c_re, s5_c_im, s5_d, s5_w_glu, hg_lb_logits, hg_norm_w, w_branch, w_out, xa_mem_norm, xa_wq, xa_wkv, xa_wo):
    bsz, seqlen, d = x_prompt.shape
    bs = x_sample.shape[0]
    depth = norm_w.shape[0]
    mem_len = mem_prompt.shape[1]
    tp = bsz * seqlen
    gn = s5_lambda_re.shape[1] * s5_lambda_re.shape[2]
    seg_gate = OFF_MO

    x = jnp.concatenate([x_prompt.reshape(tp, d), x_sample.reshape(bs, d)], axis=0)
    mem = mem_prompt.reshape(bsz * mem_len, d)
    cache_k = cache_mem_k.reshape(depth, bs, mem_len, d)
    cache_v = cache_mem_v.reshape(depth, bs, mem_len, d)

    cos_p, sin_p = _rope_tables(jnp.arange(seqlen, dtype=jnp.int32))
    cos_s, sin_s = _rope_tables(PAST_LEN + jnp.arange(1, dtype=jnp.int32))
    dmat, dec = _retention_tables(CHUNK)
    _, dec1 = _retention_tables(1)
    gamma = dec1[:, 0, 0]

    outs = {k: [] for k in ('p_ret', 'p_c', 'p_n', 'p_m', 'p_sr', 'p_si', 'p_hg', 'p_k', 'p_v',
                            's_ret', 's_c', 's_n', 's_m', 's_sr', 's_si', 's_hg')}
    for l in range(depth):
        nw = norm_w[l]
        x = _ffn(x, nw[0:2], ffn_w_in[l, 0].astype(BF16), ffn_w_out[l, 0].astype(BF16))
        wl = w_in[l]
        n_gate = 2 * N_HEADS
        w_main = jnp.concatenate([wl[:, :seg_gate], wl[:, seg_gate + n_gate:]], axis=1).astype(BF16)
        w_gate = jnp.pad(wl[:, seg_gate:seg_gate + n_gate], ((0, 0), (0, GATE_PAD - n_gate))).astype(BF16)
        u, ug = _in_proj(x, nw[2:3], w_main, w_gate)
        us, ugs = u[tp:], ug[tp:]
        ugt = ug[:tp, :n_gate].T
        h_ret, p_ret = _ret_prompt(u, bsz, seqlen, cos_p, sin_p, dmat, dec)
        mlw = ml_norm_w[l].reshape(1, -1)
        h_ml, p_c, p_n, p_m = _mlstm_prompt(u, ug, ugt, ml_gate_bias[l], mlw, bsz, seqlen)
        hgw = hg_norm_w[l].reshape(1, -1)
        h_hg, p_hg = _hgrn_prompt(u, hg_lb_logits, hgw, bsz, seqlen, l)
        ar, ai, bbr, bbi = _s5_prep(s5_lambda_re[l], s5_lambda_im[l], s5_log_step[l], s5_b_re[l], s5_b_im[l])
        ar, ai = ar.reshape(1, gn), ai.reshape(1, gn)
        bblk, cblk = _s5_weights(bbr, bbi, s5_c_re[l], s5_c_im[l])
        s5d = s5_d[l].astype(F32).reshape(1, -1)
        wglu = s5_w_glu[l].astype(BF16)
        h_s5, p_sr, p_si = _s5_prompt(u, ar, ai, bblk, cblk, s5d, wglu, bsz, seqlen)
        hs_ret, s_ret = _ret_sample(us, gamma, cos_s, sin_s, state_ret, l)
        hs_ml, s_c, s_n, s_m = _mlstm_sample(us, ugs, ml_gate_bias[l], mlw, state_mlstm_c, state_mlstm_n,
                                             state_mlstm_m, l)
        hs_hg, s_hg = _hgrn_sample(us, hg_lb_logits, hgw, state_hgrn, l)
        hs_s5, s_sr, s_si = _s5_sample(us, ar, ai, bblk, cblk, s5d, wglu,
                                       state_s5_re[l].reshape(bs, gn), state_s5_im[l].reshape(bs, gn))
        hs = [jnp.concatenate(pair, axis=0) for pair in
              ((h_ret, hs_ret), (h_ml, hs_ml), (h_s5, hs_s5), (h_hg, hs_hg))]
        w6 = jnp.concatenate([w_branch[l], w_out[l].reshape(2, d // 2, d)], axis=0).astype(BF16)
        x = _merge(hs, u, w6, x, nw[3:4])
        kv = _norm_matmul(mem, xa_mem_norm[l].reshape(1, d), xa_wkv[l].astype(BF16), "mem_kv")
        q = _norm_matmul(x, nw[4:5], xa_wq[l].astype(BF16), "xa_q")
        o_p = _attn_prompt(q, kv, bsz, seqlen, mem_len)
        o_s = _attn_sample(q[tp:].reshape(bs, 1, d), cache_k, cache_v, l)
        o = jnp.concatenate([o_p, o_s.reshape(bs, d).astype(BF16)], axis=0)
        x = _proj_res(o, xa_wo[l].astype(BF16), x, nw[5:6])
        x = _ffn(x, nw[6:8], ffn_w_in[l, 1].astype(BF16), ffn_w_out[l, 1].astype(BF16))

        xa_h, xa_dh = N_HEADS, d // N_HEADS
        outs['p_ret'].append(p_ret)
        outs['p_c'].append(p_c)
        outs['p_n'].append(p_n.reshape(bsz, N_HEADS, DK))
        outs['p_m'].append(p_m.reshape(bsz, N_HEADS))
        outs['p_sr'].append(p_sr.reshape(bsz, gn // S5_N, S5_N))
        outs['p_si'].append(p_si.reshape(bsz, gn // S5_N, S5_N))
        outs['p_hg'].append(p_hg)
        outs['p_k'].append(kv[:, :d].reshape(bsz, mem_len, xa_h, xa_dh))
        outs['p_v'].append(kv[:, d:].reshape(bsz, mem_len, xa_h, xa_dh))
        outs['s_ret'].append(s_ret[0])
        outs['s_c'].append(s_c[0])
        outs['s_n'].append(s_n[0])
        outs['s_m'].append(s_m[0])
        outs['s_sr'].append(s_sr.reshape(bs, gn // S5_N, S5_N))
        outs['s_si'].append(s_si.reshape(bs, gn // S5_N, S5_N))
        outs['s_hg'].append(s_hg[0])

    st = {k: jnp.stack(v) for k, v in outs.items()}
    return (x[:tp].reshape(bsz, seqlen, d), x[tp:].reshape(bs, 1, d),
            st['p_ret'], st['p_c'], st['p_n'], st['p_m'], st['p_sr'], st['p_si'], st['p_hg'], st['p_k'], st['p_v'],
            st['s_ret'], st['s_c'], st['s_n'], st['s_m'], st['s_sr'], st['s_si'], st['s_hg'])
```

```python
import functools
import math

import jax
import jax.numpy as jnp
from jax import lax
from jax.experimental import pallas as pl
from jax.experimental.pallas import tpu as pltpu

F32 = jnp.float32
BF16 = jnp.bfloat16

CHUNK = 128
N_BRANCH = 4
N_HEADS = 4
DK = 128
DV = 256
S5_GROUP = 16
S5_N = 64
ROPE_BASE = 10000.0
PAST_LEN = 16384

V7X_VMEM_BYTES = 64 * 1024 * 1024
MXU_K = 256
SUBLANES = 8
MIB = 1024 * 1024

OFF_RQ, OFF_RK, OFF_RV, OFF_RG = 0, 512, 1024, 2048
OFF_MQ, OFF_MK, OFF_MV, OFF_MO = 3072, 3584, 4096, 5120
OFF_SU = 6144
OFF_HQ, OFF_HF, OFF_HI, OFF_HG = 7168, 7680, 8192, 9216
OFF_GATES = 10240
N_MAIN = 18432
GATE_PAD = 128


def _cparams(sem, vmem_mib):
    return pltpu.CompilerParams(dimension_semantics=sem, vmem_limit_bytes=vmem_mib * MIB)


_ANY = pl.BlockSpec(memory_space=pl.ANY)


def _pcall(kernel_fn, inputs, in_specs, carried=None, **kw):
    carried = {o: a for o, a in (carried or {}).items() if a is not None}
    idxs = sorted(carried)
    n_in = len(inputs)

    def body(*refs):
        return kernel_fn(*refs[:n_in], *refs[n_in + len(idxs):])

    return pl.pallas_call(
        body if idxs else kernel_fn,
        in_specs=list(in_specs) + [_ANY] * len(idxs),
        input_output_aliases={n_in + j: o for j, o in enumerate(idxs)},
        **kw)(*inputs, *[carried[o] for o in idxs])


def _tile(n, cap, mult=16):
    best = None
    for t in range(mult, min(n, cap) + 1, mult):
        if n % t == 0:
            best = t
    return best if best is not None else n


def _dot(a, b):
    return jnp.dot(a, b, preferred_element_type=F32)


def _dot_nt(a, b):
    return lax.dot_general(a, b, (((1,), (1,)), ((), ())), preferred_element_type=F32)


def _rms(x, w, eps=1e-6):
    return x * lax.rsqrt(jnp.mean(x * x, axis=-1, keepdims=True) + eps) * w


def _sigmoid(x):
    return 1.0 / (1.0 + jnp.exp(-x))


def _silu(x):
    return x * _sigmoid(x)


def _log_sigmoid(x):
    return jnp.minimum(x, 0.0) - jnp.log1p(jnp.exp(-jnp.abs(x)))


def _logaddexp(a, b):
    return jnp.maximum(a, b) + jnp.log1p(jnp.exp(-jnp.abs(a - b)))


def _gelu_tanh(x):
    return 0.5 * x * (1.0 + jnp.tanh(math.sqrt(2.0 / math.pi) * (x + 0.044715 * (x * x * x))))


def _row_to_col(row):
    n = row.shape[1]
    r = lax.broadcasted_iota(jnp.int32, (n, n), 0)
    c = lax.broadcasted_iota(jnp.int32, (n, n), 1)
    return jnp.sum(jnp.where(r == c, row, 0.0), axis=1, keepdims=True)


def _cumsum_rows(x):
    c = x.shape[0]
    r = lax.broadcasted_iota(jnp.int32, (c, c), 0)
    k = lax.broadcasted_iota(jnp.int32, (c, c), 1)
    tri = jnp.where(r >= k, 1.0, 0.0).astype(BF16)
    hi = x.astype(BF16)
    r1 = x - hi.astype(F32)
    mid = r1.astype(BF16)
    lo = (r1 - mid.astype(F32)).astype(BF16)
    return _dot(tri, hi) + _dot(tri, mid) + _dot(tri, lo)


def _ffn_kernel(x_ref, nw_ref, wg_ref, wu_ref, wo_ref, o_ref, xn_ref, *, nf):
    f = pl.program_id(1)

    @pl.when(f == 0)
    def _():
        xn_ref[...] = _rms(x_ref[...], nw_ref[0:1, :]).astype(BF16)

    xn = xn_ref[...]
    g = _dot(xn, wg_ref[...])
    u = _dot(xn, wu_ref[...])
    part = _dot((_silu(g) * u).astype(BF16), wo_ref[...])

    @pl.when(f == 0)
    def _():
        o_ref[...] = part

    @pl.when(f > 0)
    def _():
        o_ref[...] += part

    @pl.when(f == nf - 1)
    def _():
        o_ref[...] = x_ref[...] + 0.5 * _rms(o_ref[...], nw_ref[1:2, :])


def _ffn(x, nw2, w_in, w_out):
    t, d = x.shape
    dff = w_out.shape[0]
    tm = _tile(t, 640)
    tf = _tile(dff, 512, 128)
    nf = dff // tf
    return pl.pallas_call(
        functools.partial(_ffn_kernel, nf=nf),
        grid=(t // tm, nf),
        in_specs=[pl.BlockSpec((tm, d), lambda i, f: (i, 0)),
                  pl.BlockSpec((2, d), lambda i, f: (0, 0)),
                  pl.BlockSpec((d, tf), lambda i, f: (0, f)),
                  pl.BlockSpec((d, tf), lambda i, f: (0, nf + f)),
                  pl.BlockSpec((tf, d), lambda i, f: (f, 0))],
        out_specs=pl.BlockSpec((tm, d), lambda i, f: (i, 0)),
        out_shape=jax.ShapeDtypeStruct((t, d), F32),
        scratch_shapes=[pltpu.VMEM((tm, d), BF16)],
        compiler_params=_cparams(("parallel", "arbitrary"), 56),
        name="ffn",
    )(x, nw2, w_in, w_in, w_out)


def _norm_matmul_kernel(x_ref, nw_ref, w_ref, o_ref, xn_ref):
    @pl.when(pl.program_id(1) == 0)
    def _():
        xn_ref[...] = _rms(x_ref[...], nw_ref[...]).astype(BF16)

    o_ref[...] = _dot(xn_ref[...], w_ref[...]).astype(o_ref.dtype)


def _norm_matmul(x, nw, w, name):
    t, d = x.shape
    n = w.shape[1]
    tm = _tile(t, 1040)
    tn = _tile(n, 1024, 128)
    return pl.pallas_call(
        _norm_matmul_kernel,
        grid=(t // tm, n // tn),
        in_specs=[pl.BlockSpec((tm, d), lambda i, j: (i, 0)),
                  pl.BlockSpec((1, d), lambda i, j: (0, 0)),
                  pl.BlockSpec((d, tn), lambda i, j: (0, j))],
        out_specs=pl.BlockSpec((tm, tn), lambda i, j: (i, j)),
        out_shape=jax.ShapeDtypeStruct((t, n), F32),
        scratch_shapes=[pltpu.VMEM((tm, d), BF16)],
        compiler_params=_cparams(("parallel", "arbitrary"), 48),
        name=name,
    )(x, nw, w)


def _mem_kv_kernel(x_ref, nw_ref, w_ref, o_ref, xn_ref):
    @pl.when(pl.program_id(1) == 0)
    def _():
        xn_ref[...] = _rms(x_ref[...], nw_ref[...]).astype(BF16)

    o_ref[0, 0] = _dot(xn_ref[...], w_ref[...])


def _mem_kv(mem, nw, w, layer, depth, prev):
    t, d = mem.shape
    tm = _tile(t, 512)
    tn = _tile(d, 1024, 128)
    npk = d // tn
    return _pcall(
        _mem_kv_kernel, (mem, nw, w),
        [pl.BlockSpec((tm, d), lambda i, j: (i, 0)),
         pl.BlockSpec((1, d), lambda i, j: (0, 0)),
         pl.BlockSpec((d, tn), lambda i, j: (0, j))],
        carried={0: prev},
        grid=(t // tm, 2 * npk),
        out_specs=pl.BlockSpec((1, 1, tm, tn), lambda i, j: (layer, j // npk, i, j % npk)),
        out_shape=jax.ShapeDtypeStruct((depth, 2, t, d), F32),
        scratch_shapes=[pltpu.VMEM((tm, d), BF16)],
        compiler_params=_cparams(("parallel", "arbitrary"), 48),
        name="mem_kv",
    )


def _in_proj_kernel(x_ref, nw_ref, w_ref, wg_ref, o_ref, og_ref, xn_ref):
    @pl.when(pl.program_id(1) == 0)
    def _():
        xn = _rms(x_ref[...], nw_ref[...]).astype(BF16)
        xn_ref[...] = xn
        og_ref[...] = _dot(xn, wg_ref[...])

    o_ref[...] = _dot(xn_ref[...], w_ref[...])


def _in_proj(x, nw, w_main, w_gate):
    t, d = x.shape
    n = w_main.shape[1]
    tm = _tile(t, 1040)
    tn = _tile(n, 1024, 128)
    return pl.pallas_call(
        _in_proj_kernel,
        grid=(t // tm, n // tn),
        in_specs=[pl.BlockSpec((tm, d), lambda i, j: (i, 0)),
                  pl.BlockSpec((1, d), lambda i, j: (0, 0)),
                  pl.BlockSpec((d, tn), lambda i, j: (0, j)),
                  pl.BlockSpec((d, GATE_PAD), lambda i, j: (0, 0))],
        out_specs=[pl.BlockSpec((tm, tn), lambda i, j: (i, j)),
                   pl.BlockSpec((tm, GATE_PAD), lambda i, j: (i, 0))],
        out_shape=[jax.ShapeDtypeStruct((t, n), F32), jax.ShapeDtypeStruct((t, GATE_PAD), F32)],
        scratch_shapes=[pltpu.VMEM((tm, d), BF16)],
        compiler_params=_cparams(("parallel", "arbitrary"), 48),
        name="in_proj",
    )(x, nw, w_main, w_gate)


def _merge_kernel(h0_ref, h1_ref, h2_ref, h3_ref, g_ref, w_ref, x_ref, nw_ref, o_ref, acc_ref, mb_ref):
    n = pl.program_id(1)
    half = mb_ref.shape[1] // 2
    for k, h_ref in enumerate((h0_ref, h1_ref, h2_ref, h3_ref)):
        @pl.when(n == k)
        def _(k=k, h_ref=h_ref):
            part = _sigmoid(g_ref[...]) * _dot(h_ref[...], w_ref[0])
            if k == 0:
                acc_ref[...] = part
            else:
                acc_ref[...] += part

    @pl.when(n == N_BRANCH)
    def _():
        mb = acc_ref[...].astype(BF16)
        mb_ref[...] = mb
        o_ref[...] = _dot(mb[:, :half], w_ref[0])

    @pl.when(n == N_BRANCH + 1)
    def _():
        out = o_ref[...] + _dot(mb_ref[:, half:], w_ref[0])
        o_ref[...] = x_ref[...] + _rms(out, nw_ref[...])


def _merge(hs, u, w6, x, nw):
    t, d = x.shape
    bw = hs[0].shape[1]
    tm = _tile(t, 416)
    gate_blk = OFF_GATES // d
    return pl.pallas_call(
        _merge_kernel,
        grid=(t // tm, N_BRANCH + 2),
        in_specs=[pl.BlockSpec((tm, bw), lambda i, n: (i, 0))] * N_BRANCH + [
            pl.BlockSpec((tm, d), lambda i, n: (i, gate_blk + jnp.minimum(n, N_BRANCH - 1))),
            pl.BlockSpec((1, bw, d), lambda i, n: (n, 0, 0)),
            pl.BlockSpec((tm, d), lambda i, n: (i, 0)),
            pl.BlockSpec((1, d), lambda i, n: (0, 0))],
        out_specs=pl.BlockSpec((tm, d), lambda i, n: (i, 0)),
        out_shape=jax.ShapeDtypeStruct((t, d), F32),
        scratch_shapes=[pltpu.VMEM((tm, d), F32), pltpu.VMEM((tm, d), BF16)],
        compiler_params=_cparams(("parallel", "arbitrary"), 56),
        name="merge",
    )(*hs, u, w6, x, nw)


def _proj_res_kernel(a_ref, w_ref, x_ref, nw_ref, o_ref):
    o_ref[...] = x_ref[...] + _rms(_dot(a_ref[...], w_ref[...]), nw_ref[...])


def _proj_res(a, w, x, nw):
    t, d = x.shape
    tm = _tile(t, 640)
    return pl.pallas_call(
        _proj_res_kernel,
        grid=(t // tm,),
        in_specs=[pl.BlockSpec((tm, d), lambda i: (i, 0)),
                  pl.BlockSpec((d, d), lambda i: (0, 0)),
                  pl.BlockSpec((tm, d), lambda i: (i, 0)),
                  pl.BlockSpec((1, d), lambda i: (0, 0))],
        out_specs=pl.BlockSpec((tm, d), lambda i: (i, 0)),
        out_shape=jax.ShapeDtypeStruct((t, d), F32),
        compiler_params=_cparams(("parallel",), 56),
        name="xa_out",
    )(a, w, x, nw)


def _rope(x, cs, sn):
    return x * cs + pltpu.roll(x, x.shape[1] // 2, 1) * sn


def _group_norm(o, eps=1e-5):
    oc = o - jnp.mean(o, axis=-1, keepdims=True)
    return oc * lax.rsqrt(jnp.mean(oc * oc, axis=-1, keepdims=True) + eps)


def _ret_kernel(q_ref, k_ref, v_ref, g_ref, cos_ref, sin_ref, dm_ref, dec_ref, h_ref, st_ref, s_ref, *, nc):
    c = pl.program_id(2)

    @pl.when(c == 0)
    def _():
        s_ref[...] = jnp.zeros_like(s_ref)

    cs, sn = cos_ref[...], sin_ref[...]
    qr = _rope(q_ref[...], cs, sn)
    kr = _rope(k_ref[...], cs, sn) * (DK ** -0.5)
    dec = dec_ref[0]
    e1, e2, e3 = dec[:, 0:1], dec[:, 1:2], dec[0:1, 2:3]
    vb = v_ref[...].astype(BF16)
    qb = qr.astype(BF16)
    s = s_ref[...]
    sc = _dot_nt(qb, kr.astype(BF16)) * dm_ref[0]
    o = _dot(sc.astype(BF16), vb) + _dot(qb, s.astype(BF16)) * e1
    s_new = e3 * s + _dot((kr * e2).T.astype(BF16), vb)
    s_ref[...] = s_new
    h_ref[...] = (_group_norm(o) * _silu(g_ref[...])).astype(h_ref.dtype)

    @pl.when(c == nc - 1)
    def _():
        st_ref[...] = s_new.reshape(st_ref.shape)


def _layer_state_spec(layer, tail):
    return pl.BlockSpec((1, 1, 1) + tail, lambda b, h, c: (layer, b, h) + (0,) * len(tail))


def _ret_prompt(u, bsz, seqlen, cos2, sin2, dmat, dec, layer, depth, prev):
    nc = seqlen // CHUNK
    row = lambda b, h, c: b * nc + c
    return _pcall(
        functools.partial(_ret_kernel, nc=nc), (u, u, u, u, cos2, sin2, dmat, dec),
        [pl.BlockSpec((CHUNK, DK), lambda b, h, c: (row(b, h, c), OFF_RQ // DK + h)),
                  pl.BlockSpec((CHUNK, DK), lambda b, h, c: (row(b, h, c), OFF_RK // DK + h)),
                  pl.BlockSpec((CHUNK, DV), lambda b, h, c: (row(b, h, c), OFF_RV // DV + h)),
                  pl.BlockSpec((CHUNK, DV), lambda b, h, c: (row(b, h, c), OFF_RG // DV + h)),
                  pl.BlockSpec((CHUNK, DK), lambda b, h, c: (c, 0)),
                  pl.BlockSpec((CHUNK, DK), lambda b, h, c: (c, 0)),
                  pl.BlockSpec((1, CHUNK, CHUNK), lambda b, h, c: (h, 0, 0)),
                  pl.BlockSpec((1, CHUNK, 8), lambda b, h, c: (h, 0, 0))],
        carried={1: prev},
        grid=(bsz, N_HEADS, nc),
        out_specs=[pl.BlockSpec((CHUNK, DV), lambda b, h, c: (row(b, h, c), h)),
                   _layer_state_spec(layer, (DK, DV))],
        out_shape=[jax.ShapeDtypeStruct((u.shape[0], N_HEADS * DV), BF16),
                   jax.ShapeDtypeStruct((depth, bsz, N_HEADS, DK, DV), F32)],
        scratch_shapes=[pltpu.VMEM((DK, DV), F32)],
        compiler_params=_cparams(("parallel", "parallel", "arbitrary"), 32),
        name="ret_prompt",
    )


def _mlstm_kernel(gb_ref, q_ref, k_ref, v_ref, mo_ref, ug_ref, ugt_ref, nw_ref,
                  h_ref, c_out, n_out, m_out, cm_ref, nv_ref, m_ref, *, nc):
    h = pl.program_id(1)
    c = pl.program_id(2)

    @pl.when(c == 0)
    def _():
        cm_ref[...] = jnp.zeros_like(cm_ref)
        nv_ref[...] = jnp.zeros_like(nv_ref)
        m_ref[...] = jnp.zeros_like(m_ref)

    n = CHUNK
    row = lax.broadcasted_iota(jnp.int32, (n, n), 0)
    col = lax.broadcasted_iota(jnp.int32, (n, n), 1)
    causal = row >= col
    bias_i = gb_ref[0, h]
    bias_f = gb_ref[1, h]
    ig_row = ugt_ref[pl.ds(h, 1), :] + bias_i
    lf_row = _log_sigmoid(ugt_ref[pl.ds(N_HEADS + h, 1), :] + bias_f)
    ug = ug_ref[...]
    lane = lax.broadcasted_iota(jnp.int32, ug.shape, 1)
    ig_col = jnp.sum(jnp.where(lane == h, ug, 0.0), axis=1, keepdims=True) + bias_i
    lf_col = _log_sigmoid(jnp.sum(jnp.where(lane == N_HEADS + h, ug, 0.0), axis=1, keepdims=True) + bias_f)
    b_col = jnp.sum(jnp.where(causal, lf_row, 0.0), axis=1, keepdims=True)
    b_row = jnp.sum(jnp.where(row <= col, lf_col, 0.0), axis=0, keepdims=True)
    b_last = b_col[n - 1:n, :]
    m = m_ref[...]
    logd = jnp.where(causal, b_col - b_row + ig_row, -jnp.inf)
    inter = m + b_col
    mi = jnp.maximum(inter, jnp.max(logd, axis=1, keepdims=True))
    w = jnp.exp(logd - mi)
    wi = jnp.exp(inter - mi)
    q = q_ref[...]
    k = k_ref[...] * (DK ** -0.5)
    qb = q.astype(BF16)
    vb = v_ref[...].astype(BF16)
    cm = cm_ref[...]
    nv = nv_ref[...]
    a = _dot_nt(qb, k.astype(BF16)) * w
    num = _dot(a.astype(BF16), vb) + wi * _dot(qb, cm.astype(BF16))
    nq = jnp.sum(a, axis=1, keepdims=True) + wi * jnp.sum(q * nv, axis=1, keepdims=True)
    hh = num / jnp.maximum(jnp.abs(nq), jnp.exp(-mi))
    m_new = mi[n - 1:n, :]
    wl = jnp.exp(b_last - b_col + ig_col - m_new)
    dp = jnp.exp(m + b_last - m_new)
    kw = k * wl
    cm_new = dp * cm + _dot(kw.T.astype(BF16), vb)
    nv_new = dp * nv + jnp.sum(kw, axis=0, keepdims=True)
    cm_ref[...] = cm_new
    nv_ref[...] = nv_new
    m_ref[...] = m_new
    h_ref[...] = (_rms(hh, nw_ref[...]) * _sigmoid(mo_ref[...])).astype(h_ref.dtype)

    @pl.when(c == nc - 1)
    def _():
        c_out[...] = cm_new.reshape(c_out.shape)
        n_out[...] = nv_new.reshape(n_out.shape)
        m_out[...] = m_new.reshape(m_out.shape)


def _mlstm_prompt(u, ug, ugt, gate_bias, norm_w, bsz, seqlen, layer, depth, prev):
    nc = seqlen // CHUNK
    row = lambda b, h, c: b * nc + c
    prev = prev or (None, None, None)
    return _pcall(
        functools.partial(_mlstm_kernel, nc=nc), (gate_bias, u, u, u, u, ug, ugt, norm_w),
        [pl.BlockSpec(memory_space=pltpu.SMEM),
                  pl.BlockSpec((CHUNK, DK), lambda b, h, c: (row(b, h, c), OFF_MQ // DK + h)),
                  pl.BlockSpec((CHUNK, DK), lambda b, h, c: (row(b, h, c), OFF_MK // DK + h)),
                  pl.BlockSpec((CHUNK, DV), lambda b, h, c: (row(b, h, c), OFF_MV // DV + h)),
                  pl.BlockSpec((CHUNK, DV), lambda b, h, c: (row(b, h, c), OFF_MO // DV + h)),
                  pl.BlockSpec((CHUNK, GATE_PAD), lambda b, h, c: (row(b, h, c), 0)),
                  pl.BlockSpec((2 * N_HEADS, CHUNK), lambda b, h, c: (0, row(b, h, c))),
                  pl.BlockSpec((1, DV), lambda b, h, c: (0, h))],
        carried={1: prev[0], 2: prev[1], 3: prev[2]},
        grid=(bsz, N_HEADS, nc),
        out_specs=[pl.BlockSpec((CHUNK, DV), lambda b, h, c: (row(b, h, c), h)),
                   _layer_state_spec(layer, (DK, DV)),
                   _layer_state_spec(layer, (1, DK)),
                   _layer_state_spec(layer, (1, 1))],
        out_shape=[jax.ShapeDtypeStruct((u.shape[0], N_HEADS * DV), BF16),
                   jax.ShapeDtypeStruct((depth, bsz, N_HEADS, DK, DV), F32),
                   jax.ShapeDtypeStruct((depth, bsz, N_HEADS, 1, DK), F32),
                   jax.ShapeDtypeStruct((depth, bsz, N_HEADS, 1, 1), F32)],
        scratch_shapes=[pltpu.VMEM((DK, DV), F32), pltpu.VMEM((1, DK), F32), pltpu.VMEM((1, 1), F32)],
        compiler_params=_cparams(("parallel", "parallel", "arbitrary"), 32),
        name="mlstm_prompt",
    )


def _hgrn_lower_bound(logits, layer):
    e = jnp.exp(logits - jnp.max(logits, axis=0, keepdims=True))
    p = e / jnp.sum(e, axis=0, keepdims=True)
    lb = jnp.zeros_like(p[0:1, :])
    for r in range(1, layer + 1):
        lb = lb + p[r:r + 1, :]
    return lb


def _hgrn_gates(fpre, lb):
    hlf = _logaddexp(jnp.log(lb), jnp.log1p(-lb) + _log_sigmoid(fpre))
    hk = (1.0 - lb) * _sigmoid(-fpre)
    return hlf, hk


def _hgrn_kernel(lg_ref, q_ref, f_ref, i_ref, g_ref, nw_ref, h_ref, st_ref, s_ref, *, nc, layer):
    c = pl.program_id(2)

    @pl.when(c == 0)
    def _():
        s_ref[...] = jnp.zeros_like(s_ref)

    n = CHUNK
    lb = _hgrn_lower_bound(lg_ref[...], layer)
    hlf, hk = _hgrn_gates(f_ref[...], lb)
    q = _silu(q_ref[...])
    ib = i_ref[...].astype(BF16)
    bc = _cumsum_rows(hlf)
    r2 = lax.broadcasted_iota(jnp.int32, (n, n), 0)
    c2 = lax.broadcasted_iota(jnp.int32, (n, n), 1)
    rowv = lax.broadcasted_iota(jnp.int32, (n, DK), 0)
    a = jnp.where(r2 == c2, _dot_nt(q.astype(BF16), hk.astype(BF16)), 0.0)
    p = bc
    s = 1
    lev = 0
    while s < n:
        right = (rowv & s) != 0
        nxt = pltpu.roll(p, n - s, 0)
        e = jnp.exp(jnp.where(right, bc - p, nxt - bc))
        qs = jnp.where(right, q * e, 0.0).astype(BF16)
        ks = jnp.where(right, 0.0, hk * e).astype(BF16)
        a = a + jnp.where((r2 >> (lev + 1)) == (c2 >> (lev + 1)), _dot_nt(qs, ks), 0.0)
        if 2 * s < n:
            p = jnp.where(right, pltpu.roll(p, s, 0), p)
        s *= 2
        lev += 1
    st = s_ref[...]
    o = _dot(a.astype(BF16), ib) + _dot((q * jnp.exp(bc)).astype(BF16), st.astype(BF16))
    bl = bc[n - 1:n, :]
    s_new = _row_to_col(jnp.exp(bl)) * st + _dot((hk * jnp.exp(bl - bc)).T.astype(BF16), ib)
    s_ref[...] = s_new
    h_ref[...] = (_rms(o, nw_ref[...]) * _silu(g_ref[...])).astype(h_ref.dtype)

    @pl.when(c == nc - 1)
    def _():
        st_ref[...] = s_new.reshape(st_ref.shape)


def _hgrn_prompt(u, logits, norm_w, bsz, seqlen, layer, prev):
    nc = seqlen // CHUNK
    depth = logits.shape[0]
    row = lambda b, h, c: b * nc + c
    return _pcall(
        functools.partial(_hgrn_kernel, nc=nc, layer=layer), (logits, u, u, u, u, norm_w),
        [pl.BlockSpec((depth, DK), lambda b, h, c: (0, h)),
                  pl.BlockSpec((CHUNK, DK), lambda b, h, c: (row(b, h, c), OFF_HQ // DK + h)),
                  pl.BlockSpec((CHUNK, DK), lambda b, h, c: (row(b, h, c), OFF_HF // DK + h)),
                  pl.BlockSpec((CHUNK, DV), lambda b, h, c: (row(b, h, c), OFF_HI // DV + h)),
                  pl.BlockSpec((CHUNK, DV), lambda b, h, c: (row(b, h, c), OFF_HG // DV + h)),
                  pl.BlockSpec((1, DV), lambda b, h, c: (0, h))],
        carried={1: prev},
        grid=(bsz, N_HEADS, nc),
        out_specs=[pl.BlockSpec((CHUNK, DV), lambda b, h, c: (row(b, h, c), h)),
                   _layer_state_spec(layer, (DK, DV))],
        out_shape=[jax.ShapeDtypeStruct((u.shape[0], N_HEADS * DV), BF16),
                   jax.ShapeDtypeStruct((depth, bsz, N_HEADS, DK, DV), F32)],
        scratch_shapes=[pltpu.VMEM((DK, DV), F32)],
        compiler_params=_cparams(("parallel", "parallel", "arbitrary"), 32),
        name="hgrn_prompt",
    )


def _s5_prep_kernel(lr_ref, li_ref, dt_ref, bre_ref, bim_ref, ar_ref, ai_ref, bbr_ref, bbi_ref):
    lr, li, dt = lr_ref[...], li_ref[...], dt_ref[...]
    mag = jnp.exp(lr * dt)
    ar = mag * jnp.cos(li * dt)
    ai = mag * jnp.sin(li * dt)
    den = lr * lr + li * li
    cr = ((ar - 1.0) * lr + ai * li) / den
    ci = (ai * lr - (ar - 1.0) * li) / den
    ar_ref[...] = ar
    ai_ref[...] = ai
    bbr_ref[...] = cr * bre_ref[...] - ci * bim_ref[...]
    bbi_ref[...] = cr * bim_ref[...] + ci * bre_ref[...]


def _s5_prep(lam_re, lam_im, log_step, b_re, b_im):
    g, n = lam_re.shape
    p = b_re.shape[-1]
    gn = g * n
    dt = jnp.repeat(jnp.exp(log_step.astype(F32)), n).reshape(gn, 1)
    col = lambda a: a.astype(F32).reshape(gn, 1)
    return pl.pallas_call(
        _s5_prep_kernel,
        out_shape=[jax.ShapeDtypeStruct((gn, 1), F32), jax.ShapeDtypeStruct((gn, 1), F32),
                   jax.ShapeDtypeStruct((gn, p), F32), jax.ShapeDtypeStruct((gn, p), F32)],
        name="s5_prep",
    )(col(lam_re), col(lam_im), dt, b_re.reshape(gn, p), b_im.reshape(gn, p))


def _s5_kernel(*refs, seq, nc, n_kt, sw):
    if seq:
        (u_ref, ar_ref, ai_ref, bblk_ref, cblk_ref, d_ref, wglu_ref,
         h_ref, sr_ref, si_ref, hr_scr, hi_scr) = refs
    else:
        (u_ref, ar_ref, ai_ref, bblk_ref, cblk_ref, d_ref, wglu_ref, h0r_ref, h0i_ref,
         h_ref, sr_ref, si_ref) = refs
    u = u_ref[...]
    ub = u.astype(BF16)
    rows = u.shape[0]
    if seq:
        c = pl.program_id(1)

        @pl.when(c == 0)
        def _():
            hr_scr[...] = jnp.zeros_like(hr_scr)
            hi_scr[...] = jnp.zeros_like(hi_scr)

    ys = []
    for kt in range(n_kt):
        lanes = slice(kt * sw, (kt + 1) * sw)
        bu = _dot(ub[:, kt * MXU_K:(kt + 1) * MXU_K], bblk_ref[kt])
        xr, xi = bu[:, :sw], bu[:, sw:]
        ar, ai = ar_ref[:, lanes], ai_ref[:, lanes]
        if seq:
            sub = lax.broadcasted_iota(jnp.int32, (SUBLANES, sw), 0)
            mr, mi = jnp.broadcast_to(ar, (SUBLANES, sw)), jnp.broadcast_to(ai, (SUBLANES, sw))
            tabr, tabi = mr, mi
            levels = []
            sft = 1
            while sft < SUBLANES:
                keep = sub >= sft
                levels.append((sft, jnp.where(keep, mr, 0.0), jnp.where(keep, mi, 0.0)))
                tr, ti = pltpu.roll(tabr, sft, 0), pltpu.roll(tabi, sft, 0)
                tabr, tabi = (jnp.where(keep, tabr * tr - tabi * ti, tabr),
                              jnp.where(keep, tabr * ti + tabi * tr, tabi))
                mr, mi = mr * mr - mi * mi, 2.0 * mr * mi
                sft *= 2
            cr, ci = hr_scr[:, lanes], hi_scr[:, lanes]
            slabs_r, slabs_i = [], []
            for j in range(rows // SUBLANES):
                rs = slice(j * SUBLANES, (j + 1) * SUBLANES)
                sr_, si_ = xr[rs, :], xi[rs, :]
                for sft, lr, li in levels:
                    qr, qi = pltpu.roll(sr_, sft, 0), pltpu.roll(si_, sft, 0)
                    sr_, si_ = sr_ + lr * qr - li * qi, si_ + lr * qi + li * qr
                crb, cib = jnp.broadcast_to(cr, (SUBLANES, sw)), jnp.broadcast_to(ci, (SUBLANES, sw))
                sr_, si_ = sr_ + tabr * crb - tabi * cib, si_ + tabr * cib + tabi * crb
                cr, ci = sr_[SUBLANES - 1:SUBLANES, :], si_[SUBLANES - 1:SUBLANES, :]
                slabs_r.append(sr_)
                slabs_i.append(si_)
            xr, xi = jnp.concatenate(slabs_r, axis=0), jnp.concatenate(slabs_i, axis=0)
            hr_scr[:, lanes] = cr
            hi_scr[:, lanes] = ci
        else:
            h0r, h0i = h0r_ref[0, :, lanes], h0i_ref[0, :, lanes]
            xr, xi = xr + ar * h0r - ai * h0i, xi + ar * h0i + ai * h0r
            sr_ref[0, :, lanes] = xr
            si_ref[0, :, lanes] = xi
        ys.append(_dot(jnp.concatenate([xr, xi], axis=1).astype(BF16), cblk_ref[kt]))
    y = jnp.concatenate(ys, axis=1) + d_ref[...] * u
    z = _gelu_tanh(y)
    h_ref[...] = (z * _sigmoid(_dot(z.astype(BF16), wglu_ref[...]))).astype(h_ref.dtype)
    if seq:
        @pl.when(c == nc - 1)
        def _():
            sr_ref[...] = hr_scr[...].reshape(sr_ref.shape)
            si_ref[...] = hi_scr[...].reshape(si_ref.shape)


def _s5_weights(bbr, bbi, c_re, c_im):
    gn, p = bbr.shape
    g = gn // S5_N
    gpt = MXU_K // p
    n_kt = g // gpt
    eye = jnp.eye(gpt, dtype=F32)

    def b_blk(bb):
        return jnp.einsum('kgnp,gh->kgphn', bb.reshape(n_kt, gpt, S5_N, p), eye).reshape(n_kt, gpt * p, gpt * S5_N)

    def c_blk(cc):
        return jnp.einsum('kgpn,gh->kgnhp', cc.astype(F32).reshape(n_kt, gpt, p, S5_N), eye).reshape(
            n_kt, gpt * S5_N, gpt * p)

    bblk = jnp.concatenate([b_blk(bbr), b_blk(bbi)], axis=2).astype(BF16)
    cblk = jnp.concatenate([c_blk(c_re), -c_blk(c_im)], axis=1).astype(BF16)
    return bblk, cblk


def _s5_prompt(u, ar, ai, bblk, cblk, d, wglu, bsz, seqlen, layer, depth, prev):
    width = wglu.shape[0]
    n_kt = bblk.shape[0]
    sw = bblk.shape[2] // 2
    ct = _tile(seqlen, 256)
    nc = seqlen // ct
    gn = ar.shape[1]
    prev = prev or (None, None)
    full = lambda shape: pl.BlockSpec(shape, lambda b, c: (0,) * len(shape))
    st_spec = pl.BlockSpec((1, 1, 1, gn), lambda b, c: (layer, b, 0, 0))
    st_shape = jax.ShapeDtypeStruct((depth, bsz, 1, gn), F32)
    return _pcall(
        functools.partial(_s5_kernel, seq=True, nc=nc, n_kt=n_kt, sw=sw), (u, ar, ai, bblk, cblk, d, wglu),
        [pl.BlockSpec((ct, width), lambda b, c: (b * nc + c, OFF_SU // width)),
         full((1, gn)), full((1, gn)), full(bblk.shape), full(cblk.shape),
         full((1, width)), full(wglu.shape)],
        carried={1: prev[0], 2: prev[1]},
        grid=(bsz, nc),
        out_specs=[pl.BlockSpec((ct, width), lambda b, c: (b * nc + c, 0)), st_spec, st_spec],
        out_shape=[jax.ShapeDtypeStruct((u.shape[0], width), BF16), st_shape, st_shape],
        scratch_shapes=[pltpu.VMEM((1, gn), F32), pltpu.VMEM((1, gn), F32)],
        compiler_params=_cparams(("parallel", "arbitrary"), 56),
        name="s5_prompt",
    )


def _s5_sample(u, tp, ar, ai, bblk, cblk, d, wglu, h0r, h0i, layer, h_buf, prev):
    width = wglu.shape[0]
    n_kt = bblk.shape[0]
    sw = bblk.shape[2] // 2
    depth, bs, gn = h0r.shape
    tb = _tile(bs, 128, 8)
    r0 = tp // tb
    prev = prev or (None, None)
    full = lambda shape: pl.BlockSpec(shape, lambda i: (0,) * len(shape))
    st_spec = pl.BlockSpec((1, tb, gn), lambda i: (layer, i, 0))
    st_shape = jax.ShapeDtypeStruct((depth, bs, gn), F32)
    return _pcall(
        functools.partial(_s5_kernel, seq=False, nc=1, n_kt=n_kt, sw=sw),
        (u, ar, ai, bblk, cblk, d, wglu, h0r, h0i),
        [pl.BlockSpec((tb, width), lambda i: (r0 + i, OFF_SU // width)),
         full((1, gn)), full((1, gn)), full(bblk.shape), full(cblk.shape),
         full((1, width)), full(wglu.shape), st_spec, st_spec],
        carried={0: h_buf, 1: prev[0], 2: prev[1]},
        grid=(bs // tb,),
        out_specs=[pl.BlockSpec((tb, width), lambda i: (r0 + i, 0)), st_spec, st_spec],
        out_shape=[jax.ShapeDtypeStruct(h_buf.shape, BF16), st_shape, st_shape],
        compiler_params=_cparams(("parallel",), 56),
        name="s5_sample",
    )


SAMPLE_BLOCK = 8


def _state_step(s, dcol, kcol, vrow, qcol):
    s_new = dcol * s + kcol * vrow
    return s_new, jnp.sum(qcol * s_new, axis=0, keepdims=True)


def _ret_step_kernel(gam_ref, q_ref, k_ref, v_ref, g_ref, cos_ref, sin_ref, st_ref, h_ref, so_ref, o_scr):
    cs, sn = cos_ref[...], sin_ref[...]
    for h in range(N_HEADS):
        qr = _rope(q_ref[:, h * DK:(h + 1) * DK], cs, sn)
        kr = _rope(k_ref[:, h * DK:(h + 1) * DK], cs, sn) * (DK ** -0.5)
        for i in range(SAMPLE_BLOCK):
            s_new, o = _state_step(st_ref[0, i, h], gam_ref[h], _row_to_col(kr[i:i + 1, :]),
                                   v_ref[i:i + 1, h * DV:(h + 1) * DV], _row_to_col(qr[i:i + 1, :]))
            so_ref[0, i, h] = s_new
            o_scr[i:i + 1, h * DV:(h + 1) * DV] = o
    for h in range(N_HEADS):
        cols = slice(h * DV, (h + 1) * DV)
        h_ref[:, cols] = (_group_norm(o_scr[:, cols]) * _silu(g_ref[:, cols])).astype(h_ref.dtype)


def _mlstm_step_kernel(gb_ref, q_ref, k_ref, v_ref, mo_ref, ug_ref, nw_ref, c_ref, n_ref, m_ref,
                       h_ref, co_ref, no_ref, mo_out_ref, o_scr):
    ug = ug_ref[...]
    m_all = m_ref[0]
    m_new_cols = []
    for h in range(N_HEADS):
        ig = ug[:, h:h + 1] + gb_ref[0, h]
        lf = _log_sigmoid(ug[:, N_HEADS + h:N_HEADS + h + 1] + gb_ref[1, h])
        m_old = m_all[:, h:h + 1]
        inter = m_old + lf
        mi = jnp.maximum(inter, ig)
        w = jnp.exp(ig - mi)
        wi = jnp.exp(inter - mi)
        m_new_cols.append(mi)
        lim = jnp.exp(-mi)
        qh = q_ref[:, h * DK:(h + 1) * DK]
        kh = k_ref[:, h * DK:(h + 1) * DK] * (DK ** -0.5)
        for i in range(SAMPLE_BLOCK):
            wk = w[i:i + 1, :] * kh[i:i + 1, :]
            c_new, num = _state_step(c_ref[0, i, h], wi[i:i + 1, :], _row_to_col(wk),
                                     v_ref[i:i + 1, h * DV:(h + 1) * DV], _row_to_col(qh[i:i + 1, :]))
            n_new = wi[i:i + 1, :] * n_ref[0, i, h:h + 1, :] + wk
            nq = jnp.sum(qh[i:i + 1, :] * n_new, axis=1, keepdims=True)
            co_ref[0, i, h] = c_new
            no_ref[0, i, h:h + 1, :] = n_new
            o_scr[i:i + 1, h * DV:(h + 1) * DV] = num / jnp.maximum(jnp.abs(nq), lim[i:i + 1, :])
    mo_out_ref[0] = jnp.concatenate(m_new_cols, axis=1)
    for h in range(N_HEADS):
        cols = slice(h * DV, (h + 1) * DV)
        h_ref[:, cols] = (_rms(o_scr[:, cols], nw_ref[:, cols]) * _sigmoid(mo_ref[:, cols])).astype(h_ref.dtype)


def _hgrn_step_kernel(lg_ref, q_ref, f_ref, i_ref, g_ref, nw_ref, st_ref, h_ref, so_ref, o_scr, *, layer):
    lb_all = _hgrn_lower_bound(lg_ref[...], layer)
    for h in range(N_HEADS):
        hlf, hk = _hgrn_gates(f_ref[:, h * DK:(h + 1) * DK], lb_all[:, h * DK:(h + 1) * DK])
        dec = jnp.exp(hlf)
        qh = _silu(q_ref[:, h * DK:(h + 1) * DK])
        for i in range(SAMPLE_BLOCK):
            s_new, o = _state_step(st_ref[0, i, h], _row_to_col(dec[i:i + 1, :]), _row_to_col(hk[i:i + 1, :]),
                                   i_ref[i:i + 1, h * DV:(h + 1) * DV], _row_to_col(qh[i:i + 1, :]))
            so_ref[0, i, h] = s_new
            o_scr[i:i + 1, h * DV:(h + 1) * DV] = o
    for h in range(N_HEADS):
        cols = slice(h * DV, (h + 1) * DV)
        h_ref[:, cols] = (_rms(o_scr[:, cols], nw_ref[:, cols]) * _silu(g_ref[:, cols])).astype(h_ref.dtype)


def _seg(width, off, tp):
    return pl.BlockSpec((SAMPLE_BLOCK, width), lambda i: (tp // SAMPLE_BLOCK + i, off // width))


def _state_spec(layer, tail):
    return pl.BlockSpec((1, SAMPLE_BLOCK) + tail, lambda i: (layer, i) + (0,) * len(tail))


_MAT = (N_HEADS, DK, DV)
_SMEM = pl.BlockSpec(memory_space=pltpu.SMEM)


def _whole(shape):
    return pl.BlockSpec(shape, lambda i: (0,) * len(shape))


def _h_rows_spec(tp):
    return pl.BlockSpec((SAMPLE_BLOCK, N_HEADS * DV), lambda i: (tp // SAMPLE_BLOCK + i, 0))


def _ret_sample(u, tp, gamma, cos2, sin2, state, layer, h_buf, prev):
    bs = state.shape[1]
    hw, kw = N_HEADS * DV, N_HEADS * DK
    return _pcall(
        _ret_step_kernel, (gamma, u, u, u, u, cos2, sin2, state),
        [_SMEM, _seg(kw, OFF_RQ, tp), _seg(kw, OFF_RK, tp), _seg(hw, OFF_RV, tp), _seg(hw, OFF_RG, tp),
         _whole((1, DK)), _whole((1, DK)), _state_spec(layer, _MAT)],
        carried={0: h_buf, 1: prev},
        grid=(bs // SAMPLE_BLOCK,),
        out_specs=[_h_rows_spec(tp), _state_spec(layer, _MAT)],
        out_shape=[jax.ShapeDtypeStruct(h_buf.shape, BF16), jax.ShapeDtypeStruct(state.shape, F32)],
        scratch_shapes=[pltpu.VMEM((SAMPLE_BLOCK, hw), F32)],
        compiler_params=_cparams(("parallel",), 40),
        name="ret_sample",
    )


def _mlstm_sample(u, ug, tp, gate_bias, norm_w, st_c, st_n, st_m, layer, h_buf, prev):
    bs = st_c.shape[1]
    hw, kw = N_HEADS * DV, N_HEADS * DK
    prev = prev or (None, None, None)
    return _pcall(
        _mlstm_step_kernel, (gate_bias, u, u, u, u, ug, norm_w, st_c, st_n, st_m),
        [_SMEM, _seg(kw, OFF_MQ, tp), _seg(kw, OFF_MK, tp), _seg(hw, OFF_MV, tp), _seg(hw, OFF_MO, tp),
         pl.BlockSpec((SAMPLE_BLOCK, GATE_PAD), lambda i: (tp // SAMPLE_BLOCK + i, 0)), _whole((1, hw)),
         _state_spec(layer, _MAT), _state_spec(layer, (N_HEADS, DK)), _state_spec(layer, (N_HEADS,))],
        carried={0: h_buf, 1: prev[0], 2: prev[1], 3: prev[2]},
        grid=(bs // SAMPLE_BLOCK,),
        out_specs=[_h_rows_spec(tp), _state_spec(layer, _MAT),
                   _state_spec(layer, (N_HEADS, DK)), _state_spec(layer, (N_HEADS,))],
        out_shape=[jax.ShapeDtypeStruct(h_buf.shape, BF16), jax.ShapeDtypeStruct(st_c.shape, F32),
                   jax.ShapeDtypeStruct(st_n.shape, F32), jax.ShapeDtypeStruct(st_m.shape, F32)],
        scratch_shapes=[pltpu.VMEM((SAMPLE_BLOCK, hw), F32)],
        compiler_params=_cparams(("parallel",), 40),
        name="mlstm_sample",
    )


def _hgrn_sample(u, tp, logits, norm_w, state, layer, h_buf, prev):
    bs = state.shape[1]
    hw, kw = N_HEADS * DV, N_HEADS * DK
    return _pcall(
        functools.partial(_hgrn_step_kernel, layer=layer), (logits, u, u, u, u, norm_w, state),
        [_whole(logits.shape), _seg(kw, OFF_HQ, tp), _seg(kw, OFF_HF, tp), _seg(hw, OFF_HI, tp),
         _seg(hw, OFF_HG, tp), _whole((1, hw)), _state_spec(layer, _MAT)],
        carried={0: h_buf, 1: prev},
        grid=(bs // SAMPLE_BLOCK,),
        out_specs=[_h_rows_spec(tp), _state_spec(layer, _MAT)],
        out_shape=[jax.ShapeDtypeStruct(h_buf.shape, BF16), jax.ShapeDtypeStruct(state.shape, F32)],
        scratch_shapes=[pltpu.VMEM((SAMPLE_BLOCK, hw), F32)],
        compiler_params=_cparams(("parallel",), 40),
        name="hgrn_sample",
    )


def _attn_prompt_kernel(q_ref, k_ref, v_ref, o_ref, *, scale):
    s = _dot_nt(q_ref[...].astype(BF16), k_ref[0, 0].astype(BF16)) * scale
    e = jnp.exp(s - jnp.max(s, axis=1, keepdims=True))
    p = e / jnp.sum(e, axis=1, keepdims=True)
    o_ref[...] = _dot(p.astype(BF16), v_ref[0, 0].astype(BF16)).astype(o_ref.dtype)


def _attn_prompt(q, kvbuf, bsz, seqlen, layer):
    t, d = q.shape
    dh = d // N_HEADS
    mem_len = kvbuf.shape[2] // bsz
    tq = _tile(seqlen, 512)
    nq = seqlen // tq
    return pl.pallas_call(
        functools.partial(_attn_prompt_kernel, scale=dh ** -0.5),
        grid=(bsz, N_HEADS, nq),
        in_specs=[pl.BlockSpec((tq, dh), lambda b, h, i: (b * nq + i, h)),
                  pl.BlockSpec((1, 1, mem_len, dh), lambda b, h, i: (layer, 0, b, h)),
                  pl.BlockSpec((1, 1, mem_len, dh), lambda b, h, i: (layer, 1, b, h))],
        out_specs=pl.BlockSpec((tq, dh), lambda b, h, i: (b * nq + i, h)),
        out_shape=jax.ShapeDtypeStruct((t, d), BF16),
        compiler_params=_cparams(("parallel", "parallel", "arbitrary"), 32),
        name="attn_prompt",
    )(q, kvbuf, kvbuf)


def _attn_sample_kernel(q_ref, k_ref, v_ref, o_ref, *, scale):
    s = jnp.sum(k_ref[0, 0] * q_ref[...], axis=2, keepdims=True) * scale
    e = jnp.exp(s - jnp.max(s, axis=0, keepdims=True))
    p = e / jnp.sum(e, axis=0, keepdims=True)
    o_ref[0] = jnp.sum(p * v_ref[0, 0], axis=0)


def _attn_sample(q4, cache_k, cache_v, layer):
    bs, nh, dh = q4.shape
    mem_len = cache_k.shape[2]
    kv_spec = pl.BlockSpec((1, 1, mem_len, nh, dh), lambda i: (layer, i, 0, 0, 0))
    return pl.pallas_call(
        functools.partial(_attn_sample_kernel, scale=dh ** -0.5),
        grid=(bs,),
        in_specs=[pl.BlockSpec((1, nh, dh), lambda i: (i, 0, 0)), kv_spec, kv_spec],
        out_specs=pl.BlockSpec((1, nh, dh), lambda i: (i, 0, 0)),
        out_shape=jax.ShapeDtypeStruct((bs, nh, dh), F32),
        compiler_params=_cparams(("parallel",), 48),
        name="attn_sample",
    )(q4, cache_k, cache_v)


def _rope_tables(pos):
    half = DK // 2
    inv = ROPE_BASE ** (-jnp.arange(half, dtype=F32) / half)
    ang = pos.astype(F32)[:, None] * inv[None, :]
    cos, sin = jnp.cos(ang), jnp.sin(ang)
    return jnp.concatenate([cos, cos], axis=1), jnp.concatenate([-sin, sin], axis=1)


def _retention_tables(c):
    lg = jnp.log1p(-jnp.exp2(-5.0 - jnp.arange(N_HEADS, dtype=F32)))
    idx = jnp.arange(c, dtype=F32)
    diff = idx[:, None] - idx[None, :]
    dmat = jnp.where(diff[None] >= 0, jnp.exp(jnp.maximum(diff, 0.0)[None] * lg[:, None, None]), 0.0)
    e1 = jnp.exp((idx[None, :] + 1.0) * lg[:, None])
    e2 = jnp.exp((c - 1.0 - idx)[None, :] * lg[:, None])
    e3 = jnp.broadcast_to(jnp.exp(c * lg)[:, None], (N_HEADS, c))
    dec = jnp.stack([e1, e2, e3] + [jnp.zeros_like(e1)] * 5, axis=-1)
    return dmat, dec


def kernel(x_prompt, x_sample, mem_prompt, state_ret, state_mlstm_c, state_mlstm_n, state_mlstm_m, state_s5_re, state_s5_im, state_hgrn, cache_mem_k, cache_mem_v, norm_w, ffn_w_in, ffn_w_out, w_in, ml_gate_bias, ml_norm_w, s5_lambda_re, s5_lambda_im, s5_log_step, s5_b_re, s5_b_im, s5_c_re, s5_c_im, s5_d, s5_w_glu, hg_lb_logits, hg_norm_w, w_branch, w_out, xa_mem_norm, xa_wq, xa_wkv, xa_wo):
    bsz, seqlen, d = x_prompt.shape
    bs = x_sample.shape[0]
    depth = norm_w.shape[0]
    mem_len = mem_prompt.shape[1]
    tp = bsz * seqlen
    gn = s5_lambda_re.shape[1] * s5_lambda_re.shape[2]
    seg_gate = OFF_MO

    x = jnp.concatenate([x_prompt.reshape(tp, d), x_sample.reshape(bs, d)], axis=0)
    mem = mem_prompt.reshape(bsz * mem_len, d)
    s5_in_re = state_s5_re.reshape(depth, bs, gn)
    s5_in_im = state_s5_im.reshape(depth, bs, gn)

    cos_p, sin_p = _rope_tables(jnp.arange(seqlen, dtype=jnp.int32))
    cos_s, sin_s = _rope_tables(PAST_LEN + jnp.arange(1, dtype=jnp.int32))
    dmat, dec = _retention_tables(CHUNK)
    _, dec1 = _retention_tables(1)
    gamma = dec1[:, 0, 0]

    p_ret = p_ml = p_hg = p_s5 = kvbuf = None
    s_ret = s_ml = s_hg = s_s5 = None
    for l in range(depth):
        nw = norm_w[l]
        x = _ffn(x, nw[0:2], ffn_w_in[l, 0].astype(BF16), ffn_w_out[l, 0].astype(BF16))
        wl = w_in[l]
        n_gate = 2 * N_HEADS
        w_main = jnp.concatenate([wl[:, :seg_gate], wl[:, seg_gate + n_gate:]], axis=1).astype(BF16)
        w_gate = jnp.pad(wl[:, seg_gate:seg_gate + n_gate], ((0, 0), (0, GATE_PAD - n_gate))).astype(BF16)
        u, ug = _in_proj(x, nw[2:3], w_main, w_gate)
        ugt = ug[:tp, :n_gate].T
        mlw = ml_norm_w[l].reshape(1, -1)
        hgw = hg_norm_w[l].reshape(1, -1)
        ar, ai, bbr, bbi = _s5_prep(s5_lambda_re[l], s5_lambda_im[l], s5_log_step[l], s5_b_re[l], s5_b_im[l])
        ar, ai = ar.reshape(1, gn), ai.reshape(1, gn)
        bblk, cblk = _s5_weights(bbr, bbi, s5_c_re[l], s5_c_im[l])
        s5d = s5_d[l].astype(F32).reshape(1, -1)
        wglu = s5_w_glu[l].astype(BF16)
        h_ret, p_ret = _ret_prompt(u, bsz, seqlen, cos_p, sin_p, dmat, dec, l, depth, p_ret)
        h_ret, s_ret = _ret_sample(u, tp, gamma, cos_s, sin_s, state_ret, l, h_ret, s_ret)
        h_ml, *p_ml = _mlstm_prompt(u, ug, ugt, ml_gate_bias[l], mlw, bsz, seqlen, l, depth, p_ml)
        h_ml, *s_ml = _mlstm_sample(u, ug, tp, ml_gate_bias[l], mlw, state_mlstm_c, state_mlstm_n,
                                    state_mlstm_m, l, h_ml, s_ml)
        h_s5, *p_s5 = _s5_prompt(u, ar, ai, bblk, cblk, s5d, wglu, bsz, seqlen, l, depth, p_s5)
        h_s5, *s_s5 = _s5_sample(u, tp, ar, ai, bblk, cblk, s5d, wglu, s5_in_re, s5_in_im, l, h_s5, s_s5)
        h_hg, p_hg = _hgrn_prompt(u, hg_lb_logits, hgw, bsz, seqlen, l, p_hg)
        h_hg, s_hg = _hgrn_sample(u, tp, hg_lb_logits, hgw, state_hgrn, l, h_hg, s_hg)
        w6 = jnp.concatenate([w_branch[l], w_out[l].reshape(2, d // 2, d)], axis=0).astype(BF16)
        x = _merge([h_ret, h_ml, h_s5, h_hg], u, w6, x, nw[3:4])
        kvbuf = _mem_kv(mem, xa_mem_norm[l].reshape(1, d), xa_wkv[l].astype(BF16), l, depth, kvbuf)
        q = _norm_matmul(x, nw[4:5], xa_wq[l].astype(BF16), "xa_q")
        o = _attn_prompt(q, kvbuf, bsz, seqlen, l)
        o_s = _attn_sample(q[tp:].reshape(bs, N_HEADS, d // N_HEADS), cache_mem_k, cache_mem_v, l)
        o = lax.dynamic_update_slice(o, o_s.reshape(bs, d).astype(BF16), (tp, 0))
        x = _proj_res(o, xa_wo[l].astype(BF16), x, nw[5:6])
        x = _ffn(x, nw[6:8], ffn_w_in[l, 1].astype(BF16), ffn_w_out[l, 1].astype(BF16))

    mem_shape = (depth, bsz, mem_len, N_HEADS, d // N_HEADS)
    s5_shape = (depth, -1, gn // S5_N, S5_N)
    return (x[:tp].reshape(bsz, seqlen, d), x[tp:].reshape(bs, 1, d),
            p_ret, p_ml[0], p_ml[1].reshape(depth, bsz, N_HEADS, DK), p_ml[2].reshape(depth, bsz, N_HEADS),
            p_s5[0].reshape(s5_shape), p_s5[1].reshape(s5_shape), p_hg,
            kvbuf[:, 0].reshape(mem_shape), kvbuf[:, 1].reshape(mem_shape),
            s_ret, s_ml[0], s_ml[1], s_ml[2], s_s5[0].reshape(s5_shape), s_s5[1].reshape(s5_shape), s_hg)
```

```python
import functools
import math

import jax
import jax.numpy as jnp
from jax import lax
from jax.experimental import pallas as pl
from jax.experimental.pallas import tpu as pltpu

F32 = jnp.float32
BF16 = jnp.bfloat16

CHUNK = 128
N_BRANCH = 4
N_HEADS = 4
DK = 128
DV = 256
S5_GROUP = 16
S5_N = 64
ROPE_BASE = 10000.0
PAST_LEN = 16384

V7X_VMEM_BYTES = 64 * 1024 * 1024
MXU_K = 256
SUBLANES = 8
MIB = 1024 * 1024

OFF_RQ, OFF_RK, OFF_RV, OFF_RG = 0, 512, 1024, 2048
OFF_MQ, OFF_MK, OFF_MV, OFF_MO = 3072, 3584, 4096, 5120
OFF_SU = 6144
OFF_HQ, OFF_HF, OFF_HI, OFF_HG = 7168, 7680, 8192, 9216
OFF_GATES = 10240
N_MAIN = 18432
GATE_PAD = 128


def _cparams(sem, vmem_mib):
    return pltpu.CompilerParams(dimension_semantics=sem, vmem_limit_bytes=vmem_mib * MIB)


_ANY = pl.BlockSpec(memory_space=pl.ANY)


def _pcall(kernel_fn, inputs, in_specs, carried=None, **kw):
    carried = {o: a for o, a in (carried or {}).items() if a is not None}
    idxs = sorted(carried)
    n_in = len(inputs)

    def body(*refs):
        return kernel_fn(*refs[:n_in], *refs[n_in + len(idxs):])

    return pl.pallas_call(
        body if idxs else kernel_fn,
        in_specs=list(in_specs) + [_ANY] * len(idxs),
        input_output_aliases={n_in + j: o for j, o in enumerate(idxs)},
        **kw)(*inputs, *[carried[o] for o in idxs])


def _tile(n, cap, mult=16):
    best = None
    for t in range(mult, min(n, cap) + 1, mult):
        if n % t == 0:
            best = t
    return best if best is not None else n


def _dot(a, b):
    return jnp.dot(a, b, preferred_element_type=F32)


def _dot_nt(a, b):
    return lax.dot_general(a, b, (((1,), (1,)), ((), ())), preferred_element_type=F32)


def _rms(x, w, eps=1e-6):
    return x * lax.rsqrt(jnp.mean(x * x, axis=-1, keepdims=True) + eps) * w


def _sigmoid(x):
    return 1.0 / (1.0 + jnp.exp(-x))


def _silu(x):
    return x * _sigmoid(x)


def _log_sigmoid(x):
    return jnp.minimum(x, 0.0) - jnp.log1p(jnp.exp(-jnp.abs(x)))


def _logaddexp(a, b):
    return jnp.maximum(a, b) + jnp.log1p(jnp.exp(-jnp.abs(a - b)))


def _gelu_tanh(x):
    return 0.5 * x * (1.0 + jnp.tanh(math.sqrt(2.0 / math.pi) * (x + 0.044715 * (x * x * x))))


def _row_to_col(row):
    n = row.shape[1]
    r = lax.broadcasted_iota(jnp.int32, (n, n), 0)
    c = lax.broadcasted_iota(jnp.int32, (n, n), 1)
    return jnp.sum(jnp.where(r == c, row, 0.0), axis=1, keepdims=True)


def _cumsum_rows(x):
    c = x.shape[0]
    r = lax.broadcasted_iota(jnp.int32, (c, c), 0)
    k = lax.broadcasted_iota(jnp.int32, (c, c), 1)
    tri = jnp.where(r >= k, 1.0, 0.0).astype(BF16)
    hi = x.astype(BF16)
    r1 = x - hi.astype(F32)
    mid = r1.astype(BF16)
    lo = (r1 - mid.astype(F32)).astype(BF16)
    return _dot(tri, hi) + _dot(tri, mid) + _dot(tri, lo)


def _ffn_kernel(x_ref, nw_ref, wg_ref, wu_ref, wo_ref, o_ref, xn_ref, *, nf, rc):
    f = pl.program_id(1)

    @pl.when(f == 0)
    def _():
        xn_ref[...] = _rms(x_ref[...], nw_ref[0:1, :]).astype(BF16)
        o_ref[...] = jnp.zeros_like(o_ref)

    for r in range(0, x_ref.shape[0], rc):
        xn = xn_ref[r:r + rc, :]
        g = _dot(xn, wg_ref[...])
        u = _dot(xn, wu_ref[...])
        o_ref[r:r + rc, :] += _dot((_silu(g) * u).astype(BF16), wo_ref[...])

    @pl.when(f == nf - 1)
    def _():
        o_ref[...] = x_ref[...] + 0.5 * _rms(o_ref[...], nw_ref[1:2, :])


def _ffn(x, nw2, w_in, w_out):
    t, d = x.shape
    dff = w_out.shape[0]
    tm = _tile(t, 1040)
    rc = _tile(tm, 260)
    tf = _tile(dff, 512, 128)
    nf = dff // tf
    return pl.pallas_call(
        functools.partial(_ffn_kernel, nf=nf, rc=rc),
        grid=(t // tm, nf),
        in_specs=[pl.BlockSpec((tm, d), lambda i, f: (i, 0), pipeline_mode=pl.Buffered(1)),
                  pl.BlockSpec((2, d), lambda i, f: (0, 0)),
                  pl.BlockSpec((d, tf), lambda i, f: (0, f)),
                  pl.BlockSpec((d, tf), lambda i, f: (0, nf + f)),
                  pl.BlockSpec((tf, d), lambda i, f: (f, 0))],
        out_specs=pl.BlockSpec((tm, d), lambda i, f: (i, 0)),
        out_shape=jax.ShapeDtypeStruct((t, d), F32),
        scratch_shapes=[pltpu.VMEM((tm, d), BF16)],
        compiler_params=_cparams(("parallel", "arbitrary"), 56),
        name="ffn",
    )(x, nw2, w_in, w_in, w_out)


def _norm_matmul_kernel(x_ref, nw_ref, w_ref, o_ref, xn_ref):
    @pl.when(pl.program_id(1) == 0)
    def _():
        xn_ref[...] = _rms(x_ref[...], nw_ref[...]).astype(BF16)

    o_ref[...] = _dot(xn_ref[...], w_ref[...]).astype(o_ref.dtype)


def _norm_matmul(x, nw, w, name):
    t, d = x.shape
    n = w.shape[1]
    tm = _tile(t, 1040)
    tn = _tile(n, 1024, 128)
    return pl.pallas_call(
        _norm_matmul_kernel,
        grid=(t // tm, n // tn),
        in_specs=[pl.BlockSpec((tm, d), lambda i, j: (i, 0)),
                  pl.BlockSpec((1, d), lambda i, j: (0, 0)),
                  pl.BlockSpec((d, tn), lambda i, j: (0, j))],
        out_specs=pl.BlockSpec((tm, tn), lambda i, j: (i, j)),
        out_shape=jax.ShapeDtypeStruct((t, n), F32),
        scratch_shapes=[pltpu.VMEM((tm, d), BF16)],
        compiler_params=_cparams(("parallel", "arbitrary"), 48),
        name=name,
    )(x, nw, w)


def _mem_kv_kernel(x_ref, nw_ref, w_ref, o_ref, xn_ref):
    @pl.when(pl.program_id(1) == 0)
    def _():
        xn_ref[...] = _rms(x_ref[...], nw_ref[...]).astype(BF16)

    o_ref[0, 0] = _dot(xn_ref[...], w_ref[...])


def _mem_kv(mem, nw, w, layer, depth, prev):
    t, d = mem.shape
    tm = _tile(t, 512)
    tn = _tile(d, 1024, 128)
    npk = d // tn
    return _pcall(
        _mem_kv_kernel, (mem, nw, w),
        [pl.BlockSpec((tm, d), lambda i, j: (i, 0)),
         pl.BlockSpec((1, d), lambda i, j: (0, 0)),
         pl.BlockSpec((d, tn), lambda i, j: (0, j))],
        carried={0: prev},
        grid=(t // tm, 2 * npk),
        out_specs=pl.BlockSpec((1, 1, tm, tn), lambda i, j: (layer, j // npk, i, j % npk)),
        out_shape=jax.ShapeDtypeStruct((depth, 2, t, d), F32),
        scratch_shapes=[pltpu.VMEM((tm, d), BF16)],
        compiler_params=_cparams(("parallel", "arbitrary"), 48),
        name="mem_kv",
    )


def _in_proj_kernel(x_ref, nw_ref, w_ref, wg_ref, o_ref, og_ref, xn_ref):
    @pl.when(pl.program_id(1) == 0)
    def _():
        xn = _rms(x_ref[...], nw_ref[...]).astype(BF16)
        xn_ref[...] = xn
        og_ref[...] = _dot(xn, wg_ref[...])

    o_ref[...] = _dot(xn_ref[...], w_ref[...])


def _in_proj(x, nw, w_main, w_gate):
    t, d = x.shape
    n = w_main.shape[1]
    tm = _tile(t, 1040)
    tn = _tile(n, 1024, 128)
    return pl.pallas_call(
        _in_proj_kernel,
        grid=(t // tm, n // tn),
        in_specs=[pl.BlockSpec((tm, d), lambda i, j: (i, 0)),
                  pl.BlockSpec((1, d), lambda i, j: (0, 0)),
                  pl.BlockSpec((d, tn), lambda i, j: (0, j)),
                  pl.BlockSpec((d, GATE_PAD), lambda i, j: (0, 0))],
        out_specs=[pl.BlockSpec((tm, tn), lambda i, j: (i, j)),
                   pl.BlockSpec((tm, GATE_PAD), lambda i, j: (i, 0))],
        out_shape=[jax.ShapeDtypeStruct((t, n), F32), jax.ShapeDtypeStruct((t, GATE_PAD), F32)],
        scratch_shapes=[pltpu.VMEM((tm, d), BF16)],
        compiler_params=_cparams(("parallel", "arbitrary"), 48),
        name="in_proj",
    )(x, nw, w_main, w_gate)


def _merge_kernel(*refs):
    h_refs, g_refs = refs[:N_BRANCH], refs[N_BRANCH:2 * N_BRANCH]
    wb_ref, wo_ref, x_ref, nw_ref, o_ref = refs[2 * N_BRANCH:]
    merged = None
    for k in range(N_BRANCH):
        term = _sigmoid(g_refs[k][...]) * _dot(h_refs[k][...], wb_ref[k])
        merged = term if merged is None else merged + term
    o_ref[...] = x_ref[...] + _rms(_dot(merged.astype(BF16), wo_ref[...]), nw_ref[...])


def _merge(hs, u, w_branch, w_out, x, nw):
    t, d = x.shape
    bw = hs[0].shape[1]
    tm = _tile(t, 208)
    gate_blk = OFF_GATES // d
    once = pl.Buffered(1)
    return pl.pallas_call(
        _merge_kernel,
        grid=(t // tm,),
        in_specs=[pl.BlockSpec((tm, bw), lambda i: (i, 0))] * N_BRANCH + [
            pl.BlockSpec((tm, d), lambda i, k=k: (i, gate_blk + k)) for k in range(N_BRANCH)] + [
            pl.BlockSpec(w_branch.shape, lambda i: (0, 0, 0), pipeline_mode=once),
            pl.BlockSpec(w_out.shape, lambda i: (0, 0), pipeline_mode=once),
            pl.BlockSpec((tm, d), lambda i: (i, 0)),
            pl.BlockSpec((1, d), lambda i: (0, 0))],
        out_specs=pl.BlockSpec((tm, d), lambda i: (i, 0)),
        out_shape=jax.ShapeDtypeStruct((t, d), F32),
        compiler_params=_cparams(("parallel",), 56),
        name="merge",
    )(*hs, *([u] * N_BRANCH), w_branch, w_out, x, nw)


def _proj_res_kernel(a_ref, w_ref, x_ref, nw_ref, o_ref):
    o_ref[...] = x_ref[...] + _rms(_dot(a_ref[...], w_ref[...]), nw_ref[...])


def _proj_res(a, w, x, nw):
    t, d = x.shape
    tm = _tile(t, 640)
    return pl.pallas_call(
        _proj_res_kernel,
        grid=(t // tm,),
        in_specs=[pl.BlockSpec((tm, d), lambda i: (i, 0)),
                  pl.BlockSpec((d, d), lambda i: (0, 0)),
                  pl.BlockSpec((tm, d), lambda i: (i, 0)),
                  pl.BlockSpec((1, d), lambda i: (0, 0))],
        out_specs=pl.BlockSpec((tm, d), lambda i: (i, 0)),
        out_shape=jax.ShapeDtypeStruct((t, d), F32),
        compiler_params=_cparams(("parallel",), 56),
        name="xa_out",
    )(a, w, x, nw)


def _rope(x, cs, sn):
    return x * cs + pltpu.roll(x, x.shape[1] // 2, 1) * sn


def _group_norm(o, eps=1e-5):
    oc = o - jnp.mean(o, axis=-1, keepdims=True)
    return oc * lax.rsqrt(jnp.mean(oc * oc, axis=-1, keepdims=True) + eps)


def _ret_kernel(q_ref, k_ref, v_ref, g_ref, cos_ref, sin_ref, dm_ref, dec_ref, h_ref, st_ref, s_ref, *, nc):
    c = pl.program_id(1)

    @pl.when(c == 0)
    def _():
        s_ref[...] = jnp.zeros_like(s_ref)

    cs, sn = cos_ref[...], sin_ref[...]
    for h in range(N_HEADS):
        kc, vc = slice(h * DK, (h + 1) * DK), slice(h * DV, (h + 1) * DV)
        qr = _rope(q_ref[:, kc], cs, sn)
        kr = _rope(k_ref[:, kc], cs, sn) * (DK ** -0.5)
        dec = dec_ref[h]
        e1, e2, e3 = dec[:, 0:1], dec[:, 1:2], dec[0:1, 2:3]
        vb = v_ref[:, vc].astype(BF16)
        qb = qr.astype(BF16)
        s = s_ref[h]
        sc = _dot_nt(qb, kr.astype(BF16)) * dm_ref[h]
        o = _dot(sc.astype(BF16), vb) + _dot(qb, s.astype(BF16)) * e1
        s_ref[h] = e3 * s + _dot((kr * e2).T.astype(BF16), vb)
        h_ref[:, vc] = (_group_norm(o) * _silu(g_ref[:, vc])).astype(h_ref.dtype)

    @pl.when(c == nc - 1)
    def _():
        st_ref[...] = s_ref[...].reshape(st_ref.shape)


def _seg_spec(nc, width, off):
    return pl.BlockSpec((CHUNK, width), lambda b, c: (b * nc + c, off // width))


def _layer_state_spec(layer, tail):
    return pl.BlockSpec((1, 1) + tail, lambda b, c: (layer, b) + (0,) * len(tail))


def _const_spec(shape):
    return pl.BlockSpec(shape, lambda b, c: (0,) * len(shape))


_HW, _KW = N_HEADS * DV, N_HEADS * DK


def _ret_prompt(u, bsz, seqlen, cos2, sin2, dmat, dec, layer, depth, prev):
    nc = seqlen // CHUNK
    return _pcall(
        functools.partial(_ret_kernel, nc=nc), (u, u, u, u, cos2, sin2, dmat, dec),
        [_seg_spec(nc, _KW, OFF_RQ), _seg_spec(nc, _KW, OFF_RK), _seg_spec(nc, _HW, OFF_RV),
         _seg_spec(nc, _HW, OFF_RG),
         pl.BlockSpec((CHUNK, DK), lambda b, c: (c, 0)),
         pl.BlockSpec((CHUNK, DK), lambda b, c: (c, 0)),
         _const_spec(dmat.shape), _const_spec(dec.shape)],
        carried={1: prev},
        grid=(bsz, nc),
        out_specs=[pl.BlockSpec((CHUNK, _HW), lambda b, c: (b * nc + c, 0)),
                   _layer_state_spec(layer, (N_HEADS, DK, DV))],
        out_shape=[jax.ShapeDtypeStruct((u.shape[0], _HW), BF16),
                   jax.ShapeDtypeStruct((depth, bsz, N_HEADS, DK, DV), F32)],
        scratch_shapes=[pltpu.VMEM((N_HEADS, DK, DV), F32)],
        compiler_params=_cparams(("parallel", "arbitrary"), 32),
        name="ret_prompt",
    )


def _mlstm_kernel(gb_ref, q_ref, k_ref, v_ref, mo_ref, ug_ref, ugt_ref, nw_ref,
                  h_ref, c_out, n_out, m_out, cm_ref, nv_ref, m_ref, *, nc):
    c = pl.program_id(1)

    @pl.when(c == 0)
    def _():
        cm_ref[...] = jnp.zeros_like(cm_ref)
        nv_ref[...] = jnp.zeros_like(nv_ref)
        m_ref[...] = jnp.zeros_like(m_ref)

    n = CHUNK
    row = lax.broadcasted_iota(jnp.int32, (n, n), 0)
    col = lax.broadcasted_iota(jnp.int32, (n, n), 1)
    causal = row >= col
    ug = ug_ref[...]
    for h in range(N_HEADS):
        kc, vc = slice(h * DK, (h + 1) * DK), slice(h * DV, (h + 1) * DV)
        bias_i = gb_ref[0, h]
        bias_f = gb_ref[1, h]
        ig_row = ugt_ref[h:h + 1, :] + bias_i
        lf_row = _log_sigmoid(ugt_ref[N_HEADS + h:N_HEADS + h + 1, :] + bias_f)
        ig_col = ug[:, h:h + 1] + bias_i
        lf_col = _log_sigmoid(ug[:, N_HEADS + h:N_HEADS + h + 1] + bias_f)
        b_col = jnp.sum(jnp.where(causal, lf_row, 0.0), axis=1, keepdims=True)
        b_row = jnp.sum(jnp.where(row <= col, lf_col, 0.0), axis=0, keepdims=True)
        b_last = b_col[n - 1:n, :]
        m = m_ref[h]
        logd = jnp.where(causal, b_col - b_row + ig_row, -jnp.inf)
        inter = m + b_col
        mi = jnp.maximum(inter, jnp.max(logd, axis=1, keepdims=True))
        w = jnp.exp(logd - mi)
        wi = jnp.exp(inter - mi)
        q = q_ref[:, kc]
        k = k_ref[:, kc] * (DK ** -0.5)
        qb = q.astype(BF16)
        vb = v_ref[:, vc].astype(BF16)
        cm = cm_ref[h]
        nv = nv_ref[h]
        a = _dot_nt(qb, k.astype(BF16)) * w
        num = _dot(a.astype(BF16), vb) + wi * _dot(qb, cm.astype(BF16))
        nq = jnp.sum(a, axis=1, keepdims=True) + wi * jnp.sum(q * nv, axis=1, keepdims=True)
        hh = num / jnp.maximum(jnp.abs(nq), jnp.exp(-mi))
        m_new = mi[n - 1:n, :]
        wl = jnp.exp(b_last - b_col + ig_col - m_new)
        dp = jnp.exp(m + b_last - m_new)
        kw = k * wl
        cm_ref[h] = dp * cm + _dot(kw.T.astype(BF16), vb)
        nv_ref[h] = dp * nv + jnp.sum(kw, axis=0, keepdims=True)
        m_ref[h] = m_new
        h_ref[:, vc] = (_rms(hh, nw_ref[:, vc]) * _sigmoid(mo_ref[:, vc])).astype(h_ref.dtype)

    @pl.when(c == nc - 1)
    def _():
        c_out[...] = cm_ref[...].reshape(c_out.shape)
        n_out[...] = nv_ref[...].reshape(n_out.shape)
        m_out[...] = m_ref[...].reshape(m_out.shape)


def _mlstm_prompt(u, ug, ugt, gate_bias, norm_w, bsz, seqlen, layer, depth, prev):
    nc = seqlen // CHUNK
    prev = prev or (None, None, None)
    return _pcall(
        functools.partial(_mlstm_kernel, nc=nc), (gate_bias, u, u, u, u, ug, ugt, norm_w),
        [pl.BlockSpec(memory_space=pltpu.SMEM),
         _seg_spec(nc, _KW, OFF_MQ), _seg_spec(nc, _KW, OFF_MK), _seg_spec(nc, _HW, OFF_MV),
         _seg_spec(nc, _HW, OFF_MO),
         pl.BlockSpec((CHUNK, GATE_PAD), lambda b, c: (b * nc + c, 0)),
         pl.BlockSpec((2 * N_HEADS, CHUNK), lambda b, c: (0, b * nc + c)),
         _const_spec((1, _HW))],
        carried={1: prev[0], 2: prev[1], 3: prev[2]},
        grid=(bsz, nc),
        out_specs=[pl.BlockSpec((CHUNK, _HW), lambda b, c: (b * nc + c, 0)),
                   _layer_state_spec(layer, (N_HEADS, DK, DV)),
                   _layer_state_spec(layer, (N_HEADS, 1, DK)),
                   _layer_state_spec(layer, (N_HEADS, 1, 1))],
        out_shape=[jax.ShapeDtypeStruct((u.shape[0], _HW), BF16),
                   jax.ShapeDtypeStruct((depth, bsz, N_HEADS, DK, DV), F32),
                   jax.ShapeDtypeStruct((depth, bsz, N_HEADS, 1, DK), F32),
                   jax.ShapeDtypeStruct((depth, bsz, N_HEADS, 1, 1), F32)],
        scratch_shapes=[pltpu.VMEM((N_HEADS, DK, DV), F32), pltpu.VMEM((N_HEADS, 1, DK), F32),
                        pltpu.VMEM((N_HEADS, 1, 1), F32)],
        compiler_params=_cparams(("parallel", "arbitrary"), 32),
        name="mlstm_prompt",
    )


def _hgrn_lower_bound(logits, layer):
    e = jnp.exp(logits - jnp.max(logits, axis=0, keepdims=True))
    p = e / jnp.sum(e, axis=0, keepdims=True)
    lb = jnp.zeros_like(p[0:1, :])
    for r in range(1, layer + 1):
        lb = lb + p[r:r + 1, :]
    return lb


def _hgrn_gates(fpre, lb):
    hlf = _logaddexp(jnp.log(lb), jnp.log1p(-lb) + _log_sigmoid(fpre))
    hk = (1.0 - lb) * _sigmoid(-fpre)
    return hlf, hk


def _hgrn_kernel(lg_ref, q_ref, f_ref, i_ref, g_ref, nw_ref, h_ref, st_ref, s_ref, *, nc, layer):
    c = pl.program_id(1)

    @pl.when(c == 0)
    def _():
        s_ref[...] = jnp.zeros_like(s_ref)

    n = CHUNK
    lb_all = _hgrn_lower_bound(lg_ref[...], layer)
    r2 = lax.broadcasted_iota(jnp.int32, (n, n), 0)
    c2 = lax.broadcasted_iota(jnp.int32, (n, n), 1)
    rowv = lax.broadcasted_iota(jnp.int32, (n, DK), 0)
    for h in range(N_HEADS):
        kc, vc = slice(h * DK, (h + 1) * DK), slice(h * DV, (h + 1) * DV)
        hlf, hk = _hgrn_gates(f_ref[:, kc], lb_all[:, kc])
        q = _silu(q_ref[:, kc])
        ib = i_ref[:, vc].astype(BF16)
        bc = _cumsum_rows(hlf)
        a = jnp.where(r2 == c2, _dot_nt(q.astype(BF16), hk.astype(BF16)), 0.0)
        p = bc
        s = 1
        lev = 0
        while s < n:
            right = (rowv & s) != 0
            nxt = pltpu.roll(p, n - s, 0)
            e = jnp.exp(jnp.where(right, bc - p, nxt - bc))
            qs = jnp.where(right, q * e, 0.0).astype(BF16)
            ks = jnp.where(right, 0.0, hk * e).astype(BF16)
            a = a + jnp.where((r2 >> (lev + 1)) == (c2 >> (lev + 1)), _dot_nt(qs, ks), 0.0)
            if 2 * s < n:
                p = jnp.where(right, pltpu.roll(p, s, 0), p)
            s *= 2
            lev += 1
        st = s_ref[h]
        o = _dot(a.astype(BF16), ib) + _dot((q * jnp.exp(bc)).astype(BF16), st.astype(BF16))
        bl = bc[n - 1:n, :]
        s_ref[h] = _row_to_col(jnp.exp(bl)) * st + _dot((hk * jnp.exp(bl - bc)).T.astype(BF16), ib)
        h_ref[:, vc] = (_rms(o, nw_ref[:, vc]) * _silu(g_ref[:, vc])).astype(h_ref.dtype)

    @pl.when(c == nc - 1)
    def _():
        st_ref[...] = s_ref[...].reshape(st_ref.shape)


def _hgrn_prompt(u, logits, norm_w, bsz, seqlen, layer, prev):
    nc = seqlen // CHUNK
    depth = logits.shape[0]
    return _pcall(
        functools.partial(_hgrn_kernel, nc=nc, layer=layer), (logits, u, u, u, u, norm_w),
        [_const_spec(logits.shape),
         _seg_spec(nc, _KW, OFF_HQ), _seg_spec(nc, _KW, OFF_HF), _seg_spec(nc, _HW, OFF_HI),
         _seg_spec(nc, _HW, OFF_HG), _const_spec((1, _HW))],
        carried={1: prev},
        grid=(bsz, nc),
        out_specs=[pl.BlockSpec((CHUNK, _HW), lambda b, c: (b * nc + c, 0)),
                   _layer_state_spec(layer, (N_HEADS, DK, DV))],
        out_shape=[jax.ShapeDtypeStruct((u.shape[0], _HW), BF16),
                   jax.ShapeDtypeStruct((depth, bsz, N_HEADS, DK, DV), F32)],
        scratch_shapes=[pltpu.VMEM((N_HEADS, DK, DV), F32)],
        compiler_params=_cparams(("parallel", "arbitrary"), 32),
        name="hgrn_prompt",
    )


def _s5_prep_kernel(lr_ref, li_ref, dt_ref, bre_ref, bim_ref, ar_ref, ai_ref, bbr_ref, bbi_ref):
    lr, li, dt = lr_ref[...], li_ref[...], dt_ref[...]
    mag = jnp.exp(lr * dt)
    ar = mag * jnp.cos(li * dt)
    ai = mag * jnp.sin(li * dt)
    den = lr * lr + li * li
    cr = ((ar - 1.0) * lr + ai * li) / den
    ci = (ai * lr - (ar - 1.0) * li) / den
    ar_ref[...] = ar
    ai_ref[...] = ai
    bbr_ref[...] = cr * bre_ref[...] - ci * bim_ref[...]
    bbi_ref[...] = cr * bim_ref[...] + ci * bre_ref[...]


def _s5_prep(lam_re, lam_im, log_step, b_re, b_im):
    g, n = lam_re.shape
    p = b_re.shape[-1]
    gn = g * n
    dt = jnp.repeat(jnp.exp(log_step.astype(F32)), n).reshape(gn, 1)
    col = lambda a: a.astype(F32).reshape(gn, 1)
    return pl.pallas_call(
        _s5_prep_kernel,
        out_shape=[jax.ShapeDtypeStruct((gn, 1), F32), jax.ShapeDtypeStruct((gn, 1), F32),
                   jax.ShapeDtypeStruct((gn, p), F32), jax.ShapeDtypeStruct((gn, p), F32)],
        name="s5_prep",
    )(col(lam_re), col(lam_im), dt, b_re.reshape(gn, p), b_im.reshape(gn, p))


def _s5_kernel(*refs, seq, nc, n_kt, sw):
    if seq:
        (u_ref, ar_ref, ai_ref, bblk_ref, cblk_ref, d_ref, wglu_ref,
         h_ref, sr_ref, si_ref, hr_scr, hi_scr) = refs
    else:
        (u_ref, ar_ref, ai_ref, bblk_ref, cblk_ref, d_ref, wglu_ref, h0r_ref, h0i_ref,
         h_ref, sr_ref, si_ref) = refs
    u = u_ref[...]
    ub = u.astype(BF16)
    rows = u.shape[0]
    if seq:
        c = pl.program_id(1)

        @pl.when(c == 0)
        def _():
            hr_scr[...] = jnp.zeros_like(hr_scr)
            hi_scr[...] = jnp.zeros_like(hi_scr)

    ys = []
    for kt in range(n_kt):
        lanes = slice(kt * sw, (kt + 1) * sw)
        bu = _dot(ub[:, kt * MXU_K:(kt + 1) * MXU_K], bblk_ref[kt])
        xr, xi = bu[:, :sw], bu[:, sw:]
        ar, ai = ar_ref[:, lanes], ai_ref[:, lanes]
        if seq:
            sub = lax.broadcasted_iota(jnp.int32, (SUBLANES, sw), 0)
            mr, mi = jnp.broadcast_to(ar, (SUBLANES, sw)), jnp.broadcast_to(ai, (SUBLANES, sw))
            tabr, tabi = mr, mi
            levels = []
            sft = 1
            while sft < SUBLANES:
                keep = sub >= sft
                levels.append((sft, jnp.where(keep, mr, 0.0), jnp.where(keep, mi, 0.0)))
                tr, ti = pltpu.roll(tabr, sft, 0), pltpu.roll(tabi, sft, 0)
                tabr, tabi = (jnp.where(keep, tabr * tr - tabi * ti, tabr),
                              jnp.where(keep, tabr * ti + tabi * tr, tabi))
                mr, mi = mr * mr - mi * mi, 2.0 * mr * mi
                sft *= 2
            cr, ci = hr_scr[:, lanes], hi_scr[:, lanes]
            slabs_r, slabs_i = [], []
            for j in range(rows // SUBLANES):
                rs = slice(j * SUBLANES, (j + 1) * SUBLANES)
                sr_, si_ = xr[rs, :], xi[rs, :]
                for sft, lr, li in levels:
                    qr, qi = pltpu.roll(sr_, sft, 0), pltpu.roll(si_, sft, 0)
                    sr_, si_ = sr_ + lr * qr - li * qi, si_ + lr * qi + li * qr
                crb, cib = jnp.broadcast_to(cr, (SUBLANES, sw)), jnp.broadcast_to(ci, (SUBLANES, sw))
                sr_, si_ = sr_ + tabr * crb - tabi * cib, si_ + tabr * cib + tabi * crb
                cr, ci = sr_[SUBLANES - 1:SUBLANES, :], si_[SUBLANES - 1:SUBLANES, :]
                slabs_r.append(sr_)
                slabs_i.append(si_)
            xr, xi = jnp.concatenate(slabs_r, axis=0), jnp.concatenate(slabs_i, axis=0)
            hr_scr[:, lanes] = cr
            hi_scr[:, lanes] = ci
        else:
            h0r, h0i = h0r_ref[0, :, lanes], h0i_ref[0, :, lanes]
            xr, xi = xr + ar * h0r - ai * h0i, xi + ar * h0i + ai * h0r
            sr_ref[0, :, lanes] = xr
            si_ref[0, :, lanes] = xi
        ys.append(_dot(jnp.concatenate([xr, xi], axis=1).astype(BF16), cblk_ref[kt]))
    y = jnp.concatenate(ys, axis=1) + d_ref[...] * u
    z = _gelu_tanh(y)
    h_ref[...] = (z * _sigmoid(_dot(z.astype(BF16), wglu_ref[...]))).astype(h_ref.dtype)
    if seq:
        @pl.when(c == nc - 1)
        def _():
            sr_ref[...] = hr_scr[...].reshape(sr_ref.shape)
            si_ref[...] = hi_scr[...].reshape(si_ref.shape)


def _s5_weights(bbr, bbi, c_re, c_im):
    gn, p = bbr.shape
    g = gn // S5_N
    gpt = MXU_K // p
    n_kt = g // gpt
    eye = jnp.eye(gpt, dtype=F32)

    def b_blk(bb):
        return jnp.einsum('kgnp,gh->kgphn', bb.reshape(n_kt, gpt, S5_N, p), eye).reshape(n_kt, gpt * p, gpt * S5_N)

    def c_blk(cc):
        return jnp.einsum('kgpn,gh->kgnhp', cc.astype(F32).reshape(n_kt, gpt, p, S5_N), eye).reshape(
            n_kt, gpt * S5_N, gpt * p)

    bblk = jnp.concatenate([b_blk(bbr), b_blk(bbi)], axis=2).astype(BF16)
    cblk = jnp.concatenate([c_blk(c_re), -c_blk(c_im)], axis=1).astype(BF16)
    return bblk, cblk


def _s5_prompt(u, ar, ai, bblk, cblk, d, wglu, bsz, seqlen, layer, depth, prev):
    width = wglu.shape[0]
    n_kt = bblk.shape[0]
    sw = bblk.shape[2] // 2
    ct = _tile(seqlen, 256)
    nc = seqlen // ct
    gn = ar.shape[1]
    prev = prev or (None, None)
    full = lambda shape: pl.BlockSpec(shape, lambda b, c: (0,) * len(shape))
    st_spec = pl.BlockSpec((1, 1, 1, gn), lambda b, c: (layer, b, 0, 0))
    st_shape = jax.ShapeDtypeStruct((depth, bsz, 1, gn), F32)
    return _pcall(
        functools.partial(_s5_kernel, seq=True, nc=nc, n_kt=n_kt, sw=sw), (u, ar, ai, bblk, cblk, d, wglu),
        [pl.BlockSpec((ct, width), lambda b, c: (b * nc + c, OFF_SU // width)),
         full((1, gn)), full((1, gn)), full(bblk.shape), full(cblk.shape),
         full((1, width)), full(wglu.shape)],
        carried={1: prev[0], 2: prev[1]},
        grid=(bsz, nc),
        out_specs=[pl.BlockSpec((ct, width), lambda b, c: (b * nc + c, 0)), st_spec, st_spec],
        out_shape=[jax.ShapeDtypeStruct((u.shape[0], width), BF16), st_shape, st_shape],
        scratch_shapes=[pltpu.VMEM((1, gn), F32), pltpu.VMEM((1, gn), F32)],
        compiler_params=_cparams(("parallel", "arbitrary"), 56),
        name="s5_prompt",
    )


def _s5_sample(u, tp, ar, ai, bblk, cblk, d, wglu, h0r, h0i, layer, h_buf, prev):
    width = wglu.shape[0]
    n_kt = bblk.shape[0]
    sw = bblk.shape[2] // 2
    depth, bs, gn = h0r.shape
    tb = _tile(bs, 128, 8)
    r0 = tp // tb
    prev = prev or (None, None)
    full = lambda shape: pl.BlockSpec(shape, lambda i: (0,) * len(shape))
    st_spec = pl.BlockSpec((1, tb, gn), lambda i: (layer, i, 0))
    st_shape = jax.ShapeDtypeStruct((depth, bs, gn), F32)
    return _pcall(
        functools.partial(_s5_kernel, seq=False, nc=1, n_kt=n_kt, sw=sw),
        (u, ar, ai, bblk, cblk, d, wglu, h0r, h0i),
        [pl.BlockSpec((tb, width), lambda i: (r0 + i, OFF_SU // width)),
         full((1, gn)), full((1, gn)), full(bblk.shape), full(cblk.shape),
         full((1, width)), full(wglu.shape), st_spec, st_spec],
        carried={0: h_buf, 1: prev[0], 2: prev[1]},
        grid=(bs // tb,),
        out_specs=[pl.BlockSpec((tb, width), lambda i: (r0 + i, 0)), st_spec, st_spec],
        out_shape=[jax.ShapeDtypeStruct(h_buf.shape, BF16), st_shape, st_shape],
        compiler_params=_cparams(("parallel",), 56),
        name="s5_sample",
    )


SAMPLE_BLOCK = 8


def _state_step(s, dcol, kcol, vrow, qcol):
    s_new = dcol * s + kcol * vrow
    return s_new, jnp.sum(qcol * s_new, axis=0, keepdims=True)


def _ret_step_kernel(gam_ref, q_ref, k_ref, v_ref, g_ref, cos_ref, sin_ref, st_ref, h_ref, so_ref, o_scr):
    cs, sn = cos_ref[...], sin_ref[...]
    for h in range(N_HEADS):
        qr = _rope(q_ref[:, h * DK:(h + 1) * DK], cs, sn)
        kr = _rope(k_ref[:, h * DK:(h + 1) * DK], cs, sn) * (DK ** -0.5)
        for i in range(SAMPLE_BLOCK):
            s_new, o = _state_step(st_ref[0, i, h], gam_ref[h], _row_to_col(kr[i:i + 1, :]),
                                   v_ref[i:i + 1, h * DV:(h + 1) * DV], _row_to_col(qr[i:i + 1, :]))
            so_ref[0, i, h] = s_new
            o_scr[i:i + 1, h * DV:(h + 1) * DV] = o
    for h in range(N_HEADS):
        cols = slice(h * DV, (h + 1) * DV)
        h_ref[:, cols] = (_group_norm(o_scr[:, cols]) * _silu(g_ref[:, cols])).astype(h_ref.dtype)


def _mlstm_step_kernel(gb_ref, q_ref, k_ref, v_ref, mo_ref, ug_ref, nw_ref, c_ref, n_ref, m_ref,
                       h_ref, co_ref, no_ref, mo_out_ref, o_scr):
    ug = ug_ref[...]
    m_all = m_ref[0]
    m_new_cols = []
    for h in range(N_HEADS):
        ig = ug[:, h:h + 1] + gb_ref[0, h]
        lf = _log_sigmoid(ug[:, N_HEADS + h:N_HEADS + h + 1] + gb_ref[1, h])
        m_old = m_all[:, h:h + 1]
        inter = m_old + lf
        mi = jnp.maximum(inter, ig)
        w = jnp.exp(ig - mi)
        wi = jnp.exp(inter - mi)
        m_new_cols.append(mi)
        lim = jnp.exp(-mi)
        qh = q_ref[:, h * DK:(h + 1) * DK]
        kh = k_ref[:, h * DK:(h + 1) * DK] * (DK ** -0.5)
        for i in range(SAMPLE_BLOCK):
            wk = w[i:i + 1, :] * kh[i:i + 1, :]
            c_new, num = _state_step(c_ref[0, i, h], wi[i:i + 1, :], _row_to_col(wk),
                                     v_ref[i:i + 1, h * DV:(h + 1) * DV], _row_to_col(qh[i:i + 1, :]))
            n_new = wi[i:i + 1, :] * n_ref[0, i, h:h + 1, :] + wk
            nq = jnp.sum(qh[i:i + 1, :] * n_new, axis=1, keepdims=True)
            co_ref[0, i, h] = c_new
            no_ref[0, i, h:h + 1, :] = n_new
            o_scr[i:i + 1, h * DV:(h + 1) * DV] = num / jnp.maximum(jnp.abs(nq), lim[i:i + 1, :])
    mo_out_ref[0] = jnp.concatenate(m_new_cols, axis=1)
    for h in range(N_HEADS):
        cols = slice(h * DV, (h + 1) * DV)
        h_ref[:, cols] = (_rms(o_scr[:, cols], nw_ref[:, cols]) * _sigmoid(mo_ref[:, cols])).astype(h_ref.dtype)


def _hgrn_step_kernel(lg_ref, q_ref, f_ref, i_ref, g_ref, nw_ref, st_ref, h_ref, so_ref, o_scr, *, layer):
    lb_all = _hgrn_lower_bound(lg_ref[...], layer)
    for h in range(N_HEADS):
        hlf, hk = _hgrn_gates(f_ref[:, h * DK:(h + 1) * DK], lb_all[:, h * DK:(h + 1) * DK])
        dec = jnp.exp(hlf)
        qh = _silu(q_ref[:, h * DK:(h + 1) * DK])
        for i in range(SAMPLE_BLOCK):
            s_new, o = _state_step(st_ref[0, i, h], _row_to_col(dec[i:i + 1, :]), _row_to_col(hk[i:i + 1, :]),
                                   i_ref[i:i + 1, h * DV:(h + 1) * DV], _row_to_col(qh[i:i + 1, :]))
            so_ref[0, i, h] = s_new
            o_scr[i:i + 1, h * DV:(h + 1) * DV] = o
    for h in range(N_HEADS):
        cols = slice(h * DV, (h + 1) * DV)
        h_ref[:, cols] = (_rms(o_scr[:, cols], nw_ref[:, cols]) * _silu(g_ref[:, cols])).astype(h_ref.dtype)


def _seg(width, off, tp):
    return pl.BlockSpec((SAMPLE_BLOCK, width), lambda i: (tp // SAMPLE_BLOCK + i, off // width))


def _state_spec(layer, tail):
    return pl.BlockSpec((1, SAMPLE_BLOCK) + tail, lambda i: (layer, i) + (0,) * len(tail))


_MAT = (N_HEADS, DK, DV)
_SMEM = pl.BlockSpec(memory_space=pltpu.SMEM)


def _whole(shape):
    return pl.BlockSpec(shape, lambda i: (0,) * len(shape))


def _h_rows_spec(tp):
    return pl.BlockSpec((SAMPLE_BLOCK, N_HEADS * DV), lambda i: (tp // SAMPLE_BLOCK + i, 0))


def _ret_sample(u, tp, gamma, cos2, sin2, state, layer, h_buf, prev):
    bs = state.shape[1]
    hw, kw = N_HEADS * DV, N_HEADS * DK
    return _pcall(
        _ret_step_kernel, (gamma, u, u, u, u, cos2, sin2, state),
        [_SMEM, _seg(kw, OFF_RQ, tp), _seg(kw, OFF_RK, tp), _seg(hw, OFF_RV, tp), _seg(hw, OFF_RG, tp),
         _whole((1, DK)), _whole((1, DK)), _state_spec(layer, _MAT)],
        carried={0: h_buf, 1: prev},
        grid=(bs // SAMPLE_BLOCK,),
        out_specs=[_h_rows_spec(tp), _state_spec(layer, _MAT)],
        out_shape=[jax.ShapeDtypeStruct(h_buf.shape, BF16), jax.ShapeDtypeStruct(state.shape, F32)],
        scratch_shapes=[pltpu.VMEM((SAMPLE_BLOCK, hw), F32)],
        compiler_params=_cparams(("parallel",), 40),
        name="ret_sample",
    )


def _mlstm_sample(u, ug, tp, gate_bias, norm_w, st_c, st_n, st_m, layer, h_buf, prev):
    bs = st_c.shape[1]
    hw, kw = N_HEADS * DV, N_HEADS * DK
    prev = prev or (None, None, None)
    return _pcall(
        _mlstm_step_kernel, (gate_bias, u, u, u, u, ug, norm_w, st_c, st_n, st_m),
        [_SMEM, _seg(kw, OFF_MQ, tp), _seg(kw, OFF_MK, tp), _seg(hw, OFF_MV, tp), _seg(hw, OFF_MO, tp),
         pl.BlockSpec((SAMPLE_BLOCK, GATE_PAD), lambda i: (tp // SAMPLE_BLOCK + i, 0)), _whole((1, hw)),
         _state_spec(layer, _MAT), _state_spec(layer, (N_HEADS, DK)), _state_spec(layer, (N_HEADS,))],
        carried={0: h_buf, 1: prev[0], 2: prev[1], 3: prev[2]},
        grid=(bs // SAMPLE_BLOCK,),
        out_specs=[_h_rows_spec(tp), _state_spec(layer, _MAT),
                   _state_spec(layer, (N_HEADS, DK)), _state_spec(layer, (N_HEADS,))],
        out_shape=[jax.ShapeDtypeStruct(h_buf.shape, BF16), jax.ShapeDtypeStruct(st_c.shape, F32),
                   jax.ShapeDtypeStruct(st_n.shape, F32), jax.ShapeDtypeStruct(st_m.shape, F32)],
        scratch_shapes=[pltpu.VMEM((SAMPLE_BLOCK, hw), F32)],
        compiler_params=_cparams(("parallel",), 40),
        name="mlstm_sample",
    )


def _hgrn_sample(u, tp, logits, norm_w, state, layer, h_buf, prev):
    bs = state.shape[1]
    hw, kw = N_HEADS * DV, N_HEADS * DK
    return _pcall(
        functools.partial(_hgrn_step_kernel, layer=layer), (logits, u, u, u, u, norm_w, state),
        [_whole(logits.shape), _seg(kw, OFF_HQ, tp), _seg(kw, OFF_HF, tp), _seg(hw, OFF_HI, tp),
         _seg(hw, OFF_HG, tp), _whole((1, hw)), _state_spec(layer, _MAT)],
        carried={0: h_buf, 1: prev},
        grid=(bs // SAMPLE_BLOCK,),
        out_specs=[_h_rows_spec(tp), _state_spec(layer, _MAT)],
        out_shape=[jax.ShapeDtypeStruct(h_buf.shape, BF16), jax.ShapeDtypeStruct(state.shape, F32)],
        scratch_shapes=[pltpu.VMEM((SAMPLE_BLOCK, hw), F32)],
        compiler_params=_cparams(("parallel",), 40),
        name="hgrn_sample",
    )


def _attn_prompt_kernel(q_ref, k_ref, v_ref, o_ref, *, scale):
    s = _dot_nt(q_ref[...].astype(BF16), k_ref[0, 0].astype(BF16)) * scale
    e = jnp.exp(s - jnp.max(s, axis=1, keepdims=True))
    p = e / jnp.sum(e, axis=1, keepdims=True)
    o_ref[...] = _dot(p.astype(BF16), v_ref[0, 0].astype(BF16)).astype(o_ref.dtype)


def _attn_prompt(q, kvbuf, bsz, seqlen, layer):
    t, d = q.shape
    dh = d // N_HEADS
    mem_len = kvbuf.shape[2] // bsz
    tq = _tile(seqlen, 512)
    nq = seqlen // tq
    return pl.pallas_call(
        functools.partial(_attn_prompt_kernel, scale=dh ** -0.5),
        grid=(bsz, N_HEADS, nq),
        in_specs=[pl.BlockSpec((tq, dh), lambda b, h, i: (b * nq + i, h)),
                  pl.BlockSpec((1, 1, mem_len, dh), lambda b, h, i: (layer, 0, b, h)),
                  pl.BlockSpec((1, 1, mem_len, dh), lambda b, h, i: (layer, 1, b, h))],
        out_specs=pl.BlockSpec((tq, dh), lambda b, h, i: (b * nq + i, h)),
        out_shape=jax.ShapeDtypeStruct((t, d), BF16),
        compiler_params=_cparams(("parallel", "parallel", "arbitrary"), 32),
        name="attn_prompt",
    )(q, kvbuf, kvbuf)


def _attn_sample_kernel(q_ref, k_ref, v_ref, o_ref, *, scale):
    s = jnp.sum(k_ref[0, 0] * q_ref[...], axis=2, keepdims=True) * scale
    e = jnp.exp(s - jnp.max(s, axis=0, keepdims=True))
    p = e / jnp.sum(e, axis=0, keepdims=True)
    o_ref[0] = jnp.sum(p * v_ref[0, 0], axis=0)


def _attn_sample(q4, cache_k, cache_v, layer):
    bs, nh, dh = q4.shape
    mem_len = cache_k.shape[2]
    kv_spec = pl.BlockSpec((1, 1, mem_len, nh, dh), lambda i: (layer, i, 0, 0, 0))
    return pl.pallas_call(
        functools.partial(_attn_sample_kernel, scale=dh ** -0.5),
        grid=(bs,),
        in_specs=[pl.BlockSpec((1, nh, dh), lambda i: (i, 0, 0)), kv_spec, kv_spec],
        out_specs=pl.BlockSpec((1, nh, dh), lambda i: (i, 0, 0)),
        out_shape=jax.ShapeDtypeStruct((bs, nh, dh), F32),
        compiler_params=_cparams(("parallel",), 48),
        name="attn_sample",
    )(q4, cache_k, cache_v)


def _rope_tables(pos):
    half = DK // 2
    inv = ROPE_BASE ** (-jnp.arange(half, dtype=F32) / half)
    ang = pos.astype(F32)[:, None] * inv[None, :]
    cos, sin = jnp.cos(ang), jnp.sin(ang)
    return jnp.concatenate([cos, cos], axis=1), jnp.concatenate([-sin, sin], axis=1)


def _retention_tables(c):
    lg = jnp.log1p(-jnp.exp2(-5.0 - jnp.arange(N_HEADS, dtype=F32)))
    idx = jnp.arange(c, dtype=F32)
    diff = idx[:, None] - idx[None, :]
    dmat = jnp.where(diff[None] >= 0, jnp.exp(jnp.maximum(diff, 0.0)[None] * lg[:, None, None]), 0.0)
    e1 = jnp.exp((idx[None, :] + 1.0) * lg[:, None])
    e2 = jnp.exp((c - 1.0 - idx)[None, :] * lg[:, None])
    e3 = jnp.broadcast_to(jnp.exp(c * lg)[:, None], (N_HEADS, c))
    dec = jnp.stack([e1, e2, e3] + [jnp.zeros_like(e1)] * 5, axis=-1)
    return dmat, dec


def kernel(x_prompt, x_sample, mem_prompt, state_ret, state_mlstm_c, state_mlstm_n, state_mlstm_m, state_s5_re, state_s5_im, state_hgrn, cache_mem_k, cache_mem_v, norm_w, ffn_w_in, ffn_w_out, w_in, ml_gate_bias, ml_norm_w, s5_lambda_re, s5_lambda_im, s5_log_step, s5_b_re, s5_b_im, s5_c_re, s5_c_im, s5_d, s5_w_glu, hg_lb_logits, hg_norm_w, w_branch, w_out, xa_mem_norm, xa_wq, xa_wkv, xa_wo):
    bsz, seqlen, d = x_prompt.shape
    bs = x_sample.shape[0]
    depth = norm_w.shape[0]
    mem_len = mem_prompt.shape[1]
    tp = bsz * seqlen
    gn = s5_lambda_re.shape[1] * s5_lambda_re.shape[2]
    seg_gate = OFF_MO

    x = jnp.concatenate([x_prompt.reshape(tp, d), x_sample.reshape(bs, d)], axis=0)
    mem = mem_prompt.reshape(bsz * mem_len, d)
    s5_in_re = state_s5_re.reshape(depth, bs, gn)
    s5_in_im = state_s5_im.reshape(depth, bs, gn)

    cos_p, sin_p = _rope_tables(jnp.arange(seqlen, dtype=jnp.int32))
    cos_s, sin_s = _rope_tables(PAST_LEN + jnp.arange(1, dtype=jnp.int32))
    dmat, dec = _retention_tables(CHUNK)
    _, dec1 = _retention_tables(1)
    gamma = dec1[:, 0, 0]

    p_ret = p_ml = p_hg = p_s5 = kvbuf = None
    s_ret = s_ml = s_hg = s_s5 = None
    for l in range(depth):
        nw = norm_w[l]
        x = _ffn(x, nw[0:2], ffn_w_in[l, 0].astype(BF16), ffn_w_out[l, 0].astype(BF16))
        wl = w_in[l]
        n_gate = 2 * N_HEADS
        w_main = jnp.concatenate([wl[:, :seg_gate], wl[:, seg_gate + n_gate:]], axis=1).astype(BF16)
        w_gate = jnp.pad(wl[:, seg_gate:seg_gate + n_gate], ((0, 0), (0, GATE_PAD - n_gate))).astype(BF16)
        u, ug = _in_proj(x, nw[2:3], w_main, w_gate)
        ugt = ug[:tp, :n_gate].T
        mlw = ml_norm_w[l].reshape(1, -1)
        hgw = hg_norm_w[l].reshape(1, -1)
        ar, ai, bbr, bbi = _s5_prep(s5_lambda_re[l], s5_lambda_im[l], s5_log_step[l], s5_b_re[l], s5_b_im[l])
        ar, ai = ar.reshape(1, gn), ai.reshape(1, gn)
        bblk, cblk = _s5_weights(bbr, bbi, s5_c_re[l], s5_c_im[l])
        s5d = s5_d[l].astype(F32).reshape(1, -1)
        wglu = s5_w_glu[l].astype(BF16)
        h_ret, p_ret = _ret_prompt(u, bsz, seqlen, cos_p, sin_p, dmat, dec, l, depth, p_ret)
        h_ret, s_ret = _ret_sample(u, tp, gamma, cos_s, sin_s, state_ret, l, h_ret, s_ret)
        h_ml, *p_ml = _mlstm_prompt(u, ug, ugt, ml_gate_bias[l], mlw, bsz, seqlen, l, depth, p_ml)
        h_ml, *s_ml = _mlstm_sample(u, ug, tp, ml_gate_bias[l], mlw, state_mlstm_c, state_mlstm_n,
                                    state_mlstm_m, l, h_ml, s_ml)
        h_s5, *p_s5 = _s5_prompt(u, ar, ai, bblk, cblk, s5d, wglu, bsz, seqlen, l, depth, p_s5)
        h_s5, *s_s5 = _s5_sample(u, tp, ar, ai, bblk, cblk, s5d, wglu, s5_in_re, s5_in_im, l, h_s5, s_s5)
        h_hg, p_hg = _hgrn_prompt(u, hg_lb_logits, hgw, bsz, seqlen, l, p_hg)
        h_hg, s_hg = _hgrn_sample(u, tp, hg_lb_logits, hgw, state_hgrn, l, h_hg, s_hg)
        x = _merge([h_ret, h_ml, h_s5, h_hg], u, w_branch[l].astype(BF16), w_out[l].astype(BF16), x, nw[3:4])
        kvbuf = _mem_kv(mem, xa_mem_norm[l].reshape(1, d), xa_wkv[l].astype(BF16), l, depth, kvbuf)
        q = _norm_matmul(x, nw[4:5], xa_wq[l].astype(BF16), "xa_q")
        o = _attn_prompt(q, kvbuf, bsz, seqlen, l)
        o_s = _attn_sample(q[tp:].reshape(bs, N_HEADS, d // N_HEADS), cache_mem_k, cache_mem_v, l)
        o = lax.dynamic_update_slice(o, o_s.reshape(bs, d).astype(BF16), (tp, 0))
        x = _proj_res(o, xa_wo[l].astype(BF16), x, nw[5:6])
        x = _ffn(x, nw[6:8], ffn_w_in[l, 1].astype(BF16), ffn_w_out[l, 1].astype(BF16))

    mem_shape = (depth, bsz, mem_len, N_HEADS, d // N_HEADS)
    s5_shape = (depth, -1, gn // S5_N, S5_N)
    return (x[:tp].reshape(bsz, seqlen, d), x[tp:].reshape(bs, 1, d),
            p_ret, p_ml[0], p_ml[1].reshape(depth, bsz, N_HEADS, DK), p_ml[2].reshape(depth, bsz, N_HEADS),
            p_s5[0].reshape(s5_shape), p_s5[1].reshape(s5_shape), p_hg,
            kvbuf[:, 0].reshape(mem_shape), kvbuf[:, 1].reshape(mem_shape),
            s_ret, s_ml[0], s_ml[1], s_ml[2], s_s5[0].reshape(s5_shape), s_s5[1].reshape(s5_shape), s_hg)
```

```python
import functools
import math

import jax
import jax.numpy as jnp
from jax import lax
from jax.experimental import pallas as pl
from jax.experimental.pallas import tpu as pltpu

F32 = jnp.float32
BF16 = jnp.bfloat16

CHUNK = 128
N_BRANCH = 4
N_HEADS = 4
DK = 128
DV = 256
S5_GROUP = 16
S5_N = 64
ROPE_BASE = 10000.0
PAST_LEN = 16384

V7X_VMEM_BYTES = 64 * 1024 * 1024
MXU_K = 256
SUBLANES = 8
MIB = 1024 * 1024

OFF_RQ, OFF_RK, OFF_RV, OFF_RG = 0, 512, 1024, 2048
OFF_MQ, OFF_MK, OFF_MV, OFF_MO = 3072, 3584, 4096, 5120
OFF_SU = 6144
OFF_HQ, OFF_HF, OFF_HI, OFF_HG = 7168, 7680, 8192, 9216
OFF_GATES = 10240
N_MAIN = 18432
GATE_PAD = 128


def _cparams(sem, vmem_mib):
    return pltpu.CompilerParams(dimension_semantics=sem, vmem_limit_bytes=vmem_mib * MIB)


_ANY = pl.BlockSpec(memory_space=pl.ANY)


def _pcall(kernel_fn, inputs, in_specs, carried=None, **kw):
    carried = {o: a for o, a in (carried or {}).items() if a is not None}
    idxs = sorted(carried)
    n_in = len(inputs)

    def body(*refs):
        return kernel_fn(*refs[:n_in], *refs[n_in + len(idxs):])

    return pl.pallas_call(
        body if idxs else kernel_fn,
        in_specs=list(in_specs) + [_ANY] * len(idxs),
        input_output_aliases={n_in + j: o for j, o in enumerate(idxs)},
        **kw)(*inputs, *[carried[o] for o in idxs])


def _tile(n, cap, mult=16):
    best = None
    for t in range(mult, min(n, cap) + 1, mult):
        if n % t == 0:
            best = t
    return best if best is not None else n


def _dot(a, b):
    return jnp.dot(a, b, preferred_element_type=F32)


def _dot_nt(a, b):
    return lax.dot_general(a, b, (((1,), (1,)), ((), ())), preferred_element_type=F32)


def _rms(x, w, eps=1e-6):
    return x * lax.rsqrt(jnp.mean(x * x, axis=-1, keepdims=True) + eps) * w


def _sigmoid(x):
    return 1.0 / (1.0 + jnp.exp(-x))


def _silu(x):
    return x * _sigmoid(x)


def _log_sigmoid(x):
    return jnp.minimum(x, 0.0) - jnp.log1p(jnp.exp(-jnp.abs(x)))


def _logaddexp(a, b):
    return jnp.maximum(a, b) + jnp.log1p(jnp.exp(-jnp.abs(a - b)))


def _gelu_tanh(x):
    return 0.5 * x * (1.0 + jnp.tanh(math.sqrt(2.0 / math.pi) * (x + 0.044715 * (x * x * x))))


def _row_to_col(row):
    n = row.shape[1]
    r = lax.broadcasted_iota(jnp.int32, (n, n), 0)
    c = lax.broadcasted_iota(jnp.int32, (n, n), 1)
    return jnp.sum(jnp.where(r == c, row, 0.0), axis=1, keepdims=True)


def _cumsum_rows(x):
    c = x.shape[0]
    r = lax.broadcasted_iota(jnp.int32, (c, c), 0)
    k = lax.broadcasted_iota(jnp.int32, (c, c), 1)
    tri = jnp.where(r >= k, 1.0, 0.0).astype(BF16)
    hi = x.astype(BF16)
    r1 = x - hi.astype(F32)
    mid = r1.astype(BF16)
    lo = (r1 - mid.astype(F32)).astype(BF16)
    return _dot(tri, hi) + _dot(tri, mid) + _dot(tri, lo)


def _ffn_up_kernel(x_ref, nw_ref, wg_ref, wu_ref, h_ref, xn_ref, *, rc):
    @pl.when(pl.program_id(1) == 0)
    def _():
        xn_ref[...] = _rms(x_ref[...], nw_ref[...]).astype(BF16)

    wg = wg_ref[0, 0].astype(BF16)
    wu = wu_ref[0, 0].astype(BF16)
    for r in range(0, x_ref.shape[0], rc):
        xn = xn_ref[r:r + rc, :]
        h_ref[r:r + rc, :] = (_silu(_dot(xn, wg)) * _dot(xn, wu)).astype(h_ref.dtype)


def _ffn_down_kernel(h_ref, w_ref, x_ref, nw_ref, o_ref, *, rc):
    for r in range(0, x_ref.shape[0], rc):
        rows = slice(r, r + rc)
        o_ref[rows, :] = x_ref[rows, :] + 0.5 * _rms(_dot(h_ref[rows, :], w_ref[...]), nw_ref[...])


def _ffn(x, nw2, w_in, layer, which, w_out):
    t, d = x.shape
    dff = w_out.shape[0]
    tm = _tile(t, 2080)
    rc = _tile(tm, 260)
    tf = _tile(dff, 512, 128)
    nf = dff // tf
    once = pl.Buffered(1)
    hidden = pl.pallas_call(
        functools.partial(_ffn_up_kernel, rc=rc),
        grid=(t // tm, nf),
        in_specs=[pl.BlockSpec((tm, d), lambda i, f: (i, 0), pipeline_mode=once),
                  pl.BlockSpec((1, d), lambda i, f: (0, 0)),
                  pl.BlockSpec((1, 1, d, tf), lambda i, f: (layer, which, 0, f)),
                  pl.BlockSpec((1, 1, d, tf), lambda i, f: (layer, which, 0, nf + f))],
        out_specs=pl.BlockSpec((tm, tf), lambda i, f: (i, f)),
        out_shape=jax.ShapeDtypeStruct((t, dff), BF16),
        scratch_shapes=[pltpu.VMEM((tm, d), BF16)],
        compiler_params=_cparams(("parallel", "arbitrary"), 56),
        name="ffn_up",
    )(x, nw2[0:1], w_in, w_in)
    tm2 = _tile(t, 416)
    return pl.pallas_call(
        functools.partial(_ffn_down_kernel, rc=_tile(tm2, 208)),
        grid=(t // tm2,),
        in_specs=[pl.BlockSpec((tm2, dff), lambda i: (i, 0)),
                  pl.BlockSpec((dff, d), lambda i: (0, 0), pipeline_mode=once),
                  pl.BlockSpec((tm2, d), lambda i: (i, 0)),
                  pl.BlockSpec((1, d), lambda i: (0, 0))],
        out_specs=pl.BlockSpec((tm2, d), lambda i: (i, 0)),
        out_shape=jax.ShapeDtypeStruct((t, d), F32),
        compiler_params=_cparams(("parallel",), 56),
        name="ffn_down",
    )(hidden, w_out, x, nw2[1:2])


def _norm_matmul_kernel(x_ref, nw_ref, w_ref, o_ref, xn_ref):
    @pl.when(pl.program_id(1) == 0)
    def _():
        xn_ref[...] = _rms(x_ref[...], nw_ref[...]).astype(BF16)

    o_ref[...] = _dot(xn_ref[...], w_ref[...]).astype(o_ref.dtype)


def _norm_matmul(x, nw, w, name):
    t, d = x.shape
    n = w.shape[1]
    tm = _tile(t, 1040)
    tn = _tile(n, 1024, 128)
    return pl.pallas_call(
        _norm_matmul_kernel,
        grid=(t // tm, n // tn),
        in_specs=[pl.BlockSpec((tm, d), lambda i, j: (i, 0)),
                  pl.BlockSpec((1, d), lambda i, j: (0, 0)),
                  pl.BlockSpec((d, tn), lambda i, j: (0, j))],
        out_specs=pl.BlockSpec((tm, tn), lambda i, j: (i, j)),
        out_shape=jax.ShapeDtypeStruct((t, n), F32),
        scratch_shapes=[pltpu.VMEM((tm, d), BF16)],
        compiler_params=_cparams(("parallel", "arbitrary"), 48),
        name=name,
    )(x, nw, w)


def _mem_kv_kernel(x_ref, nw_ref, w_ref, o_ref, xn_ref):
    @pl.when(pl.program_id(1) == 0)
    def _():
        xn_ref[...] = _rms(x_ref[...], nw_ref[...]).astype(BF16)

    o_ref[0, 0] = _dot(xn_ref[...], w_ref[...])


def _mem_kv(mem, nw, w, layer, depth, prev):
    t, d = mem.shape
    tm = _tile(t, 512)
    tn = _tile(d, 1024, 128)
    npk = d // tn
    return _pcall(
        _mem_kv_kernel, (mem, nw, w),
        [pl.BlockSpec((tm, d), lambda i, j: (i, 0)),
         pl.BlockSpec((1, d), lambda i, j: (0, 0)),
         pl.BlockSpec((d, tn), lambda i, j: (0, j))],
        carried={0: prev},
        grid=(t // tm, 2 * npk),
        out_specs=pl.BlockSpec((1, 1, tm, tn), lambda i, j: (layer, j // npk, i, j % npk)),
        out_shape=jax.ShapeDtypeStruct((depth, 2, t, d), F32),
        scratch_shapes=[pltpu.VMEM((tm, d), BF16)],
        compiler_params=_cparams(("parallel", "arbitrary"), 48),
        name="mem_kv",
    )


def _in_proj_kernel(x_ref, nw_ref, w_ref, wg_ref, o_ref, og_ref, xn_ref):
    @pl.when(pl.program_id(1) == 0)
    def _():
        xn = _rms(x_ref[...], nw_ref[...]).astype(BF16)
        xn_ref[...] = xn
        og_ref[...] = _dot(xn, wg_ref[...])

    o_ref[...] = _dot(xn_ref[...], w_ref[...])


def _in_proj(x, nw, w_main, w_gate):
    t, d = x.shape
    n = w_main.shape[1]
    tm = _tile(t, 1040)
    tn = _tile(n, 1024, 128)
    return pl.pallas_call(
        _in_proj_kernel,
        grid=(t // tm, n // tn),
        in_specs=[pl.BlockSpec((tm, d), lambda i, j: (i, 0)),
                  pl.BlockSpec((1, d), lambda i, j: (0, 0)),
                  pl.BlockSpec((d, tn), lambda i, j: (0, j)),
                  pl.BlockSpec((d, GATE_PAD), lambda i, j: (0, 0))],
        out_specs=[pl.BlockSpec((tm, tn), lambda i, j: (i, j)),
                   pl.BlockSpec((tm, GATE_PAD), lambda i, j: (i, 0))],
        out_shape=[jax.ShapeDtypeStruct((t, n), F32), jax.ShapeDtypeStruct((t, GATE_PAD), F32)],
        scratch_shapes=[pltpu.VMEM((tm, d), BF16)],
        compiler_params=_cparams(("parallel", "arbitrary"), 48),
        name="in_proj",
    )(x, nw, w_main, w_gate)


def _merge_kernel(*refs):
    h_refs, g_refs = refs[:N_BRANCH], refs[N_BRANCH:2 * N_BRANCH]
    wb_ref, wo_ref, x_ref, nw_ref, o_ref = refs[2 * N_BRANCH:]
    merged = None
    for k in range(N_BRANCH):
        term = _sigmoid(g_refs[k][...]) * _dot(h_refs[k][...], wb_ref[k])
        merged = term if merged is None else merged + term
    o_ref[...] = x_ref[...] + _rms(_dot(merged.astype(BF16), wo_ref[...]), nw_ref[...])


def _merge(hbuf, u, w_branch, w_out, x, nw):
    t, d = x.shape
    bw = hbuf.shape[1] // N_BRANCH
    tm = _tile(t, 208)
    gate_blk = OFF_GATES // d
    once = pl.Buffered(1)
    return pl.pallas_call(
        _merge_kernel,
        grid=(t // tm,),
        in_specs=[pl.BlockSpec((tm, bw), lambda i, k=k: (i, k)) for k in range(N_BRANCH)] + [
            pl.BlockSpec((tm, d), lambda i, k=k: (i, gate_blk + k)) for k in range(N_BRANCH)] + [
            pl.BlockSpec(w_branch.shape, lambda i: (0, 0, 0), pipeline_mode=once),
            pl.BlockSpec(w_out.shape, lambda i: (0, 0), pipeline_mode=once),
            pl.BlockSpec((tm, d), lambda i: (i, 0)),
            pl.BlockSpec((1, d), lambda i: (0, 0))],
        out_specs=pl.BlockSpec((tm, d), lambda i: (i, 0)),
        out_shape=jax.ShapeDtypeStruct((t, d), F32),
        compiler_params=_cparams(("parallel",), 56),
        name="merge",
    )(*([hbuf] * N_BRANCH), *([u] * N_BRANCH), w_branch, w_out, x, nw)


def _proj_res_kernel(a_ref, w_ref, x_ref, nw_ref, o_ref):
    o_ref[...] = x_ref[...] + _rms(_dot(a_ref[...], w_ref[...]), nw_ref[...])


def _proj_res(a, w, x, nw):
    t, d = x.shape
    tm = _tile(t, 640)
    return pl.pallas_call(
        _proj_res_kernel,
        grid=(t // tm,),
        in_specs=[pl.BlockSpec((tm, d), lambda i: (i, 0)),
                  pl.BlockSpec((d, d), lambda i: (0, 0)),
                  pl.BlockSpec((tm, d), lambda i: (i, 0)),
                  pl.BlockSpec((1, d), lambda i: (0, 0))],
        out_specs=pl.BlockSpec((tm, d), lambda i: (i, 0)),
        out_shape=jax.ShapeDtypeStruct((t, d), F32),
        compiler_params=_cparams(("parallel",), 56),
        name="xa_out",
    )(a, w, x, nw)


def _rope(x, cs, sn):
    return x * cs + pltpu.roll(x, x.shape[1] // 2, 1) * sn


def _group_norm(o, eps=1e-5):
    oc = o - jnp.mean(o, axis=-1, keepdims=True)
    return oc * lax.rsqrt(jnp.mean(oc * oc, axis=-1, keepdims=True) + eps)


def _ret_kernel(q_ref, k_ref, v_ref, g_ref, cos_ref, sin_ref, dm_ref, dec_ref, h_ref, st_ref, s_ref, *, nc):
    c = pl.program_id(1)

    @pl.when(c == 0)
    def _():
        s_ref[...] = jnp.zeros_like(s_ref)

    cs, sn = cos_ref[...], sin_ref[...]
    for h in range(N_HEADS):
        kc, vc = slice(h * DK, (h + 1) * DK), slice(h * DV, (h + 1) * DV)
        qr = _rope(q_ref[:, kc], cs, sn)
        kr = _rope(k_ref[:, kc], cs, sn) * (DK ** -0.5)
        dec = dec_ref[h]
        e1, e2, e3 = dec[:, 0:1], dec[:, 1:2], dec[0:1, 2:3]
        vb = v_ref[:, vc].astype(BF16)
        qb = qr.astype(BF16)
        s = s_ref[h]
        sc = _dot_nt(qb, kr.astype(BF16)) * dm_ref[h]
        o = _dot(sc.astype(BF16), vb) + _dot(qb, s.astype(BF16)) * e1
        s_ref[h] = e3 * s + _dot((kr * e2).T.astype(BF16), vb)
        h_ref[:, vc] = (_group_norm(o) * _silu(g_ref[:, vc])).astype(h_ref.dtype)

    @pl.when(c == nc - 1)
    def _():
        st_ref[...] = s_ref[...].reshape(st_ref.shape)


def _seg_spec(nc, width, off):
    return pl.BlockSpec((CHUNK, width), lambda b, c: (b * nc + c, off // width))


def _layer_state_spec(layer, tail):
    return pl.BlockSpec((1, 1) + tail, lambda b, c: (layer, b) + (0,) * len(tail))


def _const_spec(shape):
    return pl.BlockSpec(shape, lambda b, c: (0,) * len(shape))


_HW, _KW = N_HEADS * DV, N_HEADS * DK


BR_RET, BR_ML, BR_S5, BR_HG = range(N_BRANCH)


def _ret_prompt(u, hbuf, bsz, seqlen, cos2, sin2, dmat, dec, layer, depth, prev):
    nc = seqlen // CHUNK
    return _pcall(
        functools.partial(_ret_kernel, nc=nc), (u, u, u, u, cos2, sin2, dmat, dec),
        [_seg_spec(nc, _KW, OFF_RQ), _seg_spec(nc, _KW, OFF_RK), _seg_spec(nc, _HW, OFF_RV),
         _seg_spec(nc, _HW, OFF_RG),
         pl.BlockSpec((CHUNK, DK), lambda b, c: (c, 0)),
         pl.BlockSpec((CHUNK, DK), lambda b, c: (c, 0)),
         _const_spec(dmat.shape), _const_spec(dec.shape)],
        carried={0: hbuf, 1: prev},
        grid=(bsz, nc),
        out_specs=[pl.BlockSpec((CHUNK, _HW), lambda b, c: (b * nc + c, BR_RET)),
                   _layer_state_spec(layer, (N_HEADS, DK, DV))],
        out_shape=[jax.ShapeDtypeStruct(hbuf.shape, BF16),
                   jax.ShapeDtypeStruct((depth, bsz, N_HEADS, DK, DV), F32)],
        scratch_shapes=[pltpu.VMEM((N_HEADS, DK, DV), F32)],
        compiler_params=_cparams(("parallel", "arbitrary"), 32),
        name="ret_prompt",
    )


def _mlstm_kernel(gb_ref, q_ref, k_ref, v_ref, mo_ref, ug_ref, ugt_ref, nw_ref,
                  h_ref, c_out, n_out, m_out, cm_ref, nv_ref, m_ref, *, nc):
    c = pl.program_id(1)

    @pl.when(c == 0)
    def _():
        cm_ref[...] = jnp.zeros_like(cm_ref)
        nv_ref[...] = jnp.zeros_like(nv_ref)
        m_ref[...] = jnp.zeros_like(m_ref)

    n = CHUNK
    row = lax.broadcasted_iota(jnp.int32, (n, n), 0)
    col = lax.broadcasted_iota(jnp.int32, (n, n), 1)
    causal = row >= col
    ug = ug_ref[...]
    for h in range(N_HEADS):
        kc, vc = slice(h * DK, (h + 1) * DK), slice(h * DV, (h + 1) * DV)
        bias_i = gb_ref[0, h]
        bias_f = gb_ref[1, h]
        ig_row = ugt_ref[h:h + 1, :] + bias_i
        lf_row = _log_sigmoid(ugt_ref[N_HEADS + h:N_HEADS + h + 1, :] + bias_f)
        ig_col = ug[:, h:h + 1] + bias_i
        lf_col = _log_sigmoid(ug[:, N_HEADS + h:N_HEADS + h + 1] + bias_f)
        b_col = jnp.sum(jnp.where(causal, lf_row, 0.0), axis=1, keepdims=True)
        b_row = jnp.sum(jnp.where(row <= col, lf_col, 0.0), axis=0, keepdims=True)
        b_last = b_col[n - 1:n, :]
        m = m_ref[h]
        logd = jnp.where(causal, b_col - b_row + ig_row, -jnp.inf)
        inter = m + b_col
        mi = jnp.maximum(inter, jnp.max(logd, axis=1, keepdims=True))
        w = jnp.exp(logd - mi)
        wi = jnp.exp(inter - mi)
        q = q_ref[:, kc]
        k = k_ref[:, kc] * (DK ** -0.5)
        qb = q.astype(BF16)
        vb = v_ref[:, vc].astype(BF16)
        cm = cm_ref[h]
        nv = nv_ref[h]
        a = _dot_nt(qb, k.astype(BF16)) * w
        num = _dot(a.astype(BF16), vb) + wi * _dot(qb, cm.astype(BF16))
        nq = jnp.sum(a, axis=1, keepdims=True) + wi * jnp.sum(q * nv, axis=1, keepdims=True)
        hh = num / jnp.maximum(jnp.abs(nq), jnp.exp(-mi))
        m_new = mi[n - 1:n, :]
        wl = jnp.exp(b_last - b_col + ig_col - m_new)
        dp = jnp.exp(m + b_last - m_new)
        kw = k * wl
        cm_ref[h] = dp * cm + _dot(kw.T.astype(BF16), vb)
        nv_ref[h] = dp * nv + jnp.sum(kw, axis=0, keepdims=True)
        m_ref[h] = m_new
        h_ref[:, vc] = (_rms(hh, nw_ref[:, vc]) * _sigmoid(mo_ref[:, vc])).astype(h_ref.dtype)

    @pl.when(c == nc - 1)
    def _():
        c_out[...] = cm_ref[...].reshape(c_out.shape)
        n_out[...] = nv_ref[...].reshape(n_out.shape)
        m_out[...] = m_ref[...].reshape(m_out.shape)


def _mlstm_prompt(u, hbuf, ug, ugt, gate_bias, norm_w, bsz, seqlen, layer, depth, prev):
    nc = seqlen // CHUNK
    return _pcall(
        functools.partial(_mlstm_kernel, nc=nc), (gate_bias, u, u, u, u, ug, ugt, norm_w),
        [pl.BlockSpec(memory_space=pltpu.SMEM),
         _seg_spec(nc, _KW, OFF_MQ), _seg_spec(nc, _KW, OFF_MK), _seg_spec(nc, _HW, OFF_MV),
         _seg_spec(nc, _HW, OFF_MO),
         pl.BlockSpec((CHUNK, GATE_PAD), lambda b, c: (b * nc + c, 0)),
         pl.BlockSpec((2 * N_HEADS, CHUNK), lambda b, c: (0, b * nc + c)),
         _const_spec((1, _HW))],
        carried={0: hbuf, 1: prev[0], 2: prev[1], 3: prev[2]},
        grid=(bsz, nc),
        out_specs=[pl.BlockSpec((CHUNK, _HW), lambda b, c: (b * nc + c, BR_ML)),
                   _layer_state_spec(layer, (N_HEADS, DK, DV)),
                   _layer_state_spec(layer, (N_HEADS, 1, DK)),
                   _layer_state_spec(layer, (N_HEADS, 1, 1))],
        out_shape=[jax.ShapeDtypeStruct(hbuf.shape, BF16),
                   jax.ShapeDtypeStruct((depth, bsz, N_HEADS, DK, DV), F32),
                   jax.ShapeDtypeStruct((depth, bsz, N_HEADS, 1, DK), F32),
                   jax.ShapeDtypeStruct((depth, bsz, N_HEADS, 1, 1), F32)],
        scratch_shapes=[pltpu.VMEM((N_HEADS, DK, DV), F32), pltpu.VMEM((N_HEADS, 1, DK), F32),
                        pltpu.VMEM((N_HEADS, 1, 1), F32)],
        compiler_params=_cparams(("parallel", "arbitrary"), 32),
        name="mlstm_prompt",
    )


def _hgrn_lower_bound(logits, layer):
    e = jnp.exp(logits - jnp.max(logits, axis=0, keepdims=True))
    p = e / jnp.sum(e, axis=0, keepdims=True)
    lb = jnp.zeros_like(p[0:1, :])
    for r in range(1, layer + 1):
        lb = lb + p[r:r + 1, :]
    return lb


def _hgrn_gates(fpre, lb):
    hlf = _logaddexp(jnp.log(lb), jnp.log1p(-lb) + _log_sigmoid(fpre))
    hk = (1.0 - lb) * _sigmoid(-fpre)
    return hlf, hk


def _hgrn_kernel(lg_ref, q_ref, f_ref, i_ref, g_ref, nw_ref, h_ref, st_ref, s_ref, *, nc, layer):
    c = pl.program_id(1)

    @pl.when(c == 0)
    def _():
        s_ref[...] = jnp.zeros_like(s_ref)

    n = CHUNK
    lb_all = _hgrn_lower_bound(lg_ref[...], layer)
    r2 = lax.broadcasted_iota(jnp.int32, (n, n), 0)
    c2 = lax.broadcasted_iota(jnp.int32, (n, n), 1)
    rowv = lax.broadcasted_iota(jnp.int32, (n, DK), 0)
    for h in range(N_HEADS):
        kc, vc = slice(h * DK, (h + 1) * DK), slice(h * DV, (h + 1) * DV)
        hlf, hk = _hgrn_gates(f_ref[:, kc], lb_all[:, kc])
        q = _silu(q_ref[:, kc])
        ib = i_ref[:, vc].astype(BF16)
        bc = _cumsum_rows(hlf)
        a = jnp.where(r2 == c2, _dot_nt(q.astype(BF16), hk.astype(BF16)), 0.0)
        p = bc
        s = 1
        lev = 0
        while s < n:
            right = (rowv & s) != 0
            nxt = pltpu.roll(p, n - s, 0)
            e = jnp.exp(jnp.where(right, bc - p, nxt - bc))
            qs = jnp.where(right, q * e, 0.0).astype(BF16)
            ks = jnp.where(right, 0.0, hk * e).astype(BF16)
            a = a + jnp.where((r2 >> (lev + 1)) == (c2 >> (lev + 1)), _dot_nt(qs, ks), 0.0)
            if 2 * s < n:
                p = jnp.where(right, pltpu.roll(p, s, 0), p)
            s *= 2
            lev += 1
        st = s_ref[h]
        o = _dot(a.astype(BF16), ib) + _dot((q * jnp.exp(bc)).astype(BF16), st.astype(BF16))
        bl = bc[n - 1:n, :]
        s_ref[h] = _row_to_col(jnp.exp(bl)) * st + _dot((hk * jnp.exp(bl - bc)).T.astype(BF16), ib)
        h_ref[:, vc] = (_rms(o, nw_ref[:, vc]) * _silu(g_ref[:, vc])).astype(h_ref.dtype)

    @pl.when(c == nc - 1)
    def _():
        st_ref[...] = s_ref[...].reshape(st_ref.shape)


def _hgrn_prompt(u, hbuf, logits, norm_w, bsz, seqlen, layer, prev):
    nc = seqlen // CHUNK
    depth = logits.shape[0]
    return _pcall(
        functools.partial(_hgrn_kernel, nc=nc, layer=layer), (logits, u, u, u, u, norm_w),
        [_const_spec(logits.shape),
         _seg_spec(nc, _KW, OFF_HQ), _seg_spec(nc, _KW, OFF_HF), _seg_spec(nc, _HW, OFF_HI),
         _seg_spec(nc, _HW, OFF_HG), _const_spec((1, _HW))],
        carried={0: hbuf, 1: prev},
        grid=(bsz, nc),
        out_specs=[pl.BlockSpec((CHUNK, _HW), lambda b, c: (b * nc + c, BR_HG)),
                   _layer_state_spec(layer, (N_HEADS, DK, DV))],
        out_shape=[jax.ShapeDtypeStruct(hbuf.shape, BF16),
                   jax.ShapeDtypeStruct((depth, bsz, N_HEADS, DK, DV), F32)],
        scratch_shapes=[pltpu.VMEM((N_HEADS, DK, DV), F32)],
        compiler_params=_cparams(("parallel", "arbitrary"), 32),
        name="hgrn_prompt",
    )


def _s5_prep_kernel(lr_ref, li_ref, dt_ref, bre_ref, bim_ref, ar_ref, ai_ref, bbr_ref, bbi_ref):
    lr, li, dt = lr_ref[...], li_ref[...], dt_ref[...]
    mag = jnp.exp(lr * dt)
    ar = mag * jnp.cos(li * dt)
    ai = mag * jnp.sin(li * dt)
    den = lr * lr + li * li
    cr = ((ar - 1.0) * lr + ai * li) / den
    ci = (ai * lr - (ar - 1.0) * li) / den
    ar_ref[...] = ar
    ai_ref[...] = ai
    bbr_ref[...] = cr * bre_ref[...] - ci * bim_ref[...]
    bbi_ref[...] = cr * bim_ref[...] + ci * bre_ref[...]


def _s5_prep(lam_re, lam_im, log_step, b_re, b_im):
    g, n = lam_re.shape
    p = b_re.shape[-1]
    gn = g * n
    dt = jnp.repeat(jnp.exp(log_step.astype(F32)), n).reshape(gn, 1)
    col = lambda a: a.astype(F32).reshape(gn, 1)
    return pl.pallas_call(
        _s5_prep_kernel,
        out_shape=[jax.ShapeDtypeStruct((gn, 1), F32), jax.ShapeDtypeStruct((gn, 1), F32),
                   jax.ShapeDtypeStruct((gn, p), F32), jax.ShapeDtypeStruct((gn, p), F32)],
        name="s5_prep",
    )(col(lam_re), col(lam_im), dt, b_re.reshape(gn, p), b_im.reshape(gn, p))


def _s5_kernel(*refs, seq, nc, n_kt, sw):
    if seq:
        (u_ref, ar_ref, ai_ref, bblk_ref, cblk_ref, d_ref, wglu_ref,
         h_ref, sr_ref, si_ref, hr_scr, hi_scr) = refs
    else:
        (u_ref, ar_ref, ai_ref, bblk_ref, cblk_ref, d_ref, wglu_ref, h0r_ref, h0i_ref,
         h_ref, sr_ref, si_ref) = refs
    u = u_ref[...]
    ub = u.astype(BF16)
    rows = u.shape[0]
    if seq:
        c = pl.program_id(1)

        @pl.when(c == 0)
        def _():
            hr_scr[...] = jnp.zeros_like(hr_scr)
            hi_scr[...] = jnp.zeros_like(hi_scr)

    ys = []
    for kt in range(n_kt):
        lanes = slice(kt * sw, (kt + 1) * sw)
        bu = _dot(ub[:, kt * MXU_K:(kt + 1) * MXU_K], bblk_ref[kt])
        xr, xi = bu[:, :sw], bu[:, sw:]
        ar, ai = ar_ref[:, lanes], ai_ref[:, lanes]
        if seq:
            sub = lax.broadcasted_iota(jnp.int32, (SUBLANES, sw), 0)
            mr, mi = jnp.broadcast_to(ar, (SUBLANES, sw)), jnp.broadcast_to(ai, (SUBLANES, sw))
            tabr, tabi = mr, mi
            levels = []
            sft = 1
            while sft < SUBLANES:
                keep = sub >= sft
                levels.append((sft, jnp.where(keep, mr, 0.0), jnp.where(keep, mi, 0.0)))
                tr, ti = pltpu.roll(tabr, sft, 0), pltpu.roll(tabi, sft, 0)
                tabr, tabi = (jnp.where(keep, tabr * tr - tabi * ti, tabr),
                              jnp.where(keep, tabr * ti + tabi * tr, tabi))
                mr, mi = mr * mr - mi * mi, 2.0 * mr * mi
                sft *= 2
            cr, ci = hr_scr[:, lanes], hi_scr[:, lanes]
            slabs_r, slabs_i = [], []
            for j in range(rows // SUBLANES):
                rs = slice(j * SUBLANES, (j + 1) * SUBLANES)
                sr_, si_ = xr[rs, :], xi[rs, :]
                for sft, lr, li in levels:
                    qr, qi = pltpu.roll(sr_, sft, 0), pltpu.roll(si_, sft, 0)
                    sr_, si_ = sr_ + lr * qr - li * qi, si_ + lr * qi + li * qr
                crb, cib = jnp.broadcast_to(cr, (SUBLANES, sw)), jnp.broadcast_to(ci, (SUBLANES, sw))
                sr_, si_ = sr_ + tabr * crb - tabi * cib, si_ + tabr * cib + tabi * crb
                cr, ci = sr_[SUBLANES - 1:SUBLANES, :], si_[SUBLANES - 1:SUBLANES, :]
                slabs_r.append(sr_)
                slabs_i.append(si_)
            xr, xi = jnp.concatenate(slabs_r, axis=0), jnp.concatenate(slabs_i, axis=0)
            hr_scr[:, lanes] = cr
            hi_scr[:, lanes] = ci
        else:
            h0r, h0i = h0r_ref[0, :, lanes], h0i_ref[0, :, lanes]
            xr, xi = xr + ar * h0r - ai * h0i, xi + ar * h0i + ai * h0r
            sr_ref[0, :, lanes] = xr
            si_ref[0, :, lanes] = xi
        ys.append(_dot(jnp.concatenate([xr, xi], axis=1).astype(BF16), cblk_ref[kt]))
    y = jnp.concatenate(ys, axis=1) + d_ref[...] * u
    z = _gelu_tanh(y)
    h_ref[...] = (z * _sigmoid(_dot(z.astype(BF16), wglu_ref[...]))).astype(h_ref.dtype)
    if seq:
        @pl.when(c == nc - 1)
        def _():
            sr_ref[...] = hr_scr[...].reshape(sr_ref.shape)
            si_ref[...] = hi_scr[...].reshape(si_ref.shape)


def _s5_weights(bbr, bbi, c_re, c_im):
    gn, p = bbr.shape
    g = gn // S5_N
    gpt = MXU_K // p
    n_kt = g // gpt
    eye = jnp.eye(gpt, dtype=F32)

    def b_blk(bb):
        return jnp.einsum('kgnp,gh->kgphn', bb.reshape(n_kt, gpt, S5_N, p), eye).reshape(n_kt, gpt * p, gpt * S5_N)

    def c_blk(cc):
        return jnp.einsum('kgpn,gh->kgnhp', cc.astype(F32).reshape(n_kt, gpt, p, S5_N), eye).reshape(
            n_kt, gpt * S5_N, gpt * p)

    bblk = jnp.concatenate([b_blk(bbr), b_blk(bbi)], axis=2).astype(BF16)
    cblk = jnp.concatenate([c_blk(c_re), -c_blk(c_im)], axis=1).astype(BF16)
    return bblk, cblk


def _s5_prompt(u, hbuf, ar, ai, bblk, cblk, d, wglu, bsz, seqlen, layer, depth, prev):
    width = wglu.shape[0]
    n_kt = bblk.shape[0]
    sw = bblk.shape[2] // 2
    ct = _tile(seqlen, 256)
    nc = seqlen // ct
    gn = ar.shape[1]
    full = lambda shape: pl.BlockSpec(shape, lambda b, c: (0,) * len(shape))
    st_spec = pl.BlockSpec((1, 1, 1, gn), lambda b, c: (layer, b, 0, 0))
    st_shape = jax.ShapeDtypeStruct((depth, bsz, 1, gn), F32)
    return _pcall(
        functools.partial(_s5_kernel, seq=True, nc=nc, n_kt=n_kt, sw=sw), (u, ar, ai, bblk, cblk, d, wglu),
        [pl.BlockSpec((ct, width), lambda b, c: (b * nc + c, OFF_SU // width)),
         full((1, gn)), full((1, gn)), full(bblk.shape), full(cblk.shape),
         full((1, width)), full(wglu.shape)],
        carried={0: hbuf, 1: prev[0], 2: prev[1]},
        grid=(bsz, nc),
        out_specs=[pl.BlockSpec((ct, width), lambda b, c: (b * nc + c, BR_S5)), st_spec, st_spec],
        out_shape=[jax.ShapeDtypeStruct(hbuf.shape, BF16), st_shape, st_shape],
        scratch_shapes=[pltpu.VMEM((1, gn), F32), pltpu.VMEM((1, gn), F32)],
        compiler_params=_cparams(("parallel", "arbitrary"), 56),
        name="s5_prompt",
    )


def _s5_sample(u, tp, ar, ai, bblk, cblk, d, wglu, h0r, h0i, layer, h_buf, prev):
    width = wglu.shape[0]
    n_kt = bblk.shape[0]
    sw = bblk.shape[2] // 2
    depth, bs, gn = h0r.shape
    tb = _tile(bs, 128, 8)
    r0 = tp // tb
    full = lambda shape: pl.BlockSpec(shape, lambda i: (0,) * len(shape))
    st_spec = pl.BlockSpec((1, tb, gn), lambda i: (layer, i, 0))
    st_shape = jax.ShapeDtypeStruct((depth, bs, gn), F32)
    return _pcall(
        functools.partial(_s5_kernel, seq=False, nc=1, n_kt=n_kt, sw=sw),
        (u, ar, ai, bblk, cblk, d, wglu, h0r, h0i),
        [pl.BlockSpec((tb, width), lambda i: (r0 + i, OFF_SU // width)),
         full((1, gn)), full((1, gn)), full(bblk.shape), full(cblk.shape),
         full((1, width)), full(wglu.shape), st_spec, st_spec],
        carried={0: h_buf, 1: prev[0], 2: prev[1]},
        grid=(bs // tb,),
        out_specs=[pl.BlockSpec((tb, width), lambda i: (r0 + i, BR_S5)), st_spec, st_spec],
        out_shape=[jax.ShapeDtypeStruct(h_buf.shape, BF16), st_shape, st_shape],
        compiler_params=_cparams(("parallel",), 56),
        name="s5_sample",
    )


SAMPLE_BLOCK = 8


def _state_step(s, dcol, kcol, vrow, qcol):
    s_new = dcol * s + kcol * vrow
    return s_new, jnp.sum(qcol * s_new, axis=0, keepdims=True)


def _zero_other_layers(ref, slot):
    for l in range(ref.shape[0]):
        if l != slot:
            ref[l] = jnp.zeros(ref.shape[1:], ref.dtype)


def _ret_step_kernel(gam_ref, q_ref, k_ref, v_ref, g_ref, cos_ref, sin_ref, st_ref, h_ref, so_ref, o_scr, *, slot):
    _zero_other_layers(so_ref, slot)
    cs, sn = cos_ref[...], sin_ref[...]
    for h in range(N_HEADS):
        qr = _rope(q_ref[:, h * DK:(h + 1) * DK], cs, sn)
        kr = _rope(k_ref[:, h * DK:(h + 1) * DK], cs, sn) * (DK ** -0.5)
        for i in range(SAMPLE_BLOCK):
            s_new, o = _state_step(st_ref[0, i, h], gam_ref[h], _row_to_col(kr[i:i + 1, :]),
                                   v_ref[i:i + 1, h * DV:(h + 1) * DV], _row_to_col(qr[i:i + 1, :]))
            so_ref[slot, i, h] = s_new
            o_scr[i:i + 1, h * DV:(h + 1) * DV] = o
    for h in range(N_HEADS):
        cols = slice(h * DV, (h + 1) * DV)
        h_ref[:, cols] = (_group_norm(o_scr[:, cols]) * _silu(g_ref[:, cols])).astype(h_ref.dtype)


def _mlstm_step_kernel(gb_ref, q_ref, k_ref, v_ref, mo_ref, ug_ref, nw_ref, c_ref, n_ref, m_ref,
                       h_ref, co_ref, no_ref, mo_out_ref, o_scr, *, slot):
    _zero_other_layers(co_ref, slot)
    ug = ug_ref[...]
    m_all = m_ref[0]
    m_new_cols = []
    for h in range(N_HEADS):
        ig = ug[:, h:h + 1] + gb_ref[0, h]
        lf = _log_sigmoid(ug[:, N_HEADS + h:N_HEADS + h + 1] + gb_ref[1, h])
        m_old = m_all[:, h:h + 1]
        inter = m_old + lf
        mi = jnp.maximum(inter, ig)
        w = jnp.exp(ig - mi)
        wi = jnp.exp(inter - mi)
        m_new_cols.append(mi)
        lim = jnp.exp(-mi)
        qh = q_ref[:, h * DK:(h + 1) * DK]
        kh = k_ref[:, h * DK:(h + 1) * DK] * (DK ** -0.5)
        for i in range(SAMPLE_BLOCK):
            wk = w[i:i + 1, :] * kh[i:i + 1, :]
            c_new, num = _state_step(c_ref[0, i, h], wi[i:i + 1, :], _row_to_col(wk),
                                     v_ref[i:i + 1, h * DV:(h + 1) * DV], _row_to_col(qh[i:i + 1, :]))
            n_new = wi[i:i + 1, :] * n_ref[0, i, h:h + 1, :] + wk
            nq = jnp.sum(qh[i:i + 1, :] * n_new, axis=1, keepdims=True)
            co_ref[slot, i, h] = c_new
            no_ref[0, i, h:h + 1, :] = n_new
            o_scr[i:i + 1, h * DV:(h + 1) * DV] = num / jnp.maximum(jnp.abs(nq), lim[i:i + 1, :])
    mo_out_ref[0] = jnp.concatenate(m_new_cols, axis=1)
    for h in range(N_HEADS):
        cols = slice(h * DV, (h + 1) * DV)
        h_ref[:, cols] = (_rms(o_scr[:, cols], nw_ref[:, cols]) * _sigmoid(mo_ref[:, cols])).astype(h_ref.dtype)


def _hgrn_step_kernel(lg_ref, q_ref, f_ref, i_ref, g_ref, nw_ref, st_ref, h_ref, so_ref, o_scr, *, layer, slot):
    _zero_other_layers(so_ref, slot)
    lb_all = _hgrn_lower_bound(lg_ref[...], layer)
    for h in range(N_HEADS):
        hlf, hk = _hgrn_gates(f_ref[:, h * DK:(h + 1) * DK], lb_all[:, h * DK:(h + 1) * DK])
        dec = jnp.exp(hlf)
        qh = _silu(q_ref[:, h * DK:(h + 1) * DK])
        for i in range(SAMPLE_BLOCK):
            s_new, o = _state_step(st_ref[0, i, h], _row_to_col(dec[i:i + 1, :]), _row_to_col(hk[i:i + 1, :]),
                                   i_ref[i:i + 1, h * DV:(h + 1) * DV], _row_to_col(qh[i:i + 1, :]))
            so_ref[slot, i, h] = s_new
            o_scr[i:i + 1, h * DV:(h + 1) * DV] = o
    for h in range(N_HEADS):
        cols = slice(h * DV, (h + 1) * DV)
        h_ref[:, cols] = (_rms(o_scr[:, cols], nw_ref[:, cols]) * _silu(g_ref[:, cols])).astype(h_ref.dtype)


def _seg(width, off, tp):
    return pl.BlockSpec((SAMPLE_BLOCK, width), lambda i: (tp // SAMPLE_BLOCK + i, off // width))


def _state_spec(layer, tail):
    return pl.BlockSpec((1, SAMPLE_BLOCK) + tail, lambda i: (layer, i) + (0,) * len(tail))


_MAT = (N_HEADS, DK, DV)
_SMEM = pl.BlockSpec(memory_space=pltpu.SMEM)


def _whole(shape):
    return pl.BlockSpec(shape, lambda i: (0,) * len(shape))


def _h_rows_spec(tp, branch):
    return pl.BlockSpec((SAMPLE_BLOCK, _HW), lambda i: (tp // SAMPLE_BLOCK + i, branch))


def _new_mat_state(layer, depth, prev):
    if prev is None:
        return pl.BlockSpec((depth, SAMPLE_BLOCK) + _MAT, lambda i: (0, i, 0, 0, 0)), layer
    return _state_spec(layer, _MAT), 0


def _ret_sample(u, tp, gamma, cos2, sin2, state, layer, h_buf, prev):
    depth, bs = state.shape[:2]
    spec, slot = _new_mat_state(layer, depth, prev)
    return _pcall(
        functools.partial(_ret_step_kernel, slot=slot), (gamma, u, u, u, u, cos2, sin2, state),
        [_SMEM, _seg(_KW, OFF_RQ, tp), _seg(_KW, OFF_RK, tp), _seg(_HW, OFF_RV, tp), _seg(_HW, OFF_RG, tp),
         _whole((1, DK)), _whole((1, DK)), _state_spec(layer, _MAT)],
        carried={0: h_buf, 1: prev},
        grid=(bs // SAMPLE_BLOCK,),
        out_specs=[_h_rows_spec(tp, BR_RET), spec],
        out_shape=[jax.ShapeDtypeStruct(h_buf.shape, BF16), jax.ShapeDtypeStruct(state.shape, F32)],
        scratch_shapes=[pltpu.VMEM((SAMPLE_BLOCK, _HW), F32)],
        compiler_params=_cparams(("parallel",), 48),
        name="ret_sample",
    )


def _mlstm_sample(u, ug, tp, gate_bias, norm_w, st_c, st_n, st_m, layer, h_buf, prev):
    depth, bs = st_c.shape[:2]
    spec, slot = _new_mat_state(layer, depth, prev[0])
    return _pcall(
        functools.partial(_mlstm_step_kernel, slot=slot), (gate_bias, u, u, u, u, ug, norm_w, st_c, st_n, st_m),
        [_SMEM, _seg(_KW, OFF_MQ, tp), _seg(_KW, OFF_MK, tp), _seg(_HW, OFF_MV, tp), _seg(_HW, OFF_MO, tp),
         pl.BlockSpec((SAMPLE_BLOCK, GATE_PAD), lambda i: (tp // SAMPLE_BLOCK + i, 0)), _whole((1, _HW)),
         _state_spec(layer, _MAT), _state_spec(layer, (N_HEADS, DK)), _state_spec(layer, (N_HEADS,))],
        carried={0: h_buf, 1: prev[0], 2: prev[1], 3: prev[2]},
        grid=(bs // SAMPLE_BLOCK,),
        out_specs=[_h_rows_spec(tp, BR_ML), spec,
                   _state_spec(layer, (N_HEADS, DK)), _state_spec(layer, (N_HEADS,))],
        out_shape=[jax.ShapeDtypeStruct(h_buf.shape, BF16), jax.ShapeDtypeStruct(st_c.shape, F32),
                   jax.ShapeDtypeStruct(st_n.shape, F32), jax.ShapeDtypeStruct(st_m.shape, F32)],
        scratch_shapes=[pltpu.VMEM((SAMPLE_BLOCK, _HW), F32)],
        compiler_params=_cparams(("parallel",), 48),
        name="mlstm_sample",
    )


def _hgrn_sample(u, tp, logits, norm_w, state, layer, h_buf, prev):
    depth, bs = state.shape[:2]
    spec, slot = _new_mat_state(layer, depth, prev)
    return _pcall(
        functools.partial(_hgrn_step_kernel, layer=layer, slot=slot), (logits, u, u, u, u, norm_w, state),
        [_whole(logits.shape), _seg(_KW, OFF_HQ, tp), _seg(_KW, OFF_HF, tp), _seg(_HW, OFF_HI, tp),
         _seg(_HW, OFF_HG, tp), _whole((1, _HW)), _state_spec(layer, _MAT)],
        carried={0: h_buf, 1: prev},
        grid=(bs // SAMPLE_BLOCK,),
        out_specs=[_h_rows_spec(tp, BR_HG), spec],
        out_shape=[jax.ShapeDtypeStruct(h_buf.shape, BF16), jax.ShapeDtypeStruct(state.shape, F32)],
        scratch_shapes=[pltpu.VMEM((SAMPLE_BLOCK, _HW), F32)],
        compiler_params=_cparams(("parallel",), 48),
        name="hgrn_sample",
    )


def _attn_prompt_kernel(q_ref, k_ref, v_ref, o_ref, *, scale):
    s = _dot_nt(q_ref[...].astype(BF16), k_ref[0, 0].astype(BF16)) * scale
    e = jnp.exp(s - jnp.max(s, axis=1, keepdims=True))
    p = e / jnp.sum(e, axis=1, keepdims=True)
    o_ref[...] = _dot(p.astype(BF16), v_ref[0, 0].astype(BF16)).astype(o_ref.dtype)


def _attn_prompt(q, kvbuf, obuf, bsz, seqlen, layer):
    t, d = q.shape
    dh = d // N_HEADS
    mem_len = kvbuf.shape[2] // bsz
    tq = _tile(seqlen, 512)
    nq = seqlen // tq
    return _pcall(
        functools.partial(_attn_prompt_kernel, scale=dh ** -0.5), (q, kvbuf, kvbuf),
        [pl.BlockSpec((tq, dh), lambda b, h, i: (b * nq + i, h)),
         pl.BlockSpec((1, 1, mem_len, dh), lambda b, h, i: (layer, 0, b, h)),
         pl.BlockSpec((1, 1, mem_len, dh), lambda b, h, i: (layer, 1, b, h))],
        carried={0: obuf},
        grid=(bsz, N_HEADS, nq),
        out_specs=pl.BlockSpec((tq, dh), lambda b, h, i: (b * nq + i, h)),
        out_shape=jax.ShapeDtypeStruct((t, d), BF16),
        compiler_params=_cparams(("parallel", "parallel", "arbitrary"), 32),
        name="attn_prompt",
    )


def _attn_sample_kernel(q_ref, k_ref, v_ref, o_ref, *, scale):
    for h in range(q_ref.shape[1]):
        k = k_ref[0, 0, :, h, :]
        s = jnp.sum(k * q_ref[0, h:h + 1, :], axis=1, keepdims=True) * scale
        e = jnp.exp(s - jnp.max(s, axis=0, keepdims=True))
        p = e / jnp.sum(e, axis=0, keepdims=True)
        o_ref[0, h:h + 1, :] = jnp.sum(p * v_ref[0, 0, :, h, :], axis=0, keepdims=True)


def _attn_sample(q4, cache_k, cache_v, layer):
    bs, nh, dh = q4.shape
    mem_len = cache_k.shape[2]
    kv_spec = pl.BlockSpec((1, 1, mem_len, nh, dh), lambda i: (layer, i, 0, 0, 0))
    return pl.pallas_call(
        functools.partial(_attn_sample_kernel, scale=dh ** -0.5),
        grid=(bs,),
        in_specs=[pl.BlockSpec((1, nh, dh), lambda i: (i, 0, 0)), kv_spec, kv_spec],
        out_specs=pl.BlockSpec((1, nh, dh), lambda i: (i, 0, 0)),
        out_shape=jax.ShapeDtypeStruct((bs, nh, dh), F32),
        compiler_params=_cparams(("parallel",), 48),
        name="attn_sample",
    )(q4, cache_k, cache_v)


def _rope_tables(pos):
    half = DK // 2
    inv = ROPE_BASE ** (-jnp.arange(half, dtype=F32) / half)
    ang = pos.astype(F32)[:, None] * inv[None, :]
    cos, sin = jnp.cos(ang), jnp.sin(ang)
    return jnp.concatenate([cos, cos], axis=1), jnp.concatenate([-sin, sin], axis=1)


def _retention_tables(c):
    lg = jnp.log1p(-jnp.exp2(-5.0 - jnp.arange(N_HEADS, dtype=F32)))
    idx = jnp.arange(c, dtype=F32)
    diff = idx[:, None] - idx[None, :]
    dmat = jnp.where(diff[None] >= 0, jnp.exp(jnp.maximum(diff, 0.0)[None] * lg[:, None, None]), 0.0)
    e1 = jnp.exp((idx[None, :] + 1.0) * lg[:, None])
    e2 = jnp.exp((c - 1.0 - idx)[None, :] * lg[:, None])
    e3 = jnp.broadcast_to(jnp.exp(c * lg)[:, None], (N_HEADS, c))
    dec = jnp.stack([e1, e2, e3] + [jnp.zeros_like(e1)] * 5, axis=-1)
    return dmat, dec


def kernel(x_prompt, x_sample, mem_prompt, state_ret, state_mlstm_c, state_mlstm_n, state_mlstm_m, state_s5_re, state_s5_im, state_hgrn, cache_mem_k, cache_mem_v, norm_w, ffn_w_in, ffn_w_out, w_in, ml_gate_bias, ml_norm_w, s5_lambda_re, s5_lambda_im, s5_log_step, s5_b_re, s5_b_im, s5_c_re, s5_c_im, s5_d, s5_w_glu, hg_lb_logits, hg_norm_w, w_branch, w_out, xa_mem_norm, xa_wq, xa_wkv, xa_wo):
    bsz, seqlen, d = x_prompt.shape
    bs = x_sample.shape[0]
    depth = norm_w.shape[0]
    mem_len = mem_prompt.shape[1]
    tp = bsz * seqlen
    gn = s5_lambda_re.shape[1] * s5_lambda_re.shape[2]
    seg_gate = OFF_MO

    x = jnp.concatenate([x_prompt.reshape(tp, d), x_sample.reshape(bs, d)], axis=0)
    mem = mem_prompt.reshape(bsz * mem_len, d)
    s5_in_re = state_s5_re.reshape(depth, bs, gn)
    s5_in_im = state_s5_im.reshape(depth, bs, gn)

    cos_p, sin_p = _rope_tables(jnp.arange(seqlen, dtype=jnp.int32))
    cos_s, sin_s = _rope_tables(PAST_LEN + jnp.arange(1, dtype=jnp.int32))
    dmat, dec = _retention_tables(CHUNK)
    _, dec1 = _retention_tables(1)
    gamma = dec1[:, 0, 0]

    zf = lambda *shape: jnp.zeros(shape, F32)
    t = tp + bs
    p_ret = zf(depth, bsz, N_HEADS, DK, DV)
    p_hg = zf(depth, bsz, N_HEADS, DK, DV)
    p_ml = [zf(depth, bsz, N_HEADS, DK, DV), zf(depth, bsz, N_HEADS, 1, DK), zf(depth, bsz, N_HEADS, 1, 1)]
    p_s5 = [zf(depth, bsz, 1, gn), zf(depth, bsz, 1, gn)]
    kvbuf = zf(depth, 2, bsz * mem_len, d)
    s_ret = s_hg = None
    s_ml = [None, zf(depth, bs, N_HEADS, DK), zf(depth, bs, N_HEADS)]
    s_s5 = [zf(depth, bs, gn), zf(depth, bs, gn)]
    hbuf = jnp.zeros((t, N_BRANCH * _HW), BF16)
    o = jnp.zeros((t, d), BF16)
    for l in range(depth):
        nw = norm_w[l]
        x = _ffn(x, nw[0:2], ffn_w_in, l, 0, ffn_w_out[l, 0].astype(BF16))
        wl = w_in[l]
        n_gate = 2 * N_HEADS
        w_main = jnp.concatenate([wl[:, :seg_gate], wl[:, seg_gate + n_gate:]], axis=1).astype(BF16)
        w_gate = jnp.pad(wl[:, seg_gate:seg_gate + n_gate], ((0, 0), (0, GATE_PAD - n_gate))).astype(BF16)
        u, ug = _in_proj(x, nw[2:3], w_main, w_gate)
        ugt = ug[:tp, :n_gate].T
        mlw = ml_norm_w[l].reshape(1, -1)
        hgw = hg_norm_w[l].reshape(1, -1)
        ar, ai, bbr, bbi = _s5_prep(s5_lambda_re[l], s5_lambda_im[l], s5_log_step[l], s5_b_re[l], s5_b_im[l])
        ar, ai = ar.reshape(1, gn), ai.reshape(1, gn)
        bblk, cblk = _s5_weights(bbr, bbi, s5_c_re[l], s5_c_im[l])
        s5d = s5_d[l].astype(F32).reshape(1, -1)
        wglu = s5_w_glu[l].astype(BF16)
        hbuf, p_ret = _ret_prompt(u, hbuf, bsz, seqlen, cos_p, sin_p, dmat, dec, l, depth, p_ret)
        hbuf, s_ret = _ret_sample(u, tp, gamma, cos_s, sin_s, state_ret, l, hbuf, s_ret)
        hbuf, *p_ml = _mlstm_prompt(u, hbuf, ug, ugt, ml_gate_bias[l], mlw, bsz, seqlen, l, depth, p_ml)
        hbuf, *s_ml = _mlstm_sample(u, ug, tp, ml_gate_bias[l], mlw, state_mlstm_c, state_mlstm_n,
                                    state_mlstm_m, l, hbuf, s_ml)
        hbuf, *p_s5 = _s5_prompt(u, hbuf, ar, ai, bblk, cblk, s5d, wglu, bsz, seqlen, l, depth, p_s5)
        hbuf, *s_s5 = _s5_sample(u, tp, ar, ai, bblk, cblk, s5d, wglu, s5_in_re, s5_in_im, l, hbuf, s_s5)
        hbuf, p_hg = _hgrn_prompt(u, hbuf, hg_lb_logits, hgw, bsz, seqlen, l, p_hg)
        hbuf, s_hg = _hgrn_sample(u, tp, hg_lb_logits, hgw, state_hgrn, l, hbuf, s_hg)
        x = _merge(hbuf, u, w_branch[l].astype(BF16), w_out[l].astype(BF16), x, nw[3:4])
        kvbuf = _mem_kv(mem, xa_mem_norm[l].reshape(1, d), xa_wkv[l].astype(BF16), l, depth, kvbuf)
        q = _norm_matmul(x, nw[4:5], xa_wq[l].astype(BF16), "xa_q")
        o = _attn_prompt(q, kvbuf, o, bsz, seqlen, l)
        o_s = _attn_sample(q[tp:].reshape(bs, N_HEADS, d // N_HEADS), cache_mem_k, cache_mem_v, l)
        o = lax.dynamic_update_slice(o, o_s.reshape(bs, d).astype(BF16), (tp, 0))
        x = _proj_res(o, xa_wo[l].astype(BF16), x, nw[5:6])
        x = _ffn(x, nw[6:8], ffn_w_in, l, 1, ffn_w_out[l, 1].astype(BF16))

    mem_shape = (depth, bsz, mem_len, N_HEADS, d // N_HEADS)
    s5_shape = (depth, -1, gn // S5_N, S5_N)
    return (x[:tp].reshape(bsz, seqlen, d), x[tp:].reshape(bs, 1, d),
            p_ret, p_ml[0], p_ml[1].reshape(depth, bsz, N_HEADS, DK), p_ml[2].reshape(depth, bsz, N_HEADS),
            p_s5[0].reshape(s5_shape), p_s5[1].reshape(s5_shape), p_hg,
            kvbuf[:, 0].reshape(mem_shape), kvbuf[:, 1].reshape(mem_shape),
            s_ret, s_ml[0], s_ml[1], s_ml[2], s_s5[0].reshape(s5_shape), s_s5[1].reshape(s5_shape), s_hg)
```

```python
import functools
import math

import jax
import jax.numpy as jnp
from jax import lax
from jax.experimental import pallas as pl
from jax.experimental.pallas import tpu as pltpu

F32 = jnp.float32
BF16 = jnp.bfloat16

CHUNK = 128
N_BRANCH = 4
N_HEADS = 4
DK = 128
DV = 256
S5_GROUP = 16
S5_N = 64
ROPE_BASE = 10000.0
PAST_LEN = 16384

V7X_VMEM_BYTES = 64 * 1024 * 1024
MXU_K = 256
SUBLANES = 8
MIB = 1024 * 1024

OFF_RQ, OFF_RK, OFF_RV, OFF_RG = 0, 512, 1024, 2048
OFF_MQ, OFF_MK, OFF_MV, OFF_MO = 3072, 3584, 4096, 5120
OFF_SU = 6144
OFF_HQ, OFF_HF, OFF_HI, OFF_HG = 7168, 7680, 8192, 9216
OFF_GATES = 10240
N_MAIN = 18432
GATE_PAD = 128


def _cparams(sem, vmem_mib):
    return pltpu.CompilerParams(dimension_semantics=sem, vmem_limit_bytes=vmem_mib * MIB)


_ANY = pl.BlockSpec(memory_space=pl.ANY)


def _pcall(kernel_fn, inputs, in_specs, carried=None, **kw):
    carried = {o: a for o, a in (carried or {}).items() if a is not None}
    idxs = sorted(carried)
    n_in = len(inputs)

    def body(*refs):
        return kernel_fn(*refs[:n_in], *refs[n_in + len(idxs):])

    return pl.pallas_call(
        body if idxs else kernel_fn,
        in_specs=list(in_specs) + [_ANY] * len(idxs),
        input_output_aliases={n_in + j: o for j, o in enumerate(idxs)},
        **kw)(*inputs, *[carried[o] for o in idxs])


def _tile(n, cap, mult=16):
    best = None
    for t in range(mult, min(n, cap) + 1, mult):
        if n % t == 0:
            best = t
    return best if best is not None else n


def _dot(a, b):
    return jnp.dot(a, b, preferred_element_type=F32)


def _dot_nt(a, b):
    return lax.dot_general(a, b, (((1,), (1,)), ((), ())), preferred_element_type=F32)


def _rms(x, w, eps=1e-6):
    return x * lax.rsqrt(jnp.mean(x * x, axis=-1, keepdims=True) + eps) * w


def _sigmoid(x):
    return 1.0 / (1.0 + jnp.exp(-x))


def _silu(x):
    return x * _sigmoid(x)


def _log_sigmoid(x):
    return jnp.minimum(x, 0.0) - jnp.log1p(jnp.exp(-jnp.abs(x)))


def _logaddexp(a, b):
    return jnp.maximum(a, b) + jnp.log1p(jnp.exp(-jnp.abs(a - b)))


def _gelu_tanh(x):
    return 0.5 * x * (1.0 + jnp.tanh(math.sqrt(2.0 / math.pi) * (x + 0.044715 * (x * x * x))))


def _row_to_col(row):
    n = row.shape[1]
    r = lax.broadcasted_iota(jnp.int32, (n, n), 0)
    c = lax.broadcasted_iota(jnp.int32, (n, n), 1)
    return jnp.sum(jnp.where(r == c, row, 0.0), axis=1, keepdims=True)


def _cumsum_rows(x):
    c = x.shape[0]
    r = lax.broadcasted_iota(jnp.int32, (c, c), 0)
    k = lax.broadcasted_iota(jnp.int32, (c, c), 1)
    tri = jnp.where(r >= k, 1.0, 0.0).astype(BF16)
    hi = x.astype(BF16)
    r1 = x - hi.astype(F32)
    mid = r1.astype(BF16)
    lo = (r1 - mid.astype(F32)).astype(BF16)
    return _dot(tri, hi) + _dot(tri, mid) + _dot(tri, lo)


def _ffn_up_kernel(x_ref, nw_ref, wg_ref, wu_ref, h_ref, xn_ref, *, rc):
    @pl.when(pl.program_id(1) == 0)
    def _():
        xn_ref[...] = _rms(x_ref[...], nw_ref[...]).astype(BF16)

    wg = wg_ref[0, 0].astype(BF16)
    wu = wu_ref[0, 0].astype(BF16)
    for r in range(0, x_ref.shape[0], rc):
        xn = xn_ref[r:r + rc, :]
        h_ref[r:r + rc, :] = (_silu(_dot(xn, wg)) * _dot(xn, wu)).astype(h_ref.dtype)


def _ffn_down_kernel(h_ref, w_ref, x_ref, nw_ref, o_ref, *, rc):
    for r in range(0, x_ref.shape[0], rc):
        rows = slice(r, r + rc)
        o_ref[rows, :] = x_ref[rows, :] + 0.5 * _rms(_dot(h_ref[rows, :], w_ref[0, 0]), nw_ref[...])


def _ffn(x, nw2, w_in, layer, which, w_out):
    t, d = x.shape
    dff = w_out.shape[2]
    tm = _tile(t, 2080)
    rc = _tile(tm, 260)
    tf = _tile(dff, 512, 128)
    nf = dff // tf
    once = pl.Buffered(1)
    hidden = pl.pallas_call(
        functools.partial(_ffn_up_kernel, rc=rc),
        grid=(t // tm, nf),
        in_specs=[pl.BlockSpec((tm, d), lambda i, f: (i, 0), pipeline_mode=once),
                  pl.BlockSpec((1, d), lambda i, f: (0, 0)),
                  pl.BlockSpec((1, 1, d, tf), lambda i, f: (layer, which, 0, f)),
                  pl.BlockSpec((1, 1, d, tf), lambda i, f: (layer, which, 0, nf + f))],
        out_specs=pl.BlockSpec((tm, tf), lambda i, f: (i, f)),
        out_shape=jax.ShapeDtypeStruct((t, dff), BF16),
        scratch_shapes=[pltpu.VMEM((tm, d), BF16)],
        compiler_params=_cparams(("parallel", "arbitrary"), 56),
        name="ffn_up",
    )(x, nw2[0:1], w_in, w_in)
    tm2 = _tile(t, 416)
    return pl.pallas_call(
        functools.partial(_ffn_down_kernel, rc=_tile(tm2, 208)),
        grid=(t // tm2,),
        in_specs=[pl.BlockSpec((tm2, dff), lambda i: (i, 0)),
                  pl.BlockSpec((1, 1, dff, d), lambda i: (layer, which, 0, 0), pipeline_mode=once),
                  pl.BlockSpec((tm2, d), lambda i: (i, 0)),
                  pl.BlockSpec((1, d), lambda i: (0, 0))],
        out_specs=pl.BlockSpec((tm2, d), lambda i: (i, 0)),
        out_shape=jax.ShapeDtypeStruct((t, d), F32),
        compiler_params=_cparams(("parallel",), 56),
        name="ffn_down",
    )(hidden, w_out, x, nw2[1:2])


def _norm_matmul_kernel(x_ref, nw_ref, w_ref, o_ref, xn_ref):
    @pl.when(pl.program_id(1) == 0)
    def _():
        xn_ref[...] = _rms(x_ref[...], nw_ref[...]).astype(BF16)

    o_ref[...] = _dot(xn_ref[...], w_ref[0]).astype(o_ref.dtype)


def _norm_matmul(x, nw, w, layer, name):
    t, d = x.shape
    n = w.shape[2]
    tm = _tile(t, 1040)
    tn = _tile(n, 1024, 128)
    return pl.pallas_call(
        _norm_matmul_kernel,
        grid=(t // tm, n // tn),
        in_specs=[pl.BlockSpec((tm, d), lambda i, j: (i, 0)),
                  pl.BlockSpec((1, d), lambda i, j: (0, 0)),
                  pl.BlockSpec((1, d, tn), lambda i, j: (layer, 0, j))],
        out_specs=pl.BlockSpec((tm, tn), lambda i, j: (i, j)),
        out_shape=jax.ShapeDtypeStruct((t, n), F32),
        scratch_shapes=[pltpu.VMEM((tm, d), BF16)],
        compiler_params=_cparams(("parallel", "arbitrary"), 48),
        name=name,
    )(x, nw, w)


def _mem_kv_kernel(x_ref, nw_ref, w_ref, o_ref, xn_ref):
    @pl.when(pl.program_id(1) == 0)
    def _():
        xn_ref[...] = _rms(x_ref[...], nw_ref[...]).astype(BF16)

    o_ref[0, 0] = _dot(xn_ref[...], w_ref[0])


def _mem_kv(mem, nw, w, layer, depth, prev):
    t, d = mem.shape
    tm = _tile(t, 512)
    tn = _tile(d, 1024, 128)
    npk = d // tn
    return _pcall(
        _mem_kv_kernel, (mem, nw, w),
        [pl.BlockSpec((tm, d), lambda i, j: (i, 0)),
         pl.BlockSpec((1, d), lambda i, j: (0, 0)),
         pl.BlockSpec((1, d, tn), lambda i, j: (layer, 0, j))],
        carried={0: prev},
        grid=(t // tm, 2 * npk),
        out_specs=pl.BlockSpec((1, 1, tm, tn), lambda i, j: (layer, j // npk, i, j % npk)),
        out_shape=jax.ShapeDtypeStruct((depth, 2, t, d), F32),
        scratch_shapes=[pltpu.VMEM((tm, d), BF16)],
        compiler_params=_cparams(("parallel", "arbitrary"), 48),
        name="mem_kv",
    )


def _repack_kernel(a_ref, b_ref, o_ref, og_ref, *, n_plain, n_gate):
    j = pl.program_id(0)
    tn = o_ref.shape[1]

    @pl.when(j < n_plain)
    def _():
        o_ref[...] = a_ref[0].astype(BF16)

    @pl.when(j >= n_plain)
    def _():
        lane = lax.broadcasted_iota(jnp.int32, o_ref.shape, 1)
        ra = pltpu.roll(a_ref[0], tn - n_gate, 1)
        rb = pltpu.roll(b_ref[0], tn - n_gate, 1)
        o_ref[...] = jnp.where(lane < tn - n_gate, ra, rb).astype(BF16)

    @pl.when(j == n_plain)
    def _():
        g = a_ref[0, :, :GATE_PAD]
        lane = lax.broadcasted_iota(jnp.int32, g.shape, 1)
        og_ref[...] = jnp.where(lane < n_gate, g, 0.0).astype(BF16)


def _repack_w_in(w_in, layer, gate_off, n_gate):
    _, d, n = w_in.shape
    n_main = n - n_gate
    tn = _tile(math.gcd(n_main, gate_off), 512, 128)
    n_plain = gate_off // tn
    return pl.pallas_call(
        functools.partial(_repack_kernel, n_plain=n_plain, n_gate=n_gate),
        grid=(n_main // tn,),
        in_specs=[pl.BlockSpec((1, d, tn), lambda j: (layer, 0, j)),
                  pl.BlockSpec((1, d, tn), lambda j: (layer, 0, j + 1))],
        out_specs=[pl.BlockSpec((d, tn), lambda j: (0, j)),
                   pl.BlockSpec((d, GATE_PAD), lambda j: (0, 0))],
        out_shape=[jax.ShapeDtypeStruct((d, n_main), BF16), jax.ShapeDtypeStruct((d, GATE_PAD), BF16)],
        compiler_params=_cparams(("arbitrary",), 48),
        name="repack_w_in",
    )(w_in, w_in)


def _in_proj_kernel(x_ref, nw_ref, w_ref, wg_ref, o_ref, og_ref, xn_ref):
    @pl.when(pl.program_id(1) == 0)
    def _():
        xn = _rms(x_ref[...], nw_ref[...]).astype(BF16)
        xn_ref[...] = xn
        og_ref[...] = _dot(xn, wg_ref[...])

    o_ref[...] = _dot(xn_ref[...], w_ref[...])


def _in_proj(x, nw, w_main, w_gate):
    t, d = x.shape
    n = w_main.shape[1]
    tm = _tile(t, 1040)
    tn = _tile(n, 1024, 128)
    return pl.pallas_call(
        _in_proj_kernel,
        grid=(t // tm, n // tn),
        in_specs=[pl.BlockSpec((tm, d), lambda i, j: (i, 0)),
                  pl.BlockSpec((1, d), lambda i, j: (0, 0)),
                  pl.BlockSpec((d, tn), lambda i, j: (0, j)),
                  pl.BlockSpec((d, GATE_PAD), lambda i, j: (0, 0))],
        out_specs=[pl.BlockSpec((tm, tn), lambda i, j: (i, j)),
                   pl.BlockSpec((tm, GATE_PAD), lambda i, j: (i, 0))],
        out_shape=[jax.ShapeDtypeStruct((t, n), F32), jax.ShapeDtypeStruct((t, GATE_PAD), F32)],
        scratch_shapes=[pltpu.VMEM((tm, d), BF16)],
        compiler_params=_cparams(("parallel", "arbitrary"), 48),
        name="in_proj",
    )(x, nw, w_main, w_gate)


def _merge_kernel(*refs):
    h_refs, g_refs = refs[:N_BRANCH], refs[N_BRANCH:2 * N_BRANCH]
    wb_ref, wo_ref, x_ref, nw_ref, o_ref = refs[2 * N_BRANCH:]
    merged = None
    for k in range(N_BRANCH):
        term = _sigmoid(g_refs[k][...]) * _dot(h_refs[k][...], wb_ref[0, k])
        merged = term if merged is None else merged + term
    o_ref[...] = x_ref[...] + _rms(_dot(merged.astype(BF16), wo_ref[0]), nw_ref[...])


def _merge(hbuf, u, w_branch, w_out, layer, x, nw):
    t, d = x.shape
    bw = hbuf.shape[1] // N_BRANCH
    tm = _tile(t, 208)
    gate_blk = OFF_GATES // d
    once = pl.Buffered(1)
    return pl.pallas_call(
        _merge_kernel,
        grid=(t // tm,),
        in_specs=[pl.BlockSpec((tm, bw), lambda i, k=k: (i, k)) for k in range(N_BRANCH)] + [
            pl.BlockSpec((tm, d), lambda i, k=k: (i, gate_blk + k)) for k in range(N_BRANCH)] + [
            pl.BlockSpec((1,) + w_branch.shape[1:], lambda i: (layer, 0, 0, 0), pipeline_mode=once),
            pl.BlockSpec((1,) + w_out.shape[1:], lambda i: (layer, 0, 0), pipeline_mode=once),
            pl.BlockSpec((tm, d), lambda i: (i, 0)),
            pl.BlockSpec((1, d), lambda i: (0, 0))],
        out_specs=pl.BlockSpec((tm, d), lambda i: (i, 0)),
        out_shape=jax.ShapeDtypeStruct((t, d), F32),
        compiler_params=_cparams(("parallel",), 56),
        name="merge",
    )(*([hbuf] * N_BRANCH), *([u] * N_BRANCH), w_branch, w_out, x, nw)


def _proj_res_kernel(a_ref, w_ref, x_ref, nw_ref, o_ref):
    o_ref[...] = x_ref[...] + _rms(_dot(a_ref[...], w_ref[0]), nw_ref[...])


def _proj_res(a, w, layer, x, nw):
    t, d = x.shape
    tm = _tile(t, 640)
    return pl.pallas_call(
        _proj_res_kernel,
        grid=(t // tm,),
        in_specs=[pl.BlockSpec((tm, d), lambda i: (i, 0)),
                  pl.BlockSpec((1, d, d), lambda i: (layer, 0, 0), pipeline_mode=pl.Buffered(1)),
                  pl.BlockSpec((tm, d), lambda i: (i, 0)),
                  pl.BlockSpec((1, d), lambda i: (0, 0))],
        out_specs=pl.BlockSpec((tm, d), lambda i: (i, 0)),
        out_shape=jax.ShapeDtypeStruct((t, d), F32),
        compiler_params=_cparams(("parallel",), 56),
        name="xa_out",
    )(a, w, x, nw)


def _rope(x, cs, sn):
    return x * cs + pltpu.roll(x, x.shape[1] // 2, 1) * sn


def _group_norm(o, eps=1e-5):
    oc = o - jnp.mean(o, axis=-1, keepdims=True)
    return oc * lax.rsqrt(jnp.mean(oc * oc, axis=-1, keepdims=True) + eps)


def _ret_kernel(q_ref, k_ref, v_ref, g_ref, cos_ref, sin_ref, dm_ref, dec_ref, h_ref, st_ref, s_ref, *, nc):
    c = pl.program_id(1)

    @pl.when(c == 0)
    def _():
        s_ref[...] = jnp.zeros_like(s_ref)

    cs, sn = cos_ref[...], sin_ref[...]
    for h in range(N_HEADS):
        kc, vc = slice(h * DK, (h + 1) * DK), slice(h * DV, (h + 1) * DV)
        qr = _rope(q_ref[:, kc], cs, sn)
        kr = _rope(k_ref[:, kc], cs, sn) * (DK ** -0.5)
        dec = dec_ref[h]
        e1, e2, e3 = dec[:, 0:1], dec[:, 1:2], dec[0:1, 2:3]
        vb = v_ref[:, vc].astype(BF16)
        qb = qr.astype(BF16)
        s = s_ref[h]
        sc = _dot_nt(qb, kr.astype(BF16)) * dm_ref[h]
        o = _dot(sc.astype(BF16), vb) + _dot(qb, s.astype(BF16)) * e1
        s_ref[h] = e3 * s + _dot((kr * e2).T.astype(BF16), vb)
        h_ref[:, vc] = (_group_norm(o) * _silu(g_ref[:, vc])).astype(h_ref.dtype)

    @pl.when(c == nc - 1)
    def _():
        st_ref[...] = s_ref[...].reshape(st_ref.shape)


def _seg_spec(nc, width, off):
    return pl.BlockSpec((CHUNK, width), lambda b, c: (b * nc + c, off // width))


def _layer_state_spec(layer, tail):
    return pl.BlockSpec((1, 1) + tail, lambda b, c: (layer, b) + (0,) * len(tail))


def _const_spec(shape):
    return pl.BlockSpec(shape, lambda b, c: (0,) * len(shape))


_HW, _KW = N_HEADS * DV, N_HEADS * DK


BR_RET, BR_ML, BR_S5, BR_HG = range(N_BRANCH)


def _ret_prompt(u, hbuf, bsz, seqlen, cos2, sin2, dmat, dec, layer, depth, prev):
    nc = seqlen // CHUNK
    return _pcall(
        functools.partial(_ret_kernel, nc=nc), (u, u, u, u, cos2, sin2, dmat, dec),
        [_seg_spec(nc, _KW, OFF_RQ), _seg_spec(nc, _KW, OFF_RK), _seg_spec(nc, _HW, OFF_RV),
         _seg_spec(nc, _HW, OFF_RG),
         pl.BlockSpec((CHUNK, DK), lambda b, c: (c, 0)),
         pl.BlockSpec((CHUNK, DK), lambda b, c: (c, 0)),
         _const_spec(dmat.shape), _const_spec(dec.shape)],
        carried={0: hbuf, 1: prev},
        grid=(bsz, nc),
        out_specs=[pl.BlockSpec((CHUNK, _HW), lambda b, c: (b * nc + c, BR_RET)),
                   _layer_state_spec(layer, (N_HEADS, DK, DV))],
        out_shape=[jax.ShapeDtypeStruct(hbuf.shape, BF16),
                   jax.ShapeDtypeStruct((depth, bsz, N_HEADS, DK, DV), F32)],
        scratch_shapes=[pltpu.VMEM((N_HEADS, DK, DV), F32)],
        compiler_params=_cparams(("parallel", "arbitrary"), 32),
        name="ret_prompt",
    )


def _mlstm_kernel(gb_ref, q_ref, k_ref, v_ref, mo_ref, ug_ref, ugt_ref, nw_ref,
                  h_ref, c_out, n_out, m_out, cm_ref, nv_ref, m_ref, *, nc):
    c = pl.program_id(1)

    @pl.when(c == 0)
    def _():
        cm_ref[...] = jnp.zeros_like(cm_ref)
        nv_ref[...] = jnp.zeros_like(nv_ref)
        m_ref[...] = jnp.zeros_like(m_ref)

    n = CHUNK
    row = lax.broadcasted_iota(jnp.int32, (n, n), 0)
    col = lax.broadcasted_iota(jnp.int32, (n, n), 1)
    causal = row >= col
    ug = ug_ref[...]
    for h in range(N_HEADS):
        kc, vc = slice(h * DK, (h + 1) * DK), slice(h * DV, (h + 1) * DV)
        bias_i = gb_ref[0, h]
        bias_f = gb_ref[1, h]
        ig_row = ugt_ref[h:h + 1, :] + bias_i
        lf_row = _log_sigmoid(ugt_ref[N_HEADS + h:N_HEADS + h + 1, :] + bias_f)
        ig_col = ug[:, h:h + 1] + bias_i
        lf_col = _log_sigmoid(ug[:, N_HEADS + h:N_HEADS + h + 1] + bias_f)
        b_col = jnp.sum(jnp.where(causal, lf_row, 0.0), axis=1, keepdims=True)
        b_row = jnp.sum(jnp.where(row <= col, lf_col, 0.0), axis=0, keepdims=True)
        b_last = b_col[n - 1:n, :]
        m = m_ref[h]
        logd = jnp.where(causal, b_col - b_row + ig_row, -jnp.inf)
        inter = m + b_col
        mi = jnp.maximum(inter, jnp.max(logd, axis=1, keepdims=True))
        w = jnp.exp(logd - mi)
        wi = jnp.exp(inter - mi)
        q = q_ref[:, kc]
        k = k_ref[:, kc] * (DK ** -0.5)
        qb = q.astype(BF16)
        vb = v_ref[:, vc].astype(BF16)
        cm = cm_ref[h]
        nv = nv_ref[h]
        a = _dot_nt(qb, k.astype(BF16)) * w
        num = _dot(a.astype(BF16), vb) + wi * _dot(qb, cm.astype(BF16))
        nq = jnp.sum(a, axis=1, keepdims=True) + wi * jnp.sum(q * nv, axis=1, keepdims=True)
        hh = num / jnp.maximum(jnp.abs(nq), jnp.exp(-mi))
        m_new = mi[n - 1:n, :]
        wl = jnp.exp(b_last - b_col + ig_col - m_new)
        dp = jnp.exp(m + b_last - m_new)
        kw = k * wl
        cm_ref[h] = dp * cm + _dot(kw.T.astype(BF16), vb)
        nv_ref[h] = dp * nv + jnp.sum(kw, axis=0, keepdims=True)
        m_ref[h] = m_new
        h_ref[:, vc] = (_rms(hh, nw_ref[:, vc]) * _sigmoid(mo_ref[:, vc])).astype(h_ref.dtype)

    @pl.when(c == nc - 1)
    def _():
        c_out[...] = cm_ref[...].reshape(c_out.shape)
        n_out[...] = nv_ref[...].reshape(n_out.shape)
        m_out[...] = m_ref[...].reshape(m_out.shape)


def _mlstm_prompt(u, hbuf, ug, ugt, gate_bias, norm_w, bsz, seqlen, layer, depth, prev):
    nc = seqlen // CHUNK
    return _pcall(
        functools.partial(_mlstm_kernel, nc=nc), (gate_bias, u, u, u, u, ug, ugt, norm_w),
        [pl.BlockSpec(memory_space=pltpu.SMEM),
         _seg_spec(nc, _KW, OFF_MQ), _seg_spec(nc, _KW, OFF_MK), _seg_spec(nc, _HW, OFF_MV),
         _seg_spec(nc, _HW, OFF_MO),
         pl.BlockSpec((CHUNK, GATE_PAD), lambda b, c: (b * nc + c, 0)),
         pl.BlockSpec((2 * N_HEADS, CHUNK), lambda b, c: (0, b * nc + c)),
         _const_spec((1, _HW))],
        carried={0: hbuf, 1: prev[0], 2: prev[1], 3: prev[2]},
        grid=(bsz, nc),
        out_specs=[pl.BlockSpec((CHUNK, _HW), lambda b, c: (b * nc + c, BR_ML)),
                   _layer_state_spec(layer, (N_HEADS, DK, DV)),
                   _layer_state_spec(layer, (N_HEADS, 1, DK)),
                   _layer_state_spec(layer, (N_HEADS, 1, 1))],
        out_shape=[jax.ShapeDtypeStruct(hbuf.shape, BF16),
                   jax.ShapeDtypeStruct((depth, bsz, N_HEADS, DK, DV), F32),
                   jax.ShapeDtypeStruct((depth, bsz, N_HEADS, 1, DK), F32),
                   jax.ShapeDtypeStruct((depth, bsz, N_HEADS, 1, 1), F32)],
        scratch_shapes=[pltpu.VMEM((N_HEADS, DK, DV), F32), pltpu.VMEM((N_HEADS, 1, DK), F32),
                        pltpu.VMEM((N_HEADS, 1, 1), F32)],
        compiler_params=_cparams(("parallel", "arbitrary"), 32),
        name="mlstm_prompt",
    )


def _hgrn_lower_bound(logits, layer):
    e = jnp.exp(logits - jnp.max(logits, axis=0, keepdims=True))
    p = e / jnp.sum(e, axis=0, keepdims=True)
    lb = jnp.zeros_like(p[0:1, :])
    for r in range(1, layer + 1):
        lb = lb + p[r:r + 1, :]
    return lb


def _hgrn_gates(fpre, lb):
    hlf = _logaddexp(jnp.log(lb), jnp.log1p(-lb) + _log_sigmoid(fpre))
    hk = (1.0 - lb) * _sigmoid(-fpre)
    return hlf, hk


def _hgrn_kernel(lg_ref, q_ref, f_ref, i_ref, g_ref, nw_ref, h_ref, st_ref, s_ref, *, nc, layer):
    c = pl.program_id(1)

    @pl.when(c == 0)
    def _():
        s_ref[...] = jnp.zeros_like(s_ref)

    n = CHUNK
    lb_all = _hgrn_lower_bound(lg_ref[...], layer)
    r2 = lax.broadcasted_iota(jnp.int32, (n, n), 0)
    c2 = lax.broadcasted_iota(jnp.int32, (n, n), 1)
    rowv = lax.broadcasted_iota(jnp.int32, (n, DK), 0)
    n_lev = n.bit_length() - 1
    rights = [(rowv & (1 << lev)) != 0 for lev in range(n_lev)]
    pair_masks = [((r2 >> (lev + 1)) == (c2 >> (lev + 1))) & ((r2 & (1 << lev)) != 0) & ((c2 & (1 << lev)) == 0)
                  for lev in range(n_lev)]
    for h in range(N_HEADS):
        kc, vc = slice(h * DK, (h + 1) * DK), slice(h * DV, (h + 1) * DV)
        hlf, hk = _hgrn_gates(f_ref[:, kc], lb_all[:, kc])
        q = _silu(q_ref[:, kc])
        ib = i_ref[:, vc].astype(BF16)
        bc = _cumsum_rows(hlf)
        a = jnp.where(r2 == c2, _dot_nt(q.astype(BF16), hk.astype(BF16)), 0.0)
        p = bc
        for lev in range(n_lev):
            s = 1 << lev
            right = rights[lev]
            nxt = pltpu.roll(p, n - s, 0)
            e = jnp.exp(jnp.where(right, bc - p, nxt - bc))
            m = (jnp.where(right, q, hk) * e).astype(BF16)
            a = jnp.where(pair_masks[lev], _dot_nt(m, m), a)
            if lev + 1 < n_lev:
                p = jnp.where(right, pltpu.roll(p, s, 0), p)
        st = s_ref[h]
        o = _dot(a.astype(BF16), ib) + _dot((q * jnp.exp(bc)).astype(BF16), st.astype(BF16))
        bl = bc[n - 1:n, :]
        s_ref[h] = _row_to_col(jnp.exp(bl)) * st + _dot((hk * jnp.exp(bl - bc)).T.astype(BF16), ib)
        h_ref[:, vc] = (_rms(o, nw_ref[:, vc]) * _silu(g_ref[:, vc])).astype(h_ref.dtype)

    @pl.when(c == nc - 1)
    def _():
        st_ref[...] = s_ref[...].reshape(st_ref.shape)


def _hgrn_prompt(u, hbuf, logits, norm_w, bsz, seqlen, layer, prev):
    nc = seqlen // CHUNK
    depth = logits.shape[0]
    return _pcall(
        functools.partial(_hgrn_kernel, nc=nc, layer=layer), (logits, u, u, u, u, norm_w),
        [_const_spec(logits.shape),
         _seg_spec(nc, _KW, OFF_HQ), _seg_spec(nc, _KW, OFF_HF), _seg_spec(nc, _HW, OFF_HI),
         _seg_spec(nc, _HW, OFF_HG), _const_spec((1, _HW))],
        carried={0: hbuf, 1: prev},
        grid=(bsz, nc),
        out_specs=[pl.BlockSpec((CHUNK, _HW), lambda b, c: (b * nc + c, BR_HG)),
                   _layer_state_spec(layer, (N_HEADS, DK, DV))],
        out_shape=[jax.ShapeDtypeStruct(hbuf.shape, BF16),
                   jax.ShapeDtypeStruct((depth, bsz, N_HEADS, DK, DV), F32)],
        scratch_shapes=[pltpu.VMEM((N_HEADS, DK, DV), F32)],
        compiler_params=_cparams(("parallel", "arbitrary"), 32),
        name="hgrn_prompt",
    )


def _s5_prep_kernel(lr_ref, li_ref, dt_ref, bre_ref, bim_ref, ar_ref, ai_ref, bbr_ref, bbi_ref):
    lr, li, dt = lr_ref[...], li_ref[...], dt_ref[...]
    mag = jnp.exp(lr * dt)
    ar = mag * jnp.cos(li * dt)
    ai = mag * jnp.sin(li * dt)
    den = lr * lr + li * li
    cr = ((ar - 1.0) * lr + ai * li) / den
    ci = (ai * lr - (ar - 1.0) * li) / den
    ar_ref[...] = ar
    ai_ref[...] = ai
    bbr_ref[...] = cr * bre_ref[...] - ci * bim_ref[...]
    bbi_ref[...] = cr * bim_ref[...] + ci * bre_ref[...]


def _s5_prep(lam_re, lam_im, log_step, b_re, b_im):
    g, n = lam_re.shape
    p = b_re.shape[-1]
    gn = g * n
    dt = jnp.repeat(jnp.exp(log_step.astype(F32)), n).reshape(gn, 1)
    col = lambda a: a.astype(F32).reshape(gn, 1)
    return pl.pallas_call(
        _s5_prep_kernel,
        out_shape=[jax.ShapeDtypeStruct((gn, 1), F32), jax.ShapeDtypeStruct((gn, 1), F32),
                   jax.ShapeDtypeStruct((gn, p), F32), jax.ShapeDtypeStruct((gn, p), F32)],
        name="s5_prep",
    )(col(lam_re), col(lam_im), dt, b_re.reshape(gn, p), b_im.reshape(gn, p))


def _s5_kernel(*refs, seq, nc, n_kt, sw):
    if seq:
        (u_ref, ar_ref, ai_ref, bblk_ref, cblk_ref, d_ref, wglu_ref,
         h_ref, sr_ref, si_ref, hr_scr, hi_scr) = refs
    else:
        (u_ref, ar_ref, ai_ref, bblk_ref, cblk_ref, d_ref, wglu_ref, h0r_ref, h0i_ref,
         h_ref, sr_ref, si_ref) = refs
    u = u_ref[...]
    ub = u.astype(BF16)
    rows = u.shape[0]
    if seq:
        c = pl.program_id(1)

        @pl.when(c == 0)
        def _():
            hr_scr[...] = jnp.zeros_like(hr_scr)
            hi_scr[...] = jnp.zeros_like(hi_scr)

    ys = []
    for kt in range(n_kt):
        lanes = slice(kt * sw, (kt + 1) * sw)
        bu = _dot(ub[:, kt * MXU_K:(kt + 1) * MXU_K], bblk_ref[kt])
        xr, xi = bu[:, :sw], bu[:, sw:]
        ar, ai = ar_ref[:, lanes], ai_ref[:, lanes]
        if seq:
            sub = lax.broadcasted_iota(jnp.int32, (SUBLANES, sw), 0)
            mr, mi = jnp.broadcast_to(ar, (SUBLANES, sw)), jnp.broadcast_to(ai, (SUBLANES, sw))
            tabr, tabi = mr, mi
            levels = []
            sft = 1
            while sft < SUBLANES:
                keep = sub >= sft
                levels.append((sft, jnp.where(keep, mr, 0.0), jnp.where(keep, mi, 0.0)))
                tr, ti = pltpu.roll(tabr, sft, 0), pltpu.roll(tabi, sft, 0)
                tabr, tabi = (jnp.where(keep, tabr * tr - tabi * ti, tabr),
                              jnp.where(keep, tabr * ti + tabi * tr, tabi))
                mr, mi = mr * mr - mi * mi, 2.0 * mr * mi
                sft *= 2
            cr, ci = hr_scr[:, lanes], hi_scr[:, lanes]
            slabs_r, slabs_i = [], []
            for j in range(rows // SUBLANES):
                rs = slice(j * SUBLANES, (j + 1) * SUBLANES)
                sr_, si_ = xr[rs, :], xi[rs, :]
                for sft, lr, li in levels:
                    qr, qi = pltpu.roll(sr_, sft, 0), pltpu.roll(si_, sft, 0)
                    sr_, si_ = sr_ + lr * qr - li * qi, si_ + lr * qi + li * qr
                crb, cib = jnp.broadcast_to(cr, (SUBLANES, sw)), jnp.broadcast_to(ci, (SUBLANES, sw))
                sr_, si_ = sr_ + tabr * crb - tabi * cib, si_ + tabr * cib + tabi * crb
                cr, ci = sr_[SUBLANES - 1:SUBLANES, :], si_[SUBLANES - 1:SUBLANES, :]
                slabs_r.append(sr_)
                slabs_i.append(si_)
            xr, xi = jnp.concatenate(slabs_r, axis=0), jnp.concatenate(slabs_i, axis=0)
            hr_scr[:, lanes] = cr
            hi_scr[:, lanes] = ci
        else:
            h0r, h0i = h0r_ref[0, :, lanes], h0i_ref[0, :, lanes]
            xr, xi = xr + ar * h0r - ai * h0i, xi + ar * h0i + ai * h0r
            sr_ref[0, :, lanes] = xr
            si_ref[0, :, lanes] = xi
        ys.append(_dot(jnp.concatenate([xr, xi], axis=1).astype(BF16), cblk_ref[kt]))
    y = jnp.concatenate(ys, axis=1) + d_ref[...] * u
    z = _gelu_tanh(y)
    h_ref[...] = (z * _sigmoid(_dot(z.astype(BF16), wglu_ref[...]))).astype(h_ref.dtype)
    if seq:
        @pl.when(c == nc - 1)
        def _():
            sr_ref[...] = hr_scr[...].reshape(sr_ref.shape)
            si_ref[...] = hi_scr[...].reshape(si_ref.shape)


def _s5_weights(bbr, bbi, c_re, c_im):
    gn, p = bbr.shape
    g = gn // S5_N
    gpt = MXU_K // p
    n_kt = g // gpt
    eye = jnp.eye(gpt, dtype=F32)

    def b_blk(bb):
        return jnp.einsum('kgnp,gh->kgphn', bb.reshape(n_kt, gpt, S5_N, p), eye).reshape(n_kt, gpt * p, gpt * S5_N)

    def c_blk(cc):
        return jnp.einsum('kgpn,gh->kgnhp', cc.astype(F32).reshape(n_kt, gpt, p, S5_N), eye).reshape(
            n_kt, gpt * S5_N, gpt * p)

    bblk = jnp.concatenate([b_blk(bbr), b_blk(bbi)], axis=2).astype(BF16)
    cblk = jnp.concatenate([c_blk(c_re), -c_blk(c_im)], axis=1).astype(BF16)
    return bblk, cblk


def _s5_prompt(u, hbuf, ar, ai, bblk, cblk, d, wglu, bsz, seqlen, layer, depth, prev):
    width = wglu.shape[0]
    n_kt = bblk.shape[0]
    sw = bblk.shape[2] // 2
    ct = _tile(seqlen, 256)
    nc = seqlen // ct
    gn = ar.shape[1]
    full = lambda shape: pl.BlockSpec(shape, lambda b, c: (0,) * len(shape))
    st_spec = pl.BlockSpec((1, 1, 1, gn), lambda b, c: (layer, b, 0, 0))
    st_shape = jax.ShapeDtypeStruct((depth, bsz, 1, gn), F32)
    return _pcall(
        functools.partial(_s5_kernel, seq=True, nc=nc, n_kt=n_kt, sw=sw), (u, ar, ai, bblk, cblk, d, wglu),
        [pl.BlockSpec((ct, width), lambda b, c: (b * nc + c, OFF_SU // width)),
         full((1, gn)), full((1, gn)), full(bblk.shape), full(cblk.shape),
         full((1, width)), full(wglu.shape)],
        carried={0: hbuf, 1: prev[0], 2: prev[1]},
        grid=(bsz, nc),
        out_specs=[pl.BlockSpec((ct, width), lambda b, c: (b * nc + c, BR_S5)), st_spec, st_spec],
        out_shape=[jax.ShapeDtypeStruct(hbuf.shape, BF16), st_shape, st_shape],
        scratch_shapes=[pltpu.VMEM((1, gn), F32), pltpu.VMEM((1, gn), F32)],
        compiler_params=_cparams(("parallel", "arbitrary"), 56),
        name="s5_prompt",
    )


def _s5_sample(u, tp, ar, ai, bblk, cblk, d, wglu, h0r, h0i, layer, h_buf, prev):
    width = wglu.shape[0]
    n_kt = bblk.shape[0]
    sw = bblk.shape[2] // 2
    depth, bs, gn = h0r.shape
    tb = _tile(bs, 128, 8)
    r0 = tp // tb
    full = lambda shape: pl.BlockSpec(shape, lambda i: (0,) * len(shape))
    st_spec = pl.BlockSpec((1, tb, gn), lambda i: (layer, i, 0))
    st_shape = jax.ShapeDtypeStruct((depth, bs, gn), F32)
    return _pcall(
        functools.partial(_s5_kernel, seq=False, nc=1, n_kt=n_kt, sw=sw),
        (u, ar, ai, bblk, cblk, d, wglu, h0r, h0i),
        [pl.BlockSpec((tb, width), lambda i: (r0 + i, OFF_SU // width)),
         full((1, gn)), full((1, gn)), full(bblk.shape), full(cblk.shape),
         full((1, width)), full(wglu.shape), st_spec, st_spec],
        carried={0: h_buf, 1: prev[0], 2: prev[1]},
        grid=(bs // tb,),
        out_specs=[pl.BlockSpec((tb, width), lambda i: (r0 + i, BR_S5)), st_spec, st_spec],
        out_shape=[jax.ShapeDtypeStruct(h_buf.shape, BF16), st_shape, st_shape],
        compiler_params=_cparams(("parallel",), 56),
        name="s5_sample",
    )


SAMPLE_BLOCK = 8


def _state_step(s, dcol, kcol, vrow, qcol):
    s_new = dcol * s + kcol * vrow
    return s_new, jnp.sum(qcol * s_new, axis=0, keepdims=True)


def _zero_other_layers(ref, slot):
    for l in range(ref.shape[0]):
        if l != slot:
            ref[l] = jnp.zeros(ref.shape[1:], ref.dtype)


def _ret_step_kernel(gam_ref, q_ref, k_ref, v_ref, g_ref, cos_ref, sin_ref, st_ref, h_ref, so_ref, o_scr, *, slot):
    _zero_other_layers(so_ref, slot)
    cs, sn = cos_ref[...], sin_ref[...]
    for h in range(N_HEADS):
        qr = _rope(q_ref[:, h * DK:(h + 1) * DK], cs, sn)
        kr = _rope(k_ref[:, h * DK:(h + 1) * DK], cs, sn) * (DK ** -0.5)
        for i in range(SAMPLE_BLOCK):
            s_new, o = _state_step(st_ref[0, i, h], gam_ref[h], _row_to_col(kr[i:i + 1, :]),
                                   v_ref[i:i + 1, h * DV:(h + 1) * DV], _row_to_col(qr[i:i + 1, :]))
            so_ref[slot, i, h] = s_new
            o_scr[i:i + 1, h * DV:(h + 1) * DV] = o
    for h in range(N_HEADS):
        cols = slice(h * DV, (h + 1) * DV)
        h_ref[:, cols] = (_group_norm(o_scr[:, cols]) * _silu(g_ref[:, cols])).astype(h_ref.dtype)


def _mlstm_step_kernel(gb_ref, q_ref, k_ref, v_ref, mo_ref, ug_ref, nw_ref, c_ref, n_ref, m_ref,
                       h_ref, co_ref, no_ref, mo_out_ref, o_scr, *, slot):
    _zero_other_layers(co_ref, slot)
    ug = ug_ref[...]
    m_all = m_ref[0]
    m_new_cols = []
    for h in range(N_HEADS):
        ig = ug[:, h:h + 1] + gb_ref[0, h]
        lf = _log_sigmoid(ug[:, N_HEADS + h:N_HEADS + h + 1] + gb_ref[1, h])
        m_old = m_all[:, h:h + 1]
        inter = m_old + lf
        mi = jnp.maximum(inter, ig)
        w = jnp.exp(ig - mi)
        wi = jnp.exp(inter - mi)
        m_new_cols.append(mi)
        lim = jnp.exp(-mi)
        qh = q_ref[:, h * DK:(h + 1) * DK]
        kh = k_ref[:, h * DK:(h + 1) * DK] * (DK ** -0.5)
        for i in range(SAMPLE_BLOCK):
            wk = w[i:i + 1, :] * kh[i:i + 1, :]
            c_new, num = _state_step(c_ref[0, i, h], wi[i:i + 1, :], _row_to_col(wk),
                                     v_ref[i:i + 1, h * DV:(h + 1) * DV], _row_to_col(qh[i:i + 1, :]))
            n_new = wi[i:i + 1, :] * n_ref[0, i, h:h + 1, :] + wk
            nq = jnp.sum(qh[i:i + 1, :] * n_new, axis=1, keepdims=True)
            co_ref[slot, i, h] = c_new
            no_ref[0, i, h:h + 1, :] = n_new
            o_scr[i:i + 1, h * DV:(h + 1) * DV] = num / jnp.maximum(jnp.abs(nq), lim[i:i + 1, :])
    mo_out_ref[0] = jnp.concatenate(m_new_cols, axis=1)
    for h in range(N_HEADS):
        cols = slice(h * DV, (h + 1) * DV)
        h_ref[:, cols] = (_rms(o_scr[:, cols], nw_ref[:, cols]) * _sigmoid(mo_ref[:, cols])).astype(h_ref.dtype)


def _hgrn_step_kernel(lg_ref, q_ref, f_ref, i_ref, g_ref, nw_ref, st_ref, h_ref, so_ref, o_scr, *, layer, slot):
    _zero_other_layers(so_ref, slot)
    lb_all = _hgrn_lower_bound(lg_ref[...], layer)
    for h in range(N_HEADS):
        hlf, hk = _hgrn_gates(f_ref[:, h * DK:(h + 1) * DK], lb_all[:, h * DK:(h + 1) * DK])
        dec = jnp.exp(hlf)
        qh = _silu(q_ref[:, h * DK:(h + 1) * DK])
        for i in range(SAMPLE_BLOCK):
            s_new, o = _state_step(st_ref[0, i, h], _row_to_col(dec[i:i + 1, :]), _row_to_col(hk[i:i + 1, :]),
                                   i_ref[i:i + 1, h * DV:(h + 1) * DV], _row_to_col(qh[i:i + 1, :]))
            so_ref[slot, i, h] = s_new
            o_scr[i:i + 1, h * DV:(h + 1) * DV] = o
    for h in range(N_HEADS):
        cols = slice(h * DV, (h + 1) * DV)
        h_ref[:, cols] = (_rms(o_scr[:, cols], nw_ref[:, cols]) * _silu(g_ref[:, cols])).astype(h_ref.dtype)


def _seg(width, off, tp):
    return pl.BlockSpec((SAMPLE_BLOCK, width), lambda i: (tp // SAMPLE_BLOCK + i, off // width))


def _state_spec(layer, tail):
    return pl.BlockSpec((1, SAMPLE_BLOCK) + tail, lambda i: (layer, i) + (0,) * len(tail))


_MAT = (N_HEADS, DK, DV)
_SMEM = pl.BlockSpec(memory_space=pltpu.SMEM)


def _whole(shape):
    return pl.BlockSpec(shape, lambda i: (0,) * len(shape))


def _h_rows_spec(tp, branch):
    return pl.BlockSpec((SAMPLE_BLOCK, _HW), lambda i: (tp // SAMPLE_BLOCK + i, branch))


def _new_mat_state(layer, depth, prev):
    if prev is None:
        return pl.BlockSpec((depth, SAMPLE_BLOCK) + _MAT, lambda i: (0, i, 0, 0, 0)), layer
    return _state_spec(layer, _MAT), 0


def _ret_sample(u, tp, gamma, cos2, sin2, state, layer, h_buf, prev):
    depth, bs = state.shape[:2]
    spec, slot = _new_mat_state(layer, depth, prev)
    return _pcall(
        functools.partial(_ret_step_kernel, slot=slot), (gamma, u, u, u, u, cos2, sin2, state),
        [_SMEM, _seg(_KW, OFF_RQ, tp), _seg(_KW, OFF_RK, tp), _seg(_HW, OFF_RV, tp), _seg(_HW, OFF_RG, tp),
         _whole((1, DK)), _whole((1, DK)), _state_spec(layer, _MAT)],
        carried={0: h_buf, 1: prev},
        grid=(bs // SAMPLE_BLOCK,),
        out_specs=[_h_rows_spec(tp, BR_RET), spec],
        out_shape=[jax.ShapeDtypeStruct(h_buf.shape, BF16), jax.ShapeDtypeStruct(state.shape, F32)],
        scratch_shapes=[pltpu.VMEM((SAMPLE_BLOCK, _HW), F32)],
        compiler_params=_cparams(("parallel",), 48),
        name="ret_sample",
    )


def _mlstm_sample(u, ug, tp, gate_bias, norm_w, st_c, st_n, st_m, layer, h_buf, prev):
    depth, bs = st_c.shape[:2]
    spec, slot = _new_mat_state(layer, depth, prev[0])
    return _pcall(
        functools.partial(_mlstm_step_kernel, slot=slot), (gate_bias, u, u, u, u, ug, norm_w, st_c, st_n, st_m),
        [_SMEM, _seg(_KW, OFF_MQ, tp), _seg(_KW, OFF_MK, tp), _seg(_HW, OFF_MV, tp), _seg(_HW, OFF_MO, tp),
         pl.BlockSpec((SAMPLE_BLOCK, GATE_PAD), lambda i: (tp // SAMPLE_BLOCK + i, 0)), _whole((1, _HW)),
         _state_spec(layer, _MAT), _state_spec(layer, (N_HEADS, DK)), _state_spec(layer, (N_HEADS,))],
        carried={0: h_buf, 1: prev[0], 2: prev[1], 3: prev[2]},
        grid=(bs // SAMPLE_BLOCK,),
        out_specs=[_h_rows_spec(tp, BR_ML), spec,
                   _state_spec(layer, (N_HEADS, DK)), _state_spec(layer, (N_HEADS,))],
        out_shape=[jax.ShapeDtypeStruct(h_buf.shape, BF16), jax.ShapeDtypeStruct(st_c.shape, F32),
                   jax.ShapeDtypeStruct(st_n.shape, F32), jax.ShapeDtypeStruct(st_m.shape, F32)],
        scratch_shapes=[pltpu.VMEM((SAMPLE_BLOCK, _HW), F32)],
        compiler_params=_cparams(("parallel",), 48),
        name="mlstm_sample",
    )


def _hgrn_sample(u, tp, logits, norm_w, state, layer, h_buf, prev):
    depth, bs = state.shape[:2]
    spec, slot = _new_mat_state(layer, depth, prev)
    return _pcall(
        functools.partial(_hgrn_step_kernel, layer=layer, slot=slot), (logits, u, u, u, u, norm_w, state),
        [_whole(logits.shape), _seg(_KW, OFF_HQ, tp), _seg(_KW, OFF_HF, tp), _seg(_HW, OFF_HI, tp),
         _seg(_HW, OFF_HG, tp), _whole((1, _HW)), _state_spec(layer, _MAT)],
        carried={0: h_buf, 1: prev},
        grid=(bs // SAMPLE_BLOCK,),
        out_specs=[_h_rows_spec(tp, BR_HG), spec],
        out_shape=[jax.ShapeDtypeStruct(h_buf.shape, BF16), jax.ShapeDtypeStruct(state.shape, F32)],
        scratch_shapes=[pltpu.VMEM((SAMPLE_BLOCK, _HW), F32)],
        compiler_params=_cparams(("parallel",), 48),
        name="hgrn_sample",
    )


def _attn_prompt_kernel(q_ref, k_ref, v_ref, o_ref, *, scale):
    s = _dot_nt(q_ref[...].astype(BF16), k_ref[0, 0].astype(BF16)) * scale
    e = jnp.exp(s - jnp.max(s, axis=1, keepdims=True))
    p = e / jnp.sum(e, axis=1, keepdims=True)
    o_ref[...] = _dot(p.astype(BF16), v_ref[0, 0].astype(BF16)).astype(o_ref.dtype)


def _attn_prompt(q, kvbuf, obuf, bsz, seqlen, layer):
    t, d = q.shape
    dh = d // N_HEADS
    mem_len = kvbuf.shape[2] // bsz
    tq = _tile(seqlen, 512)
    nq = seqlen // tq
    return _pcall(
        functools.partial(_attn_prompt_kernel, scale=dh ** -0.5), (q, kvbuf, kvbuf),
        [pl.BlockSpec((tq, dh), lambda b, h, i: (b * nq + i, h)),
         pl.BlockSpec((1, 1, mem_len, dh), lambda b, h, i: (layer, 0, b, h)),
         pl.BlockSpec((1, 1, mem_len, dh), lambda b, h, i: (layer, 1, b, h))],
        carried={0: obuf},
        grid=(bsz, N_HEADS, nq),
        out_specs=pl.BlockSpec((tq, dh), lambda b, h, i: (b * nq + i, h)),
        out_shape=jax.ShapeDtypeStruct((t, d), BF16),
        compiler_params=_cparams(("parallel", "parallel", "arbitrary"), 32),
        name="attn_prompt",
    )


def _attn_sample_kernel(q_ref, k_ref, v_ref, o_ref, *, scale):
    for h in range(q_ref.shape[1]):
        k = k_ref[0, 0, :, h, :]
        s = jnp.sum(k * q_ref[0, h:h + 1, :], axis=1, keepdims=True) * scale
        e = jnp.exp(s - jnp.max(s, axis=0, keepdims=True))
        p = e / jnp.sum(e, axis=0, keepdims=True)
        o_ref[0, h:h + 1, :] = jnp.sum(p * v_ref[0, 0, :, h, :], axis=0, keepdims=True)


def _attn_sample(q4, cache_k, cache_v, layer):
    bs, nh, dh = q4.shape
    mem_len = cache_k.shape[2]
    kv_spec = pl.BlockSpec((1, 1, mem_len, nh, dh), lambda i: (layer, i, 0, 0, 0))
    return pl.pallas_call(
        functools.partial(_attn_sample_kernel, scale=dh ** -0.5),
        grid=(bs,),
        in_specs=[pl.BlockSpec((1, nh, dh), lambda i: (i, 0, 0)), kv_spec, kv_spec],
        out_specs=pl.BlockSpec((1, nh, dh), lambda i: (i, 0, 0)),
        out_shape=jax.ShapeDtypeStruct((bs, nh, dh), F32),
        compiler_params=_cparams(("parallel",), 48),
        name="attn_sample",
    )(q4, cache_k, cache_v)


def _rope_tables(pos):
    half = DK // 2
    inv = ROPE_BASE ** (-jnp.arange(half, dtype=F32) / half)
    ang = pos.astype(F32)[:, None] * inv[None, :]
    cos, sin = jnp.cos(ang), jnp.sin(ang)
    return jnp.concatenate([cos, cos], axis=1), jnp.concatenate([-sin, sin], axis=1)


def _retention_tables(c):
    lg = jnp.log1p(-jnp.exp2(-5.0 - jnp.arange(N_HEADS, dtype=F32)))
    idx = jnp.arange(c, dtype=F32)
    diff = idx[:, None] - idx[None, :]
    dmat = jnp.where(diff[None] >= 0, jnp.exp(jnp.maximum(diff, 0.0)[None] * lg[:, None, None]), 0.0)
    e1 = jnp.exp((idx[None, :] + 1.0) * lg[:, None])
    e2 = jnp.exp((c - 1.0 - idx)[None, :] * lg[:, None])
    e3 = jnp.broadcast_to(jnp.exp(c * lg)[:, None], (N_HEADS, c))
    dec = jnp.stack([e1, e2, e3] + [jnp.zeros_like(e1)] * 5, axis=-1)
    return dmat, dec


def kernel(x_prompt, x_sample, mem_prompt, state_ret, state_mlstm_c, state_mlstm_n, state_mlstm_m, state_s5_re, state_s5_im, state_hgrn, cache_mem_k, cache_mem_v, norm_w, ffn_w_in, ffn_w_out, w_in, ml_gate_bias, ml_norm_w, s5_lambda_re, s5_lambda_im, s5_log_step, s5_b_re, s5_b_im, s5_c_re, s5_c_im, s5_d, s5_w_glu, hg_lb_logits, hg_norm_w, w_branch, w_out, xa_mem_norm, xa_wq, xa_wkv, xa_wo):
    bsz, seqlen, d = x_prompt.shape
    bs = x_sample.shape[0]
    depth = norm_w.shape[0]
    mem_len = mem_prompt.shape[1]
    tp = bsz * seqlen
    gn = s5_lambda_re.shape[1] * s5_lambda_re.shape[2]
    seg_gate = OFF_MO

    x = jnp.concatenate([x_prompt.reshape(tp, d), x_sample.reshape(bs, d)], axis=0)
    mem = mem_prompt.reshape(bsz * mem_len, d)
    s5_in_re = state_s5_re.reshape(depth, bs, gn)
    s5_in_im = state_s5_im.reshape(depth, bs, gn)

    cos_p, sin_p = _rope_tables(jnp.arange(seqlen, dtype=jnp.int32))
    cos_s, sin_s = _rope_tables(PAST_LEN + jnp.arange(1, dtype=jnp.int32))
    dmat, dec = _retention_tables(CHUNK)
    _, dec1 = _retention_tables(1)
    gamma = dec1[:, 0, 0]

    zf = lambda *shape: jnp.zeros(shape, F32)
    t = tp + bs
    p_ret = zf(depth, bsz, N_HEADS, DK, DV)
    p_hg = zf(depth, bsz, N_HEADS, DK, DV)
    p_ml = [zf(depth, bsz, N_HEADS, DK, DV), zf(depth, bsz, N_HEADS, 1, DK), zf(depth, bsz, N_HEADS, 1, 1)]
    p_s5 = [zf(depth, bsz, 1, gn), zf(depth, bsz, 1, gn)]
    kvbuf = zf(depth, 2, bsz * mem_len, d)
    s_ret = s_hg = None
    s_ml = [None, zf(depth, bs, N_HEADS, DK), zf(depth, bs, N_HEADS)]
    s_s5 = [zf(depth, bs, gn), zf(depth, bs, gn)]
    hbuf = jnp.zeros((t, N_BRANCH * _HW), BF16)
    o = jnp.zeros((t, d), BF16)
    ffn_w_out_bf = ffn_w_out.astype(BF16)
    w_branch_bf, w_out_bf = w_branch.astype(BF16), w_out.astype(BF16)
    xa_wq_bf, xa_wkv_bf, xa_wo_bf = xa_wq.astype(BF16), xa_wkv.astype(BF16), xa_wo.astype(BF16)
    for l in range(depth):
        nw = norm_w[l]
        x = _ffn(x, nw[0:2], ffn_w_in, l, 0, ffn_w_out_bf)
        n_gate = 2 * N_HEADS
        w_main, w_gate = _repack_w_in(w_in, l, seg_gate, n_gate)
        u, ug = _in_proj(x, nw[2:3], w_main, w_gate)
        ugt = ug[:tp, :n_gate].T
        mlw = ml_norm_w[l].reshape(1, -1)
        hgw = hg_norm_w[l].reshape(1, -1)
        ar, ai, bbr, bbi = _s5_prep(s5_lambda_re[l], s5_lambda_im[l], s5_log_step[l], s5_b_re[l], s5_b_im[l])
        ar, ai = ar.reshape(1, gn), ai.reshape(1, gn)
        bblk, cblk = _s5_weights(bbr, bbi, s5_c_re[l], s5_c_im[l])
        s5d = s5_d[l].astype(F32).reshape(1, -1)
        wglu = s5_w_glu[l].astype(BF16)
        hbuf, p_ret = _ret_prompt(u, hbuf, bsz, seqlen, cos_p, sin_p, dmat, dec, l, depth, p_ret)
        hbuf, s_ret = _ret_sample(u, tp, gamma, cos_s, sin_s, state_ret, l, hbuf, s_ret)
        hbuf, *p_ml = _mlstm_prompt(u, hbuf, ug, ugt, ml_gate_bias[l], mlw, bsz, seqlen, l, depth, p_ml)
        hbuf, *s_ml = _mlstm_sample(u, ug, tp, ml_gate_bias[l], mlw, state_mlstm_c, state_mlstm_n,
                                    state_mlstm_m, l, hbuf, s_ml)
        hbuf, *p_s5 = _s5_prompt(u, hbuf, ar, ai, bblk, cblk, s5d, wglu, bsz, seqlen, l, depth, p_s5)
        hbuf, *s_s5 = _s5_sample(u, tp, ar, ai, bblk, cblk, s5d, wglu, s5_in_re, s5_in_im, l, hbuf, s_s5)
        hbuf, p_hg = _hgrn_prompt(u, hbuf, hg_lb_logits, hgw, bsz, seqlen, l, p_hg)
        hbuf, s_hg = _hgrn_sample(u, tp, hg_lb_logits, hgw, state_hgrn, l, hbuf, s_hg)
        x = _merge(hbuf, u, w_branch_bf, w_out_bf, l, x, nw[3:4])
        kvbuf = _mem_kv(mem, xa_mem_norm[l].reshape(1, d), xa_wkv_bf, l, depth, kvbuf)
        q = _norm_matmul(x, nw[4:5], xa_wq_bf, l, "xa_q")
        o = _attn_prompt(q, kvbuf, o, bsz, seqlen, l)
        o_s = _attn_sample(q[tp:].reshape(bs, N_HEADS, d // N_HEADS), cache_mem_k, cache_mem_v, l)
        o = lax.dynamic_update_slice(o, o_s.reshape(bs, d).astype(BF16), (tp, 0))
        x = _proj_res(o, xa_wo_bf, l, x, nw[5:6])
        x = _ffn(x, nw[6:8], ffn_w_in, l, 1, ffn_w_out_bf)

    mem_shape = (depth, bsz, mem_len, N_HEADS, d // N_HEADS)
    s5_shape = (depth, -1, gn // S5_N, S5_N)
    return (x[:tp].reshape(bsz, seqlen, d), x[tp:].reshape(bs, 1, d),
            p_ret, p_ml[0], p_ml[1].reshape(depth, bsz, N_HEADS, DK), p_ml[2].reshape(depth, bsz, N_HEADS),
            p_s5[0].reshape(s5_shape), p_s5[1].reshape(s5_shape), p_hg,
            kvbuf[:, 0].reshape(mem_shape), kvbuf[:, 1].reshape(mem_shape),
            s_ret, s_ml[0], s_ml[1], s_ml[2], s_s5[0].reshape(s5_shape), s_s5[1].reshape(s5_shape), s_hg)
```

```python
import functools
import math

import jax
import jax.numpy as jnp
from jax import lax
from jax.experimental import pallas as pl
from jax.experimental.pallas import tpu as pltpu

F32 = jnp.float32
BF16 = jnp.bfloat16

CHUNK = 128
N_BRANCH = 4
N_HEADS = 4
DK = 128
DV = 256
S5_GROUP = 16
S5_N = 64
ROPE_BASE = 10000.0
PAST_LEN = 16384

V7X_VMEM_BYTES = 64 * 1024 * 1024
MXU_K = 256
SUBLANES = 8
MIB = 1024 * 1024

OFF_RQ, OFF_RK, OFF_RV, OFF_RG = 0, 512, 1024, 2048
OFF_MQ, OFF_MK, OFF_MV, OFF_MO = 3072, 3584, 4096, 5120
OFF_SU = 6144
OFF_HQ, OFF_HF, OFF_HI, OFF_HG = 7168, 7680, 8192, 9216
OFF_GATES = 10240
N_MAIN = 18432
GATE_PAD = 128


def _cparams(sem, vmem_mib):
    return pltpu.CompilerParams(dimension_semantics=sem, vmem_limit_bytes=vmem_mib * MIB)


_ANY = pl.BlockSpec(memory_space=pl.ANY)


def _pcall(kernel_fn, inputs, in_specs, carried=None, **kw):
    carried = {o: a for o, a in (carried or {}).items() if a is not None}
    idxs = sorted(carried)
    n_in = len(inputs)

    def body(*refs):
        return kernel_fn(*refs[:n_in], *refs[n_in + len(idxs):])

    return pl.pallas_call(
        body if idxs else kernel_fn,
        in_specs=list(in_specs) + [_ANY] * len(idxs),
        input_output_aliases={n_in + j: o for j, o in enumerate(idxs)},
        **kw)(*inputs, *[carried[o] for o in idxs])


def _tile(n, cap, mult=16):
    best = None
    for t in range(mult, min(n, cap) + 1, mult):
        if n % t == 0:
            best = t
    return best if best is not None else n


def _dot(a, b):
    return jnp.dot(a, b, preferred_element_type=F32)


def _dot_nt(a, b):
    return lax.dot_general(a, b, (((1,), (1,)), ((), ())), preferred_element_type=F32)


def _rms(x, w, eps=1e-6):
    return x * lax.rsqrt(jnp.mean(x * x, axis=-1, keepdims=True) + eps) * w


def _sigmoid(x):
    return 1.0 / (1.0 + jnp.exp(-x))


def _silu(x):
    return x * _sigmoid(x)


def _log_sigmoid(x):
    return jnp.minimum(x, 0.0) - jnp.log1p(jnp.exp(-jnp.abs(x)))


def _logaddexp(a, b):
    return jnp.maximum(a, b) + jnp.log1p(jnp.exp(-jnp.abs(a - b)))


def _gelu_tanh(x):
    return 0.5 * x * (1.0 + jnp.tanh(math.sqrt(2.0 / math.pi) * (x + 0.044715 * (x * x * x))))


def _row_to_col(row):
    n = row.shape[1]
    r = lax.broadcasted_iota(jnp.int32, (n, n), 0)
    c = lax.broadcasted_iota(jnp.int32, (n, n), 1)
    return jnp.sum(jnp.where(r == c, row, 0.0), axis=1, keepdims=True)


def _cumsum_rows(x):
    c = x.shape[0]
    r = lax.broadcasted_iota(jnp.int32, (c, c), 0)
    k = lax.broadcasted_iota(jnp.int32, (c, c), 1)
    tri = jnp.where(r >= k, 1.0, 0.0).astype(BF16)
    hi = x.astype(BF16)
    r1 = x - hi.astype(F32)
    mid = r1.astype(BF16)
    lo = (r1 - mid.astype(F32)).astype(BF16)
    return _dot(tri, hi) + _dot(tri, mid) + _dot(tri, lo)


def _ffn_up_kernel(x_ref, nw_ref, wg_ref, wu_ref, h_ref, xn_ref, *, rc):
    @pl.when(pl.program_id(1) == 0)
    def _():
        xn_ref[...] = _rms(x_ref[...], nw_ref[...]).astype(BF16)

    wg = wg_ref[0, 0].astype(BF16)
    wu = wu_ref[0, 0].astype(BF16)
    for r in range(0, x_ref.shape[0], rc):
        xn = xn_ref[r:r + rc, :]
        h_ref[r:r + rc, :] = (_silu(_dot(xn, wg)) * _dot(xn, wu)).astype(h_ref.dtype)


def _ffn_down_kernel(h_ref, w_ref, x_ref, nw_ref, o_ref, *, rc):
    for r in range(0, x_ref.shape[0], rc):
        rows = slice(r, r + rc)
        o_ref[rows, :] = x_ref[rows, :] + 0.5 * _rms(_dot(h_ref[rows, :], w_ref[0, 0]), nw_ref[...])


def _ffn(x, nw2, w_in, layer, which, w_out):
    t, d = x.shape
    dff = w_out.shape[2]
    tm = _tile(t, 2080)
    rc = _tile(tm, 260)
    tf = _tile(dff, 512, 128)
    nf = dff // tf
    once = pl.Buffered(1)
    hidden = pl.pallas_call(
        functools.partial(_ffn_up_kernel, rc=rc),
        grid=(t // tm, nf),
        in_specs=[pl.BlockSpec((tm, d), lambda i, f: (i, 0), pipeline_mode=once),
                  pl.BlockSpec((1, d), lambda i, f: (0, 0)),
                  pl.BlockSpec((1, 1, d, tf), lambda i, f: (layer, which, 0, f)),
                  pl.BlockSpec((1, 1, d, tf), lambda i, f: (layer, which, 0, nf + f))],
        out_specs=pl.BlockSpec((tm, tf), lambda i, f: (i, f)),
        out_shape=jax.ShapeDtypeStruct((t, dff), BF16),
        scratch_shapes=[pltpu.VMEM((tm, d), BF16)],
        compiler_params=_cparams(("parallel", "arbitrary"), 56),
        name="ffn_up",
    )(x, nw2[0:1], w_in, w_in)
    tm2 = _tile(t, 416)
    return pl.pallas_call(
        functools.partial(_ffn_down_kernel, rc=_tile(tm2, 208)),
        grid=(t // tm2,),
        in_specs=[pl.BlockSpec((tm2, dff), lambda i: (i, 0)),
                  pl.BlockSpec((1, 1, dff, d), lambda i: (layer, which, 0, 0), pipeline_mode=once),
                  pl.BlockSpec((tm2, d), lambda i: (i, 0)),
                  pl.BlockSpec((1, d), lambda i: (0, 0))],
        out_specs=pl.BlockSpec((tm2, d), lambda i: (i, 0)),
        out_shape=jax.ShapeDtypeStruct((t, d), F32),
        compiler_params=_cparams(("parallel",), 56),
        name="ffn_down",
    )(hidden, w_out, x, nw2[1:2])


def _norm_matmul_kernel(x_ref, nw_ref, w_ref, o_ref, xn_ref):
    @pl.when(pl.program_id(1) == 0)
    def _():
        xn_ref[...] = _rms(x_ref[...], nw_ref[...]).astype(BF16)

    o_ref[...] = _dot(xn_ref[...], w_ref[0]).astype(o_ref.dtype)


def _norm_matmul(x, nw, w, layer, name):
    t, d = x.shape
    n = w.shape[2]
    tm = _tile(t, 1040)
    tn = _tile(n, 1024, 128)
    return pl.pallas_call(
        _norm_matmul_kernel,
        grid=(t // tm, n // tn),
        in_specs=[pl.BlockSpec((tm, d), lambda i, j: (i, 0)),
                  pl.BlockSpec((1, d), lambda i, j: (0, 0)),
                  pl.BlockSpec((1, d, tn), lambda i, j: (layer, 0, j))],
        out_specs=pl.BlockSpec((tm, tn), lambda i, j: (i, j)),
        out_shape=jax.ShapeDtypeStruct((t, n), F32),
        scratch_shapes=[pltpu.VMEM((tm, d), BF16)],
        compiler_params=_cparams(("parallel", "arbitrary"), 48),
        name=name,
    )(x, nw, w)


def _mem_kv_kernel(x_ref, nw_ref, w_ref, o_ref, xn_ref):
    @pl.when(pl.program_id(1) == 0)
    def _():
        xn_ref[...] = _rms(x_ref[...], nw_ref[...]).astype(BF16)

    o_ref[0, 0] = _dot(xn_ref[...], w_ref[0])


def _mem_kv(mem, nw, w, layer, depth, prev):
    t, d = mem.shape
    tm = _tile(t, 512)
    tn = _tile(d, 1024, 128)
    npk = d // tn
    return _pcall(
        _mem_kv_kernel, (mem, nw, w),
        [pl.BlockSpec((tm, d), lambda i, j: (i, 0)),
         pl.BlockSpec((1, d), lambda i, j: (0, 0)),
         pl.BlockSpec((1, d, tn), lambda i, j: (layer, 0, j))],
        carried={0: prev},
        grid=(t // tm, 2 * npk),
        out_specs=pl.BlockSpec((1, 1, tm, tn), lambda i, j: (layer, j // npk, i, j % npk)),
        out_shape=jax.ShapeDtypeStruct((depth, 2, t, d), F32),
        scratch_shapes=[pltpu.VMEM((tm, d), BF16)],
        compiler_params=_cparams(("parallel", "arbitrary"), 48),
        name="mem_kv",
    )


def _in_proj_kernel(x_ref, nw_ref, wt_ref, wgt_ref, o_ref, og_ref, xn_ref):
    @pl.when(pl.program_id(1) == 0)
    def _():
        xn = _rms(x_ref[...], nw_ref[...]).astype(BF16)
        xn_ref[...] = xn
        og_ref[...] = _dot_nt(xn, wgt_ref[0].astype(BF16))

    o_ref[...] = _dot_nt(xn_ref[...], wt_ref[0].astype(BF16))


def _in_proj(x, nw, w_in_t, layer, gate_off, n_gate):
    t, d = x.shape
    n = w_in_t.shape[1] - n_gate
    tm = _tile(t, 1040)
    tn = _tile(math.gcd(n, gate_off), 1024, 128)
    n_plain = gate_off // tn
    return pl.pallas_call(
        _in_proj_kernel,
        grid=(t // tm, n // tn),
        in_specs=[pl.BlockSpec((tm, d), lambda i, j: (i, 0), pipeline_mode=pl.Buffered(1)),
                  pl.BlockSpec((1, d), lambda i, j: (0, 0)),
                  pl.BlockSpec((pl.Element(1), pl.Element(tn), pl.Element(d)),
                               lambda i, j: (layer, SUBLANES * (j * (tn // SUBLANES) + jnp.where(
                                   j < n_plain, 0, n_gate // SUBLANES)), 0)),
                  pl.BlockSpec((1, GATE_PAD, d), lambda i, j: (layer, gate_off // GATE_PAD, 0))],
        out_specs=[pl.BlockSpec((tm, tn), lambda i, j: (i, j)),
                   pl.BlockSpec((tm, GATE_PAD), lambda i, j: (i, 0))],
        out_shape=[jax.ShapeDtypeStruct((t, n), F32), jax.ShapeDtypeStruct((t, GATE_PAD), F32)],
        scratch_shapes=[pltpu.VMEM((tm, d), BF16)],
        compiler_params=_cparams(("parallel", "arbitrary"), 56),
        name="in_proj",
    )(x, nw, w_in_t, w_in_t)


def _merge_kernel(*refs):
    h_refs, g_refs = refs[:N_BRANCH], refs[N_BRANCH:2 * N_BRANCH]
    wb_ref, wo_ref, x_ref, nw_ref, o_ref = refs[2 * N_BRANCH:]
    merged = None
    for k in range(N_BRANCH):
        term = _sigmoid(g_refs[k][...]) * _dot(h_refs[k][...], wb_ref[0, k])
        merged = term if merged is None else merged + term
    o_ref[...] = x_ref[...] + _rms(_dot(merged.astype(BF16), wo_ref[0]), nw_ref[...])


def _merge(hbuf, u, w_branch, w_out, layer, x, nw):
    t, d = x.shape
    bw = hbuf.shape[1] // N_BRANCH
    tm = _tile(t, 208)
    gate_blk = OFF_GATES // d
    once = pl.Buffered(1)
    return pl.pallas_call(
        _merge_kernel,
        grid=(t // tm,),
        in_specs=[pl.BlockSpec((tm, bw), lambda i, k=k: (i, k)) for k in range(N_BRANCH)] + [
            pl.BlockSpec((tm, d), lambda i, k=k: (i, gate_blk + k)) for k in range(N_BRANCH)] + [
            pl.BlockSpec((1,) + w_branch.shape[1:], lambda i: (layer, 0, 0, 0), pipeline_mode=once),
            pl.BlockSpec((1,) + w_out.shape[1:], lambda i: (layer, 0, 0), pipeline_mode=once),
            pl.BlockSpec((tm, d), lambda i: (i, 0)),
            pl.BlockSpec((1, d), lambda i: (0, 0))],
        out_specs=pl.BlockSpec((tm, d), lambda i: (i, 0)),
        out_shape=jax.ShapeDtypeStruct((t, d), F32),
        compiler_params=_cparams(("parallel",), 56),
        name="merge",
    )(*([hbuf] * N_BRANCH), *([u] * N_BRANCH), w_branch, w_out, x, nw)


def _proj_res_kernel(a_ref, w_ref, x_ref, nw_ref, o_ref):
    o_ref[...] = x_ref[...] + _rms(_dot(a_ref[...], w_ref[0]), nw_ref[...])


def _proj_res(a, w, layer, x, nw):
    t, d = x.shape
    tm = _tile(t, 640)
    return pl.pallas_call(
        _proj_res_kernel,
        grid=(t // tm,),
        in_specs=[pl.BlockSpec((tm, d), lambda i: (i, 0)),
                  pl.BlockSpec((1, d, d), lambda i: (layer, 0, 0), pipeline_mode=pl.Buffered(1)),
                  pl.BlockSpec((tm, d), lambda i: (i, 0)),
                  pl.BlockSpec((1, d), lambda i: (0, 0))],
        out_specs=pl.BlockSpec((tm, d), lambda i: (i, 0)),
        out_shape=jax.ShapeDtypeStruct((t, d), F32),
        compiler_params=_cparams(("parallel",), 56),
        name="xa_out",
    )(a, w, x, nw)


def _rope(x, cs, sn):
    return x * cs + pltpu.roll(x, x.shape[1] // 2, 1) * sn


def _group_norm(o, eps=1e-5):
    oc = o - jnp.mean(o, axis=-1, keepdims=True)
    return oc * lax.rsqrt(jnp.mean(oc * oc, axis=-1, keepdims=True) + eps)


def _ret_kernel(q_ref, k_ref, v_ref, g_ref, cos_ref, sin_ref, dm_ref, dec_ref, h_ref, st_ref, s_ref, *, nc):
    c = pl.program_id(1)

    @pl.when(c == 0)
    def _():
        s_ref[...] = jnp.zeros_like(s_ref)

    cs, sn = cos_ref[...], sin_ref[...]
    for h in range(N_HEADS):
        kc, vc = slice(h * DK, (h + 1) * DK), slice(h * DV, (h + 1) * DV)
        qr = _rope(q_ref[:, kc], cs, sn)
        kr = _rope(k_ref[:, kc], cs, sn) * (DK ** -0.5)
        dec = dec_ref[h]
        e1, e2, e3 = dec[:, 0:1], dec[:, 1:2], dec[0:1, 2:3]
        vb = v_ref[:, vc].astype(BF16)
        qb = qr.astype(BF16)
        s = s_ref[h]
        sc = _dot_nt(qb, kr.astype(BF16)) * dm_ref[h]
        o = _dot(sc.astype(BF16), vb) + _dot(qb, s.astype(BF16)) * e1
        s_ref[h] = e3 * s + _dot((kr * e2).T.astype(BF16), vb)
        h_ref[:, vc] = (_group_norm(o) * _silu(g_ref[:, vc])).astype(h_ref.dtype)

    @pl.when(c == nc - 1)
    def _():
        st_ref[...] = s_ref[...].reshape(st_ref.shape)


def _seg_spec(nc, width, off):
    return pl.BlockSpec((CHUNK, width), lambda b, c: (b * nc + c, off // width))


def _layer_state_spec(layer, tail):
    return pl.BlockSpec((1, 1) + tail, lambda b, c: (layer, b) + (0,) * len(tail))


def _const_spec(shape):
    return pl.BlockSpec(shape, lambda b, c: (0,) * len(shape))


_HW, _KW = N_HEADS * DV, N_HEADS * DK


BR_RET, BR_ML, BR_S5, BR_HG = range(N_BRANCH)


def _ret_prompt(u, hbuf, bsz, seqlen, cos2, sin2, dmat, dec, layer, depth, prev):
    nc = seqlen // CHUNK
    return _pcall(
        functools.partial(_ret_kernel, nc=nc), (u, u, u, u, cos2, sin2, dmat, dec),
        [_seg_spec(nc, _KW, OFF_RQ), _seg_spec(nc, _KW, OFF_RK), _seg_spec(nc, _HW, OFF_RV),
         _seg_spec(nc, _HW, OFF_RG),
         pl.BlockSpec((CHUNK, DK), lambda b, c: (c, 0)),
         pl.BlockSpec((CHUNK, DK), lambda b, c: (c, 0)),
         _const_spec(dmat.shape), _const_spec(dec.shape)],
        carried={0: hbuf, 1: prev},
        grid=(bsz, nc),
        out_specs=[pl.BlockSpec((CHUNK, _HW), lambda b, c: (b * nc + c, BR_RET)),
                   _layer_state_spec(layer, (N_HEADS, DK, DV))],
        out_shape=[jax.ShapeDtypeStruct(hbuf.shape, BF16),
                   jax.ShapeDtypeStruct((depth, bsz, N_HEADS, DK, DV), F32)],
        scratch_shapes=[pltpu.VMEM((N_HEADS, DK, DV), F32)],
        compiler_params=_cparams(("parallel", "arbitrary"), 32),
        name="ret_prompt",
    )


def _mlstm_kernel(gb_ref, q_ref, k_ref, v_ref, mo_ref, ug_ref, ugt_ref, nw_ref,
                  h_ref, c_out, n_out, m_out, cm_ref, nv_ref, m_ref, *, nc):
    c = pl.program_id(1)

    @pl.when(c == 0)
    def _():
        cm_ref[...] = jnp.zeros_like(cm_ref)
        nv_ref[...] = jnp.zeros_like(nv_ref)
        m_ref[...] = jnp.zeros_like(m_ref)

    n = CHUNK
    row = lax.broadcasted_iota(jnp.int32, (n, n), 0)
    col = lax.broadcasted_iota(jnp.int32, (n, n), 1)
    causal = row >= col
    ug = ug_ref[...]
    for h in range(N_HEADS):
        kc, vc = slice(h * DK, (h + 1) * DK), slice(h * DV, (h + 1) * DV)
        bias_i = gb_ref[0, h]
        bias_f = gb_ref[1, h]
        ig_row = ugt_ref[h:h + 1, :] + bias_i
        lf_row = _log_sigmoid(ugt_ref[N_HEADS + h:N_HEADS + h + 1, :] + bias_f)
        ig_col = ug[:, h:h + 1] + bias_i
        lf_col = _log_sigmoid(ug[:, N_HEADS + h:N_HEADS + h + 1] + bias_f)
        b_col = jnp.sum(jnp.where(causal, lf_row, 0.0), axis=1, keepdims=True)
        b_row = jnp.sum(jnp.where(row <= col, lf_col, 0.0), axis=0, keepdims=True)
        b_last = b_col[n - 1:n, :]
        m = m_ref[h]
        logd = jnp.where(causal, b_col - b_row + ig_row, -jnp.inf)
        inter = m + b_col
        mi = jnp.maximum(inter, jnp.max(logd, axis=1, keepdims=True))
        w = jnp.exp(logd - mi)
        wi = jnp.exp(inter - mi)
        q = q_ref[:, kc]
        k = k_ref[:, kc] * (DK ** -0.5)
        qb = q.astype(BF16)
        vb = v_ref[:, vc].astype(BF16)
        cm = cm_ref[h]
        nv = nv_ref[h]
        a = _dot_nt(qb, k.astype(BF16)) * w
        num = _dot(a.astype(BF16), vb) + wi * _dot(qb, cm.astype(BF16))
        nq = jnp.sum(a, axis=1, keepdims=True) + wi * jnp.sum(q * nv, axis=1, keepdims=True)
        hh = num / jnp.maximum(jnp.abs(nq), jnp.exp(-mi))
        m_new = mi[n - 1:n, :]
        wl = jnp.exp(b_last - b_col + ig_col - m_new)
        dp = jnp.exp(m + b_last - m_new)
        kw = k * wl
        cm_ref[h] = dp * cm + _dot(kw.T.astype(BF16), vb)
        nv_ref[h] = dp * nv + jnp.sum(kw, axis=0, keepdims=True)
        m_ref[h] = m_new
        h_ref[:, vc] = (_rms(hh, nw_ref[:, vc]) * _sigmoid(mo_ref[:, vc])).astype(h_ref.dtype)

    @pl.when(c == nc - 1)
    def _():
        c_out[...] = cm_ref[...].reshape(c_out.shape)
        n_out[...] = nv_ref[...].reshape(n_out.shape)
        m_out[...] = m_ref[...].reshape(m_out.shape)


def _mlstm_prompt(u, hbuf, ug, ugt, gate_bias, norm_w, bsz, seqlen, layer, depth, prev):
    nc = seqlen // CHUNK
    return _pcall(
        functools.partial(_mlstm_kernel, nc=nc), (gate_bias, u, u, u, u, ug, ugt, norm_w),
        [pl.BlockSpec(memory_space=pltpu.SMEM),
         _seg_spec(nc, _KW, OFF_MQ), _seg_spec(nc, _KW, OFF_MK), _seg_spec(nc, _HW, OFF_MV),
         _seg_spec(nc, _HW, OFF_MO),
         pl.BlockSpec((CHUNK, GATE_PAD), lambda b, c: (b * nc + c, 0)),
         pl.BlockSpec((2 * N_HEADS, CHUNK), lambda b, c: (0, b * nc + c)),
         _const_spec((1, _HW))],
        carried={0: hbuf, 1: prev[0], 2: prev[1], 3: prev[2]},
        grid=(bsz, nc),
        out_specs=[pl.BlockSpec((CHUNK, _HW), lambda b, c: (b * nc + c, BR_ML)),
                   _layer_state_spec(layer, (N_HEADS, DK, DV)),
                   _layer_state_spec(layer, (N_HEADS, 1, DK)),
                   _layer_state_spec(layer, (N_HEADS, 1, 1))],
        out_shape=[jax.ShapeDtypeStruct(hbuf.shape, BF16),
                   jax.ShapeDtypeStruct((depth, bsz, N_HEADS, DK, DV), F32),
                   jax.ShapeDtypeStruct((depth, bsz, N_HEADS, 1, DK), F32),
                   jax.ShapeDtypeStruct((depth, bsz, N_HEADS, 1, 1), F32)],
        scratch_shapes=[pltpu.VMEM((N_HEADS, DK, DV), F32), pltpu.VMEM((N_HEADS, 1, DK), F32),
                        pltpu.VMEM((N_HEADS, 1, 1), F32)],
        compiler_params=_cparams(("parallel", "arbitrary"), 32),
        name="mlstm_prompt",
    )


def _hgrn_lower_bound(logits, layer):
    e = jnp.exp(logits - jnp.max(logits, axis=0, keepdims=True))
    p = e / jnp.sum(e, axis=0, keepdims=True)
    lb = jnp.zeros_like(p[0:1, :])
    for r in range(1, layer + 1):
        lb = lb + p[r:r + 1, :]
    return lb


def _hgrn_gates(fpre, lb):
    hlf = _logaddexp(jnp.log(lb), jnp.log1p(-lb) + _log_sigmoid(fpre))
    hk = (1.0 - lb) * _sigmoid(-fpre)
    return hlf, hk


def _hgrn_kernel(lg_ref, q_ref, f_ref, i_ref, g_ref, nw_ref, h_ref, st_ref, s_ref, *, nc, layer):
    c = pl.program_id(1)

    @pl.when(c == 0)
    def _():
        s_ref[...] = jnp.zeros_like(s_ref)

    n = CHUNK
    lb_all = _hgrn_lower_bound(lg_ref[...], layer)
    r2 = lax.broadcasted_iota(jnp.int32, (n, n), 0)
    c2 = lax.broadcasted_iota(jnp.int32, (n, n), 1)
    rowv = lax.broadcasted_iota(jnp.int32, (n, DK), 0)
    n_lev = n.bit_length() - 1
    rights = [(rowv & (1 << lev)) != 0 for lev in range(n_lev)]
    pair_masks = [((r2 >> (lev + 1)) == (c2 >> (lev + 1))) & ((r2 & (1 << lev)) != 0) & ((c2 & (1 << lev)) == 0)
                  for lev in range(n_lev)]
    for h in range(N_HEADS):
        kc, vc = slice(h * DK, (h + 1) * DK), slice(h * DV, (h + 1) * DV)
        hlf, hk = _hgrn_gates(f_ref[:, kc], lb_all[:, kc])
        q = _silu(q_ref[:, kc])
        ib = i_ref[:, vc].astype(BF16)
        bc = _cumsum_rows(hlf)
        a = jnp.where(r2 == c2, _dot_nt(q.astype(BF16), hk.astype(BF16)), 0.0)
        p = bc
        for lev in range(n_lev):
            s = 1 << lev
            right = rights[lev]
            nxt = pltpu.roll(p, n - s, 0)
            e = jnp.exp(jnp.where(right, bc - p, nxt - bc))
            m = (jnp.where(right, q, hk) * e).astype(BF16)
            a = jnp.where(pair_masks[lev], _dot_nt(m, m), a)
            if lev + 1 < n_lev:
                p = jnp.where(right, pltpu.roll(p, s, 0), p)
        st = s_ref[h]
        o = _dot(a.astype(BF16), ib) + _dot((q * jnp.exp(bc)).astype(BF16), st.astype(BF16))
        bl = bc[n - 1:n, :]
        s_ref[h] = _row_to_col(jnp.exp(bl)) * st + _dot((hk * jnp.exp(bl - bc)).T.astype(BF16), ib)
        h_ref[:, vc] = (_rms(o, nw_ref[:, vc]) * _silu(g_ref[:, vc])).astype(h_ref.dtype)

    @pl.when(c == nc - 1)
    def _():
        st_ref[...] = s_ref[...].reshape(st_ref.shape)


def _hgrn_prompt(u, hbuf, logits, norm_w, bsz, seqlen, layer, prev):
    nc = seqlen // CHUNK
    depth = logits.shape[0]
    return _pcall(
        functools.partial(_hgrn_kernel, nc=nc, layer=layer), (logits, u, u, u, u, norm_w),
        [_const_spec(logits.shape),
         _seg_spec(nc, _KW, OFF_HQ), _seg_spec(nc, _KW, OFF_HF), _seg_spec(nc, _HW, OFF_HI),
         _seg_spec(nc, _HW, OFF_HG), _const_spec((1, _HW))],
        carried={0: hbuf, 1: prev},
        grid=(bsz, nc),
        out_specs=[pl.BlockSpec((CHUNK, _HW), lambda b, c: (b * nc + c, BR_HG)),
                   _layer_state_spec(layer, (N_HEADS, DK, DV))],
        out_shape=[jax.ShapeDtypeStruct(hbuf.shape, BF16),
                   jax.ShapeDtypeStruct((depth, bsz, N_HEADS, DK, DV), F32)],
        scratch_shapes=[pltpu.VMEM((N_HEADS, DK, DV), F32)],
        compiler_params=_cparams(("parallel", "arbitrary"), 32),
        name="hgrn_prompt",
    )


def _s5_prep_kernel(lr_ref, li_ref, dt_ref, bre_ref, bim_ref, ar_ref, ai_ref, bbr_ref, bbi_ref):
    lr, li, dt = lr_ref[...], li_ref[...], dt_ref[...]
    mag = jnp.exp(lr * dt)
    ar = mag * jnp.cos(li * dt)
    ai = mag * jnp.sin(li * dt)
    den = lr * lr + li * li
    cr = ((ar - 1.0) * lr + ai * li) / den
    ci = (ai * lr - (ar - 1.0) * li) / den
    ar_ref[...] = ar
    ai_ref[...] = ai
    bbr_ref[...] = cr * bre_ref[...] - ci * bim_ref[...]
    bbi_ref[...] = cr * bim_ref[...] + ci * bre_ref[...]


def _s5_prep(lam_re, lam_im, log_step, b_re, b_im):
    g, n = lam_re.shape
    p = b_re.shape[-1]
    gn = g * n
    dt = jnp.repeat(jnp.exp(log_step.astype(F32)), n).reshape(gn, 1)
    col = lambda a: a.astype(F32).reshape(gn, 1)
    return pl.pallas_call(
        _s5_prep_kernel,
        out_shape=[jax.ShapeDtypeStruct((gn, 1), F32), jax.ShapeDtypeStruct((gn, 1), F32),
                   jax.ShapeDtypeStruct((gn, p), F32), jax.ShapeDtypeStruct((gn, p), F32)],
        name="s5_prep",
    )(col(lam_re), col(lam_im), dt, b_re.reshape(gn, p), b_im.reshape(gn, p))


def _s5_kernel(*refs, seq, nc, n_kt, sw):
    if seq:
        (u_ref, ar_ref, ai_ref, bblk_ref, cblk_ref, d_ref, wglu_ref,
         h_ref, sr_ref, si_ref, hr_scr, hi_scr) = refs
    else:
        (u_ref, ar_ref, ai_ref, bblk_ref, cblk_ref, d_ref, wglu_ref, h0r_ref, h0i_ref,
         h_ref, sr_ref, si_ref) = refs
    u = u_ref[...]
    ub = u.astype(BF16)
    rows = u.shape[0]
    if seq:
        c = pl.program_id(1)

        @pl.when(c == 0)
        def _():
            hr_scr[...] = jnp.zeros_like(hr_scr)
            hi_scr[...] = jnp.zeros_like(hi_scr)

    ys = []
    for kt in range(n_kt):
        lanes = slice(kt * sw, (kt + 1) * sw)
        bu = _dot(ub[:, kt * MXU_K:(kt + 1) * MXU_K], bblk_ref[kt])
        xr, xi = bu[:, :sw], bu[:, sw:]
        ar, ai = ar_ref[:, lanes], ai_ref[:, lanes]
        if seq:
            sub = lax.broadcasted_iota(jnp.int32, (SUBLANES, sw), 0)
            mr, mi = jnp.broadcast_to(ar, (SUBLANES, sw)), jnp.broadcast_to(ai, (SUBLANES, sw))
            tabr, tabi = mr, mi
            levels = []
            sft = 1
            while sft < SUBLANES:
                keep = sub >= sft
                levels.append((sft, jnp.where(keep, mr, 0.0), jnp.where(keep, mi, 0.0)))
                tr, ti = pltpu.roll(tabr, sft, 0), pltpu.roll(tabi, sft, 0)
                tabr, tabi = (jnp.where(keep, tabr * tr - tabi * ti, tabr),
                              jnp.where(keep, tabr * ti + tabi * tr, tabi))
                mr, mi = mr * mr - mi * mi, 2.0 * mr * mi
                sft *= 2
            cr, ci = hr_scr[:, lanes], hi_scr[:, lanes]
            slabs_r, slabs_i = [], []
            for j in range(rows // SUBLANES):
                rs = slice(j * SUBLANES, (j + 1) * SUBLANES)
                sr_, si_ = xr[rs, :], xi[rs, :]
                for sft, lr, li in levels:
                    qr, qi = pltpu.roll(sr_, sft, 0), pltpu.roll(si_, sft, 0)
                    sr_, si_ = sr_ + lr * qr - li * qi, si_ + lr * qi + li * qr
                crb, cib = jnp.broadcast_to(cr, (SUBLANES, sw)), jnp.broadcast_to(ci, (SUBLANES, sw))
                sr_, si_ = sr_ + tabr * crb - tabi * cib, si_ + tabr * cib + tabi * crb
                cr, ci = sr_[SUBLANES - 1:SUBLANES, :], si_[SUBLANES - 1:SUBLANES, :]
                slabs_r.append(sr_)
                slabs_i.append(si_)
            xr, xi = jnp.concatenate(slabs_r, axis=0), jnp.concatenate(slabs_i, axis=0)
            hr_scr[:, lanes] = cr
            hi_scr[:, lanes] = ci
        else:
            h0r, h0i = h0r_ref[0, :, lanes], h0i_ref[0, :, lanes]
            xr, xi = xr + ar * h0r - ai * h0i, xi + ar * h0i + ai * h0r
            sr_ref[0, :, lanes] = xr
            si_ref[0, :, lanes] = xi
        ys.append(_dot(jnp.concatenate([xr, xi], axis=1).astype(BF16), cblk_ref[kt]))
    y = jnp.concatenate(ys, axis=1) + d_ref[...] * u
    z = _gelu_tanh(y)
    h_ref[...] = (z * _sigmoid(_dot(z.astype(BF16), wglu_ref[...]))).astype(h_ref.dtype)
    if seq:
        @pl.when(c == nc - 1)
        def _():
            sr_ref[...] = hr_scr[...].reshape(sr_ref.shape)
            si_ref[...] = hi_scr[...].reshape(si_ref.shape)


def _s5_weights(bbr, bbi, c_re, c_im):
    gn, p = bbr.shape
    g = gn // S5_N
    gpt = MXU_K // p
    n_kt = g // gpt
    eye = jnp.eye(gpt, dtype=F32)

    def b_blk(bb):
        return jnp.einsum('kgnp,gh->kgphn', bb.reshape(n_kt, gpt, S5_N, p), eye).reshape(n_kt, gpt * p, gpt * S5_N)

    def c_blk(cc):
        return jnp.einsum('kgpn,gh->kgnhp', cc.astype(F32).reshape(n_kt, gpt, p, S5_N), eye).reshape(
            n_kt, gpt * S5_N, gpt * p)

    bblk = jnp.concatenate([b_blk(bbr), b_blk(bbi)], axis=2).astype(BF16)
    cblk = jnp.concatenate([c_blk(c_re), -c_blk(c_im)], axis=1).astype(BF16)
    return bblk, cblk


def _s5_prompt(u, hbuf, ar, ai, bblk, cblk, d, wglu, bsz, seqlen, layer, depth, prev):
    width = wglu.shape[0]
    n_kt = bblk.shape[0]
    sw = bblk.shape[2] // 2
    ct = _tile(seqlen, 256)
    nc = seqlen // ct
    gn = ar.shape[1]
    full = lambda shape: pl.BlockSpec(shape, lambda b, c: (0,) * len(shape))
    st_spec = pl.BlockSpec((1, 1, 1, gn), lambda b, c: (layer, b, 0, 0))
    st_shape = jax.ShapeDtypeStruct((depth, bsz, 1, gn), F32)
    return _pcall(
        functools.partial(_s5_kernel, seq=True, nc=nc, n_kt=n_kt, sw=sw), (u, ar, ai, bblk, cblk, d, wglu),
        [pl.BlockSpec((ct, width), lambda b, c: (b * nc + c, OFF_SU // width)),
         full((1, gn)), full((1, gn)), full(bblk.shape), full(cblk.shape),
         full((1, width)), full(wglu.shape)],
        carried={0: hbuf, 1: prev[0], 2: prev[1]},
        grid=(bsz, nc),
        out_specs=[pl.BlockSpec((ct, width), lambda b, c: (b * nc + c, BR_S5)), st_spec, st_spec],
        out_shape=[jax.ShapeDtypeStruct(hbuf.shape, BF16), st_shape, st_shape],
        scratch_shapes=[pltpu.VMEM((1, gn), F32), pltpu.VMEM((1, gn), F32)],
        compiler_params=_cparams(("parallel", "arbitrary"), 56),
        name="s5_prompt",
    )


def _s5_sample(u, tp, ar, ai, bblk, cblk, d, wglu, h0r, h0i, layer, h_buf, prev):
    width = wglu.shape[0]
    n_kt = bblk.shape[0]
    sw = bblk.shape[2] // 2
    depth, bs, gn = h0r.shape
    tb = _tile(bs, 128, 8)
    r0 = tp // tb
    full = lambda shape: pl.BlockSpec(shape, lambda i: (0,) * len(shape))
    st_spec = pl.BlockSpec((1, tb, gn), lambda i: (layer, i, 0))
    st_shape = jax.ShapeDtypeStruct((depth, bs, gn), F32)
    return _pcall(
        functools.partial(_s5_kernel, seq=False, nc=1, n_kt=n_kt, sw=sw),
        (u, ar, ai, bblk, cblk, d, wglu, h0r, h0i),
        [pl.BlockSpec((tb, width), lambda i: (r0 + i, OFF_SU // width)),
         full((1, gn)), full((1, gn)), full(bblk.shape), full(cblk.shape),
         full((1, width)), full(wglu.shape), st_spec, st_spec],
        carried={0: h_buf, 1: prev[0], 2: prev[1]},
        grid=(bs // tb,),
        out_specs=[pl.BlockSpec((tb, width), lambda i: (r0 + i, BR_S5)), st_spec, st_spec],
        out_shape=[jax.ShapeDtypeStruct(h_buf.shape, BF16), st_shape, st_shape],
        compiler_params=_cparams(("parallel",), 56),
        name="s5_sample",
    )


SAMPLE_BLOCK = 8


def _state_step(s, dcol, kcol, vrow, qcol):
    s_new = dcol * s + kcol * vrow
    return s_new, jnp.sum(qcol * s_new, axis=0, keepdims=True)


def _zero_other_layers(ref, slot):
    for l in range(ref.shape[0]):
        if l != slot:
            ref[l] = jnp.zeros(ref.shape[1:], ref.dtype)


def _ret_step_kernel(gam_ref, q_ref, k_ref, v_ref, g_ref, cos_ref, sin_ref, st_ref, h_ref, so_ref, o_scr, *, slot):
    _zero_other_layers(so_ref, slot)
    cs, sn = cos_ref[...], sin_ref[...]
    for h in range(N_HEADS):
        qr = _rope(q_ref[:, h * DK:(h + 1) * DK], cs, sn)
        kr = _rope(k_ref[:, h * DK:(h + 1) * DK], cs, sn) * (DK ** -0.5)
        for i in range(SAMPLE_BLOCK):
            s_new, o = _state_step(st_ref[0, i, h], gam_ref[h], _row_to_col(kr[i:i + 1, :]),
                                   v_ref[i:i + 1, h * DV:(h + 1) * DV], _row_to_col(qr[i:i + 1, :]))
            so_ref[slot, i, h] = s_new
            o_scr[i:i + 1, h * DV:(h + 1) * DV] = o
    for h in range(N_HEADS):
        cols = slice(h * DV, (h + 1) * DV)
        h_ref[:, cols] = (_group_norm(o_scr[:, cols]) * _silu(g_ref[:, cols])).astype(h_ref.dtype)


def _mlstm_step_kernel(gb_ref, q_ref, k_ref, v_ref, mo_ref, ug_ref, nw_ref, c_ref, n_ref, m_ref,
                       h_ref, co_ref, no_ref, mo_out_ref, o_scr, *, slot):
    _zero_other_layers(co_ref, slot)
    ug = ug_ref[...]
    m_all = m_ref[0]
    m_new_cols = []
    for h in range(N_HEADS):
        ig = ug[:, h:h + 1] + gb_ref[0, h]
        lf = _log_sigmoid(ug[:, N_HEADS + h:N_HEADS + h + 1] + gb_ref[1, h])
        m_old = m_all[:, h:h + 1]
        inter = m_old + lf
        mi = jnp.maximum(inter, ig)
        w = jnp.exp(ig - mi)
        wi = jnp.exp(inter - mi)
        m_new_cols.append(mi)
        lim = jnp.exp(-mi)
        qh = q_ref[:, h * DK:(h + 1) * DK]
        kh = k_ref[:, h * DK:(h + 1) * DK] * (DK ** -0.5)
        for i in range(SAMPLE_BLOCK):
            wk = w[i:i + 1, :] * kh[i:i + 1, :]
            c_new, num = _state_step(c_ref[0, i, h], wi[i:i + 1, :], _row_to_col(wk),
                                     v_ref[i:i + 1, h * DV:(h + 1) * DV], _row_to_col(qh[i:i + 1, :]))
            n_new = wi[i:i + 1, :] * n_ref[0, i, h:h + 1, :] + wk
            nq = jnp.sum(qh[i:i + 1, :] * n_new, axis=1, keepdims=True)
            co_ref[slot, i, h] = c_new
            no_ref[0, i, h:h + 1, :] = n_new
            o_scr[i:i + 1, h * DV:(h + 1) * DV] = num / jnp.maximum(jnp.abs(nq), lim[i:i + 1, :])
    mo_out_ref[0] = jnp.concatenate(m_new_cols, axis=1)
    for h in range(N_HEADS):
        cols = slice(h * DV, (h + 1) * DV)
        h_ref[:, cols] = (_rms(o_scr[:, cols], nw_ref[:, cols]) * _sigmoid(mo_ref[:, cols])).astype(h_ref.dtype)


def _hgrn_step_kernel(lg_ref, q_ref, f_ref, i_ref, g_ref, nw_ref, st_ref, h_ref, so_ref, o_scr, *, layer, slot):
    _zero_other_layers(so_ref, slot)
    lb_all = _hgrn_lower_bound(lg_ref[...], layer)
    for h in range(N_HEADS):
        hlf, hk = _hgrn_gates(f_ref[:, h * DK:(h + 1) * DK], lb_all[:, h * DK:(h + 1) * DK])
        dec = jnp.exp(hlf)
        qh = _silu(q_ref[:, h * DK:(h + 1) * DK])
        for i in range(SAMPLE_BLOCK):
            s_new, o = _state_step(st_ref[0, i, h], _row_to_col(dec[i:i + 1, :]), _row_to_col(hk[i:i + 1, :]),
                                   i_ref[i:i + 1, h * DV:(h + 1) * DV], _row_to_col(qh[i:i + 1, :]))
            so_ref[slot, i, h] = s_new
            o_scr[i:i + 1, h * DV:(h + 1) * DV] = o
    for h in range(N_HEADS):
        cols = slice(h * DV, (h + 1) * DV)
        h_ref[:, cols] = (_rms(o_scr[:, cols], nw_ref[:, cols]) * _silu(g_ref[:, cols])).astype(h_ref.dtype)


def _seg(width, off, tp):
    return pl.BlockSpec((SAMPLE_BLOCK, width), lambda i: (tp // SAMPLE_BLOCK + i, off // width))


def _state_spec(layer, tail):
    return pl.BlockSpec((1, SAMPLE_BLOCK) + tail, lambda i: (layer, i) + (0,) * len(tail))


_MAT = (N_HEADS, DK, DV)
_SMEM = pl.BlockSpec(memory_space=pltpu.SMEM)


def _whole(shape):
    return pl.BlockSpec(shape, lambda i: (0,) * len(shape))


def _h_rows_spec(tp, branch):
    return pl.BlockSpec((SAMPLE_BLOCK, _HW), lambda i: (tp // SAMPLE_BLOCK + i, branch))


def _new_mat_state(layer, depth, prev):
    if prev is None:
        return pl.BlockSpec((depth, SAMPLE_BLOCK) + _MAT, lambda i: (0, i, 0, 0, 0)), layer
    return _state_spec(layer, _MAT), 0


def _ret_sample(u, tp, gamma, cos2, sin2, state, layer, h_buf, prev):
    depth, bs = state.shape[:2]
    spec, slot = _new_mat_state(layer, depth, prev)
    return _pcall(
        functools.partial(_ret_step_kernel, slot=slot), (gamma, u, u, u, u, cos2, sin2, state),
        [_SMEM, _seg(_KW, OFF_RQ, tp), _seg(_KW, OFF_RK, tp), _seg(_HW, OFF_RV, tp), _seg(_HW, OFF_RG, tp),
         _whole((1, DK)), _whole((1, DK)), _state_spec(layer, _MAT)],
        carried={0: h_buf, 1: prev},
        grid=(bs // SAMPLE_BLOCK,),
        out_specs=[_h_rows_spec(tp, BR_RET), spec],
        out_shape=[jax.ShapeDtypeStruct(h_buf.shape, BF16), jax.ShapeDtypeStruct(state.shape, F32)],
        scratch_shapes=[pltpu.VMEM((SAMPLE_BLOCK, _HW), F32)],
        compiler_params=_cparams(("parallel",), 48),
        name="ret_sample",
    )


def _mlstm_sample(u, ug, tp, gate_bias, norm_w, st_c, st_n, st_m, layer, h_buf, prev):
    depth, bs = st_c.shape[:2]
    spec, slot = _new_mat_state(layer, depth, prev[0])
    return _pcall(
        functools.partial(_mlstm_step_kernel, slot=slot), (gate_bias, u, u, u, u, ug, norm_w, st_c, st_n, st_m),
        [_SMEM, _seg(_KW, OFF_MQ, tp), _seg(_KW, OFF_MK, tp), _seg(_HW, OFF_MV, tp), _seg(_HW, OFF_MO, tp),
         pl.BlockSpec((SAMPLE_BLOCK, GATE_PAD), lambda i: (tp // SAMPLE_BLOCK + i, 0)), _whole((1, _HW)),
         _state_spec(layer, _MAT), _state_spec(layer, (N_HEADS, DK)), _state_spec(layer, (N_HEADS,))],
        carried={0: h_buf, 1: prev[0], 2: prev[1], 3: prev[2]},
        grid=(bs // SAMPLE_BLOCK,),
        out_specs=[_h_rows_spec(tp, BR_ML), spec,
                   _state_spec(layer, (N_HEADS, DK)), _state_spec(layer, (N_HEADS,))],
        out_shape=[jax.ShapeDtypeStruct(h_buf.shape, BF16), jax.ShapeDtypeStruct(st_c.shape, F32),
                   jax.ShapeDtypeStruct(st_n.shape, F32), jax.ShapeDtypeStruct(st_m.shape, F32)],
        scratch_shapes=[pltpu.VMEM((SAMPLE_BLOCK, _HW), F32)],
        compiler_params=_cparams(("parallel",), 48),
        name="mlstm_sample",
    )


def _hgrn_sample(u, tp, logits, norm_w, state, layer, h_buf, prev):
    depth, bs = state.shape[:2]
    spec, slot = _new_mat_state(layer, depth, prev)
    return _pcall(
        functools.partial(_hgrn_step_kernel, layer=layer, slot=slot), (logits, u, u, u, u, norm_w, state),
        [_whole(logits.shape), _seg(_KW, OFF_HQ, tp), _seg(_KW, OFF_HF, tp), _seg(_HW, OFF_HI, tp),
         _seg(_HW, OFF_HG, tp), _whole((1, _HW)), _state_spec(layer, _MAT)],
        carried={0: h_buf, 1: prev},
        grid=(bs // SAMPLE_BLOCK,),
        out_specs=[_h_rows_spec(tp, BR_HG), spec],
        out_shape=[jax.ShapeDtypeStruct(h_buf.shape, BF16), jax.ShapeDtypeStruct(state.shape, F32)],
        scratch_shapes=[pltpu.VMEM((SAMPLE_BLOCK, _HW), F32)],
        compiler_params=_cparams(("parallel",), 48),
        name="hgrn_sample",
    )


def _attn_prompt_kernel(q_ref, k_ref, v_ref, o_ref, *, scale):
    s = _dot_nt(q_ref[...].astype(BF16), k_ref[0, 0].astype(BF16)) * scale
    e = jnp.exp(s - jnp.max(s, axis=1, keepdims=True))
    p = e / jnp.sum(e, axis=1, keepdims=True)
    o_ref[...] = _dot(p.astype(BF16), v_ref[0, 0].astype(BF16)).astype(o_ref.dtype)


def _attn_prompt(q, kvbuf, obuf, bsz, seqlen, layer):
    t, d = q.shape
    dh = d // N_HEADS
    mem_len = kvbuf.shape[2] // bsz
    tq = _tile(seqlen, 512)
    nq = seqlen // tq
    return _pcall(
        functools.partial(_attn_prompt_kernel, scale=dh ** -0.5), (q, kvbuf, kvbuf),
        [pl.BlockSpec((tq, dh), lambda b, h, i: (b * nq + i, h)),
         pl.BlockSpec((1, 1, mem_len, dh), lambda b, h, i: (layer, 0, b, h)),
         pl.BlockSpec((1, 1, mem_len, dh), lambda b, h, i: (layer, 1, b, h))],
        carried={0: obuf},
        grid=(bsz, N_HEADS, nq),
        out_specs=pl.BlockSpec((tq, dh), lambda b, h, i: (b * nq + i, h)),
        out_shape=jax.ShapeDtypeStruct((t, d), BF16),
        compiler_params=_cparams(("parallel", "parallel", "arbitrary"), 32),
        name="attn_prompt",
    )


def _attn_sample_kernel(q_ref, k_ref, v_ref, o_ref, *, scale):
    for h in range(q_ref.shape[1]):
        k = k_ref[0, 0, :, h, :]
        s = jnp.sum(k * q_ref[0, h:h + 1, :], axis=1, keepdims=True) * scale
        e = jnp.exp(s - jnp.max(s, axis=0, keepdims=True))
        p = e / jnp.sum(e, axis=0, keepdims=True)
        o_ref[0, h:h + 1, :] = jnp.sum(p * v_ref[0, 0, :, h, :], axis=0, keepdims=True)


def _attn_sample(q4, cache_k, cache_v, layer):
    bs, nh, dh = q4.shape
    mem_len = cache_k.shape[2]
    kv_spec = pl.BlockSpec((1, 1, mem_len, nh, dh), lambda i: (layer, i, 0, 0, 0))
    return pl.pallas_call(
        functools.partial(_attn_sample_kernel, scale=dh ** -0.5),
        grid=(bs,),
        in_specs=[pl.BlockSpec((1, nh, dh), lambda i: (i, 0, 0)), kv_spec, kv_spec],
        out_specs=pl.BlockSpec((1, nh, dh), lambda i: (i, 0, 0)),
        out_shape=jax.ShapeDtypeStruct((bs, nh, dh), F32),
        compiler_params=_cparams(("parallel",), 48),
        name="attn_sample",
    )(q4, cache_k, cache_v)


def _rope_tables(pos):
    half = DK // 2
    inv = ROPE_BASE ** (-jnp.arange(half, dtype=F32) / half)
    ang = pos.astype(F32)[:, None] * inv[None, :]
    cos, sin = jnp.cos(ang), jnp.sin(ang)
    return jnp.concatenate([cos, cos], axis=1), jnp.concatenate([-sin, sin], axis=1)


def _retention_tables(c):
    lg = jnp.log1p(-jnp.exp2(-5.0 - jnp.arange(N_HEADS, dtype=F32)))
    idx = jnp.arange(c, dtype=F32)
    diff = idx[:, None] - idx[None, :]
    dmat = jnp.where(diff[None] >= 0, jnp.exp(jnp.maximum(diff, 0.0)[None] * lg[:, None, None]), 0.0)
    e1 = jnp.exp((idx[None, :] + 1.0) * lg[:, None])
    e2 = jnp.exp((c - 1.0 - idx)[None, :] * lg[:, None])
    e3 = jnp.broadcast_to(jnp.exp(c * lg)[:, None], (N_HEADS, c))
    dec = jnp.stack([e1, e2, e3] + [jnp.zeros_like(e1)] * 5, axis=-1)
    return dmat, dec


def kernel(x_prompt, x_sample, mem_prompt, state_ret, state_mlstm_c, state_mlstm_n, state_mlstm_m, state_s5_re, state_s5_im, state_hgrn, cache_mem_k, cache_mem_v, norm_w, ffn_w_in, ffn_w_out, w_in, ml_gate_bias, ml_norm_w, s5_lambda_re, s5_lambda_im, s5_log_step, s5_b_re, s5_b_im, s5_c_re, s5_c_im, s5_d, s5_w_glu, hg_lb_logits, hg_norm_w, w_branch, w_out, xa_mem_norm, xa_wq, xa_wkv, xa_wo):
    bsz, seqlen, d = x_prompt.shape
    bs = x_sample.shape[0]
    depth = norm_w.shape[0]
    mem_len = mem_prompt.shape[1]
    tp = bsz * seqlen
    gn = s5_lambda_re.shape[1] * s5_lambda_re.shape[2]
    seg_gate = OFF_MO

    x = jnp.concatenate([x_prompt.reshape(tp, d), x_sample.reshape(bs, d)], axis=0)
    mem = mem_prompt.reshape(bsz * mem_len, d)
    s5_in_re = state_s5_re.reshape(depth, bs, gn)
    s5_in_im = state_s5_im.reshape(depth, bs, gn)

    cos_p, sin_p = _rope_tables(jnp.arange(seqlen, dtype=jnp.int32))
    cos_s, sin_s = _rope_tables(PAST_LEN + jnp.arange(1, dtype=jnp.int32))
    dmat, dec = _retention_tables(CHUNK)
    _, dec1 = _retention_tables(1)
    gamma = dec1[:, 0, 0]

    zf = lambda *shape: jnp.zeros(shape, F32)
    t = tp + bs
    p_ret = zf(depth, bsz, N_HEADS, DK, DV)
    p_hg = zf(depth, bsz, N_HEADS, DK, DV)
    p_ml = [zf(depth, bsz, N_HEADS, DK, DV), zf(depth, bsz, N_HEADS, 1, DK), zf(depth, bsz, N_HEADS, 1, 1)]
    p_s5 = [zf(depth, bsz, 1, gn), zf(depth, bsz, 1, gn)]
    kvbuf = zf(depth, 2, bsz * mem_len, d)
    s_ret = s_hg = None
    s_ml = [None, zf(depth, bs, N_HEADS, DK), zf(depth, bs, N_HEADS)]
    s_s5 = [zf(depth, bs, gn), zf(depth, bs, gn)]
    hbuf = jnp.zeros((t, N_BRANCH * _HW), BF16)
    o = jnp.zeros((t, d), BF16)
    ffn_w_out_bf = ffn_w_out.astype(BF16)
    w_in_t = jnp.swapaxes(w_in, 1, 2)
    w_branch_bf, w_out_bf = w_branch.astype(BF16), w_out.astype(BF16)
    xa_wq_bf, xa_wkv_bf, xa_wo_bf = xa_wq.astype(BF16), xa_wkv.astype(BF16), xa_wo.astype(BF16)
    for l in range(depth):
        nw = norm_w[l]
        x = _ffn(x, nw[0:2], ffn_w_in, l, 0, ffn_w_out_bf)
        n_gate = 2 * N_HEADS
        u, ug = _in_proj(x, nw[2:3], w_in_t, l, seg_gate, n_gate)
        ugt = ug[:tp, :n_gate].T
        mlw = ml_norm_w[l].reshape(1, -1)
        hgw = hg_norm_w[l].reshape(1, -1)
        ar, ai, bbr, bbi = _s5_prep(s5_lambda_re[l], s5_lambda_im[l], s5_log_step[l], s5_b_re[l], s5_b_im[l])
        ar, ai = ar.reshape(1, gn), ai.reshape(1, gn)
        bblk, cblk = _s5_weights(bbr, bbi, s5_c_re[l], s5_c_im[l])
        s5d = s5_d[l].astype(F32).reshape(1, -1)
        wglu = s5_w_glu[l].astype(BF16)
        hbuf, p_ret = _ret_prompt(u, hbuf, bsz, seqlen, cos_p, sin_p, dmat, dec, l, depth, p_ret)
        hbuf, s_ret = _ret_sample(u, tp, gamma, cos_s, sin_s, state_ret, l, hbuf, s_ret)
        hbuf, *p_ml = _mlstm_prompt(u, hbuf, ug, ugt, ml_gate_bias[l], mlw, bsz, seqlen, l, depth, p_ml)
        hbuf, *s_ml = _mlstm_sample(u, ug, tp, ml_gate_bias[l], mlw, state_mlstm_c, state_mlstm_n,
                                    state_mlstm_m, l, hbuf, s_ml)
        hbuf, *p_s5 = _s5_prompt(u, hbuf, ar, ai, bblk, cblk, s5d, wglu, bsz, seqlen, l, depth, p_s5)
        hbuf, *s_s5 = _s5_sample(u, tp, ar, ai, bblk, cblk, s5d, wglu, s5_in_re, s5_in_im, l, hbuf, s_s5)
        hbuf, p_hg = _hgrn_prompt(u, hbuf, hg_lb_logits, hgw, bsz, seqlen, l, p_hg)
        hbuf, s_hg = _hgrn_sample(u, tp, hg_lb_logits, hgw, state_hgrn, l, hbuf, s_hg)
        x = _merge(hbuf, u, w_branch_bf, w_out_bf, l, x, nw[3:4])
        kvbuf = _mem_kv(mem, xa_mem_norm[l].reshape(1, d), xa_wkv_bf, l, depth, kvbuf)
        q = _norm_matmul(x, nw[4:5], xa_wq_bf, l, "xa_q")
        o = _attn_prompt(q, kvbuf, o, bsz, seqlen, l)
        o_s = _attn_sample(q[tp:].reshape(bs, N_HEADS, d // N_HEADS), cache_mem_k, cache_mem_v, l)
        o = lax.dynamic_update_slice(o, o_s.reshape(bs, d).astype(BF16), (tp, 0))
        x = _proj_res(o, xa_wo_bf, l, x, nw[5:6])
        x = _ffn(x, nw[6:8], ffn_w_in, l, 1, ffn_w_out_bf)

    mem_shape = (depth, bsz, mem_len, N_HEADS, d // N_HEADS)
    s5_shape = (depth, -1, gn // S5_N, S5_N)
    return (x[:tp].reshape(bsz, seqlen, d), x[tp:].reshape(bs, 1, d),
            p_ret, p_ml[0], p_ml[1].reshape(depth, bsz, N_HEADS, DK), p_ml[2].reshape(depth, bsz, N_HEADS),
            p_s5[0].reshape(s5_shape), p_s5[1].reshape(s5_shape), p_hg,
            kvbuf[:, 0].reshape(mem_shape), kvbuf[:, 1].reshape(mem_shape),
            s_ret, s_ml[0], s_ml[1], s_ml[2], s_s5[0].reshape(s5_shape), s_s5[1].reshape(s5_shape), s_hg)
```

```python
import functools
import math

import jax
import jax.numpy as jnp
from jax import lax
from jax.experimental import pallas as pl
from jax.experimental.pallas import tpu as pltpu

F32 = jnp.float32
BF16 = jnp.bfloat16

CHUNK = 128
N_BRANCH = 4
N_HEADS = 4
DK = 128
DV = 256
S5_GROUP = 16
S5_N = 64
ROPE_BASE = 10000.0
PAST_LEN = 16384

V7X_VMEM_BYTES = 64 * 1024 * 1024
MXU_K = 256
SUBLANES = 8
MIB = 1024 * 1024

OFF_RQ, OFF_RK, OFF_RV, OFF_RG = 0, 512, 1024, 2048
OFF_MQ, OFF_MK, OFF_MV, OFF_MO = 3072, 3584, 4096, 5120
OFF_SU = 6144
OFF_HQ, OFF_HF, OFF_HI, OFF_HG = 7168, 7680, 8192, 9216
OFF_GATES = 10240
N_MAIN = 18432
GATE_PAD = 128


def _cparams(sem, vmem_mib):
    return pltpu.CompilerParams(dimension_semantics=sem, vmem_limit_bytes=vmem_mib * MIB)


_ANY = pl.BlockSpec(memory_space=pl.ANY)


def _pcall(kernel_fn, inputs, in_specs, carried=None, **kw):
    carried = {o: a for o, a in (carried or {}).items() if a is not None}
    idxs = sorted(carried)
    n_in = len(inputs)

    def body(*refs):
        return kernel_fn(*refs[:n_in], *refs[n_in + len(idxs):])

    return pl.pallas_call(
        body if idxs else kernel_fn,
        in_specs=list(in_specs) + [_ANY] * len(idxs),
        input_output_aliases={n_in + j: o for j, o in enumerate(idxs)},
        **kw)(*inputs, *[carried[o] for o in idxs])


def _tile(n, cap, mult=16):
    best = None
    for t in range(mult, min(n, cap) + 1, mult):
        if n % t == 0:
            best = t
    return best if best is not None else n


def _dot(a, b):
    return jnp.dot(a, b, preferred_element_type=F32)


def _dot_nt(a, b):
    return lax.dot_general(a, b, (((1,), (1,)), ((), ())), preferred_element_type=F32)


def _rms(x, w, eps=1e-6):
    return x * lax.rsqrt(jnp.mean(x * x, axis=-1, keepdims=True) + eps) * w


def _sigmoid(x):
    return 1.0 / (1.0 + jnp.exp(-x))


def _silu(x):
    return x * _sigmoid(x)


def _log_sigmoid(x):
    return jnp.minimum(x, 0.0) - jnp.log1p(jnp.exp(-jnp.abs(x)))


def _logaddexp(a, b):
    return jnp.maximum(a, b) + jnp.log1p(jnp.exp(-jnp.abs(a - b)))


def _gelu_tanh(x):
    return 0.5 * x * (1.0 + jnp.tanh(math.sqrt(2.0 / math.pi) * (x + 0.044715 * (x * x * x))))


def _row_to_col(row):
    n = row.shape[1]
    r = lax.broadcasted_iota(jnp.int32, (n, n), 0)
    c = lax.broadcasted_iota(jnp.int32, (n, n), 1)
    return jnp.sum(jnp.where(r == c, row, 0.0), axis=1, keepdims=True)


def _cumsum_rows(x):
    c = x.shape[0]
    r = lax.broadcasted_iota(jnp.int32, (c, c), 0)
    k = lax.broadcasted_iota(jnp.int32, (c, c), 1)
    tri = jnp.where(r >= k, 1.0, 0.0).astype(BF16)
    hi = x.astype(BF16)
    r1 = x - hi.astype(F32)
    mid = r1.astype(BF16)
    lo = (r1 - mid.astype(F32)).astype(BF16)
    return _dot(tri, hi) + _dot(tri, mid) + _dot(tri, lo)


def _ffn_up_kernel(x_ref, nw_ref, wg_ref, wu_ref, h_ref, xn_ref, *, rc):
    @pl.when(pl.program_id(1) == 0)
    def _():
        xn_ref[...] = _rms(x_ref[...], nw_ref[...]).astype(BF16)

    wg = wg_ref[0, 0].astype(BF16)
    wu = wu_ref[0, 0].astype(BF16)
    for r in range(0, x_ref.shape[0], rc):
        xn = xn_ref[r:r + rc, :]
        h_ref[r:r + rc, :] = (_silu(_dot(xn, wg)) * _dot(xn, wu)).astype(h_ref.dtype)


def _ffn_down_kernel(h_ref, w_ref, x_ref, nw_ref, o_ref, *, rc):
    for r in range(0, x_ref.shape[0], rc):
        rows = slice(r, r + rc)
        o_ref[rows, :] = x_ref[rows, :] + 0.5 * _rms(_dot(h_ref[rows, :], w_ref[0, 0]), nw_ref[...])


def _ffn(x, nw2, w_in, layer, which, w_out):
    t, d = x.shape
    dff = w_out.shape[2]
    tm = _tile(t, 2080)
    rc = _tile(tm, 260)
    tf = _tile(dff, 512, 128)
    nf = dff // tf
    once = pl.Buffered(1)
    hidden = pl.pallas_call(
        functools.partial(_ffn_up_kernel, rc=rc),
        grid=(t // tm, nf),
        in_specs=[pl.BlockSpec((tm, d), lambda i, f: (i, 0), pipeline_mode=once),
                  pl.BlockSpec((1, d), lambda i, f: (0, 0)),
                  pl.BlockSpec((1, 1, d, tf), lambda i, f: (layer, which, 0, f)),
                  pl.BlockSpec((1, 1, d, tf), lambda i, f: (layer, which, 0, nf + f))],
        out_specs=pl.BlockSpec((tm, tf), lambda i, f: (i, f)),
        out_shape=jax.ShapeDtypeStruct((t, dff), BF16),
        scratch_shapes=[pltpu.VMEM((tm, d), BF16)],
        compiler_params=_cparams(("parallel", "arbitrary"), 56),
        name="ffn_up",
    )(x, nw2[0:1], w_in, w_in)
    tm2 = _tile(t, 416)
    return pl.pallas_call(
        functools.partial(_ffn_down_kernel, rc=_tile(tm2, 208)),
        grid=(t // tm2,),
        in_specs=[pl.BlockSpec((tm2, dff), lambda i: (i, 0)),
                  pl.BlockSpec((1, 1, dff, d), lambda i: (layer, which, 0, 0), pipeline_mode=once),
                  pl.BlockSpec((tm2, d), lambda i: (i, 0)),
                  pl.BlockSpec((1, d), lambda i: (0, 0))],
        out_specs=pl.BlockSpec((tm2, d), lambda i: (i, 0)),
        out_shape=jax.ShapeDtypeStruct((t, d), F32),
        compiler_params=_cparams(("parallel",), 56),
        name="ffn_down",
    )(hidden, w_out, x, nw2[1:2])


def _norm_matmul_kernel(x_ref, nw_ref, w_ref, o_ref, xn_ref):
    @pl.when(pl.program_id(1) == 0)
    def _():
        xn_ref[...] = _rms(x_ref[...], nw_ref[...]).astype(BF16)

    o_ref[...] = _dot(xn_ref[...], w_ref[0]).astype(o_ref.dtype)


def _norm_matmul(x, nw, w, layer, name):
    t, d = x.shape
    n = w.shape[2]
    tm = _tile(t, 1040)
    tn = _tile(n, 1024, 128)
    return pl.pallas_call(
        _norm_matmul_kernel,
        grid=(t // tm, n // tn),
        in_specs=[pl.BlockSpec((tm, d), lambda i, j: (i, 0)),
                  pl.BlockSpec((1, d), lambda i, j: (0, 0)),
                  pl.BlockSpec((1, d, tn), lambda i, j: (layer, 0, j))],
        out_specs=pl.BlockSpec((tm, tn), lambda i, j: (i, j)),
        out_shape=jax.ShapeDtypeStruct((t, n), F32),
        scratch_shapes=[pltpu.VMEM((tm, d), BF16)],
        compiler_params=_cparams(("parallel", "arbitrary"), 48),
        name=name,
    )(x, nw, w)


def _mem_kv_kernel(x_ref, nw_ref, w_ref, o_ref, xn_ref):
    @pl.when(pl.program_id(1) == 0)
    def _():
        xn_ref[...] = _rms(x_ref[...], nw_ref[...]).astype(BF16)

    o_ref[0, 0] = _dot(xn_ref[...], w_ref[0])


def _mem_kv(mem, nw, w, layer, depth, prev):
    t, d = mem.shape
    tm = _tile(t, 512)
    tn = _tile(d, 1024, 128)
    npk = d // tn
    return _pcall(
        _mem_kv_kernel, (mem, nw, w),
        [pl.BlockSpec((tm, d), lambda i, j: (i, 0)),
         pl.BlockSpec((1, d), lambda i, j: (0, 0)),
         pl.BlockSpec((1, d, tn), lambda i, j: (layer, 0, j))],
        carried={0: prev},
        grid=(t // tm, 2 * npk),
        out_specs=pl.BlockSpec((1, 1, tm, tn), lambda i, j: (layer, j // npk, i, j % npk)),
        out_shape=jax.ShapeDtypeStruct((depth, 2, t, d), F32),
        scratch_shapes=[pltpu.VMEM((tm, d), BF16)],
        compiler_params=_cparams(("parallel", "arbitrary"), 48),
        name="mem_kv",
    )


def _in_proj_kernel(x_ref, nw_ref, wt_ref, wgt_ref, o_ref, og_ref, xn_ref):
    @pl.when(pl.program_id(1) == 0)
    def _():
        xn = _rms(x_ref[...], nw_ref[...]).astype(BF16)
        xn_ref[...] = xn
        og_ref[...] = _dot_nt(xn, wgt_ref[0].astype(BF16))

    o_ref[...] = _dot_nt(xn_ref[...], wt_ref[0].astype(BF16))


def _in_proj(x, nw, w_in_t, layer, gate_off, n_gate):
    t, d = x.shape
    n = w_in_t.shape[1] - n_gate
    tm = _tile(t, 2080)
    tn = _tile(math.gcd(n, gate_off), 512, 128)
    n_plain = gate_off // tn
    return pl.pallas_call(
        _in_proj_kernel,
        grid=(t // tm, n // tn),
        in_specs=[pl.BlockSpec((tm, d), lambda i, j: (i, 0), pipeline_mode=pl.Buffered(1)),
                  pl.BlockSpec((1, d), lambda i, j: (0, 0)),
                  pl.BlockSpec((pl.Element(1), pl.Element(tn), pl.Element(d)),
                               lambda i, j: (layer, SUBLANES * (j * (tn // SUBLANES) + jnp.where(
                                   j < n_plain, 0, n_gate // SUBLANES)), 0)),
                  pl.BlockSpec((1, GATE_PAD, d), lambda i, j: (layer, gate_off // GATE_PAD, 0))],
        out_specs=[pl.BlockSpec((tm, tn), lambda i, j: (i, j)),
                   pl.BlockSpec((tm, GATE_PAD), lambda i, j: (i, 0))],
        out_shape=[jax.ShapeDtypeStruct((t, n), F32), jax.ShapeDtypeStruct((t, GATE_PAD), F32)],
        scratch_shapes=[pltpu.VMEM((tm, d), BF16)],
        compiler_params=_cparams(("parallel", "arbitrary"), 56),
        name="in_proj",
    )(x, nw, w_in_t, w_in_t)


def _merge_kernel(*refs):
    h_refs, g_refs = refs[:N_BRANCH], refs[N_BRANCH:2 * N_BRANCH]
    wb_ref, wo_ref, x_ref, nw_ref, o_ref = refs[2 * N_BRANCH:]
    merged = None
    for k in range(N_BRANCH):
        term = _sigmoid(g_refs[k][...]) * _dot(h_refs[k][...], wb_ref[0, k])
        merged = term if merged is None else merged + term
    o_ref[...] = x_ref[...] + _rms(_dot(merged.astype(BF16), wo_ref[0]), nw_ref[...])


def _merge(hbuf, u, w_branch, w_out, layer, x, nw):
    t, d = x.shape
    bw = hbuf.shape[1] // N_BRANCH
    tm = _tile(t, 208)
    gate_blk = OFF_GATES // d
    once = pl.Buffered(1)
    return pl.pallas_call(
        _merge_kernel,
        grid=(t // tm,),
        in_specs=[pl.BlockSpec((tm, bw), lambda i, k=k: (i, k)) for k in range(N_BRANCH)] + [
            pl.BlockSpec((tm, d), lambda i, k=k: (i, gate_blk + k)) for k in range(N_BRANCH)] + [
            pl.BlockSpec((1,) + w_branch.shape[1:], lambda i: (layer, 0, 0, 0), pipeline_mode=once),
            pl.BlockSpec((1,) + w_out.shape[1:], lambda i: (layer, 0, 0), pipeline_mode=once),
            pl.BlockSpec((tm, d), lambda i: (i, 0)),
            pl.BlockSpec((1, d), lambda i: (0, 0))],
        out_specs=pl.BlockSpec((tm, d), lambda i: (i, 0)),
        out_shape=jax.ShapeDtypeStruct((t, d), F32),
        compiler_params=_cparams(("parallel",), 56),
        name="merge",
    )(*([hbuf] * N_BRANCH), *([u] * N_BRANCH), w_branch, w_out, x, nw)


def _proj_res_kernel(a_ref, w_ref, x_ref, nw_ref, o_ref):
    o_ref[...] = x_ref[...] + _rms(_dot(a_ref[...], w_ref[0]), nw_ref[...])


def _proj_res(a, w, layer, x, nw):
    t, d = x.shape
    tm = _tile(t, 640)
    return pl.pallas_call(
        _proj_res_kernel,
        grid=(t // tm,),
        in_specs=[pl.BlockSpec((tm, d), lambda i: (i, 0)),
                  pl.BlockSpec((1, d, d), lambda i: (layer, 0, 0), pipeline_mode=pl.Buffered(1)),
                  pl.BlockSpec((tm, d), lambda i: (i, 0)),
                  pl.BlockSpec((1, d), lambda i: (0, 0))],
        out_specs=pl.BlockSpec((tm, d), lambda i: (i, 0)),
        out_shape=jax.ShapeDtypeStruct((t, d), F32),
        compiler_params=_cparams(("parallel",), 56),
        name="xa_out",
    )(a, w, x, nw)


def _rope(x, cs, sn):
    return x * cs + pltpu.roll(x, x.shape[1] // 2, 1) * sn


def _group_norm(o, eps=1e-5):
    oc = o - jnp.mean(o, axis=-1, keepdims=True)
    return oc * lax.rsqrt(jnp.mean(oc * oc, axis=-1, keepdims=True) + eps)


def _ret_kernel(q_ref, k_ref, v_ref, g_ref, cos_ref, sin_ref, dm_ref, dec_ref, h_ref, st_ref, s_ref, *, nc):
    c = pl.program_id(1)

    @pl.when(c == 0)
    def _():
        s_ref[...] = jnp.zeros_like(s_ref)

    cs, sn = cos_ref[...], sin_ref[...]
    hs = range(N_HEADS)
    kc = [slice(h * DK, (h + 1) * DK) for h in hs]
    vc = [slice(h * DV, (h + 1) * DV) for h in hs]
    s = [s_ref[h] for h in hs]
    dec = [dec_ref[h] for h in hs]
    qb = [_rope(q_ref[:, kc[h]], cs, sn).astype(BF16) for h in hs]
    kr = [_rope(k_ref[:, kc[h]], cs, sn) * (DK ** -0.5) for h in hs]
    vb = [v_ref[:, vc[h]].astype(BF16) for h in hs]
    sc = [_dot_nt(qb[h], kr[h].astype(BF16)) * dm_ref[h] for h in hs]
    qs = [_dot(qb[h], s[h].astype(BF16)) * dec[h][:, 0:1] for h in hs]
    o = [_dot(sc[h].astype(BF16), vb[h]) + qs[h] for h in hs]
    s_new = [dec[h][0:1, 2:3] * s[h] + _dot((kr[h] * dec[h][:, 1:2]).T.astype(BF16), vb[h]) for h in hs]
    out = [(_group_norm(o[h]) * _silu(g_ref[:, vc[h]])).astype(h_ref.dtype) for h in hs]
    for h in hs:
        s_ref[h] = s_new[h]
        h_ref[:, vc[h]] = out[h]

    @pl.when(c == nc - 1)
    def _():
        st_ref[...] = s_ref[...].reshape(st_ref.shape)


def _seg_spec(nc, width, off):
    return pl.BlockSpec((CHUNK, width), lambda b, c: (b * nc + c, off // width))


def _layer_state_spec(layer, tail):
    return pl.BlockSpec((1, 1) + tail, lambda b, c: (layer, b) + (0,) * len(tail))


def _const_spec(shape):
    return pl.BlockSpec(shape, lambda b, c: (0,) * len(shape))


_HW, _KW = N_HEADS * DV, N_HEADS * DK


BR_RET, BR_ML, BR_S5, BR_HG = range(N_BRANCH)


def _ret_prompt(u, hbuf, bsz, seqlen, cos2, sin2, dmat, dec, layer, depth, prev):
    nc = seqlen // CHUNK
    return _pcall(
        functools.partial(_ret_kernel, nc=nc), (u, u, u, u, cos2, sin2, dmat, dec),
        [_seg_spec(nc, _KW, OFF_RQ), _seg_spec(nc, _KW, OFF_RK), _seg_spec(nc, _HW, OFF_RV),
         _seg_spec(nc, _HW, OFF_RG),
         pl.BlockSpec((CHUNK, DK), lambda b, c: (c, 0)),
         pl.BlockSpec((CHUNK, DK), lambda b, c: (c, 0)),
         _const_spec(dmat.shape), _const_spec(dec.shape)],
        carried={0: hbuf, 1: prev},
        grid=(bsz, nc),
        out_specs=[pl.BlockSpec((CHUNK, _HW), lambda b, c: (b * nc + c, BR_RET)),
                   _layer_state_spec(layer, (N_HEADS, DK, DV))],
        out_shape=[jax.ShapeDtypeStruct(hbuf.shape, BF16),
                   jax.ShapeDtypeStruct((depth, bsz, N_HEADS, DK, DV), F32)],
        scratch_shapes=[pltpu.VMEM((N_HEADS, DK, DV), F32)],
        compiler_params=_cparams(("parallel", "arbitrary"), 32),
        name="ret_prompt",
    )


def _mlstm_kernel(gb_ref, q_ref, k_ref, v_ref, mo_ref, ug_ref, ugt_ref, nw_ref,
                  h_ref, c_out, n_out, m_out, cm_ref, nv_ref, m_ref, *, nc):
    c = pl.program_id(1)

    @pl.when(c == 0)
    def _():
        cm_ref[...] = jnp.zeros_like(cm_ref)
        nv_ref[...] = jnp.zeros_like(nv_ref)
        m_ref[...] = jnp.zeros_like(m_ref)

    n = CHUNK
    row = lax.broadcasted_iota(jnp.int32, (n, n), 0)
    col = lax.broadcasted_iota(jnp.int32, (n, n), 1)
    causal = row >= col
    ug = ug_ref[...]
    hs = range(N_HEADS)
    kc = [slice(h * DK, (h + 1) * DK) for h in hs]
    vc = [slice(h * DV, (h + 1) * DV) for h in hs]
    cm = [cm_ref[h] for h in hs]
    nv = [nv_ref[h] for h in hs]
    m = [m_ref[h] for h in hs]
    ig_row = [ugt_ref[h:h + 1, :] + gb_ref[0, h] for h in hs]
    lf_row = [_log_sigmoid(ugt_ref[N_HEADS + h:N_HEADS + h + 1, :] + gb_ref[1, h]) for h in hs]
    ig_col = [ug[:, h:h + 1] + gb_ref[0, h] for h in hs]
    lf_col = [_log_sigmoid(ug[:, N_HEADS + h:N_HEADS + h + 1] + gb_ref[1, h]) for h in hs]
    b_col = [jnp.sum(jnp.where(causal, lf_row[h], 0.0), axis=1, keepdims=True) for h in hs]
    b_row = [jnp.sum(jnp.where(row <= col, lf_col[h], 0.0), axis=0, keepdims=True) for h in hs]
    b_last = [b_col[h][n - 1:n, :] for h in hs]
    logd = [jnp.where(causal, b_col[h] - b_row[h] + ig_row[h], -jnp.inf) for h in hs]
    inter = [m[h] + b_col[h] for h in hs]
    mi = [jnp.maximum(inter[h], jnp.max(logd[h], axis=1, keepdims=True)) for h in hs]
    w = [jnp.exp(logd[h] - mi[h]) for h in hs]
    wi = [jnp.exp(inter[h] - mi[h]) for h in hs]
    q = [q_ref[:, kc[h]] for h in hs]
    k = [k_ref[:, kc[h]] * (DK ** -0.5) for h in hs]
    qb = [q[h].astype(BF16) for h in hs]
    vb = [v_ref[:, vc[h]].astype(BF16) for h in hs]
    a = [_dot_nt(qb[h], k[h].astype(BF16)) * w[h] for h in hs]
    qc = [_dot(qb[h], cm[h].astype(BF16)) for h in hs]
    num = [_dot(a[h].astype(BF16), vb[h]) + wi[h] * qc[h] for h in hs]
    nq = [jnp.sum(a[h], axis=1, keepdims=True) + wi[h] * jnp.sum(q[h] * nv[h], axis=1, keepdims=True) for h in hs]
    hh = [num[h] / jnp.maximum(jnp.abs(nq[h]), jnp.exp(-mi[h])) for h in hs]
    m_new = [mi[h][n - 1:n, :] for h in hs]
    wl = [jnp.exp(b_last[h] - b_col[h] + ig_col[h] - m_new[h]) for h in hs]
    dp = [jnp.exp(m[h] + b_last[h] - m_new[h]) for h in hs]
    kw = [k[h] * wl[h] for h in hs]
    cm_new = [dp[h] * cm[h] + _dot(kw[h].T.astype(BF16), vb[h]) for h in hs]
    nv_new = [dp[h] * nv[h] + jnp.sum(kw[h], axis=0, keepdims=True) for h in hs]
    out = [(_rms(hh[h], nw_ref[:, vc[h]]) * _sigmoid(mo_ref[:, vc[h]])).astype(h_ref.dtype) for h in hs]
    for h in hs:
        cm_ref[h] = cm_new[h]
        nv_ref[h] = nv_new[h]
        m_ref[h] = m_new[h]
        h_ref[:, vc[h]] = out[h]

    @pl.when(c == nc - 1)
    def _():
        c_out[...] = cm_ref[...].reshape(c_out.shape)
        n_out[...] = nv_ref[...].reshape(n_out.shape)
        m_out[...] = m_ref[...].reshape(m_out.shape)


def _mlstm_prompt(u, hbuf, ug, ugt, gate_bias, norm_w, bsz, seqlen, layer, depth, prev):
    nc = seqlen // CHUNK
    return _pcall(
        functools.partial(_mlstm_kernel, nc=nc), (gate_bias, u, u, u, u, ug, ugt, norm_w),
        [pl.BlockSpec(memory_space=pltpu.SMEM),
         _seg_spec(nc, _KW, OFF_MQ), _seg_spec(nc, _KW, OFF_MK), _seg_spec(nc, _HW, OFF_MV),
         _seg_spec(nc, _HW, OFF_MO),
         pl.BlockSpec((CHUNK, GATE_PAD), lambda b, c: (b * nc + c, 0)),
         pl.BlockSpec((2 * N_HEADS, CHUNK), lambda b, c: (0, b * nc + c)),
         _const_spec((1, _HW))],
        carried={0: hbuf, 1: prev[0], 2: prev[1], 3: prev[2]},
        grid=(bsz, nc),
        out_specs=[pl.BlockSpec((CHUNK, _HW), lambda b, c: (b * nc + c, BR_ML)),
                   _layer_state_spec(layer, (N_HEADS, DK, DV)),
                   _layer_state_spec(layer, (N_HEADS, 1, DK)),
                   _layer_state_spec(layer, (N_HEADS, 1, 1))],
        out_shape=[jax.ShapeDtypeStruct(hbuf.shape, BF16),
                   jax.ShapeDtypeStruct((depth, bsz, N_HEADS, DK, DV), F32),
                   jax.ShapeDtypeStruct((depth, bsz, N_HEADS, 1, DK), F32),
                   jax.ShapeDtypeStruct((depth, bsz, N_HEADS, 1, 1), F32)],
        scratch_shapes=[pltpu.VMEM((N_HEADS, DK, DV), F32), pltpu.VMEM((N_HEADS, 1, DK), F32),
                        pltpu.VMEM((N_HEADS, 1, 1), F32)],
        compiler_params=_cparams(("parallel", "arbitrary"), 32),
        name="mlstm_prompt",
    )


def _hgrn_lower_bound(logits, layer):
    e = jnp.exp(logits - jnp.max(logits, axis=0, keepdims=True))
    p = e / jnp.sum(e, axis=0, keepdims=True)
    lb = jnp.zeros_like(p[0:1, :])
    for r in range(1, layer + 1):
        lb = lb + p[r:r + 1, :]
    return lb


def _hgrn_gates(fpre, lb):
    hlf = _logaddexp(jnp.log(lb), jnp.log1p(-lb) + _log_sigmoid(fpre))
    hk = (1.0 - lb) * _sigmoid(-fpre)
    return hlf, hk


def _hgrn_kernel(lg_ref, q_ref, f_ref, i_ref, g_ref, nw_ref, h_ref, st_ref, s_ref, *, nc, layer):
    c = pl.program_id(1)

    @pl.when(c == 0)
    def _():
        s_ref[...] = jnp.zeros_like(s_ref)

    n = CHUNK
    lb_all = _hgrn_lower_bound(lg_ref[...], layer)
    r2 = lax.broadcasted_iota(jnp.int32, (n, n), 0)
    c2 = lax.broadcasted_iota(jnp.int32, (n, n), 1)
    rowv = lax.broadcasted_iota(jnp.int32, (n, DK), 0)
    n_lev = n.bit_length() - 1
    rights = [(rowv & (1 << lev)) != 0 for lev in range(n_lev)]
    pair_masks = [((r2 >> (lev + 1)) == (c2 >> (lev + 1))) & ((r2 & (1 << lev)) != 0) & ((c2 & (1 << lev)) == 0)
                  for lev in range(n_lev)]
    states = [s_ref[h] for h in range(N_HEADS)]
    new_states, outs = [], []
    for h in range(N_HEADS):
        kc, vc = slice(h * DK, (h + 1) * DK), slice(h * DV, (h + 1) * DV)
        hlf, hk = _hgrn_gates(f_ref[:, kc], lb_all[:, kc])
        q = _silu(q_ref[:, kc])
        ib = i_ref[:, vc].astype(BF16)
        bc = _cumsum_rows(hlf)
        a = jnp.where(r2 == c2, _dot_nt(q.astype(BF16), hk.astype(BF16)), 0.0)
        p = bc
        for lev in range(n_lev):
            s = 1 << lev
            right = rights[lev]
            nxt = pltpu.roll(p, n - s, 0)
            e = jnp.exp(jnp.where(right, bc - p, nxt - bc))
            m = (jnp.where(right, q, hk) * e).astype(BF16)
            a = jnp.where(pair_masks[lev], _dot_nt(m, m), a)
            if lev + 1 < n_lev:
                p = jnp.where(right, pltpu.roll(p, s, 0), p)
        st = states[h]
        o = _dot(a.astype(BF16), ib) + _dot((q * jnp.exp(bc)).astype(BF16), st.astype(BF16))
        bl = bc[n - 1:n, :]
        new_states.append(_row_to_col(jnp.exp(bl)) * st + _dot((hk * jnp.exp(bl - bc)).T.astype(BF16), ib))
        outs.append((_rms(o, nw_ref[:, vc]) * _silu(g_ref[:, vc])).astype(h_ref.dtype))
    for h in range(N_HEADS):
        s_ref[h] = new_states[h]
        h_ref[:, h * DV:(h + 1) * DV] = outs[h]

    @pl.when(c == nc - 1)
    def _():
        st_ref[...] = s_ref[...].reshape(st_ref.shape)


def _hgrn_prompt(u, hbuf, logits, norm_w, bsz, seqlen, layer, prev):
    nc = seqlen // CHUNK
    depth = logits.shape[0]
    return _pcall(
        functools.partial(_hgrn_kernel, nc=nc, layer=layer), (logits, u, u, u, u, norm_w),
        [_const_spec(logits.shape),
         _seg_spec(nc, _KW, OFF_HQ), _seg_spec(nc, _KW, OFF_HF), _seg_spec(nc, _HW, OFF_HI),
         _seg_spec(nc, _HW, OFF_HG), _const_spec((1, _HW))],
        carried={0: hbuf, 1: prev},
        grid=(bsz, nc),
        out_specs=[pl.BlockSpec((CHUNK, _HW), lambda b, c: (b * nc + c, BR_HG)),
                   _layer_state_spec(layer, (N_HEADS, DK, DV))],
        out_shape=[jax.ShapeDtypeStruct(hbuf.shape, BF16),
                   jax.ShapeDtypeStruct((depth, bsz, N_HEADS, DK, DV), F32)],
        scratch_shapes=[pltpu.VMEM((N_HEADS, DK, DV), F32)],
        compiler_params=_cparams(("parallel", "arbitrary"), 32),
        name="hgrn_prompt",
    )


def _s5_prep_kernel(lr_ref, li_ref, dt_ref, bre_ref, bim_ref, ar_ref, ai_ref, bbr_ref, bbi_ref):
    lr, li, dt = lr_ref[...], li_ref[...], dt_ref[...]
    mag = jnp.exp(lr * dt)
    ar = mag * jnp.cos(li * dt)
    ai = mag * jnp.sin(li * dt)
    den = lr * lr + li * li
    cr = ((ar - 1.0) * lr + ai * li) / den
    ci = (ai * lr - (ar - 1.0) * li) / den
    ar_ref[...] = ar
    ai_ref[...] = ai
    bbr_ref[...] = cr * bre_ref[...] - ci * bim_ref[...]
    bbi_ref[...] = cr * bim_ref[...] + ci * bre_ref[...]


def _s5_prep(lam_re, lam_im, log_step, b_re, b_im):
    g, n = lam_re.shape
    p = b_re.shape[-1]
    gn = g * n
    dt = jnp.repeat(jnp.exp(log_step.astype(F32)), n).reshape(gn, 1)
    col = lambda a: a.astype(F32).reshape(gn, 1)
    return pl.pallas_call(
        _s5_prep_kernel,
        out_shape=[jax.ShapeDtypeStruct((gn, 1), F32), jax.ShapeDtypeStruct((gn, 1), F32),
                   jax.ShapeDtypeStruct((gn, p), F32), jax.ShapeDtypeStruct((gn, p), F32)],
        name="s5_prep",
    )(col(lam_re), col(lam_im), dt, b_re.reshape(gn, p), b_im.reshape(gn, p))


def _s5_kernel(*refs, seq, nc, n_kt, sw):
    if seq:
        (u_ref, ar_ref, ai_ref, bblk_ref, cblk_ref, d_ref, wglu_ref,
         h_ref, sr_ref, si_ref, hr_scr, hi_scr) = refs
    else:
        (u_ref, ar_ref, ai_ref, bblk_ref, cblk_ref, d_ref, wglu_ref, h0r_ref, h0i_ref,
         h_ref, sr_ref, si_ref) = refs
    u = u_ref[...]
    ub = u.astype(BF16)
    rows = u.shape[0]
    if seq:
        c = pl.program_id(1)

        @pl.when(c == 0)
        def _():
            hr_scr[...] = jnp.zeros_like(hr_scr)
            hi_scr[...] = jnp.zeros_like(hi_scr)

    ys = []
    for kt in range(n_kt):
        lanes = slice(kt * sw, (kt + 1) * sw)
        bu = _dot(ub[:, kt * MXU_K:(kt + 1) * MXU_K], bblk_ref[kt])
        xr, xi = bu[:, :sw], bu[:, sw:]
        ar, ai = ar_ref[:, lanes], ai_ref[:, lanes]
        if seq:
            sub = lax.broadcasted_iota(jnp.int32, (SUBLANES, sw), 0)
            mr, mi = jnp.broadcast_to(ar, (SUBLANES, sw)), jnp.broadcast_to(ai, (SUBLANES, sw))
            tabr, tabi = mr, mi
            levels = []
            sft = 1
            while sft < SUBLANES:
                keep = sub >= sft
                levels.append((sft, jnp.where(keep, mr, 0.0), jnp.where(keep, mi, 0.0)))
                tr, ti = pltpu.roll(tabr, sft, 0), pltpu.roll(tabi, sft, 0)
                tabr, tabi = (jnp.where(keep, tabr * tr - tabi * ti, tabr),
                              jnp.where(keep, tabr * ti + tabi * tr, tabi))
                mr, mi = mr * mr - mi * mi, 2.0 * mr * mi
                sft *= 2
            cr, ci = hr_scr[:, lanes], hi_scr[:, lanes]
            slabs_r, slabs_i = [], []
            for j in range(rows // SUBLANES):
                rs = slice(j * SUBLANES, (j + 1) * SUBLANES)
                sr_, si_ = xr[rs, :], xi[rs, :]
                for sft, lr, li in levels:
                    qr, qi = pltpu.roll(sr_, sft, 0), pltpu.roll(si_, sft, 0)
                    sr_, si_ = sr_ + lr * qr - li * qi, si_ + lr * qi + li * qr
                crb, cib = jnp.broadcast_to(cr, (SUBLANES, sw)), jnp.broadcast_to(ci, (SUBLANES, sw))
                sr_, si_ = sr_ + tabr * crb - tabi * cib, si_ + tabr * cib + tabi * crb
                cr, ci = sr_[SUBLANES - 1:SUBLANES, :], si_[SUBLANES - 1:SUBLANES, :]
                slabs_r.append(sr_)
                slabs_i.append(si_)
            xr, xi = jnp.concatenate(slabs_r, axis=0), jnp.concatenate(slabs_i, axis=0)
            hr_scr[:, lanes] = cr
            hi_scr[:, lanes] = ci
        else:
            h0r, h0i = h0r_ref[0, :, lanes], h0i_ref[0, :, lanes]
            xr, xi = xr + ar * h0r - ai * h0i, xi + ar * h0i + ai * h0r
            sr_ref[0, :, lanes] = xr
            si_ref[0, :, lanes] = xi
        ys.append(_dot(jnp.concatenate([xr, xi], axis=1).astype(BF16), cblk_ref[kt]))
    y = jnp.concatenate(ys, axis=1) + d_ref[...] * u
    z = _gelu_tanh(y)
    h_ref[...] = (z * _sigmoid(_dot(z.astype(BF16), wglu_ref[...]))).astype(h_ref.dtype)
    if seq:
        @pl.when(c == nc - 1)
        def _():
            sr_ref[...] = hr_scr[...].reshape(sr_ref.shape)
            si_ref[...] = hi_scr[...].reshape(si_ref.shape)


def _s5_weights(bbr, bbi, c_re, c_im):
    gn, p = bbr.shape
    g = gn // S5_N
    gpt = MXU_K // p
    n_kt = g // gpt
    eye = jnp.eye(gpt, dtype=F32)

    def b_blk(bb):
        return jnp.einsum('kgnp,gh->kgphn', bb.reshape(n_kt, gpt, S5_N, p), eye).reshape(n_kt, gpt * p, gpt * S5_N)

    def c_blk(cc):
        return jnp.einsum('kgpn,gh->kgnhp', cc.astype(F32).reshape(n_kt, gpt, p, S5_N), eye).reshape(
            n_kt, gpt * S5_N, gpt * p)

    bblk = jnp.concatenate([b_blk(bbr), b_blk(bbi)], axis=2).astype(BF16)
    cblk = jnp.concatenate([c_blk(c_re), -c_blk(c_im)], axis=1).astype(BF16)
    return bblk, cblk


def _s5_prompt(u, hbuf, ar, ai, bblk, cblk, d, wglu, bsz, seqlen, layer, depth, prev):
    width = wglu.shape[0]
    n_kt = bblk.shape[0]
    sw = bblk.shape[2] // 2
    ct = _tile(seqlen, 256)
    nc = seqlen // ct
    gn = ar.shape[1]
    full = lambda shape: pl.BlockSpec(shape, lambda b, c: (0,) * len(shape))
    st_spec = pl.BlockSpec((1, 1, 1, gn), lambda b, c: (layer, b, 0, 0))
    st_shape = jax.ShapeDtypeStruct((depth, bsz, 1, gn), F32)
    return _pcall(
        functools.partial(_s5_kernel, seq=True, nc=nc, n_kt=n_kt, sw=sw), (u, ar, ai, bblk, cblk, d, wglu),
        [pl.BlockSpec((ct, width), lambda b, c: (b * nc + c, OFF_SU // width)),
         full((1, gn)), full((1, gn)), full(bblk.shape), full(cblk.shape),
         full((1, width)), full(wglu.shape)],
        carried={0: hbuf, 1: prev[0], 2: prev[1]},
        grid=(bsz, nc),
        out_specs=[pl.BlockSpec((ct, width), lambda b, c: (b * nc + c, BR_S5)), st_spec, st_spec],
        out_shape=[jax.ShapeDtypeStruct(hbuf.shape, BF16), st_shape, st_shape],
        scratch_shapes=[pltpu.VMEM((1, gn), F32), pltpu.VMEM((1, gn), F32)],
        compiler_params=_cparams(("parallel", "arbitrary"), 56),
        name="s5_prompt",
    )


def _s5_sample(u, tp, ar, ai, bblk, cblk, d, wglu, h0r, h0i, layer, h_buf, prev):
    width = wglu.shape[0]
    n_kt = bblk.shape[0]
    sw = bblk.shape[2] // 2
    depth, bs, gn = h0r.shape
    tb = _tile(bs, 128, 8)
    r0 = tp // tb
    full = lambda shape: pl.BlockSpec(shape, lambda i: (0,) * len(shape))
    st_spec = pl.BlockSpec((1, tb, gn), lambda i: (layer, i, 0))
    st_shape = jax.ShapeDtypeStruct((depth, bs, gn), F32)
    return _pcall(
        functools.partial(_s5_kernel, seq=False, nc=1, n_kt=n_kt, sw=sw),
        (u, ar, ai, bblk, cblk, d, wglu, h0r, h0i),
        [pl.BlockSpec((tb, width), lambda i: (r0 + i, OFF_SU // width)),
         full((1, gn)), full((1, gn)), full(bblk.shape), full(cblk.shape),
         full((1, width)), full(wglu.shape), st_spec, st_spec],
        carried={0: h_buf, 1: prev[0], 2: prev[1]},
        grid=(bs // tb,),
        out_specs=[pl.BlockSpec((tb, width), lambda i: (r0 + i, BR_S5)), st_spec, st_spec],
        out_shape=[jax.ShapeDtypeStruct(h_buf.shape, BF16), st_shape, st_shape],
        compiler_params=_cparams(("parallel",), 56),
        name="s5_sample",
    )


SAMPLE_BLOCK = 8


def _state_step(s, dcol, kcol, vrow, qcol):
    s_new = dcol * s + kcol * vrow
    return s_new, jnp.sum(qcol * s_new, axis=0, keepdims=True)


def _zero_other_layers(ref, slot):
    for l in range(ref.shape[0]):
        if l != slot:
            ref[l] = jnp.zeros(ref.shape[1:], ref.dtype)


def _ret_step_kernel(gam_ref, q_ref, k_ref, v_ref, g_ref, cos_ref, sin_ref, st_ref, h_ref, so_ref, o_scr, *, slot):
    _zero_other_layers(so_ref, slot)
    cs, sn = cos_ref[...], sin_ref[...]
    for h in range(N_HEADS):
        qr = _rope(q_ref[:, h * DK:(h + 1) * DK], cs, sn)
        kr = _rope(k_ref[:, h * DK:(h + 1) * DK], cs, sn) * (DK ** -0.5)
        for i in range(SAMPLE_BLOCK):
            s_new, o = _state_step(st_ref[0, i, h], gam_ref[h], _row_to_col(kr[i:i + 1, :]),
                                   v_ref[i:i + 1, h * DV:(h + 1) * DV], _row_to_col(qr[i:i + 1, :]))
            so_ref[slot, i, h] = s_new
            o_scr[i:i + 1, h * DV:(h + 1) * DV] = o
    for h in range(N_HEADS):
        cols = slice(h * DV, (h + 1) * DV)
        h_ref[:, cols] = (_group_norm(o_scr[:, cols]) * _silu(g_ref[:, cols])).astype(h_ref.dtype)


def _mlstm_step_kernel(gb_ref, q_ref, k_ref, v_ref, mo_ref, ug_ref, nw_ref, c_ref, n_ref, m_ref,
                       h_ref, co_ref, no_ref, mo_out_ref, o_scr, *, slot):
    _zero_other_layers(co_ref, slot)
    ug = ug_ref[...]
    m_all = m_ref[0]
    m_new_cols = []
    for h in range(N_HEADS):
        ig = ug[:, h:h + 1] + gb_ref[0, h]
        lf = _log_sigmoid(ug[:, N_HEADS + h:N_HEADS + h + 1] + gb_ref[1, h])
        m_old = m_all[:, h:h + 1]
        inter = m_old + lf
        mi = jnp.maximum(inter, ig)
        w = jnp.exp(ig - mi)
        wi = jnp.exp(inter - mi)
        m_new_cols.append(mi)
        lim = jnp.exp(-mi)
        qh = q_ref[:, h * DK:(h + 1) * DK]
        kh = k_ref[:, h * DK:(h + 1) * DK] * (DK ** -0.5)
        for i in range(SAMPLE_BLOCK):
            wk = w[i:i + 1, :] * kh[i:i + 1, :]
            c_new, num = _state_step(c_ref[0, i, h], wi[i:i + 1, :], _row_to_col(wk),
                                     v_ref[i:i + 1, h * DV:(h + 1) * DV], _row_to_col(qh[i:i + 1, :]))
            n_new = wi[i:i + 1, :] * n_ref[0, i, h:h + 1, :] + wk
            nq = jnp.sum(qh[i:i + 1, :] * n_new, axis=1, keepdims=True)
            co_ref[slot, i, h] = c_new
            no_ref[0, i, h:h + 1, :] = n_new
            o_scr[i:i + 1, h * DV:(h + 1) * DV] = num / jnp.maximum(jnp.abs(nq), lim[i:i + 1, :])
    mo_out_ref[0] = jnp.concatenate(m_new_cols, axis=1)
    for h in range(N_HEADS):
        cols = slice(h * DV, (h + 1) * DV)
        h_ref[:, cols] = (_rms(o_scr[:, cols], nw_ref[:, cols]) * _sigmoid(mo_ref[:, cols])).astype(h_ref.dtype)


def _hgrn_step_kernel(lg_ref, q_ref, f_ref, i_ref, g_ref, nw_ref, st_ref, h_ref, so_ref, o_scr, *, layer, slot):
    _zero_other_layers(so_ref, slot)
    lb_all = _hgrn_lower_bound(lg_ref[...], layer)
    for h in range(N_HEADS):
        hlf, hk = _hgrn_gates(f_ref[:, h * DK:(h + 1) * DK], lb_all[:, h * DK:(h + 1) * DK])
        dec = jnp.exp(hlf)
        qh = _silu(q_ref[:, h * DK:(h + 1) * DK])
        for i in range(SAMPLE_BLOCK):
            s_new, o = _state_step(st_ref[0, i, h], _row_to_col(dec[i:i + 1, :]), _row_to_col(hk[i:i + 1, :]),
                                   i_ref[i:i + 1, h * DV:(h + 1) * DV], _row_to_col(qh[i:i + 1, :]))
            so_ref[slot, i, h] = s_new
            o_scr[i:i + 1, h * DV:(h + 1) * DV] = o
    for h in range(N_HEADS):
        cols = slice(h * DV, (h + 1) * DV)
        h_ref[:, cols] = (_rms(o_scr[:, cols], nw_ref[:, cols]) * _silu(g_ref[:, cols])).astype(h_ref.dtype)


def _seg(width, off, tp):
    return pl.BlockSpec((SAMPLE_BLOCK, width), lambda i: (tp // SAMPLE_BLOCK + i, off // width))


def _state_spec(layer, tail):
    return pl.BlockSpec((1, SAMPLE_BLOCK) + tail, lambda i: (layer, i) + (0,) * len(tail))


_MAT = (N_HEADS, DK, DV)
_SMEM = pl.BlockSpec(memory_space=pltpu.SMEM)


def _whole(shape):
    return pl.BlockSpec(shape, lambda i: (0,) * len(shape))


def _h_rows_spec(tp, branch):
    return pl.BlockSpec((SAMPLE_BLOCK, _HW), lambda i: (tp // SAMPLE_BLOCK + i, branch))


def _new_mat_state(layer, depth, prev):
    if prev is None:
        return pl.BlockSpec((depth, SAMPLE_BLOCK) + _MAT, lambda i: (0, i, 0, 0, 0)), layer
    return _state_spec(layer, _MAT), 0


def _ret_sample(u, tp, gamma, cos2, sin2, state, layer, h_buf, prev):
    depth, bs = state.shape[:2]
    spec, slot = _new_mat_state(layer, depth, prev)
    return _pcall(
        functools.partial(_ret_step_kernel, slot=slot), (gamma, u, u, u, u, cos2, sin2, state),
        [_SMEM, _seg(_KW, OFF_RQ, tp), _seg(_KW, OFF_RK, tp), _seg(_HW, OFF_RV, tp), _seg(_HW, OFF_RG, tp),
         _whole((1, DK)), _whole((1, DK)), _state_spec(layer, _MAT)],
        carried={0: h_buf, 1: prev},
        grid=(bs // SAMPLE_BLOCK,),
        out_specs=[_h_rows_spec(tp, BR_RET), spec],
        out_shape=[jax.ShapeDtypeStruct(h_buf.shape, BF16), jax.ShapeDtypeStruct(state.shape, F32)],
        scratch_shapes=[pltpu.VMEM((SAMPLE_BLOCK, _HW), F32)],
        compiler_params=_cparams(("parallel",), 48),
        name="ret_sample",
    )


def _mlstm_sample(u, ug, tp, gate_bias, norm_w, st_c, st_n, st_m, layer, h_buf, prev):
    depth, bs = st_c.shape[:2]
    spec, slot = _new_mat_state(layer, depth, prev[0])
    return _pcall(
        functools.partial(_mlstm_step_kernel, slot=slot), (gate_bias, u, u, u, u, ug, norm_w, st_c, st_n, st_m),
        [_SMEM, _seg(_KW, OFF_MQ, tp), _seg(_KW, OFF_MK, tp), _seg(_HW, OFF_MV, tp), _seg(_HW, OFF_MO, tp),
         pl.BlockSpec((SAMPLE_BLOCK, GATE_PAD), lambda i: (tp // SAMPLE_BLOCK + i, 0)), _whole((1, _HW)),
         _state_spec(layer, _MAT), _state_spec(layer, (N_HEADS, DK)), _state_spec(layer, (N_HEADS,))],
        carried={0: h_buf, 1: prev[0], 2: prev[1], 3: prev[2]},
        grid=(bs // SAMPLE_BLOCK,),
        out_specs=[_h_rows_spec(tp, BR_ML), spec,
                   _state_spec(layer, (N_HEADS, DK)), _state_spec(layer, (N_HEADS,))],
        out_shape=[jax.ShapeDtypeStruct(h_buf.shape, BF16), jax.ShapeDtypeStruct(st_c.shape, F32),
                   jax.ShapeDtypeStruct(st_n.shape, F32), jax.ShapeDtypeStruct(st_m.shape, F32)],
        scratch_shapes=[pltpu.VMEM((SAMPLE_BLOCK, _HW), F32)],
        compiler_params=_cparams(("parallel",), 48),
        name="mlstm_sample",
    )


def _hgrn_sample(u, tp, logits, norm_w, state, layer, h_buf, prev):
    depth, bs = state.shape[:2]
    spec, slot = _new_mat_state(layer, depth, prev)
    return _pcall(
        functools.partial(_hgrn_step_kernel, layer=layer, slot=slot), (logits, u, u, u, u, norm_w, state),
        [_whole(logits.shape), _seg(_KW, OFF_HQ, tp), _seg(_KW, OFF_HF, tp), _seg(_HW, OFF_HI, tp),
         _seg(_HW, OFF_HG, tp), _whole((1, _HW)), _state_spec(layer, _MAT)],
        carried={0: h_buf, 1: prev},
        grid=(bs // SAMPLE_BLOCK,),
        out_specs=[_h_rows_spec(tp, BR_HG), spec],
        out_shape=[jax.ShapeDtypeStruct(h_buf.shape, BF16), jax.ShapeDtypeStruct(state.shape, F32)],
        scratch_shapes=[pltpu.VMEM((SAMPLE_BLOCK, _HW), F32)],
        compiler_params=_cparams(("parallel",), 48),
        name="hgrn_sample",
    )


def _attn_prompt_kernel(q_ref, k_ref, v_ref, o_ref, *, scale):
    s = _dot_nt(q_ref[...].astype(BF16), k_ref[0, 0].astype(BF16)) * scale
    e = jnp.exp(s - jnp.max(s, axis=1, keepdims=True))
    p = e / jnp.sum(e, axis=1, keepdims=True)
    o_ref[...] = _dot(p.astype(BF16), v_ref[0, 0].astype(BF16)).astype(o_ref.dtype)


def _attn_prompt(q, kvbuf, obuf, bsz, seqlen, layer):
    t, d = q.shape
    dh = d // N_HEADS
    mem_len = kvbuf.shape[2] // bsz
    tq = _tile(seqlen, 2048)
    nq = seqlen // tq
    return _pcall(
        functools.partial(_attn_prompt_kernel, scale=dh ** -0.5), (q, kvbuf, kvbuf),
        [pl.BlockSpec((tq, dh), lambda b, h, i: (b * nq + i, h)),
         pl.BlockSpec((1, 1, mem_len, dh), lambda b, h, i: (layer, 0, b, h)),
         pl.BlockSpec((1, 1, mem_len, dh), lambda b, h, i: (layer, 1, b, h))],
        carried={0: obuf},
        grid=(bsz, N_HEADS, nq),
        out_specs=pl.BlockSpec((tq, dh), lambda b, h, i: (b * nq + i, h)),
        out_shape=jax.ShapeDtypeStruct((t, d), BF16),
        compiler_params=_cparams(("parallel", "parallel", "arbitrary"), 32),
        name="attn_prompt",
    )


def _attn_sample_kernel(q_ref, k_ref, v_ref, o_ref, *, scale):
    for h in range(q_ref.shape[1]):
        k = k_ref[0, 0, :, h, :]
        s = jnp.sum(k * q_ref[0, h:h + 1, :], axis=1, keepdims=True) * scale
        e = jnp.exp(s - jnp.max(s, axis=0, keepdims=True))
        p = e / jnp.sum(e, axis=0, keepdims=True)
        o_ref[0, h:h + 1, :] = jnp.sum(p * v_ref[0, 0, :, h, :], axis=0, keepdims=True)


def _attn_sample(q4, cache_k, cache_v, layer):
    bs, nh, dh = q4.shape
    mem_len = cache_k.shape[2]
    kv_spec = pl.BlockSpec((1, 1, mem_len, nh, dh), lambda i: (layer, i, 0, 0, 0))
    return pl.pallas_call(
        functools.partial(_attn_sample_kernel, scale=dh ** -0.5),
        grid=(bs,),
        in_specs=[pl.BlockSpec((1, nh, dh), lambda i: (i, 0, 0)), kv_spec, kv_spec],
        out_specs=pl.BlockSpec((1, nh, dh), lambda i: (i, 0, 0)),
        out_shape=jax.ShapeDtypeStruct((bs, nh, dh), F32),
        compiler_params=_cparams(("parallel",), 48),
        name="attn_sample",
    )(q4, cache_k, cache_v)


def _rope_tables(pos):
    half = DK // 2
    inv = ROPE_BASE ** (-jnp.arange(half, dtype=F32) / half)
    ang = pos.astype(F32)[:, None] * inv[None, :]
    cos, sin = jnp.cos(ang), jnp.sin(ang)
    return jnp.concatenate([cos, cos], axis=1), jnp.concatenate([-sin, sin], axis=1)


def _retention_tables(c):
    lg = jnp.log1p(-jnp.exp2(-5.0 - jnp.arange(N_HEADS, dtype=F32)))
    idx = jnp.arange(c, dtype=F32)
    diff = idx[:, None] - idx[None, :]
    dmat = jnp.where(diff[None] >= 0, jnp.exp(jnp.maximum(diff, 0.0)[None] * lg[:, None, None]), 0.0)
    e1 = jnp.exp((idx[None, :] + 1.0) * lg[:, None])
    e2 = jnp.exp((c - 1.0 - idx)[None, :] * lg[:, None])
    e3 = jnp.broadcast_to(jnp.exp(c * lg)[:, None], (N_HEADS, c))
    dec = jnp.stack([e1, e2, e3] + [jnp.zeros_like(e1)] * 5, axis=-1)
    return dmat, dec


def kernel(x_prompt, x_sample, mem_prompt, state_ret, state_mlstm_c, state_mlstm_n, state_mlstm_m, state_s5_re, state_s5_im, state_hgrn, cache_mem_k, cache_mem_v, norm_w, ffn_w_in, ffn_w_out, w_in, ml_gate_bias, ml_norm_w, s5_lambda_re, s5_lambda_im, s5_log_step, s5_b_re, s5_b_im, s5_c_re, s5_c_im, s5_d, s5_w_glu, hg_lb_logits, hg_norm_w, w_branch, w_out, xa_mem_norm, xa_wq, xa_wkv, xa_wo):
    bsz, seqlen, d = x_prompt.shape
    bs = x_sample.shape[0]
    depth = norm_w.shape[0]
    mem_len = mem_prompt.shape[1]
    tp = bsz * seqlen
    gn = s5_lambda_re.shape[1] * s5_lambda_re.shape[2]
    seg_gate = OFF_MO

    x = jnp.concatenate([x_prompt.reshape(tp, d), x_sample.reshape(bs, d)], axis=0)
    mem = mem_prompt.reshape(bsz * mem_len, d)
    s5_in_re = state_s5_re.reshape(depth, bs, gn)
    s5_in_im = state_s5_im.reshape(depth, bs, gn)

    cos_p, sin_p = _rope_tables(jnp.arange(seqlen, dtype=jnp.int32))
    cos_s, sin_s = _rope_tables(PAST_LEN + jnp.arange(1, dtype=jnp.int32))
    dmat, dec = _retention_tables(CHUNK)
    _, dec1 = _retention_tables(1)
    gamma = dec1[:, 0, 0]

    zf = lambda *shape: jnp.zeros(shape, F32)
    t = tp + bs
    p_ret = zf(depth, bsz, N_HEADS, DK, DV)
    p_hg = zf(depth, bsz, N_HEADS, DK, DV)
    p_ml = [zf(depth, bsz, N_HEADS, DK, DV), zf(depth, bsz, N_HEADS, 1, DK), zf(depth, bsz, N_HEADS, 1, 1)]
    p_s5 = [zf(depth, bsz, 1, gn), zf(depth, bsz, 1, gn)]
    kvbuf = zf(depth, 2, bsz * mem_len, d)
    s_ret = s_hg = None
    s_ml = [None, zf(depth, bs, N_HEADS, DK), zf(depth, bs, N_HEADS)]
    s_s5 = [zf(depth, bs, gn), zf(depth, bs, gn)]
    hbuf = jnp.zeros((t, N_BRANCH * _HW), BF16)
    o = jnp.zeros((t, d), BF16)
    ffn_w_out_bf = ffn_w_out.astype(BF16)
    w_in_t = jnp.swapaxes(w_in, 1, 2)
    w_branch_bf, w_out_bf = w_branch.astype(BF16), w_out.astype(BF16)
    xa_wq_bf, xa_wkv_bf, xa_wo_bf = xa_wq.astype(BF16), xa_wkv.astype(BF16), xa_wo.astype(BF16)
    for l in range(depth):
        nw = norm_w[l]
        x = _ffn(x, nw[0:2], ffn_w_in, l, 0, ffn_w_out_bf)
        n_gate = 2 * N_HEADS
        u, ug = _in_proj(x, nw[2:3], w_in_t, l, seg_gate, n_gate)
        ugt = ug[:tp, :n_gate].T
        mlw = ml_norm_w[l].reshape(1, -1)
        hgw = hg_norm_w[l].reshape(1, -1)
        ar, ai, bbr, bbi = _s5_prep(s5_lambda_re[l], s5_lambda_im[l], s5_log_step[l], s5_b_re[l], s5_b_im[l])
        ar, ai = ar.reshape(1, gn), ai.reshape(1, gn)
        bblk, cblk = _s5_weights(bbr, bbi, s5_c_re[l], s5_c_im[l])
        s5d = s5_d[l].astype(F32).reshape(1, -1)
        wglu = s5_w_glu[l].astype(BF16)
        hbuf, p_ret = _ret_prompt(u, hbuf, bsz, seqlen, cos_p, sin_p, dmat, dec, l, depth, p_ret)
        hbuf, s_ret = _ret_sample(u, tp, gamma, cos_s, sin_s, state_ret, l, hbuf, s_ret)
        hbuf, *p_ml = _mlstm_prompt(u, hbuf, ug, ugt, ml_gate_bias[l], mlw, bsz, seqlen, l, depth, p_ml)
        hbuf, *s_ml = _mlstm_sample(u, ug, tp, ml_gate_bias[l], mlw, state_mlstm_c, state_mlstm_n,
                                    state_mlstm_m, l, hbuf, s_ml)
        hbuf, *p_s5 = _s5_prompt(u, hbuf, ar, ai, bblk, cblk, s5d, wglu, bsz, seqlen, l, depth, p_s5)
        hbuf, *s_s5 = _s5_sample(u, tp, ar, ai, bblk, cblk, s5d, wglu, s5_in_re, s5_in_im, l, hbuf, s_s5)
        hbuf, p_hg = _hgrn_prompt(u, hbuf, hg_lb_logits, hgw, bsz, seqlen, l, p_hg)
        hbuf, s_hg = _hgrn_sample(u, tp, hg_lb_logits, hgw, state_hgrn, l, hbuf, s_hg)
        x = _merge(hbuf, u, w_branch_bf, w_out_bf, l, x, nw[3:4])
        kvbuf = _mem_kv(mem, xa_mem_norm[l].reshape(1, d), xa_wkv_bf, l, depth, kvbuf)
        q = _norm_matmul(x, nw[4:5], xa_wq_bf, l, "xa_q")
        o = _attn_prompt(q, kvbuf, o, bsz, seqlen, l)
        o_s = _attn_sample(q[tp:].reshape(bs, N_HEADS, d // N_HEADS), cache_mem_k, cache_mem_v, l)
        o = lax.dynamic_update_slice(o, o_s.reshape(bs, d).astype(BF16), (tp, 0))
        x = _proj_res(o, xa_wo_bf, l, x, nw[5:6])
        x = _ffn(x, nw[6:8], ffn_w_in, l, 1, ffn_w_out_bf)

    mem_shape = (depth, bsz, mem_len, N_HEADS, d // N_HEADS)
    s5_shape = (depth, -1, gn // S5_N, S5_N)
    return (x[:tp].reshape(bsz, seqlen, d), x[tp:].reshape(bs, 1, d),
            p_ret, p_ml[0], p_ml[1].reshape(depth, bsz, N_HEADS, DK), p_ml[2].reshape(depth, bsz, N_HEADS),
            p_s5[0].reshape(s5_shape), p_s5[1].reshape(s5_shape), p_hg,
            kvbuf[:, 0].reshape(mem_shape), kvbuf[:, 1].reshape(mem_shape),
            s_ret, s_ml[0], s_ml[1], s_ml[2], s_s5[0].reshape(s5_shape), s_s5[1].reshape(s5_shape), s_hg)
```

```python
import functools
import math

import jax
import jax.numpy as jnp
from jax import lax
from jax.experimental import pallas as pl
from jax.experimental.pallas import tpu as pltpu

F32 = jnp.float32
BF16 = jnp.bfloat16

CHUNK = 128
N_BRANCH = 4
N_HEADS = 4
DK = 128
DV = 256
S5_GROUP = 16
S5_N = 64
ROPE_BASE = 10000.0
PAST_LEN = 16384

V7X_VMEM_BYTES = 64 * 1024 * 1024
MXU_K = 256
SUBLANES = 8
MIB = 1024 * 1024

OFF_RQ, OFF_RK, OFF_RV, OFF_RG = 0, 512, 1024, 2048
OFF_MQ, OFF_MK, OFF_MV, OFF_MO = 3072, 3584, 4096, 5120
OFF_SU = 6144
OFF_HQ, OFF_HF, OFF_HI, OFF_HG = 7168, 7680, 8192, 9216
OFF_GATES = 10240
N_MAIN = 18432
GATE_PAD = 128


def _cparams(sem, vmem_mib):
    return pltpu.CompilerParams(dimension_semantics=sem, vmem_limit_bytes=vmem_mib * MIB)


_ANY = pl.BlockSpec(memory_space=pl.ANY)


def _pcall(kernel_fn, inputs, in_specs, carried=None, **kw):
    carried = {o: a for o, a in (carried or {}).items() if a is not None}
    idxs = sorted(carried)
    n_in = len(inputs)

    def body(*refs):
        return kernel_fn(*refs[:n_in], *refs[n_in + len(idxs):])

    return pl.pallas_call(
        body if idxs else kernel_fn,
        in_specs=list(in_specs) + [_ANY] * len(idxs),
        input_output_aliases={n_in + j: o for j, o in enumerate(idxs)},
        **kw)(*inputs, *[carried[o] for o in idxs])


def _tile(n, cap, mult=16):
    best = None
    for t in range(mult, min(n, cap) + 1, mult):
        if n % t == 0:
            best = t
    return best if best is not None else n


def _dot(a, b):
    return jnp.dot(a, b, preferred_element_type=F32)


def _dot_nt(a, b):
    return lax.dot_general(a, b, (((1,), (1,)), ((), ())), preferred_element_type=F32)


def _rms(x, w, eps=1e-6):
    return x * lax.rsqrt(jnp.mean(x * x, axis=-1, keepdims=True) + eps) * w


def _sigmoid(x):
    return 1.0 / (1.0 + jnp.exp(-x))


def _silu(x):
    return x * _sigmoid(x)


def _log_sigmoid(x):
    return jnp.minimum(x, 0.0) - jnp.log1p(jnp.exp(-jnp.abs(x)))


def _logaddexp(a, b):
    return jnp.maximum(a, b) + jnp.log1p(jnp.exp(-jnp.abs(a - b)))


def _gelu_tanh(x):
    return 0.5 * x * (1.0 + jnp.tanh(math.sqrt(2.0 / math.pi) * (x + 0.044715 * (x * x * x))))


def _row_to_col(row):
    n = row.shape[1]
    r = lax.broadcasted_iota(jnp.int32, (n, n), 0)
    c = lax.broadcasted_iota(jnp.int32, (n, n), 1)
    return jnp.sum(jnp.where(r == c, row, 0.0), axis=1, keepdims=True)


def _cumsum_rows(x):
    c = x.shape[0]
    r = lax.broadcasted_iota(jnp.int32, (c, c), 0)
    k = lax.broadcasted_iota(jnp.int32, (c, c), 1)
    tri = jnp.where(r >= k, 1.0, 0.0).astype(BF16)
    hi = x.astype(BF16)
    r1 = x - hi.astype(F32)
    mid = r1.astype(BF16)
    lo = (r1 - mid.astype(F32)).astype(BF16)
    return _dot(tri, hi) + _dot(tri, mid) + _dot(tri, lo)


def _ffn_up_kernel(x_ref, nw_ref, wg_ref, wu_ref, h_ref, xn_ref, *, rc):
    @pl.when(pl.program_id(1) == 0)
    def _():
        xn_ref[...] = _rms(x_ref[...], nw_ref[...]).astype(BF16)

    wg = wg_ref[0, 0].astype(BF16)
    wu = wu_ref[0, 0].astype(BF16)
    for r in range(0, x_ref.shape[0], rc):
        xn = xn_ref[r:r + rc, :]
        h_ref[r:r + rc, :] = (_silu(_dot(xn, wg)) * _dot(xn, wu)).astype(h_ref.dtype)


def _ffn_down_kernel(h_ref, w_ref, x_ref, nw_ref, o_ref, *, rc):
    for r in range(0, x_ref.shape[0], rc):
        rows = slice(r, r + rc)
        o_ref[rows, :] = x_ref[rows, :] + 0.5 * _rms(_dot(h_ref[rows, :], w_ref[0, 0]), nw_ref[...])


def _ffn(x, nw2, w_in, layer, which, w_out):
    t, d = x.shape
    dff = w_out.shape[2]
    tm = _tile(t, 2080)
    rc = _tile(tm, 260)
    tf = _tile(dff, 512, 128)
    nf = dff // tf
    once = pl.Buffered(1)
    hidden = pl.pallas_call(
        functools.partial(_ffn_up_kernel, rc=rc),
        grid=(t // tm, nf),
        in_specs=[pl.BlockSpec((tm, d), lambda i, f: (i, 0), pipeline_mode=once),
                  pl.BlockSpec((1, d), lambda i, f: (0, 0)),
                  pl.BlockSpec((1, 1, d, tf), lambda i, f: (layer, which, 0, f)),
                  pl.BlockSpec((1, 1, d, tf), lambda i, f: (layer, which, 0, nf + f))],
        out_specs=pl.BlockSpec((tm, tf), lambda i, f: (i, f)),
        out_shape=jax.ShapeDtypeStruct((t, dff), BF16),
        scratch_shapes=[pltpu.VMEM((tm, d), BF16)],
        compiler_params=_cparams(("parallel", "arbitrary"), 56),
        name="ffn_up",
    )(x, nw2[0:1], w_in, w_in)
    tm2 = _tile(t, 416)
    return pl.pallas_call(
        functools.partial(_ffn_down_kernel, rc=_tile(tm2, 208)),
        grid=(t // tm2,),
        in_specs=[pl.BlockSpec((tm2, dff), lambda i: (i, 0)),
                  pl.BlockSpec((1, 1, dff, d), lambda i: (layer, which, 0, 0), pipeline_mode=once),
                  pl.BlockSpec((tm2, d), lambda i: (i, 0)),
                  pl.BlockSpec((1, d), lambda i: (0, 0))],
        out_specs=pl.BlockSpec((tm2, d), lambda i: (i, 0)),
        out_shape=jax.ShapeDtypeStruct((t, d), F32),
        compiler_params=_cparams(("parallel",), 56),
        name="ffn_down",
    )(hidden, w_out, x, nw2[1:2])


def _norm_matmul_kernel(x_ref, nw_ref, w_ref, o_ref, xn_ref):
    @pl.when(pl.program_id(1) == 0)
    def _():
        xn_ref[...] = _rms(x_ref[...], nw_ref[...]).astype(BF16)

    o_ref[...] = _dot(xn_ref[...], w_ref[0]).astype(o_ref.dtype)


def _norm_matmul(x, nw, w, layer, name):
    t, d = x.shape
    n = w.shape[2]
    tm = _tile(t, 1040)
    tn = _tile(n, 1024, 128)
    return pl.pallas_call(
        _norm_matmul_kernel,
        grid=(t // tm, n // tn),
        in_specs=[pl.BlockSpec((tm, d), lambda i, j: (i, 0)),
                  pl.BlockSpec((1, d), lambda i, j: (0, 0)),
                  pl.BlockSpec((1, d, tn), lambda i, j: (layer, 0, j))],
        out_specs=pl.BlockSpec((tm, tn), lambda i, j: (i, j)),
        out_shape=jax.ShapeDtypeStruct((t, n), F32),
        scratch_shapes=[pltpu.VMEM((tm, d), BF16)],
        compiler_params=_cparams(("parallel", "arbitrary"), 48),
        name=name,
    )(x, nw, w)


def _mem_kv_kernel(x_ref, nw_ref, w_ref, o_ref, xn_ref):
    @pl.when(pl.program_id(1) == 0)
    def _():
        xn_ref[...] = _rms(x_ref[...], nw_ref[...]).astype(BF16)

    o_ref[0, 0] = _dot(xn_ref[...], w_ref[0])


def _mem_kv(mem, nw, w, layer, depth, prev):
    t, d = mem.shape
    tm = _tile(t, 512)
    tn = _tile(d, 1024, 128)
    npk = d // tn
    return _pcall(
        _mem_kv_kernel, (mem, nw, w),
        [pl.BlockSpec((tm, d), lambda i, j: (i, 0)),
         pl.BlockSpec((1, d), lambda i, j: (0, 0)),
         pl.BlockSpec((1, d, tn), lambda i, j: (layer, 0, j))],
        carried={0: prev},
        grid=(t // tm, 2 * npk),
        out_specs=pl.BlockSpec((1, 1, tm, tn), lambda i, j: (layer, j // npk, i, j % npk)),
        out_shape=jax.ShapeDtypeStruct((depth, 2, t, d), F32),
        scratch_shapes=[pltpu.VMEM((tm, d), BF16)],
        compiler_params=_cparams(("parallel", "arbitrary"), 48),
        name="mem_kv",
    )


def _in_proj_kernel(x_ref, nw_ref, wt_ref, wgt_ref, o_ref, og_ref, xn_ref):
    @pl.when(pl.program_id(1) == 0)
    def _():
        xn = _rms(x_ref[...], nw_ref[...]).astype(BF16)
        xn_ref[...] = xn
        og_ref[...] = _dot_nt(xn, wgt_ref[0].astype(BF16))

    o_ref[...] = _dot_nt(xn_ref[...], wt_ref[0].astype(BF16))


def _in_proj(x, nw, w_in_t, layer, gate_off, n_gate):
    t, d = x.shape
    n = w_in_t.shape[1] - n_gate
    tm = _tile(t, 2080)
    tn = _tile(math.gcd(n, gate_off), 512, 128)
    n_plain = gate_off // tn
    return pl.pallas_call(
        _in_proj_kernel,
        grid=(t // tm, n // tn),
        in_specs=[pl.BlockSpec((tm, d), lambda i, j: (i, 0), pipeline_mode=pl.Buffered(1)),
                  pl.BlockSpec((1, d), lambda i, j: (0, 0)),
                  pl.BlockSpec((pl.Element(1), pl.Element(tn), pl.Element(d)),
                               lambda i, j: (layer, SUBLANES * (j * (tn // SUBLANES) + jnp.where(
                                   j < n_plain, 0, n_gate // SUBLANES)), 0)),
                  pl.BlockSpec((1, GATE_PAD, d), lambda i, j: (layer, gate_off // GATE_PAD, 0))],
        out_specs=[pl.BlockSpec((tm, tn), lambda i, j: (i, j)),
                   pl.BlockSpec((tm, GATE_PAD), lambda i, j: (i, 0))],
        out_shape=[jax.ShapeDtypeStruct((t, n), F32), jax.ShapeDtypeStruct((t, GATE_PAD), F32)],
        scratch_shapes=[pltpu.VMEM((tm, d), BF16)],
        compiler_params=_cparams(("parallel", "arbitrary"), 56),
        name="in_proj",
    )(x, nw, w_in_t, w_in_t)


def _merge_kernel(*refs):
    h_refs, g_refs = refs[:N_BRANCH], refs[N_BRANCH:2 * N_BRANCH]
    wb_ref, wo_ref, x_ref, nw_ref, o_ref = refs[2 * N_BRANCH:]
    merged = None
    for k in range(N_BRANCH):
        term = _sigmoid(g_refs[k][...]) * _dot(h_refs[k][...], wb_ref[0, k])
        merged = term if merged is None else merged + term
    o_ref[...] = x_ref[...] + _rms(_dot(merged.astype(BF16), wo_ref[0]), nw_ref[...])


def _merge(hbuf, u, w_branch, w_out, layer, x, nw):
    t, d = x.shape
    bw = hbuf.shape[1] // N_BRANCH
    tm = _tile(t, 208)
    gate_blk = OFF_GATES // d
    once = pl.Buffered(1)
    return pl.pallas_call(
        _merge_kernel,
        grid=(t // tm,),
        in_specs=[pl.BlockSpec((tm, bw), lambda i, k=k: (i, k)) for k in range(N_BRANCH)] + [
            pl.BlockSpec((tm, d), lambda i, k=k: (i, gate_blk + k)) for k in range(N_BRANCH)] + [
            pl.BlockSpec((1,) + w_branch.shape[1:], lambda i: (layer, 0, 0, 0), pipeline_mode=once),
            pl.BlockSpec((1,) + w_out.shape[1:], lambda i: (layer, 0, 0), pipeline_mode=once),
            pl.BlockSpec((tm, d), lambda i: (i, 0)),
            pl.BlockSpec((1, d), lambda i: (0, 0))],
        out_specs=pl.BlockSpec((tm, d), lambda i: (i, 0)),
        out_shape=jax.ShapeDtypeStruct((t, d), F32),
        compiler_params=_cparams(("parallel",), 56),
        name="merge",
    )(*([hbuf] * N_BRANCH), *([u] * N_BRANCH), w_branch, w_out, x, nw)


def _proj_res_kernel(a_ref, w_ref, x_ref, nw_ref, o_ref):
    o_ref[...] = x_ref[...] + _rms(_dot(a_ref[...], w_ref[0]), nw_ref[...])


def _proj_res(a, w, layer, x, nw):
    t, d = x.shape
    tm = _tile(t, 640)
    return pl.pallas_call(
        _proj_res_kernel,
        grid=(t // tm,),
        in_specs=[pl.BlockSpec((tm, d), lambda i: (i, 0)),
                  pl.BlockSpec((1, d, d), lambda i: (layer, 0, 0), pipeline_mode=pl.Buffered(1)),
                  pl.BlockSpec((tm, d), lambda i: (i, 0)),
                  pl.BlockSpec((1, d), lambda i: (0, 0))],
        out_specs=pl.BlockSpec((tm, d), lambda i: (i, 0)),
        out_shape=jax.ShapeDtypeStruct((t, d), F32),
        compiler_params=_cparams(("parallel",), 56),
        name="xa_out",
    )(a, w, x, nw)


def _rope(x, cs, sn):
    return x * cs + pltpu.roll(x, x.shape[1] // 2, 1) * sn


def _group_norm(o, eps=1e-5):
    oc = o - jnp.mean(o, axis=-1, keepdims=True)
    return oc * lax.rsqrt(jnp.mean(oc * oc, axis=-1, keepdims=True) + eps)


def _ret_kernel(q_ref, k_ref, v_ref, g_ref, cos_ref, sin_ref, dm_ref, dec_ref, h_ref, st_ref, s_ref, *, nc):
    c = pl.program_id(1)

    @pl.when(c == 0)
    def _():
        s_ref[...] = jnp.zeros_like(s_ref)

    cs, sn = cos_ref[...], sin_ref[...]
    hs = range(N_HEADS)
    kc = [slice(h * DK, (h + 1) * DK) for h in hs]
    vc = [slice(h * DV, (h + 1) * DV) for h in hs]
    s = [s_ref[h] for h in hs]
    dec = [dec_ref[h] for h in hs]
    qb = [_rope(q_ref[:, kc[h]], cs, sn).astype(BF16) for h in hs]
    kr = [_rope(k_ref[:, kc[h]], cs, sn) * (DK ** -0.5) for h in hs]
    vb = [v_ref[:, vc[h]].astype(BF16) for h in hs]
    sc = [_dot_nt(qb[h], kr[h].astype(BF16)) * dm_ref[h] for h in hs]
    qs = [_dot(qb[h], s[h].astype(BF16)) * dec[h][:, 0:1] for h in hs]
    o = [_dot(sc[h].astype(BF16), vb[h]) + qs[h] for h in hs]
    s_new = [dec[h][0:1, 2:3] * s[h] + _dot((kr[h] * dec[h][:, 1:2]).T.astype(BF16), vb[h]) for h in hs]
    out = [(_group_norm(o[h]) * _silu(g_ref[:, vc[h]])).astype(h_ref.dtype) for h in hs]
    for h in hs:
        s_ref[h] = s_new[h]
        h_ref[:, vc[h]] = out[h]

    @pl.when(c == nc - 1)
    def _():
        st_ref[...] = s_ref[...].reshape(st_ref.shape)


def _seg_spec(nc, width, off):
    return pl.BlockSpec((CHUNK, width), lambda b, c: (b * nc + c, off // width))


def _layer_state_spec(layer, tail):
    return pl.BlockSpec((1, 1) + tail, lambda b, c: (layer, b) + (0,) * len(tail))


def _const_spec(shape):
    return pl.BlockSpec(shape, lambda b, c: (0,) * len(shape))


_HW, _KW = N_HEADS * DV, N_HEADS * DK


BR_RET, BR_ML, BR_S5, BR_HG = range(N_BRANCH)


def _ret_prompt(u, hbuf, bsz, seqlen, cos2, sin2, dmat, dec, layer, depth, prev):
    nc = seqlen // CHUNK
    return _pcall(
        functools.partial(_ret_kernel, nc=nc), (u, u, u, u, cos2, sin2, dmat, dec),
        [_seg_spec(nc, _KW, OFF_RQ), _seg_spec(nc, _KW, OFF_RK), _seg_spec(nc, _HW, OFF_RV),
         _seg_spec(nc, _HW, OFF_RG),
         pl.BlockSpec((CHUNK, DK), lambda b, c: (c, 0)),
         pl.BlockSpec((CHUNK, DK), lambda b, c: (c, 0)),
         _const_spec(dmat.shape), _const_spec(dec.shape)],
        carried={0: hbuf, 1: prev},
        grid=(bsz, nc),
        out_specs=[pl.BlockSpec((CHUNK, _HW), lambda b, c: (b * nc + c, BR_RET)),
                   _layer_state_spec(layer, (N_HEADS, DK, DV))],
        out_shape=[jax.ShapeDtypeStruct(hbuf.shape, BF16),
                   jax.ShapeDtypeStruct((depth, bsz, N_HEADS, DK, DV), F32)],
        scratch_shapes=[pltpu.VMEM((N_HEADS, DK, DV), F32)],
        compiler_params=_cparams(("parallel", "arbitrary"), 32),
        name="ret_prompt",
    )


def _mlstm_kernel(gb_ref, q_ref, k_ref, v_ref, mo_ref, ug_ref, ugt_ref, nw_ref,
                  h_ref, c_out, n_out, m_out, cm_ref, nv_ref, m_ref, *, nc):
    c = pl.program_id(1)

    @pl.when(c == 0)
    def _():
        cm_ref[...] = jnp.zeros_like(cm_ref)
        nv_ref[...] = jnp.zeros_like(nv_ref)
        m_ref[...] = jnp.zeros_like(m_ref)

    n = CHUNK
    row = lax.broadcasted_iota(jnp.int32, (n, n), 0)
    col = lax.broadcasted_iota(jnp.int32, (n, n), 1)
    causal = row >= col
    ug = ug_ref[...]
    hs = range(N_HEADS)
    kc = [slice(h * DK, (h + 1) * DK) for h in hs]
    vc = [slice(h * DV, (h + 1) * DV) for h in hs]
    cm = [cm_ref[h] for h in hs]
    nv = [nv_ref[h] for h in hs]
    m = [m_ref[h] for h in hs]
    ig_row = [ugt_ref[h:h + 1, :] + gb_ref[0, h] for h in hs]
    lf_row = [_log_sigmoid(ugt_ref[N_HEADS + h:N_HEADS + h + 1, :] + gb_ref[1, h]) for h in hs]
    ig_col = [ug[:, h:h + 1] + gb_ref[0, h] for h in hs]
    lf_col = [_log_sigmoid(ug[:, N_HEADS + h:N_HEADS + h + 1] + gb_ref[1, h]) for h in hs]
    b_col = [jnp.sum(jnp.where(causal, lf_row[h], 0.0), axis=1, keepdims=True) for h in hs]
    b_row = [jnp.sum(jnp.where(row <= col, lf_col[h], 0.0), axis=0, keepdims=True) for h in hs]
    b_last = [b_col[h][n - 1:n, :] for h in hs]
    logd = [jnp.where(causal, b_col[h] - b_row[h] + ig_row[h], -jnp.inf) for h in hs]
    inter = [m[h] + b_col[h] for h in hs]
    mi = [jnp.maximum(inter[h], jnp.max(logd[h], axis=1, keepdims=True)) for h in hs]
    w = [jnp.exp(logd[h] - mi[h]) for h in hs]
    wi = [jnp.exp(inter[h] - mi[h]) for h in hs]
    q = [q_ref[:, kc[h]] for h in hs]
    k = [k_ref[:, kc[h]] * (DK ** -0.5) for h in hs]
    qb = [q[h].astype(BF16) for h in hs]
    vb = [v_ref[:, vc[h]].astype(BF16) for h in hs]
    a = [_dot_nt(qb[h], k[h].astype(BF16)) * w[h] for h in hs]
    qc = [_dot(qb[h], cm[h].astype(BF16)) for h in hs]
    num = [_dot(a[h].astype(BF16), vb[h]) + wi[h] * qc[h] for h in hs]
    nq = [jnp.sum(a[h], axis=1, keepdims=True) + wi[h] * jnp.sum(q[h] * nv[h], axis=1, keepdims=True) for h in hs]
    hh = [num[h] / jnp.maximum(jnp.abs(nq[h]), jnp.exp(-mi[h])) for h in hs]
    m_new = [mi[h][n - 1:n, :] for h in hs]
    wl = [jnp.exp(b_last[h] - b_col[h] + ig_col[h] - m_new[h]) for h in hs]
    dp = [jnp.exp(m[h] + b_last[h] - m_new[h]) for h in hs]
    kw = [k[h] * wl[h] for h in hs]
    cm_new = [dp[h] * cm[h] + _dot(kw[h].T.astype(BF16), vb[h]) for h in hs]
    nv_new = [dp[h] * nv[h] + jnp.sum(kw[h], axis=0, keepdims=True) for h in hs]
    out = [(_rms(hh[h], nw_ref[:, vc[h]]) * _sigmoid(mo_ref[:, vc[h]])).astype(h_ref.dtype) for h in hs]
    for h in hs:
        cm_ref[h] = cm_new[h]
        nv_ref[h] = nv_new[h]
        m_ref[h] = m_new[h]
        h_ref[:, vc[h]] = out[h]

    @pl.when(c == nc - 1)
    def _():
        c_out[...] = cm_ref[...].reshape(c_out.shape)
        n_out[...] = nv_ref[...].reshape(n_out.shape)
        m_out[...] = m_ref[...].reshape(m_out.shape)


def _mlstm_prompt(u, hbuf, ug, ugt, gate_bias, norm_w, bsz, seqlen, layer, depth, prev):
    nc = seqlen // CHUNK
    return _pcall(
        functools.partial(_mlstm_kernel, nc=nc), (gate_bias, u, u, u, u, ug, ugt, norm_w),
        [pl.BlockSpec(memory_space=pltpu.SMEM),
         _seg_spec(nc, _KW, OFF_MQ), _seg_spec(nc, _KW, OFF_MK), _seg_spec(nc, _HW, OFF_MV),
         _seg_spec(nc, _HW, OFF_MO),
         pl.BlockSpec((CHUNK, GATE_PAD), lambda b, c: (b * nc + c, 0)),
         pl.BlockSpec((2 * N_HEADS, CHUNK), lambda b, c: (0, b * nc + c)),
         _const_spec((1, _HW))],
        carried={0: hbuf, 1: prev[0], 2: prev[1], 3: prev[2]},
        grid=(bsz, nc),
        out_specs=[pl.BlockSpec((CHUNK, _HW), lambda b, c: (b * nc + c, BR_ML)),
                   _layer_state_spec(layer, (N_HEADS, DK, DV)),
                   _layer_state_spec(layer, (N_HEADS, 1, DK)),
                   _layer_state_spec(layer, (N_HEADS, 1, 1))],
        out_shape=[jax.ShapeDtypeStruct(hbuf.shape, BF16),
                   jax.ShapeDtypeStruct((depth, bsz, N_HEADS, DK, DV), F32),
                   jax.ShapeDtypeStruct((depth, bsz, N_HEADS, 1, DK), F32),
                   jax.ShapeDtypeStruct((depth, bsz, N_HEADS, 1, 1), F32)],
        scratch_shapes=[pltpu.VMEM((N_HEADS, DK, DV), F32), pltpu.VMEM((N_HEADS, 1, DK), F32),
                        pltpu.VMEM((N_HEADS, 1, 1), F32)],
        compiler_params=_cparams(("parallel", "arbitrary"), 32),
        name="mlstm_prompt",
    )


def _hgrn_lower_bound(logits, layer):
    e = jnp.exp(logits - jnp.max(logits, axis=0, keepdims=True))
    p = e / jnp.sum(e, axis=0, keepdims=True)
    lb = jnp.zeros_like(p[0:1, :])
    for r in range(1, layer + 1):
        lb = lb + p[r:r + 1, :]
    return lb


def _hgrn_gates(fpre, lb):
    hlf = _logaddexp(jnp.log(lb), jnp.log1p(-lb) + _log_sigmoid(fpre))
    hk = (1.0 - lb) * _sigmoid(-fpre)
    return hlf, hk


def _hgrn_kernel(lg_ref, q_ref, f_ref, i_ref, g_ref, nw_ref, h_ref, st_ref, s_ref, *, nc, layer):
    c = pl.program_id(1)

    @pl.when(c == 0)
    def _():
        s_ref[...] = jnp.zeros_like(s_ref)

    n = CHUNK
    lb_all = _hgrn_lower_bound(lg_ref[...], layer)
    r2 = lax.broadcasted_iota(jnp.int32, (n, n), 0)
    c2 = lax.broadcasted_iota(jnp.int32, (n, n), 1)
    rowv = lax.broadcasted_iota(jnp.int32, (n, DK), 0)
    n_lev = n.bit_length() - 1
    rights = [(rowv & (1 << lev)) != 0 for lev in range(n_lev)]
    pair_masks = [((r2 >> (lev + 1)) == (c2 >> (lev + 1))) & ((r2 & (1 << lev)) != 0) & ((c2 & (1 << lev)) == 0)
                  for lev in range(n_lev)]
    hs = range(N_HEADS)
    kc = [slice(h * DK, (h + 1) * DK) for h in hs]
    vc = [slice(h * DV, (h + 1) * DV) for h in hs]
    st = [s_ref[h] for h in hs]
    gates = [_hgrn_gates(f_ref[:, kc[h]], lb_all[:, kc[h]]) for h in hs]
    hk = [gates[h][1] for h in hs]
    q = [_silu(q_ref[:, kc[h]]) for h in hs]
    ib = [i_ref[:, vc[h]].astype(BF16) for h in hs]
    bc = [_cumsum_rows(gates[h][0]) for h in hs]
    a = [jnp.where(r2 == c2, _dot_nt(q[h].astype(BF16), hk[h].astype(BF16)), 0.0) for h in hs]
    p = list(bc)
    for lev in range(n_lev):
        s = 1 << lev
        right = rights[lev]
        nxt = [pltpu.roll(p[h], n - s, 0) for h in hs]
        e = [jnp.exp(jnp.where(right, bc[h] - p[h], nxt[h] - bc[h])) for h in hs]
        m = [(jnp.where(right, q[h], hk[h]) * e[h]).astype(BF16) for h in hs]
        a = [jnp.where(pair_masks[lev], _dot_nt(m[h], m[h]), a[h]) for h in hs]
        if lev + 1 < n_lev:
            p = [jnp.where(right, pltpu.roll(p[h], s, 0), p[h]) for h in hs]
    qd = [_dot((q[h] * jnp.exp(bc[h])).astype(BF16), st[h].astype(BF16)) for h in hs]
    o = [_dot(a[h].astype(BF16), ib[h]) + qd[h] for h in hs]
    bl = [bc[h][n - 1:n, :] for h in hs]
    kd = [_dot((hk[h] * jnp.exp(bl[h] - bc[h])).T.astype(BF16), ib[h]) for h in hs]
    new_states = [_row_to_col(jnp.exp(bl[h])) * st[h] + kd[h] for h in hs]
    outs = [(_rms(o[h], nw_ref[:, vc[h]]) * _silu(g_ref[:, vc[h]])).astype(h_ref.dtype) for h in hs]
    for h in hs:
        s_ref[h] = new_states[h]
        h_ref[:, vc[h]] = outs[h]

    @pl.when(c == nc - 1)
    def _():
        st_ref[...] = s_ref[...].reshape(st_ref.shape)


def _hgrn_prompt(u, hbuf, logits, norm_w, bsz, seqlen, layer, prev):
    nc = seqlen // CHUNK
    depth = logits.shape[0]
    return _pcall(
        functools.partial(_hgrn_kernel, nc=nc, layer=layer), (logits, u, u, u, u, norm_w),
        [_const_spec(logits.shape),
         _seg_spec(nc, _KW, OFF_HQ), _seg_spec(nc, _KW, OFF_HF), _seg_spec(nc, _HW, OFF_HI),
         _seg_spec(nc, _HW, OFF_HG), _const_spec((1, _HW))],
        carried={0: hbuf, 1: prev},
        grid=(bsz, nc),
        out_specs=[pl.BlockSpec((CHUNK, _HW), lambda b, c: (b * nc + c, BR_HG)),
                   _layer_state_spec(layer, (N_HEADS, DK, DV))],
        out_shape=[jax.ShapeDtypeStruct(hbuf.shape, BF16),
                   jax.ShapeDtypeStruct((depth, bsz, N_HEADS, DK, DV), F32)],
        scratch_shapes=[pltpu.VMEM((N_HEADS, DK, DV), F32)],
        compiler_params=_cparams(("parallel", "arbitrary"), 32),
        name="hgrn_prompt",
    )


def _s5_prep_kernel(lr_ref, li_ref, dt_ref, bre_ref, bim_ref, ar_ref, ai_ref, bbr_ref, bbi_ref):
    lr, li, dt = lr_ref[...], li_ref[...], dt_ref[...]
    mag = jnp.exp(lr * dt)
    ar = mag * jnp.cos(li * dt)
    ai = mag * jnp.sin(li * dt)
    den = lr * lr + li * li
    cr = ((ar - 1.0) * lr + ai * li) / den
    ci = (ai * lr - (ar - 1.0) * li) / den
    ar_ref[...] = ar
    ai_ref[...] = ai
    bbr_ref[...] = cr * bre_ref[...] - ci * bim_ref[...]
    bbi_ref[...] = cr * bim_ref[...] + ci * bre_ref[...]


def _s5_prep(lam_re, lam_im, log_step, b_re, b_im):
    g, n = lam_re.shape
    p = b_re.shape[-1]
    gn = g * n
    dt = jnp.repeat(jnp.exp(log_step.astype(F32)), n).reshape(gn, 1)
    col = lambda a: a.astype(F32).reshape(gn, 1)
    return pl.pallas_call(
        _s5_prep_kernel,
        out_shape=[jax.ShapeDtypeStruct((gn, 1), F32), jax.ShapeDtypeStruct((gn, 1), F32),
                   jax.ShapeDtypeStruct((gn, p), F32), jax.ShapeDtypeStruct((gn, p), F32)],
        name="s5_prep",
    )(col(lam_re), col(lam_im), dt, b_re.reshape(gn, p), b_im.reshape(gn, p))


def _s5_kernel(*refs, seq, nc, n_kt, sw):
    if seq:
        (u_ref, ar_ref, ai_ref, bblk_ref, cblk_ref, d_ref, wglu_ref,
         h_ref, sr_ref, si_ref, hr_scr, hi_scr) = refs
    else:
        (u_ref, ar_ref, ai_ref, bblk_ref, cblk_ref, d_ref, wglu_ref, h0r_ref, h0i_ref,
         h_ref, sr_ref, si_ref) = refs
    u = u_ref[...]
    ub = u.astype(BF16)
    rows = u.shape[0]
    if seq:
        c = pl.program_id(1)

        @pl.when(c == 0)
        def _():
            hr_scr[...] = jnp.zeros_like(hr_scr)
            hi_scr[...] = jnp.zeros_like(hi_scr)

    ys = []
    for kt in range(n_kt):
        lanes = slice(kt * sw, (kt + 1) * sw)
        bu = _dot(ub[:, kt * MXU_K:(kt + 1) * MXU_K], bblk_ref[kt])
        xr, xi = bu[:, :sw], bu[:, sw:]
        ar, ai = ar_ref[:, lanes], ai_ref[:, lanes]
        if seq:
            sub = lax.broadcasted_iota(jnp.int32, (SUBLANES, sw), 0)
            mr, mi = jnp.broadcast_to(ar, (SUBLANES, sw)), jnp.broadcast_to(ai, (SUBLANES, sw))
            tabr, tabi = mr, mi
            levels = []
            sft = 1
            while sft < SUBLANES:
                keep = sub >= sft
                levels.append((sft, jnp.where(keep, mr, 0.0), jnp.where(keep, mi, 0.0)))
                tr, ti = pltpu.roll(tabr, sft, 0), pltpu.roll(tabi, sft, 0)
                tabr, tabi = (jnp.where(keep, tabr * tr - tabi * ti, tabr),
                              jnp.where(keep, tabr * ti + tabi * tr, tabi))
                mr, mi = mr * mr - mi * mi, 2.0 * mr * mi
                sft *= 2
            cr, ci = hr_scr[:, lanes], hi_scr[:, lanes]
            slabs_r, slabs_i = [], []
            for j in range(rows // SUBLANES):
                rs = slice(j * SUBLANES, (j + 1) * SUBLANES)
                sr_, si_ = xr[rs, :], xi[rs, :]
                for sft, lr, li in levels:
                    qr, qi = pltpu.roll(sr_, sft, 0), pltpu.roll(si_, sft, 0)
                    sr_, si_ = sr_ + lr * qr - li * qi, si_ + lr * qi + li * qr
                crb, cib = jnp.broadcast_to(cr, (SUBLANES, sw)), jnp.broadcast_to(ci, (SUBLANES, sw))
                sr_, si_ = sr_ + tabr * crb - tabi * cib, si_ + tabr * cib + tabi * crb
                cr, ci = sr_[SUBLANES - 1:SUBLANES, :], si_[SUBLANES - 1:SUBLANES, :]
                slabs_r.append(sr_)
                slabs_i.append(si_)
            xr, xi = jnp.concatenate(slabs_r, axis=0), jnp.concatenate(slabs_i, axis=0)
            hr_scr[:, lanes] = cr
            hi_scr[:, lanes] = ci
        else:
            h0r, h0i = h0r_ref[0, :, lanes], h0i_ref[0, :, lanes]
            xr, xi = xr + ar * h0r - ai * h0i, xi + ar * h0i + ai * h0r
            sr_ref[0, :, lanes] = xr
            si_ref[0, :, lanes] = xi
        ys.append(_dot(jnp.concatenate([xr, xi], axis=1).astype(BF16), cblk_ref[kt]))
    y = jnp.concatenate(ys, axis=1) + d_ref[...] * u
    z = _gelu_tanh(y)
    h_ref[...] = (z * _sigmoid(_dot(z.astype(BF16), wglu_ref[...]))).astype(h_ref.dtype)
    if seq:
        @pl.when(c == nc - 1)
        def _():
            sr_ref[...] = hr_scr[...].reshape(sr_ref.shape)
            si_ref[...] = hi_scr[...].reshape(si_ref.shape)


def _s5_weights(bbr, bbi, c_re, c_im):
    gn, p = bbr.shape
    g = gn // S5_N
    gpt = MXU_K // p
    n_kt = g // gpt
    eye = jnp.eye(gpt, dtype=F32)

    def b_blk(bb):
        return jnp.einsum('kgnp,gh->kgphn', bb.reshape(n_kt, gpt, S5_N, p), eye).reshape(n_kt, gpt * p, gpt * S5_N)

    def c_blk(cc):
        return jnp.einsum('kgpn,gh->kgnhp', cc.astype(F32).reshape(n_kt, gpt, p, S5_N), eye).reshape(
            n_kt, gpt * S5_N, gpt * p)

    bblk = jnp.concatenate([b_blk(bbr), b_blk(bbi)], axis=2).astype(BF16)
    cblk = jnp.concatenate([c_blk(c_re), -c_blk(c_im)], axis=1).astype(BF16)
    return bblk, cblk


def _s5_prompt(u, hbuf, ar, ai, bblk, cblk, d, wglu, bsz, seqlen, layer, depth, prev):
    width = wglu.shape[0]
    n_kt = bblk.shape[0]
    sw = bblk.shape[2] // 2
    ct = _tile(seqlen, 256)
    nc = seqlen // ct
    gn = ar.shape[1]
    full = lambda shape: pl.BlockSpec(shape, lambda b, c: (0,) * len(shape))
    st_spec = pl.BlockSpec((1, 1, 1, gn), lambda b, c: (layer, b, 0, 0))
    st_shape = jax.ShapeDtypeStruct((depth, bsz, 1, gn), F32)
    return _pcall(
        functools.partial(_s5_kernel, seq=True, nc=nc, n_kt=n_kt, sw=sw), (u, ar, ai, bblk, cblk, d, wglu),
        [pl.BlockSpec((ct, width), lambda b, c: (b * nc + c, OFF_SU // width)),
         full((1, gn)), full((1, gn)), full(bblk.shape), full(cblk.shape),
         full((1, width)), full(wglu.shape)],
        carried={0: hbuf, 1: prev[0], 2: prev[1]},
        grid=(bsz, nc),
        out_specs=[pl.BlockSpec((ct, width), lambda b, c: (b * nc + c, BR_S5)), st_spec, st_spec],
        out_shape=[jax.ShapeDtypeStruct(hbuf.shape, BF16), st_shape, st_shape],
        scratch_shapes=[pltpu.VMEM((1, gn), F32), pltpu.VMEM((1, gn), F32)],
        compiler_params=_cparams(("parallel", "arbitrary"), 56),
        name="s5_prompt",
    )


def _s5_sample(u, tp, ar, ai, bblk, cblk, d, wglu, h0r, h0i, layer, h_buf, prev):
    width = wglu.shape[0]
    n_kt = bblk.shape[0]
    sw = bblk.shape[2] // 2
    depth, bs, gn = h0r.shape
    tb = _tile(bs, 128, 8)
    r0 = tp // tb
    full = lambda shape: pl.BlockSpec(shape, lambda i: (0,) * len(shape))
    st_spec = pl.BlockSpec((1, tb, gn), lambda i: (layer, i, 0))
    st_shape = jax.ShapeDtypeStruct((depth, bs, gn), F32)
    return _pcall(
        functools.partial(_s5_kernel, seq=False, nc=1, n_kt=n_kt, sw=sw),
        (u, ar, ai, bblk, cblk, d, wglu, h0r, h0i),
        [pl.BlockSpec((tb, width), lambda i: (r0 + i, OFF_SU // width)),
         full((1, gn)), full((1, gn)), full(bblk.shape), full(cblk.shape),
         full((1, width)), full(wglu.shape), st_spec, st_spec],
        carried={0: h_buf, 1: prev[0], 2: prev[1]},
        grid=(bs // tb,),
        out_specs=[pl.BlockSpec((tb, width), lambda i: (r0 + i, BR_S5)), st_spec, st_spec],
        out_shape=[jax.ShapeDtypeStruct(h_buf.shape, BF16), st_shape, st_shape],
        compiler_params=_cparams(("parallel",), 56),
        name="s5_sample",
    )


SAMPLE_BLOCK = 8


def _state_step(s, dcol, kcol, vrow, qcol):
    s_new = dcol * s + kcol * vrow
    return s_new, jnp.sum(qcol * s_new, axis=0, keepdims=True)


def _zero_other_layers(ref, slot):
    for l in range(ref.shape[0]):
        if l != slot:
            ref[l] = jnp.zeros(ref.shape[1:], ref.dtype)


def _ret_step_kernel(gam_ref, q_ref, k_ref, v_ref, g_ref, cos_ref, sin_ref, st_ref, h_ref, so_ref, o_scr, *, slot):
    _zero_other_layers(so_ref, slot)
    cs, sn = cos_ref[...], sin_ref[...]
    for h in range(N_HEADS):
        qr = _rope(q_ref[:, h * DK:(h + 1) * DK], cs, sn)
        kr = _rope(k_ref[:, h * DK:(h + 1) * DK], cs, sn) * (DK ** -0.5)
        for i in range(SAMPLE_BLOCK):
            s_new, o = _state_step(st_ref[0, i, h], gam_ref[h], _row_to_col(kr[i:i + 1, :]),
                                   v_ref[i:i + 1, h * DV:(h + 1) * DV], _row_to_col(qr[i:i + 1, :]))
            so_ref[slot, i, h] = s_new
            o_scr[i:i + 1, h * DV:(h + 1) * DV] = o
    for h in range(N_HEADS):
        cols = slice(h * DV, (h + 1) * DV)
        h_ref[:, cols] = (_group_norm(o_scr[:, cols]) * _silu(g_ref[:, cols])).astype(h_ref.dtype)


def _mlstm_step_kernel(gb_ref, q_ref, k_ref, v_ref, mo_ref, ug_ref, nw_ref, c_ref, n_ref, m_ref,
                       h_ref, co_ref, no_ref, mo_out_ref, o_scr, *, slot):
    _zero_other_layers(co_ref, slot)
    ug = ug_ref[...]
    m_all = m_ref[0]
    m_new_cols = []
    for h in range(N_HEADS):
        ig = ug[:, h:h + 1] + gb_ref[0, h]
        lf = _log_sigmoid(ug[:, N_HEADS + h:N_HEADS + h + 1] + gb_ref[1, h])
        m_old = m_all[:, h:h + 1]
        inter = m_old + lf
        mi = jnp.maximum(inter, ig)
        w = jnp.exp(ig - mi)
        wi = jnp.exp(inter - mi)
        m_new_cols.append(mi)
        lim = jnp.exp(-mi)
        qh = q_ref[:, h * DK:(h + 1) * DK]
        kh = k_ref[:, h * DK:(h + 1) * DK] * (DK ** -0.5)
        for i in range(SAMPLE_BLOCK):
            wk = w[i:i + 1, :] * kh[i:i + 1, :]
            c_new, num = _state_step(c_ref[0, i, h], wi[i:i + 1, :], _row_to_col(wk),
                                     v_ref[i:i + 1, h * DV:(h + 1) * DV], _row_to_col(qh[i:i + 1, :]))
            n_new = wi[i:i + 1, :] * n_ref[0, i, h:h + 1, :] + wk
            nq = jnp.sum(qh[i:i + 1, :] * n_new, axis=1, keepdims=True)
            co_ref[slot, i, h] = c_new
            no_ref[0, i, h:h + 1, :] = n_new
            o_scr[i:i + 1, h * DV:(h + 1) * DV] = num / jnp.maximum(jnp.abs(nq), lim[i:i + 1, :])
    mo_out_ref[0] = jnp.concatenate(m_new_cols, axis=1)
    for h in range(N_HEADS):
        cols = slice(h * DV, (h + 1) * DV)
        h_ref[:, cols] = (_rms(o_scr[:, cols], nw_ref[:, cols]) * _sigmoid(mo_ref[:, cols])).astype(h_ref.dtype)


def _hgrn_step_kernel(lg_ref, q_ref, f_ref, i_ref, g_ref, nw_ref, st_ref, h_ref, so_ref, o_scr, *, layer, slot):
    _zero_other_layers(so_ref, slot)
    lb_all = _hgrn_lower_bound(lg_ref[...], layer)
    for h in range(N_HEADS):
        hlf, hk = _hgrn_gates(f_ref[:, h * DK:(h + 1) * DK], lb_all[:, h * DK:(h + 1) * DK])
        dec = jnp.exp(hlf)
        qh = _silu(q_ref[:, h * DK:(h + 1) * DK])
        for i in range(SAMPLE_BLOCK):
            s_new, o = _state_step(st_ref[0, i, h], _row_to_col(dec[i:i + 1, :]), _row_to_col(hk[i:i + 1, :]),
                                   i_ref[i:i + 1, h * DV:(h + 1) * DV], _row_to_col(qh[i:i + 1, :]))
            so_ref[slot, i, h] = s_new
            o_scr[i:i + 1, h * DV:(h + 1) * DV] = o
    for h in range(N_HEADS):
        cols = slice(h * DV, (h + 1) * DV)
        h_ref[:, cols] = (_rms(o_scr[:, cols], nw_ref[:, cols]) * _silu(g_ref[:, cols])).astype(h_ref.dtype)


def _seg(width, off, tp):
    return pl.BlockSpec((SAMPLE_BLOCK, width), lambda i: (tp // SAMPLE_BLOCK + i, off // width))


def _state_spec(layer, tail):
    return pl.BlockSpec((1, SAMPLE_BLOCK) + tail, lambda i: (layer, i) + (0,) * len(tail))


_MAT = (N_HEADS, DK, DV)
_SMEM = pl.BlockSpec(memory_space=pltpu.SMEM)


def _whole(shape):
    return pl.BlockSpec(shape, lambda i: (0,) * len(shape))


def _h_rows_spec(tp, branch):
    return pl.BlockSpec((SAMPLE_BLOCK, _HW), lambda i: (tp // SAMPLE_BLOCK + i, branch))


def _new_mat_state(layer, depth, prev):
    if prev is None:
        return pl.BlockSpec((depth, SAMPLE_BLOCK) + _MAT, lambda i: (0, i, 0, 0, 0)), layer
    return _state_spec(layer, _MAT), 0


def _ret_sample(u, tp, gamma, cos2, sin2, state, layer, h_buf, prev):
    depth, bs = state.shape[:2]
    spec, slot = _new_mat_state(layer, depth, prev)
    return _pcall(
        functools.partial(_ret_step_kernel, slot=slot), (gamma, u, u, u, u, cos2, sin2, state),
        [_SMEM, _seg(_KW, OFF_RQ, tp), _seg(_KW, OFF_RK, tp), _seg(_HW, OFF_RV, tp), _seg(_HW, OFF_RG, tp),
         _whole((1, DK)), _whole((1, DK)), _state_spec(layer, _MAT)],
        carried={0: h_buf, 1: prev},
        grid=(bs // SAMPLE_BLOCK,),
        out_specs=[_h_rows_spec(tp, BR_RET), spec],
        out_shape=[jax.ShapeDtypeStruct(h_buf.shape, BF16), jax.ShapeDtypeStruct(state.shape, F32)],
        scratch_shapes=[pltpu.VMEM((SAMPLE_BLOCK, _HW), F32)],
        compiler_params=_cparams(("parallel",), 48),
        name="ret_sample",
    )


def _mlstm_sample(u, ug, tp, gate_bias, norm_w, st_c, st_n, st_m, layer, h_buf, prev):
    depth, bs = st_c.shape[:2]
    spec, slot = _new_mat_state(layer, depth, prev[0])
    return _pcall(
        functools.partial(_mlstm_step_kernel, slot=slot), (gate_bias, u, u, u, u, ug, norm_w, st_c, st_n, st_m),
        [_SMEM, _seg(_KW, OFF_MQ, tp), _seg(_KW, OFF_MK, tp), _seg(_HW, OFF_MV, tp), _seg(_HW, OFF_MO, tp),
         pl.BlockSpec((SAMPLE_BLOCK, GATE_PAD), lambda i: (tp // SAMPLE_BLOCK + i, 0)), _whole((1, _HW)),
         _state_spec(layer, _MAT), _state_spec(layer, (N_HEADS, DK)), _state_spec(layer, (N_HEADS,))],
        carried={0: h_buf, 1: prev[0], 2: prev[1], 3: prev[2]},
        grid=(bs // SAMPLE_BLOCK,),
        out_specs=[_h_rows_spec(tp, BR_ML), spec,
                   _state_spec(layer, (N_HEADS, DK)), _state_spec(layer, (N_HEADS,))],
        out_shape=[jax.ShapeDtypeStruct(h_buf.shape, BF16), jax.ShapeDtypeStruct(st_c.shape, F32),
                   jax.ShapeDtypeStruct(st_n.shape, F32), jax.ShapeDtypeStruct(st_m.shape, F32)],
        scratch_shapes=[pltpu.VMEM((SAMPLE_BLOCK, _HW), F32)],
        compiler_params=_cparams(("parallel",), 48),
        name="mlstm_sample",
    )


def _hgrn_sample(u, tp, logits, norm_w, state, layer, h_buf, prev):
    depth, bs = state.shape[:2]
    spec, slot = _new_mat_state(layer, depth, prev)
    return _pcall(
        functools.partial(_hgrn_step_kernel, layer=layer, slot=slot), (logits, u, u, u, u, norm_w, state),
        [_whole(logits.shape), _seg(_KW, OFF_HQ, tp), _seg(_KW, OFF_HF, tp), _seg(_HW, OFF_HI, tp),
         _seg(_HW, OFF_HG, tp), _whole((1, _HW)), _state_spec(layer, _MAT)],
        carried={0: h_buf, 1: prev},
        grid=(bs // SAMPLE_BLOCK,),
        out_specs=[_h_rows_spec(tp, BR_HG), spec],
        out_shape=[jax.ShapeDtypeStruct(h_buf.shape, BF16), jax.ShapeDtypeStruct(state.shape, F32)],
        scratch_shapes=[pltpu.VMEM((SAMPLE_BLOCK, _HW), F32)],
        compiler_params=_cparams(("parallel",), 48),
        name="hgrn_sample",
    )


def _attn_prompt_kernel(q_ref, k_ref, v_ref, o_ref, *, scale):
    s = _dot_nt(q_ref[...].astype(BF16), k_ref[0, 0].astype(BF16)) * scale
    e = jnp.exp(s - jnp.max(s, axis=1, keepdims=True))
    p = e / jnp.sum(e, axis=1, keepdims=True)
    o_ref[...] = _dot(p.astype(BF16), v_ref[0, 0].astype(BF16)).astype(o_ref.dtype)


def _attn_prompt(q, kvbuf, obuf, bsz, seqlen, layer):
    t, d = q.shape
    dh = d // N_HEADS
    mem_len = kvbuf.shape[2] // bsz
    tq = _tile(seqlen, 2048)
    nq = seqlen // tq
    return _pcall(
        functools.partial(_attn_prompt_kernel, scale=dh ** -0.5), (q, kvbuf, kvbuf),
        [pl.BlockSpec((tq, dh), lambda b, h, i: (b * nq + i, h)),
         pl.BlockSpec((1, 1, mem_len, dh), lambda b, h, i: (layer, 0, b, h)),
         pl.BlockSpec((1, 1, mem_len, dh), lambda b, h, i: (layer, 1, b, h))],
        carried={0: obuf},
        grid=(bsz, N_HEADS, nq),
        out_specs=pl.BlockSpec((tq, dh), lambda b, h, i: (b * nq + i, h)),
        out_shape=jax.ShapeDtypeStruct((t, d), BF16),
        compiler_params=_cparams(("parallel", "parallel", "arbitrary"), 32),
        name="attn_prompt",
    )


def _attn_sample_kernel(q_ref, k_ref, v_ref, o_ref, *, scale):
    for h in range(q_ref.shape[1]):
        k = k_ref[0, 0, :, h, :]
        s = jnp.sum(k * q_ref[0, h:h + 1, :], axis=1, keepdims=True) * scale
        e = jnp.exp(s - jnp.max(s, axis=0, keepdims=True))
        p = e / jnp.sum(e, axis=0, keepdims=True)
        o_ref[0, h:h + 1, :] = jnp.sum(p * v_ref[0, 0, :, h, :], axis=0, keepdims=True)


def _attn_sample(q4, cache_k, cache_v, layer):
    bs, nh, dh = q4.shape
    mem_len = cache_k.shape[2]
    kv_spec = pl.BlockSpec((1, 1, mem_len, nh, dh), lambda i: (layer, i, 0, 0, 0))
    return pl.pallas_call(
        functools.partial(_attn_sample_kernel, scale=dh ** -0.5),
        grid=(bs,),
        in_specs=[pl.BlockSpec((1, nh, dh), lambda i: (i, 0, 0)), kv_spec, kv_spec],
        out_specs=pl.BlockSpec((1, nh, dh), lambda i: (i, 0, 0)),
        out_shape=jax.ShapeDtypeStruct((bs, nh, dh), F32),
        compiler_params=_cparams(("parallel",), 48),
        name="attn_sample",
    )(q4, cache_k, cache_v)


def _rope_tables(pos):
    half = DK // 2
    inv = ROPE_BASE ** (-jnp.arange(half, dtype=F32) / half)
    ang = pos.astype(F32)[:, None] * inv[None, :]
    cos, sin = jnp.cos(ang), jnp.sin(ang)
    return jnp.concatenate([cos, cos], axis=1), jnp.concatenate([-sin, sin], axis=1)


def _retention_tables(c):
    lg = jnp.log1p(-jnp.exp2(-5.0 - jnp.arange(N_HEADS, dtype=F32)))
    idx = jnp.arange(c, dtype=F32)
    diff = idx[:, None] - idx[None, :]
    dmat = jnp.where(diff[None] >= 0, jnp.exp(jnp.maximum(diff, 0.0)[None] * lg[:, None, None]), 0.0)
    e1 = jnp.exp((idx[None, :] + 1.0) * lg[:, None])
    e2 = jnp.exp((c - 1.0 - idx)[None, :] * lg[:, None])
    e3 = jnp.broadcast_to(jnp.exp(c * lg)[:, None], (N_HEADS, c))
    dec = jnp.stack([e1, e2, e3] + [jnp.zeros_like(e1)] * 5, axis=-1)
    return dmat, dec


def kernel(x_prompt, x_sample, mem_prompt, state_ret, state_mlstm_c, state_mlstm_n, state_mlstm_m, state_s5_re, state_s5_im, state_hgrn, cache_mem_k, cache_mem_v, norm_w, ffn_w_in, ffn_w_out, w_in, ml_gate_bias, ml_norm_w, s5_lambda_re, s5_lambda_im, s5_log_step, s5_b_re, s5_b_im, s5_c_re, s5_c_im, s5_d, s5_w_glu, hg_lb_logits, hg_norm_w, w_branch, w_out, xa_mem_norm, xa_wq, xa_wkv, xa_wo):
    bsz, seqlen, d = x_prompt.shape
    bs = x_sample.shape[0]
    depth = norm_w.shape[0]
    mem_len = mem_prompt.shape[1]
    tp = bsz * seqlen
    gn = s5_lambda_re.shape[1] * s5_lambda_re.shape[2]
    seg_gate = OFF_MO

    x = jnp.concatenate([x_prompt.reshape(tp, d), x_sample.reshape(bs, d)], axis=0)
    mem = mem_prompt.reshape(bsz * mem_len, d)
    s5_in_re = state_s5_re.reshape(depth, bs, gn)
    s5_in_im = state_s5_im.reshape(depth, bs, gn)

    cos_p, sin_p = _rope_tables(jnp.arange(seqlen, dtype=jnp.int32))
    cos_s, sin_s = _rope_tables(PAST_LEN + jnp.arange(1, dtype=jnp.int32))
    dmat, dec = _retention_tables(CHUNK)
    _, dec1 = _retention_tables(1)
    gamma = dec1[:, 0, 0]

    zf = lambda *shape: jnp.zeros(shape, F32)
    t = tp + bs
    p_ret = zf(depth, bsz, N_HEADS, DK, DV)
    p_hg = zf(depth, bsz, N_HEADS, DK, DV)
    p_ml = [zf(depth, bsz, N_HEADS, DK, DV), zf(depth, bsz, N_HEADS, 1, DK), zf(depth, bsz, N_HEADS, 1, 1)]
    p_s5 = [zf(depth, bsz, 1, gn), zf(depth, bsz, 1, gn)]
    kvbuf = zf(depth, 2, bsz * mem_len, d)
    s_ret = s_hg = None
    s_ml = [None, zf(depth, bs, N_HEADS, DK), zf(depth, bs, N_HEADS)]
    s_s5 = [zf(depth, bs, gn), zf(depth, bs, gn)]
    hbuf = jnp.zeros((t, N_BRANCH * _HW), BF16)
    o = jnp.zeros((t, d), BF16)
    ffn_w_out_bf = ffn_w_out.astype(BF16)
    w_in_t = jnp.swapaxes(w_in, 1, 2)
    w_branch_bf, w_out_bf = w_branch.astype(BF16), w_out.astype(BF16)
    xa_wq_bf, xa_wkv_bf, xa_wo_bf = xa_wq.astype(BF16), xa_wkv.astype(BF16), xa_wo.astype(BF16)
    for l in range(depth):
        nw = norm_w[l]
        x = _ffn(x, nw[0:2], ffn_w_in, l, 0, ffn_w_out_bf)
        n_gate = 2 * N_HEADS
        u, ug = _in_proj(x, nw[2:3], w_in_t, l, seg_gate, n_gate)
        ugt = ug[:tp, :n_gate].T
        mlw = ml_norm_w[l].reshape(1, -1)
        hgw = hg_norm_w[l].reshape(1, -1)
        ar, ai, bbr, bbi = _s5_prep(s5_lambda_re[l], s5_lambda_im[l], s5_log_step[l], s5_b_re[l], s5_b_im[l])
        ar, ai = ar.reshape(1, gn), ai.reshape(1, gn)
        bblk, cblk = _s5_weights(bbr, bbi, s5_c_re[l], s5_c_im[l])
        s5d = s5_d[l].astype(F32).reshape(1, -1)
        wglu = s5_w_glu[l].astype(BF16)
        hbuf, p_ret = _ret_prompt(u, hbuf, bsz, seqlen, cos_p, sin_p, dmat, dec, l, depth, p_ret)
        hbuf, s_ret = _ret_sample(u, tp, gamma, cos_s, sin_s, state_ret, l, hbuf, s_ret)
        hbuf, *p_ml = _mlstm_prompt(u, hbuf, ug, ugt, ml_gate_bias[l], mlw, bsz, seqlen, l, depth, p_ml)
        hbuf, *s_ml = _mlstm_sample(u, ug, tp, ml_gate_bias[l], mlw, state_mlstm_c, state_mlstm_n,
                                    state_mlstm_m, l, hbuf, s_ml)
        hbuf, *p_s5 = _s5_prompt(u, hbuf, ar, ai, bblk, cblk, s5d, wglu, bsz, seqlen, l, depth, p_s5)
        hbuf, *s_s5 = _s5_sample(u, tp, ar, ai, bblk, cblk, s5d, wglu, s5_in_re, s5_in_im, l, hbuf, s_s5)
        hbuf, p_hg = _hgrn_prompt(u, hbuf, hg_lb_logits, hgw, bsz, seqlen, l, p_hg)
        hbuf, s_hg = _hgrn_sample(u, tp, hg_lb_logits, hgw, state_hgrn, l, hbuf, s_hg)
        x = _merge(hbuf, u, w_branch_bf, w_out_bf, l, x, nw[3:4])
        kvbuf = _mem_kv(mem, xa_mem_norm[l].reshape(1, d), xa_wkv_bf, l, depth, kvbuf)
        q = _norm_matmul(x, nw[4:5], xa_wq_bf, l, "xa_q")
        o = _attn_prompt(q, kvbuf, o, bsz, seqlen, l)
        o_s = _attn_sample(q[tp:].reshape(bs, N_HEADS, d // N_HEADS), cache_mem_k, cache_mem_v, l)
        o = lax.dynamic_update_slice(o, o_s.reshape(bs, d).astype(BF16), (tp, 0))
        x = _proj_res(o, xa_wo_bf, l, x, nw[5:6])
        x = _ffn(x, nw[6:8], ffn_w_in, l, 1, ffn_w_out_bf)

    mem_shape = (depth, bsz, mem_len, N_HEADS, d // N_HEADS)
    s5_shape = (depth, -1, gn // S5_N, S5_N)
    return (x[:tp].reshape(bsz, seqlen, d), x[tp:].reshape(bs, 1, d),
            p_ret, p_ml[0], p_ml[1].reshape(depth, bsz, N_HEADS, DK), p_ml[2].reshape(depth, bsz, N_HEADS),
            p_s5[0].reshape(s5_shape), p_s5[1].reshape(s5_shape), p_hg,
            kvbuf[:, 0].reshape(mem_shape), kvbuf[:, 1].reshape(mem_shape),
            s_ret, s_ml[0], s_ml[1], s_ml[2], s_s5[0].reshape(s5_shape), s_s5[1].reshape(s5_shape), s_hg)
```

```python
import functools
import math

import jax
import jax.numpy as jnp
from jax import lax
from jax.experimental import pallas as pl
from jax.experimental.pallas import tpu as pltpu

F32 = jnp.float32
BF16 = jnp.bfloat16

CHUNK = 128
N_BRANCH = 4
N_HEADS = 4
DK = 128
DV = 256
S5_GROUP = 16
S5_N = 64
ROPE_BASE = 10000.0
PAST_LEN = 16384

V7X_VMEM_BYTES = 64 * 1024 * 1024
MXU_K = 256
SUBLANES = 8
MIB = 1024 * 1024

OFF_RQ, OFF_RK, OFF_RV, OFF_RG = 0, 512, 1024, 2048
OFF_MQ, OFF_MK, OFF_MV, OFF_MO = 3072, 3584, 4096, 5120
OFF_SU = 6144
OFF_HQ, OFF_HF, OFF_HI, OFF_HG = 7168, 7680, 8192, 9216
OFF_GATES = 10240
N_MAIN = 18432
GATE_PAD = 128


def _cparams(sem, vmem_mib):
    return pltpu.CompilerParams(dimension_semantics=sem, vmem_limit_bytes=vmem_mib * MIB)


_ANY = pl.BlockSpec(memory_space=pl.ANY)


def _pcall(kernel_fn, inputs, in_specs, carried=None, **kw):
    carried = {o: a for o, a in (carried or {}).items() if a is not None}
    idxs = sorted(carried)
    n_in = len(inputs)

    def body(*refs):
        return kernel_fn(*refs[:n_in], *refs[n_in + len(idxs):])

    return pl.pallas_call(
        body if idxs else kernel_fn,
        in_specs=list(in_specs) + [_ANY] * len(idxs),
        input_output_aliases={n_in + j: o for j, o in enumerate(idxs)},
        **kw)(*inputs, *[carried[o] for o in idxs])


def _tile(n, cap, mult=16):
    best = None
    for t in range(mult, min(n, cap) + 1, mult):
        if n % t == 0:
            best = t
    return best if best is not None else n


def _dot(a, b):
    return jnp.dot(a, b, preferred_element_type=F32)


def _dot_nt(a, b):
    return lax.dot_general(a, b, (((1,), (1,)), ((), ())), preferred_element_type=F32)


def _rms(x, w, eps=1e-6):
    return x * lax.rsqrt(jnp.mean(x * x, axis=-1, keepdims=True) + eps) * w


def _sigmoid(x):
    return 1.0 / (1.0 + jnp.exp(-x))


def _silu(x):
    return x * _sigmoid(x)


def _log_sigmoid(x):
    return jnp.minimum(x, 0.0) - jnp.log1p(jnp.exp(-jnp.abs(x)))


def _logaddexp(a, b):
    return jnp.maximum(a, b) + jnp.log1p(jnp.exp(-jnp.abs(a - b)))


def _gelu_tanh(x):
    return 0.5 * x * (1.0 + jnp.tanh(math.sqrt(2.0 / math.pi) * (x + 0.044715 * (x * x * x))))


def _row_to_col(row):
    n = row.shape[1]
    r = lax.broadcasted_iota(jnp.int32, (n, n), 0)
    c = lax.broadcasted_iota(jnp.int32, (n, n), 1)
    return jnp.sum(jnp.where(r == c, row, 0.0), axis=1, keepdims=True)


def _cumsum_rows(x):
    c = x.shape[0]
    r = lax.broadcasted_iota(jnp.int32, (c, c), 0)
    k = lax.broadcasted_iota(jnp.int32, (c, c), 1)
    tri = jnp.where(r >= k, 1.0, 0.0).astype(BF16)
    hi = x.astype(BF16)
    r1 = x - hi.astype(F32)
    mid = r1.astype(BF16)
    lo = (r1 - mid.astype(F32)).astype(BF16)
    return _dot(tri, hi) + _dot(tri, mid) + _dot(tri, lo)


def _emit(o_ref, n_ref, nwn_ref, rows, out):
    o_ref[rows, :] = out
    n_ref[rows, :] = _rms(out, nwn_ref[...]).astype(n_ref.dtype)


def _norm_cast_kernel(x_ref, nw_ref, n_ref):
    n_ref[...] = _rms(x_ref[...], nw_ref[...]).astype(n_ref.dtype)


def _norm_cast(x, nw):
    t, d = x.shape
    tm = _tile(t, 1040)
    return pl.pallas_call(
        _norm_cast_kernel,
        grid=(t // tm,),
        in_specs=[pl.BlockSpec((tm, d), lambda i: (i, 0)), pl.BlockSpec((1, d), lambda i: (0, 0))],
        out_specs=pl.BlockSpec((tm, d), lambda i: (i, 0)),
        out_shape=jax.ShapeDtypeStruct((t, d), BF16),
        compiler_params=_cparams(("parallel",), 48),
        name="norm_cast",
    )(x, nw)


def _ffn_up_kernel(xn_ref, wg_ref, wu_ref, h_ref, *, rc):
    wg = wg_ref[0, 0].astype(BF16)
    wu = wu_ref[0, 0].astype(BF16)
    for r in range(0, xn_ref.shape[0], rc):
        xn = xn_ref[r:r + rc, :]
        h_ref[r:r + rc, :] = (_silu(_dot(xn, wg)) * _dot(xn, wu)).astype(h_ref.dtype)


def _ffn_down_kernel(h_ref, w_ref, x_ref, nw_ref, nwn_ref, o_ref, n_ref, *, rc):
    for r in range(0, x_ref.shape[0], rc):
        rows = slice(r, r + rc)
        out = x_ref[rows, :] + 0.5 * _rms(_dot(h_ref[rows, :], w_ref[0, 0]), nw_ref[...])
        _emit(o_ref, n_ref, nwn_ref, rows, out)


def _ffn(x, xn, nw_post, nw_next, w_in, layer, which, w_out):
    t, d = x.shape
    dff = w_out.shape[2]
    tm = _tile(t, 2080)
    rc = _tile(tm, 260)
    tf = _tile(dff, 512, 128)
    nf = dff // tf
    hidden = pl.pallas_call(
        functools.partial(_ffn_up_kernel, rc=rc),
        grid=(t // tm, nf),
        in_specs=[pl.BlockSpec((tm, d), lambda i, f: (i, 0)),
                  pl.BlockSpec((1, 1, d, tf), lambda i, f: (layer, which, 0, f)),
                  pl.BlockSpec((1, 1, d, tf), lambda i, f: (layer, which, 0, nf + f))],
        out_specs=pl.BlockSpec((tm, tf), lambda i, f: (i, f)),
        out_shape=jax.ShapeDtypeStruct((t, dff), BF16),
        compiler_params=_cparams(("parallel", "arbitrary"), 56),
        name="ffn_up",
    )(xn, w_in, w_in)
    tm2 = _tile(t, 416)
    row_spec = pl.BlockSpec((tm2, d), lambda i: (i, 0))
    vec_spec = pl.BlockSpec((1, d), lambda i: (0, 0))
    return pl.pallas_call(
        functools.partial(_ffn_down_kernel, rc=_tile(tm2, 208)),
        grid=(t // tm2,),
        in_specs=[pl.BlockSpec((tm2, dff), lambda i: (i, 0)),
                  pl.BlockSpec((1, 1, dff, d), lambda i: (layer, which, 0, 0), pipeline_mode=pl.Buffered(1)),
                  row_spec, vec_spec, vec_spec],
        out_specs=[row_spec, row_spec],
        out_shape=[jax.ShapeDtypeStruct((t, d), F32), jax.ShapeDtypeStruct((t, d), BF16)],
        compiler_params=_cparams(("parallel",), 56),
        name="ffn_down",
    )(hidden, w_out, x, nw_post, nw_next)


def _matmul_kernel(xn_ref, w_ref, o_ref):
    o_ref[...] = _dot(xn_ref[...], w_ref[0]).astype(o_ref.dtype)


def _matmul(xn, w, layer, name):
    t, d = xn.shape
    n = w.shape[2]
    tm = _tile(t, 1040)
    tn = _tile(n, 1024, 128)
    return pl.pallas_call(
        _matmul_kernel,
        grid=(t // tm, n // tn),
        in_specs=[pl.BlockSpec((tm, d), lambda i, j: (i, 0)),
                  pl.BlockSpec((1, d, tn), lambda i, j: (layer, 0, j))],
        out_specs=pl.BlockSpec((tm, tn), lambda i, j: (i, j)),
        out_shape=jax.ShapeDtypeStruct((t, n), F32),
        compiler_params=_cparams(("parallel", "arbitrary"), 48),
        name=name,
    )(xn, w)


def _mem_kv_kernel(x_ref, nw_ref, w_ref, o_ref, xn_ref):
    @pl.when(pl.program_id(1) == 0)
    def _():
        xn_ref[...] = _rms(x_ref[...], nw_ref[...]).astype(BF16)

    o_ref[0, 0] = _dot(xn_ref[...], w_ref[0])


def _mem_kv(mem, nw, w, layer, depth, prev):
    t, d = mem.shape
    tm = _tile(t, 512)
    tn = _tile(d, 1024, 128)
    npk = d // tn
    return _pcall(
        _mem_kv_kernel, (mem, nw, w),
        [pl.BlockSpec((tm, d), lambda i, j: (i, 0)),
         pl.BlockSpec((1, d), lambda i, j: (0, 0)),
         pl.BlockSpec((1, d, tn), lambda i, j: (layer, 0, j))],
        carried={0: prev},
        grid=(t // tm, 2 * npk),
        out_specs=pl.BlockSpec((1, 1, tm, tn), lambda i, j: (layer, j // npk, i, j % npk)),
        out_shape=jax.ShapeDtypeStruct((depth, 2, t, d), F32),
        scratch_shapes=[pltpu.VMEM((tm, d), BF16)],
        compiler_params=_cparams(("parallel", "arbitrary"), 48),
        name="mem_kv",
    )


def _in_proj_kernel(xn_ref, wt_ref, wgt_ref, o_ref, og_ref):
    @pl.when(pl.program_id(1) == 0)
    def _():
        og_ref[...] = _dot_nt(xn_ref[...], wgt_ref[0].astype(BF16))

    o_ref[...] = _dot_nt(xn_ref[...], wt_ref[0].astype(BF16))


def _in_proj(xn, w_in_t, layer, gate_off, n_gate):
    t, d = xn.shape
    n = w_in_t.shape[1] - n_gate
    tm = _tile(t, 2080)
    tn = _tile(math.gcd(n, gate_off), 512, 128)
    n_plain = gate_off // tn
    return pl.pallas_call(
        _in_proj_kernel,
        grid=(t // tm, n // tn),
        in_specs=[pl.BlockSpec((tm, d), lambda i, j: (i, 0)),
                  pl.BlockSpec((pl.Element(1), pl.Element(tn), pl.Element(d)),
                               lambda i, j: (layer, SUBLANES * (j * (tn // SUBLANES) + jnp.where(
                                   j < n_plain, 0, n_gate // SUBLANES)), 0)),
                  pl.BlockSpec((1, GATE_PAD, d), lambda i, j: (layer, gate_off // GATE_PAD, 0))],
        out_specs=[pl.BlockSpec((tm, tn), lambda i, j: (i, j)),
                   pl.BlockSpec((tm, GATE_PAD), lambda i, j: (i, 0))],
        out_shape=[jax.ShapeDtypeStruct((t, n), F32), jax.ShapeDtypeStruct((t, GATE_PAD), F32)],
        compiler_params=_cparams(("parallel", "arbitrary"), 56),
        name="in_proj",
    )(xn, w_in_t, w_in_t)


def _merge_kernel(*refs):
    h_refs, g_refs = refs[:N_BRANCH], refs[N_BRANCH:2 * N_BRANCH]
    wb_ref, wo_ref, x_ref, nw_ref, nwn_ref, o_ref, n_ref = refs[2 * N_BRANCH:]
    merged = None
    for k in range(N_BRANCH):
        term = _sigmoid(g_refs[k][...]) * _dot(h_refs[k][...], wb_ref[0, k])
        merged = term if merged is None else merged + term
    out = x_ref[...] + _rms(_dot(merged.astype(BF16), wo_ref[0]), nw_ref[...])
    _emit(o_ref, n_ref, nwn_ref, slice(None), out)


def _merge(hbuf, u, w_branch, w_out, layer, x, nw, nw_next):
    t, d = x.shape
    bw = hbuf.shape[1] // N_BRANCH
    tm = _tile(t, 208)
    gate_blk = OFF_GATES // d
    once = pl.Buffered(1)
    return pl.pallas_call(
        _merge_kernel,
        grid=(t // tm,),
        in_specs=[pl.BlockSpec((tm, bw), lambda i, k=k: (i, k)) for k in range(N_BRANCH)] + [
            pl.BlockSpec((tm, d), lambda i, k=k: (i, gate_blk + k)) for k in range(N_BRANCH)] + [
            pl.BlockSpec((1,) + w_branch.shape[1:], lambda i: (layer, 0, 0, 0), pipeline_mode=once),
            pl.BlockSpec((1,) + w_out.shape[1:], lambda i: (layer, 0, 0), pipeline_mode=once),
            pl.BlockSpec((tm, d), lambda i: (i, 0)),
            pl.BlockSpec((1, d), lambda i: (0, 0)),
            pl.BlockSpec((1, d), lambda i: (0, 0))],
        out_specs=[pl.BlockSpec((tm, d), lambda i: (i, 0)), pl.BlockSpec((tm, d), lambda i: (i, 0))],
        out_shape=[jax.ShapeDtypeStruct((t, d), F32), jax.ShapeDtypeStruct((t, d), BF16)],
        compiler_params=_cparams(("parallel",), 56),
        name="merge",
    )(*([hbuf] * N_BRANCH), *([u] * N_BRANCH), w_branch, w_out, x, nw, nw_next)


def _proj_res_kernel(a_ref, w_ref, x_ref, nw_ref, nwn_ref, o_ref, n_ref):
    out = x_ref[...] + _rms(_dot(a_ref[...], w_ref[0]), nw_ref[...])
    _emit(o_ref, n_ref, nwn_ref, slice(None), out)


def _proj_res(a, w, layer, x, nw, nw_next):
    t, d = x.shape
    tm = _tile(t, 640)
    row_spec = pl.BlockSpec((tm, d), lambda i: (i, 0))
    vec_spec = pl.BlockSpec((1, d), lambda i: (0, 0))
    return pl.pallas_call(
        _proj_res_kernel,
        grid=(t // tm,),
        in_specs=[row_spec,
                  pl.BlockSpec((1, d, d), lambda i: (layer, 0, 0), pipeline_mode=pl.Buffered(1)),
                  row_spec, vec_spec, vec_spec],
        out_specs=[row_spec, row_spec],
        out_shape=[jax.ShapeDtypeStruct((t, d), F32), jax.ShapeDtypeStruct((t, d), BF16)],
        compiler_params=_cparams(("parallel",), 56),
        name="xa_out",
    )(a, w, x, nw, nw_next)


def _rope(x, cs, sn):
    return x * cs + pltpu.roll(x, x.shape[1] // 2, 1) * sn


def _group_norm(o, eps=1e-5):
    oc = o - jnp.mean(o, axis=-1, keepdims=True)
    return oc * lax.rsqrt(jnp.mean(oc * oc, axis=-1, keepdims=True) + eps)


def _ret_kernel(q_ref, k_ref, v_ref, g_ref, cos_ref, sin_ref, dm_ref, dec_ref, h_ref, st_ref, s_ref, *, nc):
    c = pl.program_id(1)

    @pl.when(c == 0)
    def _():
        s_ref[...] = jnp.zeros_like(s_ref)

    cs, sn = cos_ref[...], sin_ref[...]
    hs = range(N_HEADS)
    kc = [slice(h * DK, (h + 1) * DK) for h in hs]
    vc = [slice(h * DV, (h + 1) * DV) for h in hs]
    s = [s_ref[h] for h in hs]
    dec = [dec_ref[h] for h in hs]
    qb = [_rope(q_ref[:, kc[h]], cs, sn).astype(BF16) for h in hs]
    kr = [_rope(k_ref[:, kc[h]], cs, sn) * (DK ** -0.5) for h in hs]
    vb = [v_ref[:, vc[h]].astype(BF16) for h in hs]
    sc = [_dot_nt(qb[h], kr[h].astype(BF16)) * dm_ref[h] for h in hs]
    qs = [_dot(qb[h], s[h].astype(BF16)) * dec[h][:, 0:1] for h in hs]
    o = [_dot(sc[h].astype(BF16), vb[h]) + qs[h] for h in hs]
    s_new = [dec[h][0:1, 2:3] * s[h] + _dot((kr[h] * dec[h][:, 1:2]).T.astype(BF16), vb[h]) for h in hs]
    out = [(_group_norm(o[h]) * _silu(g_ref[:, vc[h]])).astype(h_ref.dtype) for h in hs]
    for h in hs:
        s_ref[h] = s_new[h]
        h_ref[:, vc[h]] = out[h]

    @pl.when(c == nc - 1)
    def _():
        st_ref[...] = s_ref[...].reshape(st_ref.shape)


def _seg_spec(nc, width, off):
    return pl.BlockSpec((CHUNK, width), lambda b, c: (b * nc + c, off // width))


def _layer_state_spec(layer, tail):
    return pl.BlockSpec((1, 1) + tail, lambda b, c: (layer, b) + (0,) * len(tail))


def _const_spec(shape):
    return pl.BlockSpec(shape, lambda b, c: (0,) * len(shape))


_HW, _KW = N_HEADS * DV, N_HEADS * DK


BR_RET, BR_ML, BR_S5, BR_HG = range(N_BRANCH)


def _ret_prompt(u, hbuf, bsz, seqlen, cos2, sin2, dmat, dec, layer, depth, prev):
    nc = seqlen // CHUNK
    return _pcall(
        functools.partial(_ret_kernel, nc=nc), (u, u, u, u, cos2, sin2, dmat, dec),
        [_seg_spec(nc, _KW, OFF_RQ), _seg_spec(nc, _KW, OFF_RK), _seg_spec(nc, _HW, OFF_RV),
         _seg_spec(nc, _HW, OFF_RG),
         pl.BlockSpec((CHUNK, DK), lambda b, c: (c, 0)),
         pl.BlockSpec((CHUNK, DK), lambda b, c: (c, 0)),
         _const_spec(dmat.shape), _const_spec(dec.shape)],
        carried={0: hbuf, 1: prev},
        grid=(bsz, nc),
        out_specs=[pl.BlockSpec((CHUNK, _HW), lambda b, c: (b * nc + c, BR_RET)),
                   _layer_state_spec(layer, (N_HEADS, DK, DV))],
        out_shape=[jax.ShapeDtypeStruct(hbuf.shape, BF16),
                   jax.ShapeDtypeStruct((depth, bsz, N_HEADS, DK, DV), F32)],
        scratch_shapes=[pltpu.VMEM((N_HEADS, DK, DV), F32)],
        compiler_params=_cparams(("parallel", "arbitrary"), 32),
        name="ret_prompt",
    )


def _mlstm_kernel(gb_ref, q_ref, k_ref, v_ref, mo_ref, ug_ref, ugt_ref, nw_ref,
                  h_ref, c_out, n_out, m_out, cm_ref, nv_ref, m_ref, *, nc):
    c = pl.program_id(1)

    @pl.when(c == 0)
    def _():
        cm_ref[...] = jnp.zeros_like(cm_ref)
        nv_ref[...] = jnp.zeros_like(nv_ref)
        m_ref[...] = jnp.zeros_like(m_ref)

    n = CHUNK
    row = lax.broadcasted_iota(jnp.int32, (n, n), 0)
    col = lax.broadcasted_iota(jnp.int32, (n, n), 1)
    causal = row >= col
    ug = ug_ref[...]
    hs = range(N_HEADS)
    kc = [slice(h * DK, (h + 1) * DK) for h in hs]
    vc = [slice(h * DV, (h + 1) * DV) for h in hs]
    cm = [cm_ref[h] for h in hs]
    nv = [nv_ref[h] for h in hs]
    m = [m_ref[h] for h in hs]
    ig_row = [ugt_ref[h:h + 1, :] + gb_ref[0, h] for h in hs]
    lf_row = [_log_sigmoid(ugt_ref[N_HEADS + h:N_HEADS + h + 1, :] + gb_ref[1, h]) for h in hs]
    ig_col = [ug[:, h:h + 1] + gb_ref[0, h] for h in hs]
    lf_col = [_log_sigmoid(ug[:, N_HEADS + h:N_HEADS + h + 1] + gb_ref[1, h]) for h in hs]
    b_col = [jnp.sum(jnp.where(causal, lf_row[h], 0.0), axis=1, keepdims=True) for h in hs]
    b_row = [jnp.sum(jnp.where(row <= col, lf_col[h], 0.0), axis=0, keepdims=True) for h in hs]
    b_last = [b_col[h][n - 1:n, :] for h in hs]
    logd = [jnp.where(causal, b_col[h] - b_row[h] + ig_row[h], -jnp.inf) for h in hs]
    inter = [m[h] + b_col[h] for h in hs]
    mi = [jnp.maximum(inter[h], jnp.max(logd[h], axis=1, keepdims=True)) for h in hs]
    w = [jnp.exp(logd[h] - mi[h]) for h in hs]
    wi = [jnp.exp(inter[h] - mi[h]) for h in hs]
    q = [q_ref[:, kc[h]] for h in hs]
    k = [k_ref[:, kc[h]] * (DK ** -0.5) for h in hs]
    qb = [q[h].astype(BF16) for h in hs]
    vb = [v_ref[:, vc[h]].astype(BF16) for h in hs]
    a = [_dot_nt(qb[h], k[h].astype(BF16)) * w[h] for h in hs]
    qc = [_dot(qb[h], cm[h].astype(BF16)) for h in hs]
    num = [_dot(a[h].astype(BF16), vb[h]) + wi[h] * qc[h] for h in hs]
    nq = [jnp.sum(a[h], axis=1, keepdims=True) + wi[h] * jnp.sum(q[h] * nv[h], axis=1, keepdims=True) for h in hs]
    hh = [num[h] / jnp.maximum(jnp.abs(nq[h]), jnp.exp(-mi[h])) for h in hs]
    m_new = [mi[h][n - 1:n, :] for h in hs]
    wl = [jnp.exp(b_last[h] - b_col[h] + ig_col[h] - m_new[h]) for h in hs]
    dp = [jnp.exp(m[h] + b_last[h] - m_new[h]) for h in hs]
    kw = [k[h] * wl[h] for h in hs]
    cm_new = [dp[h] * cm[h] + _dot(kw[h].T.astype(BF16), vb[h]) for h in hs]
    nv_new = [dp[h] * nv[h] + jnp.sum(kw[h], axis=0, keepdims=True) for h in hs]
    out = [(_rms(hh[h], nw_ref[:, vc[h]]) * _sigmoid(mo_ref[:, vc[h]])).astype(h_ref.dtype) for h in hs]
    for h in hs:
        cm_ref[h] = cm_new[h]
        nv_ref[h] = nv_new[h]
        m_ref[h] = m_new[h]
        h_ref[:, vc[h]] = out[h]

    @pl.when(c == nc - 1)
    def _():
        c_out[...] = cm_ref[...].reshape(c_out.shape)
        n_out[...] = nv_ref[...].reshape(n_out.shape)
        m_out[...] = m_ref[...].reshape(m_out.shape)


def _mlstm_prompt(u, hbuf, ug, ugt, gate_bias, norm_w, bsz, seqlen, layer, depth, prev):
    nc = seqlen // CHUNK
    return _pcall(
        functools.partial(_mlstm_kernel, nc=nc), (gate_bias, u, u, u, u, ug, ugt, norm_w),
        [pl.BlockSpec(memory_space=pltpu.SMEM),
         _seg_spec(nc, _KW, OFF_MQ), _seg_spec(nc, _KW, OFF_MK), _seg_spec(nc, _HW, OFF_MV),
         _seg_spec(nc, _HW, OFF_MO),
         pl.BlockSpec((CHUNK, GATE_PAD), lambda b, c: (b * nc + c, 0)),
         pl.BlockSpec((2 * N_HEADS, CHUNK), lambda b, c: (0, b * nc + c)),
         _const_spec((1, _HW))],
        carried={0: hbuf, 1: prev[0], 2: prev[1], 3: prev[2]},
        grid=(bsz, nc),
        out_specs=[pl.BlockSpec((CHUNK, _HW), lambda b, c: (b * nc + c, BR_ML)),
                   _layer_state_spec(layer, (N_HEADS, DK, DV)),
                   _layer_state_spec(layer, (N_HEADS, 1, DK)),
                   _layer_state_spec(layer, (N_HEADS, 1, 1))],
        out_shape=[jax.ShapeDtypeStruct(hbuf.shape, BF16),
                   jax.ShapeDtypeStruct((depth, bsz, N_HEADS, DK, DV), F32),
                   jax.ShapeDtypeStruct((depth, bsz, N_HEADS, 1, DK), F32),
                   jax.ShapeDtypeStruct((depth, bsz, N_HEADS, 1, 1), F32)],
        scratch_shapes=[pltpu.VMEM((N_HEADS, DK, DV), F32), pltpu.VMEM((N_HEADS, 1, DK), F32),
                        pltpu.VMEM((N_HEADS, 1, 1), F32)],
        compiler_params=_cparams(("parallel", "arbitrary"), 32),
        name="mlstm_prompt",
    )


def _hgrn_lower_bound(logits, layer):
    e = jnp.exp(logits - jnp.max(logits, axis=0, keepdims=True))
    p = e / jnp.sum(e, axis=0, keepdims=True)
    lb = jnp.zeros_like(p[0:1, :])
    for r in range(1, layer + 1):
        lb = lb + p[r:r + 1, :]
    return lb


def _hgrn_gates(fpre, lb):
    hlf = _logaddexp(jnp.log(lb), jnp.log1p(-lb) + _log_sigmoid(fpre))
    hk = (1.0 - lb) * _sigmoid(-fpre)
    return hlf, hk


def _hgrn_kernel(lg_ref, q_ref, f_ref, i_ref, g_ref, nw_ref, h_ref, st_ref, s_ref, *, nc, layer):
    c = pl.program_id(1)

    @pl.when(c == 0)
    def _():
        s_ref[...] = jnp.zeros_like(s_ref)

    n = CHUNK
    lb_all = _hgrn_lower_bound(lg_ref[...], layer)
    r2 = lax.broadcasted_iota(jnp.int32, (n, n), 0)
    c2 = lax.broadcasted_iota(jnp.int32, (n, n), 1)
    rowv = lax.broadcasted_iota(jnp.int32, (n, DK), 0)
    n_lev = n.bit_length() - 1
    rights = [(rowv & (1 << lev)) != 0 for lev in range(n_lev)]
    pair_masks = [((r2 >> (lev + 1)) == (c2 >> (lev + 1))) & ((r2 & (1 << lev)) != 0) & ((c2 & (1 << lev)) == 0)
                  for lev in range(n_lev)]
    hs = range(N_HEADS)
    kc = [slice(h * DK, (h + 1) * DK) for h in hs]
    vc = [slice(h * DV, (h + 1) * DV) for h in hs]
    st = [s_ref[h] for h in hs]
    gates = [_hgrn_gates(f_ref[:, kc[h]], lb_all[:, kc[h]]) for h in hs]
    hk = [gates[h][1] for h in hs]
    q = [_silu(q_ref[:, kc[h]]) for h in hs]
    ib = [i_ref[:, vc[h]].astype(BF16) for h in hs]
    bc = [_cumsum_rows(gates[h][0]) for h in hs]
    a = [jnp.where(r2 == c2, _dot_nt(q[h].astype(BF16), hk[h].astype(BF16)), 0.0) for h in hs]
    p = list(bc)
    for lev in range(n_lev):
        s = 1 << lev
        right = rights[lev]
        nxt = [pltpu.roll(p[h], n - s, 0) for h in hs]
        e = [jnp.exp(jnp.where(right, bc[h] - p[h], nxt[h] - bc[h])) for h in hs]
        m = [(jnp.where(right, q[h], hk[h]) * e[h]).astype(BF16) for h in hs]
        a = [jnp.where(pair_masks[lev], _dot_nt(m[h], m[h]), a[h]) for h in hs]
        if lev + 1 < n_lev:
            p = [jnp.where(right, pltpu.roll(p[h], s, 0), p[h]) for h in hs]
    qd = [_dot((q[h] * jnp.exp(bc[h])).astype(BF16), st[h].astype(BF16)) for h in hs]
    o = [_dot(a[h].astype(BF16), ib[h]) + qd[h] for h in hs]
    bl = [bc[h][n - 1:n, :] for h in hs]
    kd = [_dot((hk[h] * jnp.exp(bl[h] - bc[h])).T.astype(BF16), ib[h]) for h in hs]
    new_states = [_row_to_col(jnp.exp(bl[h])) * st[h] + kd[h] for h in hs]
    outs = [(_rms(o[h], nw_ref[:, vc[h]]) * _silu(g_ref[:, vc[h]])).astype(h_ref.dtype) for h in hs]
    for h in hs:
        s_ref[h] = new_states[h]
        h_ref[:, vc[h]] = outs[h]

    @pl.when(c == nc - 1)
    def _():
        st_ref[...] = s_ref[...].reshape(st_ref.shape)


def _hgrn_prompt(u, hbuf, logits, norm_w, bsz, seqlen, layer, prev):
    nc = seqlen // CHUNK
    depth = logits.shape[0]
    return _pcall(
        functools.partial(_hgrn_kernel, nc=nc, layer=layer), (logits, u, u, u, u, norm_w),
        [_const_spec(logits.shape),
         _seg_spec(nc, _KW, OFF_HQ), _seg_spec(nc, _KW, OFF_HF), _seg_spec(nc, _HW, OFF_HI),
         _seg_spec(nc, _HW, OFF_HG), _const_spec((1, _HW))],
        carried={0: hbuf, 1: prev},
        grid=(bsz, nc),
        out_specs=[pl.BlockSpec((CHUNK, _HW), lambda b, c: (b * nc + c, BR_HG)),
                   _layer_state_spec(layer, (N_HEADS, DK, DV))],
        out_shape=[jax.ShapeDtypeStruct(hbuf.shape, BF16),
                   jax.ShapeDtypeStruct((depth, bsz, N_HEADS, DK, DV), F32)],
        scratch_shapes=[pltpu.VMEM((N_HEADS, DK, DV), F32)],
        compiler_params=_cparams(("parallel", "arbitrary"), 32),
        name="hgrn_prompt",
    )


def _s5_prep_kernel(lr_ref, li_ref, dt_ref, bre_ref, bim_ref, ar_ref, ai_ref, bbr_ref, bbi_ref):
    lr, li, dt = lr_ref[...], li_ref[...], dt_ref[...]
    mag = jnp.exp(lr * dt)
    ar = mag * jnp.cos(li * dt)
    ai = mag * jnp.sin(li * dt)
    den = lr * lr + li * li
    cr = ((ar - 1.0) * lr + ai * li) / den
    ci = (ai * lr - (ar - 1.0) * li) / den
    ar_ref[...] = ar
    ai_ref[...] = ai
    bbr_ref[...] = cr * bre_ref[...] - ci * bim_ref[...]
    bbi_ref[...] = cr * bim_ref[...] + ci * bre_ref[...]


def _s5_prep(lam_re, lam_im, log_step, b_re, b_im):
    g, n = lam_re.shape
    p = b_re.shape[-1]
    gn = g * n
    dt = jnp.repeat(jnp.exp(log_step.astype(F32)), n).reshape(gn, 1)
    col = lambda a: a.astype(F32).reshape(gn, 1)
    return pl.pallas_call(
        _s5_prep_kernel,
        out_shape=[jax.ShapeDtypeStruct((gn, 1), F32), jax.ShapeDtypeStruct((gn, 1), F32),
                   jax.ShapeDtypeStruct((gn, p), F32), jax.ShapeDtypeStruct((gn, p), F32)],
        name="s5_prep",
    )(col(lam_re), col(lam_im), dt, b_re.reshape(gn, p), b_im.reshape(gn, p))


def _s5_kernel(*refs, seq, nc, n_kt, sw):
    if seq:
        (u_ref, ar_ref, ai_ref, bblk_ref, cblk_ref, d_ref, wglu_ref,
         h_ref, sr_ref, si_ref, hr_scr, hi_scr) = refs
    else:
        (u_ref, ar_ref, ai_ref, bblk_ref, cblk_ref, d_ref, wglu_ref, h0r_ref, h0i_ref,
         h_ref, sr_ref, si_ref) = refs
    u = u_ref[...]
    ub = u.astype(BF16)
    rows = u.shape[0]
    if seq:
        c = pl.program_id(1)

        @pl.when(c == 0)
        def _():
            hr_scr[...] = jnp.zeros_like(hr_scr)
            hi_scr[...] = jnp.zeros_like(hi_scr)

    ys = []
    for kt in range(n_kt):
        lanes = slice(kt * sw, (kt + 1) * sw)
        bu = _dot(ub[:, kt * MXU_K:(kt + 1) * MXU_K], bblk_ref[kt])
        xr, xi = bu[:, :sw], bu[:, sw:]
        ar, ai = ar_ref[:, lanes], ai_ref[:, lanes]
        if seq:
            sub = lax.broadcasted_iota(jnp.int32, (SUBLANES, sw), 0)
            mr, mi = jnp.broadcast_to(ar, (SUBLANES, sw)), jnp.broadcast_to(ai, (SUBLANES, sw))
            tabr, tabi = mr, mi
            levels = []
            sft = 1
            while sft < SUBLANES:
                keep = sub >= sft
                levels.append((sft, jnp.where(keep, mr, 0.0), jnp.where(keep, mi, 0.0)))
                tr, ti = pltpu.roll(tabr, sft, 0), pltpu.roll(tabi, sft, 0)
                tabr, tabi = (jnp.where(keep, tabr * tr - tabi * ti, tabr),
                              jnp.where(keep, tabr * ti + tabi * tr, tabi))
                mr, mi = mr * mr - mi * mi, 2.0 * mr * mi
                sft *= 2
            cr, ci = hr_scr[:, lanes], hi_scr[:, lanes]
            slabs_r, slabs_i = [], []
            for j in range(rows // SUBLANES):
                rs = slice(j * SUBLANES, (j + 1) * SUBLANES)
                sr_, si_ = xr[rs, :], xi[rs, :]
                for sft, lr, li in levels:
                    qr, qi = pltpu.roll(sr_, sft, 0), pltpu.roll(si_, sft, 0)
                    sr_, si_ = sr_ + lr * qr - li * qi, si_ + lr * qi + li * qr
                crb, cib = jnp.broadcast_to(cr, (SUBLANES, sw)), jnp.broadcast_to(ci, (SUBLANES, sw))
                sr_, si_ = sr_ + tabr * crb - tabi * cib, si_ + tabr * cib + tabi * crb
                cr, ci = sr_[SUBLANES - 1:SUBLANES, :], si_[SUBLANES - 1:SUBLANES, :]
                slabs_r.append(sr_)
                slabs_i.append(si_)
            xr, xi = jnp.concatenate(slabs_r, axis=0), jnp.concatenate(slabs_i, axis=0)
            hr_scr[:, lanes] = cr
            hi_scr[:, lanes] = ci
        else:
            h0r, h0i = h0r_ref[0, :, lanes], h0i_ref[0, :, lanes]
            xr, xi = xr + ar * h0r - ai * h0i, xi + ar * h0i + ai * h0r
            sr_ref[0, :, lanes] = xr
            si_ref[0, :, lanes] = xi
        ys.append(_dot(jnp.concatenate([xr, xi], axis=1).astype(BF16), cblk_ref[kt]))
    y = jnp.concatenate(ys, axis=1) + d_ref[...] * u
    z = _gelu_tanh(y)
    h_ref[...] = (z * _sigmoid(_dot(z.astype(BF16), wglu_ref[...]))).astype(h_ref.dtype)
    if seq:
        @pl.when(c == nc - 1)
        def _():
            sr_ref[...] = hr_scr[...].reshape(sr_ref.shape)
            si_ref[...] = hi_scr[...].reshape(si_ref.shape)


def _s5_weights(bbr, bbi, c_re, c_im):
    gn, p = bbr.shape
    g = gn // S5_N
    gpt = MXU_K // p
    n_kt = g // gpt
    eye = jnp.eye(gpt, dtype=F32)

    def b_blk(bb):
        return jnp.einsum('kgnp,gh->kgphn', bb.reshape(n_kt, gpt, S5_N, p), eye).reshape(n_kt, gpt * p, gpt * S5_N)

    def c_blk(cc):
        return jnp.einsum('kgpn,gh->kgnhp', cc.astype(F32).reshape(n_kt, gpt, p, S5_N), eye).reshape(
            n_kt, gpt * S5_N, gpt * p)

    bblk = jnp.concatenate([b_blk(bbr), b_blk(bbi)], axis=2).astype(BF16)
    cblk = jnp.concatenate([c_blk(c_re), -c_blk(c_im)], axis=1).astype(BF16)
    return bblk, cblk


def _s5_prompt(u, hbuf, ar, ai, bblk, cblk, d, wglu, bsz, seqlen, layer, depth, prev):
    width = wglu.shape[0]
    n_kt = bblk.shape[0]
    sw = bblk.shape[2] // 2
    ct = _tile(seqlen, 256)
    nc = seqlen // ct
    gn = ar.shape[1]
    full = lambda shape: pl.BlockSpec(shape, lambda b, c: (0,) * len(shape))
    st_spec = pl.BlockSpec((1, 1, 1, gn), lambda b, c: (layer, b, 0, 0))
    st_shape = jax.ShapeDtypeStruct((depth, bsz, 1, gn), F32)
    return _pcall(
        functools.partial(_s5_kernel, seq=True, nc=nc, n_kt=n_kt, sw=sw), (u, ar, ai, bblk, cblk, d, wglu),
        [pl.BlockSpec((ct, width), lambda b, c: (b * nc + c, OFF_SU // width)),
         full((1, gn)), full((1, gn)), full(bblk.shape), full(cblk.shape),
         full((1, width)), full(wglu.shape)],
        carried={0: hbuf, 1: prev[0], 2: prev[1]},
        grid=(bsz, nc),
        out_specs=[pl.BlockSpec((ct, width), lambda b, c: (b * nc + c, BR_S5)), st_spec, st_spec],
        out_shape=[jax.ShapeDtypeStruct(hbuf.shape, BF16), st_shape, st_shape],
        scratch_shapes=[pltpu.VMEM((1, gn), F32), pltpu.VMEM((1, gn), F32)],
        compiler_params=_cparams(("parallel", "arbitrary"), 56),
        name="s5_prompt",
    )


def _s5_sample(u, tp, ar, ai, bblk, cblk, d, wglu, h0r, h0i, layer, h_buf, prev):
    width = wglu.shape[0]
    n_kt = bblk.shape[0]
    sw = bblk.shape[2] // 2
    depth, bs, gn = h0r.shape
    tb = _tile(bs, 128, 8)
    r0 = tp // tb
    full = lambda shape: pl.BlockSpec(shape, lambda i: (0,) * len(shape))
    st_spec = pl.BlockSpec((1, tb, gn), lambda i: (layer, i, 0))
    st_shape = jax.ShapeDtypeStruct((depth, bs, gn), F32)
    return _pcall(
        functools.partial(_s5_kernel, seq=False, nc=1, n_kt=n_kt, sw=sw),
        (u, ar, ai, bblk, cblk, d, wglu, h0r, h0i),
        [pl.BlockSpec((tb, width), lambda i: (r0 + i, OFF_SU // width)),
         full((1, gn)), full((1, gn)), full(bblk.shape), full(cblk.shape),
         full((1, width)), full(wglu.shape), st_spec, st_spec],
        carried={0: h_buf, 1: prev[0], 2: prev[1]},
        grid=(bs // tb,),
        out_specs=[pl.BlockSpec((tb, width), lambda i: (r0 + i, BR_S5)), st_spec, st_spec],
        out_shape=[jax.ShapeDtypeStruct(h_buf.shape, BF16), st_shape, st_shape],
        compiler_params=_cparams(("parallel",), 56),
        name="s5_sample",
    )


SAMPLE_BLOCK = 8


def _state_step(s, dcol, kcol, vrow, qcol):
    s_new = dcol * s + kcol * vrow
    return s_new, jnp.sum(qcol * s_new, axis=0, keepdims=True)


def _zero_other_layers(ref, slot):
    for l in range(ref.shape[0]):
        if l != slot:
            ref[l] = jnp.zeros(ref.shape[1:], ref.dtype)


def _ret_step_kernel(gam_ref, q_ref, k_ref, v_ref, g_ref, cos_ref, sin_ref, st_ref, h_ref, so_ref, o_scr, *, slot):
    _zero_other_layers(so_ref, slot)
    cs, sn = cos_ref[...], sin_ref[...]
    for h in range(N_HEADS):
        qr = _rope(q_ref[:, h * DK:(h + 1) * DK], cs, sn)
        kr = _rope(k_ref[:, h * DK:(h + 1) * DK], cs, sn) * (DK ** -0.5)
        for i in range(SAMPLE_BLOCK):
            s_new, o = _state_step(st_ref[0, i, h], gam_ref[h], _row_to_col(kr[i:i + 1, :]),
                                   v_ref[i:i + 1, h * DV:(h + 1) * DV], _row_to_col(qr[i:i + 1, :]))
            so_ref[slot, i, h] = s_new
            o_scr[i:i + 1, h * DV:(h + 1) * DV] = o
    for h in range(N_HEADS):
        cols = slice(h * DV, (h + 1) * DV)
        h_ref[:, cols] = (_group_norm(o_scr[:, cols]) * _silu(g_ref[:, cols])).astype(h_ref.dtype)


def _mlstm_step_kernel(gb_ref, q_ref, k_ref, v_ref, mo_ref, ug_ref, nw_ref, c_ref, n_ref, m_ref,
                       h_ref, co_ref, no_ref, mo_out_ref, o_scr, *, slot):
    _zero_other_layers(co_ref, slot)
    ug = ug_ref[...]
    m_all = m_ref[0]
    m_new_cols = []
    for h in range(N_HEADS):
        ig = ug[:, h:h + 1] + gb_ref[0, h]
        lf = _log_sigmoid(ug[:, N_HEADS + h:N_HEADS + h + 1] + gb_ref[1, h])
        m_old = m_all[:, h:h + 1]
        inter = m_old + lf
        mi = jnp.maximum(inter, ig)
        w = jnp.exp(ig - mi)
        wi = jnp.exp(inter - mi)
        m_new_cols.append(mi)
        lim = jnp.exp(-mi)
        qh = q_ref[:, h * DK:(h + 1) * DK]
        kh = k_ref[:, h * DK:(h + 1) * DK] * (DK ** -0.5)
        for i in range(SAMPLE_BLOCK):
            wk = w[i:i + 1, :] * kh[i:i + 1, :]
            c_new, num = _state_step(c_ref[0, i, h], wi[i:i + 1, :], _row_to_col(wk),
                                     v_ref[i:i + 1, h * DV:(h + 1) * DV], _row_to_col(qh[i:i + 1, :]))
            n_new = wi[i:i + 1, :] * n_ref[0, i, h:h + 1, :] + wk
            nq = jnp.sum(qh[i:i + 1, :] * n_new, axis=1, keepdims=True)
            co_ref[slot, i, h] = c_new
            no_ref[0, i, h:h + 1, :] = n_new
            o_scr[i:i + 1, h * DV:(h + 1) * DV] = num / jnp.maximum(jnp.abs(nq), lim[i:i + 1, :])
    mo_out_ref[0] = jnp.concatenate(m_new_cols, axis=1)
    for h in range(N_HEADS):
        cols = slice(h * DV, (h + 1) * DV)
        h_ref[:, cols] = (_rms(o_scr[:, cols], nw_ref[:, cols]) * _sigmoid(mo_ref[:, cols])).astype(h_ref.dtype)


def _hgrn_step_kernel(lg_ref, q_ref, f_ref, i_ref, g_ref, nw_ref, st_ref, h_ref, so_ref, o_scr, *, layer, slot):
    _zero_other_layers(so_ref, slot)
    lb_all = _hgrn_lower_bound(lg_ref[...], layer)
    for h in range(N_HEADS):
        hlf, hk = _hgrn_gates(f_ref[:, h * DK:(h + 1) * DK], lb_all[:, h * DK:(h + 1) * DK])
        dec = jnp.exp(hlf)
        qh = _silu(q_ref[:, h * DK:(h + 1) * DK])
        for i in range(SAMPLE_BLOCK):
            s_new, o = _state_step(st_ref[0, i, h], _row_to_col(dec[i:i + 1, :]), _row_to_col(hk[i:i + 1, :]),
                                   i_ref[i:i + 1, h * DV:(h + 1) * DV], _row_to_col(qh[i:i + 1, :]))
            so_ref[slot, i, h] = s_new
            o_scr[i:i + 1, h * DV:(h + 1) * DV] = o
    for h in range(N_HEADS):
        cols = slice(h * DV, (h + 1) * DV)
        h_ref[:, cols] = (_rms(o_scr[:, cols], nw_ref[:, cols]) * _silu(g_ref[:, cols])).astype(h_ref.dtype)


def _seg(width, off, tp):
    return pl.BlockSpec((SAMPLE_BLOCK, width), lambda i: (tp // SAMPLE_BLOCK + i, off // width))


def _state_spec(layer, tail):
    return pl.BlockSpec((1, SAMPLE_BLOCK) + tail, lambda i: (layer, i) + (0,) * len(tail))


_MAT = (N_HEADS, DK, DV)
_SMEM = pl.BlockSpec(memory_space=pltpu.SMEM)


def _whole(shape):
    return pl.BlockSpec(shape, lambda i: (0,) * len(shape))


def _h_rows_spec(tp, branch):
    return pl.BlockSpec((SAMPLE_BLOCK, _HW), lambda i: (tp // SAMPLE_BLOCK + i, branch))


def _new_mat_state(layer, depth, prev):
    if prev is None:
        return pl.BlockSpec((depth, SAMPLE_BLOCK) + _MAT, lambda i: (0, i, 0, 0, 0)), layer
    return _state_spec(layer, _MAT), 0


def _ret_sample(u, tp, gamma, cos2, sin2, state, layer, h_buf, prev):
    depth, bs = state.shape[:2]
    spec, slot = _new_mat_state(layer, depth, prev)
    return _pcall(
        functools.partial(_ret_step_kernel, slot=slot), (gamma, u, u, u, u, cos2, sin2, state),
        [_SMEM, _seg(_KW, OFF_RQ, tp), _seg(_KW, OFF_RK, tp), _seg(_HW, OFF_RV, tp), _seg(_HW, OFF_RG, tp),
         _whole((1, DK)), _whole((1, DK)), _state_spec(layer, _MAT)],
        carried={0: h_buf, 1: prev},
        grid=(bs // SAMPLE_BLOCK,),
        out_specs=[_h_rows_spec(tp, BR_RET), spec],
        out_shape=[jax.ShapeDtypeStruct(h_buf.shape, BF16), jax.ShapeDtypeStruct(state.shape, F32)],
        scratch_shapes=[pltpu.VMEM((SAMPLE_BLOCK, _HW), F32)],
        compiler_params=_cparams(("parallel",), 48),
        name="ret_sample",
    )


def _mlstm_sample(u, ug, tp, gate_bias, norm_w, st_c, st_n, st_m, layer, h_buf, prev):
    depth, bs = st_c.shape[:2]
    spec, slot = _new_mat_state(layer, depth, prev[0])
    return _pcall(
        functools.partial(_mlstm_step_kernel, slot=slot), (gate_bias, u, u, u, u, ug, norm_w, st_c, st_n, st_m),
        [_SMEM, _seg(_KW, OFF_MQ, tp), _seg(_KW, OFF_MK, tp), _seg(_HW, OFF_MV, tp), _seg(_HW, OFF_MO, tp),
         pl.BlockSpec((SAMPLE_BLOCK, GATE_PAD), lambda i: (tp // SAMPLE_BLOCK + i, 0)), _whole((1, _HW)),
         _state_spec(layer, _MAT), _state_spec(layer, (N_HEADS, DK)), _state_spec(layer, (N_HEADS,))],
        carried={0: h_buf, 1: prev[0], 2: prev[1], 3: prev[2]},
        grid=(bs // SAMPLE_BLOCK,),
        out_specs=[_h_rows_spec(tp, BR_ML), spec,
                   _state_spec(layer, (N_HEADS, DK)), _state_spec(layer, (N_HEADS,))],
        out_shape=[jax.ShapeDtypeStruct(h_buf.shape, BF16), jax.ShapeDtypeStruct(st_c.shape, F32),
                   jax.ShapeDtypeStruct(st_n.shape, F32), jax.ShapeDtypeStruct(st_m.shape, F32)],
        scratch_shapes=[pltpu.VMEM((SAMPLE_BLOCK, _HW), F32)],
        compiler_params=_cparams(("parallel",), 48),
        name="mlstm_sample",
    )


def _hgrn_sample(u, tp, logits, norm_w, state, layer, h_buf, prev):
    depth, bs = state.shape[:2]
    spec, slot = _new_mat_state(layer, depth, prev)
    return _pcall(
        functools.partial(_hgrn_step_kernel, layer=layer, slot=slot), (logits, u, u, u, u, norm_w, state),
        [_whole(logits.shape), _seg(_KW, OFF_HQ, tp), _seg(_KW, OFF_HF, tp), _seg(_HW, OFF_HI, tp),
         _seg(_HW, OFF_HG, tp), _whole((1, _HW)), _state_spec(layer, _MAT)],
        carried={0: h_buf, 1: prev},
        grid=(bs // SAMPLE_BLOCK,),
        out_specs=[_h_rows_spec(tp, BR_HG), spec],
        out_shape=[jax.ShapeDtypeStruct(h_buf.shape, BF16), jax.ShapeDtypeStruct(state.shape, F32)],
        scratch_shapes=[pltpu.VMEM((SAMPLE_BLOCK, _HW), F32)],
        compiler_params=_cparams(("parallel",), 48),
        name="hgrn_sample",
    )


def _attn_prompt_kernel(q_ref, k_ref, v_ref, o_ref, *, scale):
    s = _dot_nt(q_ref[...].astype(BF16), k_ref[0, 0].astype(BF16)) * scale
    e = jnp.exp(s - jnp.max(s, axis=1, keepdims=True))
    p = e / jnp.sum(e, axis=1, keepdims=True)
    o_ref[...] = _dot(p.astype(BF16), v_ref[0, 0].astype(BF16)).astype(o_ref.dtype)


def _attn_prompt(q, kvbuf, obuf, bsz, seqlen, layer):
    t, d = q.shape
    dh = d // N_HEADS
    mem_len = kvbuf.shape[2] // bsz
    tq = _tile(seqlen, 2048)
    nq = seqlen // tq
    return _pcall(
        functools.partial(_attn_prompt_kernel, scale=dh ** -0.5), (q, kvbuf, kvbuf),
        [pl.BlockSpec((tq, dh), lambda b, h, i: (b * nq + i, h)),
         pl.BlockSpec((1, 1, mem_len, dh), lambda b, h, i: (layer, 0, b, h)),
         pl.BlockSpec((1, 1, mem_len, dh), lambda b, h, i: (layer, 1, b, h))],
        carried={0: obuf},
        grid=(bsz, N_HEADS, nq),
        out_specs=pl.BlockSpec((tq, dh), lambda b, h, i: (b * nq + i, h)),
        out_shape=jax.ShapeDtypeStruct((t, d), BF16),
        compiler_params=_cparams(("parallel", "parallel", "arbitrary"), 32),
        name="attn_prompt",
    )


def _attn_sample_kernel(q_ref, k_ref, v_ref, o_ref, *, scale):
    for h in range(q_ref.shape[1]):
        k = k_ref[0, 0, :, h, :]
        s = jnp.sum(k * q_ref[0, h:h + 1, :], axis=1, keepdims=True) * scale
        e = jnp.exp(s - jnp.max(s, axis=0, keepdims=True))
        p = e / jnp.sum(e, axis=0, keepdims=True)
        o_ref[0, h:h + 1, :] = jnp.sum(p * v_ref[0, 0, :, h, :], axis=0, keepdims=True)


def _attn_sample(q4, cache_k, cache_v, layer):
    bs, nh, dh = q4.shape
    mem_len = cache_k.shape[2]
    kv_spec = pl.BlockSpec((1, 1, mem_len, nh, dh), lambda i: (layer, i, 0, 0, 0))
    return pl.pallas_call(
        functools.partial(_attn_sample_kernel, scale=dh ** -0.5),
        grid=(bs,),
        in_specs=[pl.BlockSpec((1, nh, dh), lambda i: (i, 0, 0)), kv_spec, kv_spec],
        out_specs=pl.BlockSpec((1, nh, dh), lambda i: (i, 0, 0)),
        out_shape=jax.ShapeDtypeStruct((bs, nh, dh), F32),
        compiler_params=_cparams(("parallel",), 48),
        name="attn_sample",
    )(q4, cache_k, cache_v)


def _rope_tables(pos):
    half = DK // 2
    inv = ROPE_BASE ** (-jnp.arange(half, dtype=F32) / half)
    ang = pos.astype(F32)[:, None] * inv[None, :]
    cos, sin = jnp.cos(ang), jnp.sin(ang)
    return jnp.concatenate([cos, cos], axis=1), jnp.concatenate([-sin, sin], axis=1)


def _retention_tables(c):
    lg = jnp.log1p(-jnp.exp2(-5.0 - jnp.arange(N_HEADS, dtype=F32)))
    idx = jnp.arange(c, dtype=F32)
    diff = idx[:, None] - idx[None, :]
    dmat = jnp.where(diff[None] >= 0, jnp.exp(jnp.maximum(diff, 0.0)[None] * lg[:, None, None]), 0.0)
    e1 = jnp.exp((idx[None, :] + 1.0) * lg[:, None])
    e2 = jnp.exp((c - 1.0 - idx)[None, :] * lg[:, None])
    e3 = jnp.broadcast_to(jnp.exp(c * lg)[:, None], (N_HEADS, c))
    dec = jnp.stack([e1, e2, e3] + [jnp.zeros_like(e1)] * 5, axis=-1)
    return dmat, dec


def kernel(x_prompt, x_sample, mem_prompt, state_ret, state_mlstm_c, state_mlstm_n, state_mlstm_m, state_s5_re, state_s5_im, state_hgrn, cache_mem_k, cache_mem_v, norm_w, ffn_w_in, ffn_w_out, w_in, ml_gate_bias, ml_norm_w, s5_lambda_re, s5_lambda_im, s5_log_step, s5_b_re, s5_b_im, s5_c_re, s5_c_im, s5_d, s5_w_glu, hg_lb_logits, hg_norm_w, w_branch, w_out, xa_mem_norm, xa_wq, xa_wkv, xa_wo):
    bsz, seqlen, d = x_prompt.shape
    bs = x_sample.shape[0]
    depth = norm_w.shape[0]
    mem_len = mem_prompt.shape[1]
    tp = bsz * seqlen
    gn = s5_lambda_re.shape[1] * s5_lambda_re.shape[2]
    seg_gate = OFF_MO

    x = jnp.concatenate([x_prompt.reshape(tp, d), x_sample.reshape(bs, d)], axis=0)
    mem = mem_prompt.reshape(bsz * mem_len, d)
    s5_in_re = state_s5_re.reshape(depth, bs, gn)
    s5_in_im = state_s5_im.reshape(depth, bs, gn)

    cos_p, sin_p = _rope_tables(jnp.arange(seqlen, dtype=jnp.int32))
    cos_s, sin_s = _rope_tables(PAST_LEN + jnp.arange(1, dtype=jnp.int32))
    dmat, dec = _retention_tables(CHUNK)
    _, dec1 = _retention_tables(1)
    gamma = dec1[:, 0, 0]

    zf = lambda *shape: jnp.zeros(shape, F32)
    t = tp + bs
    p_ret = zf(depth, bsz, N_HEADS, DK, DV)
    p_hg = zf(depth, bsz, N_HEADS, DK, DV)
    p_ml = [zf(depth, bsz, N_HEADS, DK, DV), zf(depth, bsz, N_HEADS, 1, DK), zf(depth, bsz, N_HEADS, 1, 1)]
    p_s5 = [zf(depth, bsz, 1, gn), zf(depth, bsz, 1, gn)]
    kvbuf = zf(depth, 2, bsz * mem_len, d)
    s_ret = s_hg = None
    s_ml = [None, zf(depth, bs, N_HEADS, DK), zf(depth, bs, N_HEADS)]
    s_s5 = [zf(depth, bs, gn), zf(depth, bs, gn)]
    hbuf = jnp.zeros((t, N_BRANCH * _HW), BF16)
    o = jnp.zeros((t, d), BF16)
    ffn_w_out_bf = ffn_w_out.astype(BF16)
    w_in_t = jnp.swapaxes(w_in, 1, 2)
    w_branch_bf, w_out_bf = w_branch.astype(BF16), w_out.astype(BF16)
    xa_wq_bf, xa_wkv_bf, xa_wo_bf = xa_wq.astype(BF16), xa_wkv.astype(BF16), xa_wo.astype(BF16)
    xn = _norm_cast(x, norm_w[0, 0:1])
    for l in range(depth):
        nw = norm_w[l]
        x, xn = _ffn(x, xn, nw[1:2], nw[2:3], ffn_w_in, l, 0, ffn_w_out_bf)
        n_gate = 2 * N_HEADS
        u, ug = _in_proj(xn, w_in_t, l, seg_gate, n_gate)
        ugt = ug[:tp, :n_gate].T
        mlw = ml_norm_w[l].reshape(1, -1)
        hgw = hg_norm_w[l].reshape(1, -1)
        ar, ai, bbr, bbi = _s5_prep(s5_lambda_re[l], s5_lambda_im[l], s5_log_step[l], s5_b_re[l], s5_b_im[l])
        ar, ai = ar.reshape(1, gn), ai.reshape(1, gn)
        bblk, cblk = _s5_weights(bbr, bbi, s5_c_re[l], s5_c_im[l])
        s5d = s5_d[l].astype(F32).reshape(1, -1)
        wglu = s5_w_glu[l].astype(BF16)
        hbuf, p_ret = _ret_prompt(u, hbuf, bsz, seqlen, cos_p, sin_p, dmat, dec, l, depth, p_ret)
        hbuf, s_ret = _ret_sample(u, tp, gamma, cos_s, sin_s, state_ret, l, hbuf, s_ret)
        hbuf, *p_ml = _mlstm_prompt(u, hbuf, ug, ugt, ml_gate_bias[l], mlw, bsz, seqlen, l, depth, p_ml)
        hbuf, *s_ml = _mlstm_sample(u, ug, tp, ml_gate_bias[l], mlw, state_mlstm_c, state_mlstm_n,
                                    state_mlstm_m, l, hbuf, s_ml)
        hbuf, *p_s5 = _s5_prompt(u, hbuf, ar, ai, bblk, cblk, s5d, wglu, bsz, seqlen, l, depth, p_s5)
        hbuf, *s_s5 = _s5_sample(u, tp, ar, ai, bblk, cblk, s5d, wglu, s5_in_re, s5_in_im, l, hbuf, s_s5)
        hbuf, p_hg = _hgrn_prompt(u, hbuf, hg_lb_logits, hgw, bsz, seqlen, l, p_hg)
        hbuf, s_hg = _hgrn_sample(u, tp, hg_lb_logits, hgw, state_hgrn, l, hbuf, s_hg)
        x, xn = _merge(hbuf, u, w_branch_bf, w_out_bf, l, x, nw[3:4], nw[4:5])
        kvbuf = _mem_kv(mem, xa_mem_norm[l].reshape(1, d), xa_wkv_bf, l, depth, kvbuf)
        q = _matmul(xn, xa_wq_bf, l, "xa_q")
        o = _attn_prompt(q, kvbuf, o, bsz, seqlen, l)
        o_s = _attn_sample(q[tp:].reshape(bs, N_HEADS, d // N_HEADS), cache_mem_k, cache_mem_v, l)
        o = lax.dynamic_update_slice(o, o_s.reshape(bs, d).astype(BF16), (tp, 0))
        x, xn = _proj_res(o, xa_wo_bf, l, x, nw[5:6], nw[6:7])
        x, xn = _ffn(x, xn, nw[7:8], norm_w[(l + 1) % depth, 0:1], ffn_w_in, l, 1, ffn_w_out_bf)

    mem_shape = (depth, bsz, mem_len, N_HEADS, d // N_HEADS)
    s5_shape = (depth, -1, gn // S5_N, S5_N)
    return (x[:tp].reshape(bsz, seqlen, d), x[tp:].reshape(bs, 1, d),
            p_ret, p_ml[0], p_ml[1].reshape(depth, bsz, N_HEADS, DK), p_ml[2].reshape(depth, bsz, N_HEADS),
            p_s5[0].reshape(s5_shape), p_s5[1].reshape(s5_shape), p_hg,
            kvbuf[:, 0].reshape(mem_shape), kvbuf[:, 1].reshape(mem_shape),
            s_ret, s_ml[0], s_ml[1], s_ml[2], s_s5[0].reshape(s5_shape), s_s5[1].reshape(s5_shape), s_hg)
```

```python
import functools
import math

import jax
import jax.numpy as jnp
from jax import lax
from jax.experimental import pallas as pl
from jax.experimental.pallas import tpu as pltpu

F32 = jnp.float32
BF16 = jnp.bfloat16

CHUNK = 128
N_BRANCH = 4
N_HEADS = 4
DK = 128
DV = 256
S5_GROUP = 16
S5_N = 64
ROPE_BASE = 10000.0
PAST_LEN = 16384

V7X_VMEM_BYTES = 64 * 1024 * 1024
MXU_K = 256
SUBLANES = 8
MIB = 1024 * 1024

OFF_RQ, OFF_RK, OFF_RV, OFF_RG = 0, 512, 1024, 2048
OFF_MQ, OFF_MK, OFF_MV, OFF_MO = 3072, 3584, 4096, 5120
OFF_SU = 6144
OFF_HQ, OFF_HF, OFF_HI, OFF_HG = 7168, 7680, 8192, 9216
OFF_GATES = 10240
N_MAIN = 18432
GATE_PAD = 128


def _cparams(sem, vmem_mib):
    return pltpu.CompilerParams(dimension_semantics=sem, vmem_limit_bytes=vmem_mib * MIB)


_ANY = pl.BlockSpec(memory_space=pl.ANY)


def _pcall(kernel_fn, inputs, in_specs, carried=None, **kw):
    carried = {o: a for o, a in (carried or {}).items() if a is not None}
    idxs = sorted(carried)
    n_in = len(inputs)

    def body(*refs):
        return kernel_fn(*refs[:n_in], *refs[n_in + len(idxs):])

    return pl.pallas_call(
        body if idxs else kernel_fn,
        in_specs=list(in_specs) + [_ANY] * len(idxs),
        input_output_aliases={n_in + j: o for j, o in enumerate(idxs)},
        **kw)(*inputs, *[carried[o] for o in idxs])


def _tile(n, cap, mult=16):
    best = None
    for t in range(mult, min(n, cap) + 1, mult):
        if n % t == 0:
            best = t
    return best if best is not None else n


def _dot(a, b):
    return jnp.dot(a, b, preferred_element_type=F32)


def _dot_nt(a, b):
    return lax.dot_general(a, b, (((1,), (1,)), ((), ())), preferred_element_type=F32)


def _rms(x, w, eps=1e-6):
    return x * lax.rsqrt(jnp.mean(x * x, axis=-1, keepdims=True) + eps) * w


def _sigmoid(x):
    return 1.0 / (1.0 + jnp.exp(-x))


def _silu(x):
    return x * _sigmoid(x)


def _log_sigmoid(x):
    return jnp.minimum(x, 0.0) - jnp.log1p(jnp.exp(-jnp.abs(x)))


def _logaddexp(a, b):
    return jnp.maximum(a, b) + jnp.log1p(jnp.exp(-jnp.abs(a - b)))


def _gelu_tanh(x):
    return 0.5 * x * (1.0 + jnp.tanh(math.sqrt(2.0 / math.pi) * (x + 0.044715 * (x * x * x))))


def _row_to_col(row):
    n = row.shape[1]
    r = lax.broadcasted_iota(jnp.int32, (n, n), 0)
    c = lax.broadcasted_iota(jnp.int32, (n, n), 1)
    return jnp.sum(jnp.where(r == c, row, 0.0), axis=1, keepdims=True)


def _cumsum_rows(x):
    c = x.shape[0]
    r = lax.broadcasted_iota(jnp.int32, (c, c), 0)
    k = lax.broadcasted_iota(jnp.int32, (c, c), 1)
    tri = jnp.where(r >= k, 1.0, 0.0).astype(BF16)
    hi = x.astype(BF16)
    r1 = x - hi.astype(F32)
    mid = r1.astype(BF16)
    lo = (r1 - mid.astype(F32)).astype(BF16)
    return _dot(tri, hi) + _dot(tri, mid) + _dot(tri, lo)


def _emit(o_ref, n_ref, nwn_ref, rows, out):
    o_ref[rows, :] = out
    n_ref[rows, :] = _rms(out, nwn_ref[...]).astype(n_ref.dtype)


def _norm_cast_kernel(x_ref, nw_ref, n_ref):
    n_ref[...] = _rms(x_ref[...], nw_ref[...]).astype(n_ref.dtype)


def _norm_cast(x, nw):
    t, d = x.shape
    tm = _tile(t, 1040)
    return pl.pallas_call(
        _norm_cast_kernel,
        grid=(t // tm,),
        in_specs=[pl.BlockSpec((tm, d), lambda i: (i, 0)), pl.BlockSpec((1, d), lambda i: (0, 0))],
        out_specs=pl.BlockSpec((tm, d), lambda i: (i, 0)),
        out_shape=jax.ShapeDtypeStruct((t, d), BF16),
        compiler_params=_cparams(("parallel",), 48),
        name="norm_cast",
    )(x, nw)


def _ffn_up_kernel(xn_ref, wg_ref, wu_ref, h_ref, *, rc):
    wg = wg_ref[0, 0].astype(BF16)
    wu = wu_ref[0, 0].astype(BF16)
    for r in range(0, xn_ref.shape[0], rc):
        xn = xn_ref[r:r + rc, :]
        h_ref[r:r + rc, :] = (_silu(_dot(xn, wg)) * _dot(xn, wu)).astype(h_ref.dtype)


def _ffn_down_kernel(h_ref, w_ref, x_ref, nw_ref, nwn_ref, o_ref, n_ref, *, rc):
    for r in range(0, x_ref.shape[0], rc):
        rows = slice(r, r + rc)
        out = x_ref[rows, :] + 0.5 * _rms(_dot(h_ref[rows, :], w_ref[0, 0]), nw_ref[...])
        _emit(o_ref, n_ref, nwn_ref, rows, out)


def _ffn(x, xn, nw_post, nw_next, w_in, layer, which, w_out):
    t, d = x.shape
    dff = w_out.shape[2]
    tm = _tile(t, 2080)
    rc = _tile(tm, 260)
    tf = _tile(dff, 512, 128)
    nf = dff // tf
    hidden = pl.pallas_call(
        functools.partial(_ffn_up_kernel, rc=rc),
        grid=(t // tm, nf),
        in_specs=[pl.BlockSpec((tm, d), lambda i, f: (i, 0)),
                  pl.BlockSpec((1, 1, d, tf), lambda i, f: (layer, which, 0, f)),
                  pl.BlockSpec((1, 1, d, tf), lambda i, f: (layer, which, 0, nf + f))],
        out_specs=pl.BlockSpec((tm, tf), lambda i, f: (i, f)),
        out_shape=jax.ShapeDtypeStruct((t, dff), BF16),
        compiler_params=_cparams(("parallel", "arbitrary"), 56),
        name="ffn_up",
    )(xn, w_in, w_in)
    tm2 = _tile(t, 416)
    row_spec = pl.BlockSpec((tm2, d), lambda i: (i, 0))
    vec_spec = pl.BlockSpec((1, d), lambda i: (0, 0))
    return pl.pallas_call(
        functools.partial(_ffn_down_kernel, rc=_tile(tm2, 208)),
        grid=(t // tm2,),
        in_specs=[pl.BlockSpec((tm2, dff), lambda i: (i, 0)),
                  pl.BlockSpec((1, 1, dff, d), lambda i: (layer, which, 0, 0), pipeline_mode=pl.Buffered(1)),
                  row_spec, vec_spec, vec_spec],
        out_specs=[row_spec, row_spec],
        out_shape=[jax.ShapeDtypeStruct((t, d), F32), jax.ShapeDtypeStruct((t, d), BF16)],
        compiler_params=_cparams(("parallel",), 56),
        name="ffn_down",
    )(hidden, w_out, x, nw_post, nw_next)


def _matmul_kernel(xn_ref, w_ref, o_ref):
    o_ref[...] = _dot(xn_ref[...], w_ref[0]).astype(o_ref.dtype)


def _matmul(xn, w, layer, name):
    t, d = xn.shape
    n = w.shape[2]
    tm = _tile(t, 1040)
    tn = _tile(n, 1024, 128)
    return pl.pallas_call(
        _matmul_kernel,
        grid=(t // tm, n // tn),
        in_specs=[pl.BlockSpec((tm, d), lambda i, j: (i, 0)),
                  pl.BlockSpec((1, d, tn), lambda i, j: (layer, 0, j))],
        out_specs=pl.BlockSpec((tm, tn), lambda i, j: (i, j)),
        out_shape=jax.ShapeDtypeStruct((t, n), F32),
        compiler_params=_cparams(("parallel", "arbitrary"), 48),
        name=name,
    )(xn, w)


def _mem_kv_kernel(x_ref, nw_ref, w_ref, o_ref, xn_ref):
    @pl.when(pl.program_id(1) == 0)
    def _():
        xn_ref[...] = _rms(x_ref[...], nw_ref[...]).astype(BF16)

    o_ref[0, 0] = _dot(xn_ref[...], w_ref[0])


def _mem_kv(mem, nw, w, layer, depth, prev):
    t, d = mem.shape
    tm = _tile(t, 512)
    tn = _tile(d, 1024, 128)
    npk = d // tn
    return _pcall(
        _mem_kv_kernel, (mem, nw, w),
        [pl.BlockSpec((tm, d), lambda i, j: (i, 0)),
         pl.BlockSpec((1, d), lambda i, j: (0, 0)),
         pl.BlockSpec((1, d, tn), lambda i, j: (layer, 0, j))],
        carried={0: prev},
        grid=(t // tm, 2 * npk),
        out_specs=pl.BlockSpec((1, 1, tm, tn), lambda i, j: (layer, j // npk, i, j % npk)),
        out_shape=jax.ShapeDtypeStruct((depth, 2, t, d), F32),
        scratch_shapes=[pltpu.VMEM((tm, d), BF16)],
        compiler_params=_cparams(("parallel", "arbitrary"), 48),
        name="mem_kv",
    )


def _in_proj_kernel(xn_ref, wt_ref, wgt_ref, o_ref, og_ref):
    @pl.when(pl.program_id(1) == 0)
    def _():
        og_ref[...] = _dot_nt(xn_ref[...], wgt_ref[0].astype(BF16))

    o_ref[...] = _dot_nt(xn_ref[...], wt_ref[0].astype(BF16))


def _in_proj(xn, w_in_t, layer, gate_off, n_gate):
    t, d = xn.shape
    n = w_in_t.shape[1] - n_gate
    tm = _tile(t, 2080)
    tn = _tile(math.gcd(n, gate_off), 512, 128)
    n_plain = gate_off // tn
    return pl.pallas_call(
        _in_proj_kernel,
        grid=(t // tm, n // tn),
        in_specs=[pl.BlockSpec((tm, d), lambda i, j: (i, 0)),
                  pl.BlockSpec((pl.Element(1), pl.Element(tn), pl.Element(d)),
                               lambda i, j: (layer, SUBLANES * (j * (tn // SUBLANES) + jnp.where(
                                   j < n_plain, 0, n_gate // SUBLANES)), 0)),
                  pl.BlockSpec((1, GATE_PAD, d), lambda i, j: (layer, gate_off // GATE_PAD, 0))],
        out_specs=[pl.BlockSpec((tm, tn), lambda i, j: (i, j)),
                   pl.BlockSpec((tm, GATE_PAD), lambda i, j: (i, 0))],
        out_shape=[jax.ShapeDtypeStruct((t, n), F32), jax.ShapeDtypeStruct((t, GATE_PAD), F32)],
        compiler_params=_cparams(("parallel", "arbitrary"), 56),
        name="in_proj",
    )(xn, w_in_t, w_in_t)


def _merge_kernel(*refs):
    h_refs, g_refs = refs[:N_BRANCH], refs[N_BRANCH:2 * N_BRANCH]
    wb_ref, wo_ref, x_ref, nw_ref, nwn_ref, o_ref, n_ref = refs[2 * N_BRANCH:]
    merged = None
    for k in range(N_BRANCH):
        term = _sigmoid(g_refs[k][...]) * _dot(h_refs[k][...], wb_ref[0, k])
        merged = term if merged is None else merged + term
    out = x_ref[...] + _rms(_dot(merged.astype(BF16), wo_ref[0]), nw_ref[...])
    _emit(o_ref, n_ref, nwn_ref, slice(None), out)


def _merge(hbuf, u, w_branch, w_out, layer, x, nw, nw_next):
    t, d = x.shape
    bw = hbuf.shape[1] // N_BRANCH
    tm = _tile(t, 208)
    gate_blk = OFF_GATES // d
    once = pl.Buffered(1)
    return pl.pallas_call(
        _merge_kernel,
        grid=(t // tm,),
        in_specs=[pl.BlockSpec((tm, bw), lambda i, k=k: (i, k)) for k in range(N_BRANCH)] + [
            pl.BlockSpec((tm, d), lambda i, k=k: (i, gate_blk + k)) for k in range(N_BRANCH)] + [
            pl.BlockSpec((1,) + w_branch.shape[1:], lambda i: (layer, 0, 0, 0), pipeline_mode=once),
            pl.BlockSpec((1,) + w_out.shape[1:], lambda i: (layer, 0, 0), pipeline_mode=once),
            pl.BlockSpec((tm, d), lambda i: (i, 0)),
            pl.BlockSpec((1, d), lambda i: (0, 0)),
            pl.BlockSpec((1, d), lambda i: (0, 0))],
        out_specs=[pl.BlockSpec((tm, d), lambda i: (i, 0)), pl.BlockSpec((tm, d), lambda i: (i, 0))],
        out_shape=[jax.ShapeDtypeStruct((t, d), F32), jax.ShapeDtypeStruct((t, d), BF16)],
        compiler_params=_cparams(("parallel",), 56),
        name="merge",
    )(*([hbuf] * N_BRANCH), *([u] * N_BRANCH), w_branch, w_out, x, nw, nw_next)


def _proj_res_kernel(a_ref, w_ref, x_ref, nw_ref, nwn_ref, o_ref, n_ref):
    out = x_ref[...] + _rms(_dot(a_ref[...], w_ref[0]), nw_ref[...])
    _emit(o_ref, n_ref, nwn_ref, slice(None), out)


def _proj_res(a, w, layer, x, nw, nw_next):
    t, d = x.shape
    tm = _tile(t, 640)
    row_spec = pl.BlockSpec((tm, d), lambda i: (i, 0))
    vec_spec = pl.BlockSpec((1, d), lambda i: (0, 0))
    return pl.pallas_call(
        _proj_res_kernel,
        grid=(t // tm,),
        in_specs=[row_spec,
                  pl.BlockSpec((1, d, d), lambda i: (layer, 0, 0), pipeline_mode=pl.Buffered(1)),
                  row_spec, vec_spec, vec_spec],
        out_specs=[row_spec, row_spec],
        out_shape=[jax.ShapeDtypeStruct((t, d), F32), jax.ShapeDtypeStruct((t, d), BF16)],
        compiler_params=_cparams(("parallel",), 56),
        name="xa_out",
    )(a, w, x, nw, nw_next)


def _rope(x, cs, sn):
    return x * cs + pltpu.roll(x, x.shape[1] // 2, 1) * sn


def _group_norm(o, eps=1e-5):
    oc = o - jnp.mean(o, axis=-1, keepdims=True)
    return oc * lax.rsqrt(jnp.mean(oc * oc, axis=-1, keepdims=True) + eps)


def _ret_kernel(q_ref, k_ref, v_ref, g_ref, cos_ref, sin_ref, dm_ref, dec_ref, h_ref, st_ref, s_ref, *, nc):
    c = pl.program_id(1)

    @pl.when(c == 0)
    def _():
        s_ref[...] = jnp.zeros_like(s_ref)

    cs, sn = cos_ref[...], sin_ref[...]
    hs = range(N_HEADS)
    kc = [slice(h * DK, (h + 1) * DK) for h in hs]
    vc = [slice(h * DV, (h + 1) * DV) for h in hs]
    s = [s_ref[h] for h in hs]
    dec = [dec_ref[h] for h in hs]
    qb = [_rope(q_ref[:, kc[h]], cs, sn).astype(BF16) for h in hs]
    kr = [_rope(k_ref[:, kc[h]], cs, sn) * (DK ** -0.5) for h in hs]
    vb = [v_ref[:, vc[h]].astype(BF16) for h in hs]
    sc = [_dot_nt(qb[h], kr[h].astype(BF16)) * dm_ref[h] for h in hs]
    qs = [_dot(qb[h], s[h].astype(BF16)) * dec[h][:, 0:1] for h in hs]
    o = [_dot(sc[h].astype(BF16), vb[h]) + qs[h] for h in hs]
    s_new = [dec[h][0:1, 2:3] * s[h] + _dot((kr[h] * dec[h][:, 1:2]).T.astype(BF16), vb[h]) for h in hs]
    out = [(_group_norm(o[h]) * _silu(g_ref[:, vc[h]])).astype(h_ref.dtype) for h in hs]
    for h in hs:
        s_ref[h] = s_new[h]
        h_ref[:, vc[h]] = out[h]

    @pl.when(c == nc - 1)
    def _():
        st_ref[...] = s_ref[...].reshape(st_ref.shape)


def _seg_spec(nc, width, off):
    return pl.BlockSpec((CHUNK, width), lambda b, c: (b * nc + c, off // width))


def _layer_state_spec(layer, tail):
    return pl.BlockSpec((1, 1) + tail, lambda b, c: (layer, b) + (0,) * len(tail))


def _const_spec(shape):
    return pl.BlockSpec(shape, lambda b, c: (0,) * len(shape))


_HW, _KW = N_HEADS * DV, N_HEADS * DK


BR_RET, BR_ML, BR_S5, BR_HG = range(N_BRANCH)


def _ret_prompt(u, hbuf, bsz, seqlen, cos2, sin2, dmat, dec, layer, depth, prev):
    nc = seqlen // CHUNK
    return _pcall(
        functools.partial(_ret_kernel, nc=nc), (u, u, u, u, cos2, sin2, dmat, dec),
        [_seg_spec(nc, _KW, OFF_RQ), _seg_spec(nc, _KW, OFF_RK), _seg_spec(nc, _HW, OFF_RV),
         _seg_spec(nc, _HW, OFF_RG),
         pl.BlockSpec((CHUNK, DK), lambda b, c: (c, 0)),
         pl.BlockSpec((CHUNK, DK), lambda b, c: (c, 0)),
         _const_spec(dmat.shape), _const_spec(dec.shape)],
        carried={0: hbuf, 1: prev},
        grid=(bsz, nc),
        out_specs=[pl.BlockSpec((CHUNK, _HW), lambda b, c: (b * nc + c, BR_RET)),
                   _layer_state_spec(layer, (N_HEADS, DK, DV))],
        out_shape=[jax.ShapeDtypeStruct(hbuf.shape, BF16),
                   jax.ShapeDtypeStruct((depth, bsz, N_HEADS, DK, DV), F32)],
        scratch_shapes=[pltpu.VMEM((N_HEADS, DK, DV), F32)],
        compiler_params=_cparams(("parallel", "arbitrary"), 32),
        name="ret_prompt",
    )


def _mlstm_kernel(gb_ref, q_ref, k_ref, v_ref, mo_ref, ug_ref, ugt_ref, nw_ref,
                  h_ref, c_out, n_out, m_out, cm_ref, nv_ref, m_ref, *, nc):
    c = pl.program_id(1)

    @pl.when(c == 0)
    def _():
        cm_ref[...] = jnp.zeros_like(cm_ref)
        nv_ref[...] = jnp.zeros_like(nv_ref)
        m_ref[...] = jnp.zeros_like(m_ref)

    n = CHUNK
    row = lax.broadcasted_iota(jnp.int32, (n, n), 0)
    col = lax.broadcasted_iota(jnp.int32, (n, n), 1)
    causal = row >= col
    ug = ug_ref[...]
    hs = range(N_HEADS)
    kc = [slice(h * DK, (h + 1) * DK) for h in hs]
    vc = [slice(h * DV, (h + 1) * DV) for h in hs]
    cm = [cm_ref[h] for h in hs]
    nv = [nv_ref[h] for h in hs]
    m = [m_ref[h] for h in hs]
    ig_row = [ugt_ref[h:h + 1, :] + gb_ref[0, h] for h in hs]
    lf_row = [_log_sigmoid(ugt_ref[N_HEADS + h:N_HEADS + h + 1, :] + gb_ref[1, h]) for h in hs]
    ig_col = [ug[:, h:h + 1] + gb_ref[0, h] for h in hs]
    lf_col = [_log_sigmoid(ug[:, N_HEADS + h:N_HEADS + h + 1] + gb_ref[1, h]) for h in hs]
    b_col = [jnp.sum(jnp.where(causal, lf_row[h], 0.0), axis=1, keepdims=True) for h in hs]
    b_row = [jnp.sum(jnp.where(row <= col, lf_col[h], 0.0), axis=0, keepdims=True) for h in hs]
    b_last = [b_col[h][n - 1:n, :] for h in hs]
    logd = [jnp.where(causal, b_col[h] - b_row[h] + ig_row[h], -jnp.inf) for h in hs]
    inter = [m[h] + b_col[h] for h in hs]
    mi = [jnp.maximum(inter[h], jnp.max(logd[h], axis=1, keepdims=True)) for h in hs]
    w = [jnp.exp(logd[h] - mi[h]) for h in hs]
    wi = [jnp.exp(inter[h] - mi[h]) for h in hs]
    q = [q_ref[:, kc[h]] for h in hs]
    k = [k_ref[:, kc[h]] * (DK ** -0.5) for h in hs]
    qb = [q[h].astype(BF16) for h in hs]
    vb = [v_ref[:, vc[h]].astype(BF16) for h in hs]
    a = [_dot_nt(qb[h], k[h].astype(BF16)) * w[h] for h in hs]
    qc = [_dot(qb[h], cm[h].astype(BF16)) for h in hs]
    num = [_dot(a[h].astype(BF16), vb[h]) + wi[h] * qc[h] for h in hs]
    nq = [jnp.sum(a[h], axis=1, keepdims=True) + wi[h] * jnp.sum(q[h] * nv[h], axis=1, keepdims=True) for h in hs]
    hh = [num[h] / jnp.maximum(jnp.abs(nq[h]), jnp.exp(-mi[h])) for h in hs]
    m_new = [mi[h][n - 1:n, :] for h in hs]
    wl = [jnp.exp(b_last[h] - b_col[h] + ig_col[h] - m_new[h]) for h in hs]
    dp = [jnp.exp(m[h] + b_last[h] - m_new[h]) for h in hs]
    kw = [k[h] * wl[h] for h in hs]
    cm_new = [dp[h] * cm[h] + _dot(kw[h].T.astype(BF16), vb[h]) for h in hs]
    nv_new = [dp[h] * nv[h] + jnp.sum(kw[h], axis=0, keepdims=True) for h in hs]
    out = [(_rms(hh[h], nw_ref[:, vc[h]]) * _sigmoid(mo_ref[:, vc[h]])).astype(h_ref.dtype) for h in hs]
    for h in hs:
        cm_ref[h] = cm_new[h]
        nv_ref[h] = nv_new[h]
        m_ref[h] = m_new[h]
        h_ref[:, vc[h]] = out[h]

    @pl.when(c == nc - 1)
    def _():
        c_out[...] = cm_ref[...].reshape(c_out.shape)
        n_out[...] = nv_ref[...].reshape(n_out.shape)
        m_out[...] = m_ref[...].reshape(m_out.shape)


def _mlstm_prompt(u, hbuf, ug, ugt, gate_bias, norm_w, bsz, seqlen, layer, depth, prev):
    nc = seqlen // CHUNK
    return _pcall(
        functools.partial(_mlstm_kernel, nc=nc), (gate_bias, u, u, u, u, ug, ugt, norm_w),
        [pl.BlockSpec(memory_space=pltpu.SMEM),
         _seg_spec(nc, _KW, OFF_MQ), _seg_spec(nc, _KW, OFF_MK), _seg_spec(nc, _HW, OFF_MV),
         _seg_spec(nc, _HW, OFF_MO),
         pl.BlockSpec((CHUNK, GATE_PAD), lambda b, c: (b * nc + c, 0)),
         pl.BlockSpec((2 * N_HEADS, CHUNK), lambda b, c: (0, b * nc + c)),
         _const_spec((1, _HW))],
        carried={0: hbuf, 1: prev[0], 2: prev[1], 3: prev[2]},
        grid=(bsz, nc),
        out_specs=[pl.BlockSpec((CHUNK, _HW), lambda b, c: (b * nc + c, BR_ML)),
                   _layer_state_spec(layer, (N_HEADS, DK, DV)),
                   _layer_state_spec(layer, (N_HEADS, 1, DK)),
                   _layer_state_spec(layer, (N_HEADS, 1, 1))],
        out_shape=[jax.ShapeDtypeStruct(hbuf.shape, BF16),
                   jax.ShapeDtypeStruct((depth, bsz, N_HEADS, DK, DV), F32),
                   jax.ShapeDtypeStruct((depth, bsz, N_HEADS, 1, DK), F32),
                   jax.ShapeDtypeStruct((depth, bsz, N_HEADS, 1, 1), F32)],
        scratch_shapes=[pltpu.VMEM((N_HEADS, DK, DV), F32), pltpu.VMEM((N_HEADS, 1, DK), F32),
                        pltpu.VMEM((N_HEADS, 1, 1), F32)],
        compiler_params=_cparams(("parallel", "arbitrary"), 32),
        name="mlstm_prompt",
    )


def _hgrn_lower_bound(logits, layer):
    e = jnp.exp(logits - jnp.max(logits, axis=0, keepdims=True))
    p = e / jnp.sum(e, axis=0, keepdims=True)
    lb = jnp.zeros_like(p[0:1, :])
    for r in range(1, layer + 1):
        lb = lb + p[r:r + 1, :]
    return lb


def _hgrn_gates(fpre, lb):
    hlf = _logaddexp(jnp.log(lb), jnp.log1p(-lb) + _log_sigmoid(fpre))
    hk = (1.0 - lb) * _sigmoid(-fpre)
    return hlf, hk


def _hgrn_kernel(lg_ref, q_ref, f_ref, i_ref, g_ref, nw_ref, h_ref, st_ref, s_ref, *, nc, layer):
    c = pl.program_id(1)

    @pl.when(c == 0)
    def _():
        s_ref[...] = jnp.zeros_like(s_ref)

    n = CHUNK
    lb_all = _hgrn_lower_bound(lg_ref[...], layer)
    r2 = lax.broadcasted_iota(jnp.int32, (n, n), 0)
    c2 = lax.broadcasted_iota(jnp.int32, (n, n), 1)
    rowv = lax.broadcasted_iota(jnp.int32, (n, DK), 0)
    n_lev = n.bit_length() - 1
    rights = [(rowv & (1 << lev)) != 0 for lev in range(n_lev)]
    pair_masks = [((r2 >> (lev + 1)) == (c2 >> (lev + 1))) & ((r2 & (1 << lev)) != 0) & ((c2 & (1 << lev)) == 0)
                  for lev in range(n_lev)]
    hs = range(N_HEADS)
    kc = [slice(h * DK, (h + 1) * DK) for h in hs]
    vc = [slice(h * DV, (h + 1) * DV) for h in hs]
    st = [s_ref[h] for h in hs]
    gates = [_hgrn_gates(f_ref[:, kc[h]], lb_all[:, kc[h]]) for h in hs]
    hk = [gates[h][1] for h in hs]
    q = [_silu(q_ref[:, kc[h]]) for h in hs]
    ib = [i_ref[:, vc[h]].astype(BF16) for h in hs]
    bc = [_cumsum_rows(gates[h][0]) for h in hs]
    a = [jnp.where(r2 == c2, _dot_nt(q[h].astype(BF16), hk[h].astype(BF16)), 0.0) for h in hs]
    p = list(bc)
    for lev in range(n_lev):
        s = 1 << lev
        right = rights[lev]
        nxt = [pltpu.roll(p[h], n - s, 0) for h in hs]
        e = [jnp.exp(jnp.where(right, bc[h] - p[h], nxt[h] - bc[h])) for h in hs]
        m = [(jnp.where(right, q[h], hk[h]) * e[h]).astype(BF16) for h in hs]
        a = [jnp.where(pair_masks[lev], _dot_nt(m[h], m[h]), a[h]) for h in hs]
        if lev + 1 < n_lev:
            p = [jnp.where(right, pltpu.roll(p[h], s, 0), p[h]) for h in hs]
    qd = [_dot((q[h] * jnp.exp(bc[h])).astype(BF16), st[h].astype(BF16)) for h in hs]
    o = [_dot(a[h].astype(BF16), ib[h]) + qd[h] for h in hs]
    bl = [bc[h][n - 1:n, :] for h in hs]
    kd = [_dot((hk[h] * jnp.exp(bl[h] - bc[h])).T.astype(BF16), ib[h]) for h in hs]
    new_states = [_row_to_col(jnp.exp(bl[h])) * st[h] + kd[h] for h in hs]
    outs = [(_rms(o[h], nw_ref[:, vc[h]]) * _silu(g_ref[:, vc[h]])).astype(h_ref.dtype) for h in hs]
    for h in hs:
        s_ref[h] = new_states[h]
        h_ref[:, vc[h]] = outs[h]

    @pl.when(c == nc - 1)
    def _():
        st_ref[...] = s_ref[...].reshape(st_ref.shape)


def _hgrn_prompt(u, hbuf, logits, norm_w, bsz, seqlen, layer, prev):
    nc = seqlen // CHUNK
    depth = logits.shape[0]
    return _pcall(
        functools.partial(_hgrn_kernel, nc=nc, layer=layer), (logits, u, u, u, u, norm_w),
        [_const_spec(logits.shape),
         _seg_spec(nc, _KW, OFF_HQ), _seg_spec(nc, _KW, OFF_HF), _seg_spec(nc, _HW, OFF_HI),
         _seg_spec(nc, _HW, OFF_HG), _const_spec((1, _HW))],
        carried={0: hbuf, 1: prev},
        grid=(bsz, nc),
        out_specs=[pl.BlockSpec((CHUNK, _HW), lambda b, c: (b * nc + c, BR_HG)),
                   _layer_state_spec(layer, (N_HEADS, DK, DV))],
        out_shape=[jax.ShapeDtypeStruct(hbuf.shape, BF16),
                   jax.ShapeDtypeStruct((depth, bsz, N_HEADS, DK, DV), F32)],
        scratch_shapes=[pltpu.VMEM((N_HEADS, DK, DV), F32)],
        compiler_params=_cparams(("parallel", "arbitrary"), 32),
        name="hgrn_prompt",
    )


def _s5_prep_kernel(lr_ref, li_ref, dt_ref, bre_ref, bim_ref, ar_ref, ai_ref, bbr_ref, bbi_ref):
    lr, li, dt = lr_ref[...], li_ref[...], dt_ref[...]
    mag = jnp.exp(lr * dt)
    ar = mag * jnp.cos(li * dt)
    ai = mag * jnp.sin(li * dt)
    den = lr * lr + li * li
    cr = ((ar - 1.0) * lr + ai * li) / den
    ci = (ai * lr - (ar - 1.0) * li) / den
    ar_ref[...] = ar
    ai_ref[...] = ai
    bbr_ref[...] = cr * bre_ref[...] - ci * bim_ref[...]
    bbi_ref[...] = cr * bim_ref[...] + ci * bre_ref[...]


def _s5_prep(lam_re, lam_im, log_step, b_re, b_im):
    g, n = lam_re.shape
    p = b_re.shape[-1]
    gn = g * n
    dt = jnp.repeat(jnp.exp(log_step.astype(F32)), n).reshape(gn, 1)
    col = lambda a: a.astype(F32).reshape(gn, 1)
    return pl.pallas_call(
        _s5_prep_kernel,
        out_shape=[jax.ShapeDtypeStruct((gn, 1), F32), jax.ShapeDtypeStruct((gn, 1), F32),
                   jax.ShapeDtypeStruct((gn, p), F32), jax.ShapeDtypeStruct((gn, p), F32)],
        name="s5_prep",
    )(col(lam_re), col(lam_im), dt, b_re.reshape(gn, p), b_im.reshape(gn, p))


def _s5_kernel(*refs, seq, nc, n_kt, sw):
    if seq:
        (u_ref, ar_ref, ai_ref, bblk_ref, cblk_ref, d_ref, wglu_ref,
         h_ref, sr_ref, si_ref, hr_scr, hi_scr) = refs
    else:
        (u_ref, ar_ref, ai_ref, bblk_ref, cblk_ref, d_ref, wglu_ref, h0r_ref, h0i_ref,
         h_ref, sr_ref, si_ref) = refs
    u = u_ref[...]
    ub = u.astype(BF16)
    rows = u.shape[0]
    if seq:
        c = pl.program_id(1)

        @pl.when(c == 0)
        def _():
            hr_scr[...] = jnp.zeros_like(hr_scr)
            hi_scr[...] = jnp.zeros_like(hi_scr)

    ys = []
    for kt in range(n_kt):
        lanes = slice(kt * sw, (kt + 1) * sw)
        bu = _dot(ub[:, kt * MXU_K:(kt + 1) * MXU_K], bblk_ref[kt])
        xr, xi = bu[:, :sw], bu[:, sw:]
        ar, ai = ar_ref[:, lanes], ai_ref[:, lanes]
        if seq:
            sub = lax.broadcasted_iota(jnp.int32, (SUBLANES, sw), 0)
            mr, mi = jnp.broadcast_to(ar, (SUBLANES, sw)), jnp.broadcast_to(ai, (SUBLANES, sw))
            tabr, tabi = mr, mi
            levels = []
            sft = 1
            while sft < SUBLANES:
                keep = sub >= sft
                levels.append((sft, jnp.where(keep, mr, 0.0), jnp.where(keep, mi, 0.0)))
                tr, ti = pltpu.roll(tabr, sft, 0), pltpu.roll(tabi, sft, 0)
                tabr, tabi = (jnp.where(keep, tabr * tr - tabi * ti, tabr),
                              jnp.where(keep, tabr * ti + tabi * tr, tabi))
                mr, mi = mr * mr - mi * mi, 2.0 * mr * mi
                sft *= 2
            cr, ci = hr_scr[:, lanes], hi_scr[:, lanes]
            slabs_r, slabs_i = [], []
            for j in range(rows // SUBLANES):
                rs = slice(j * SUBLANES, (j + 1) * SUBLANES)
                sr_, si_ = xr[rs, :], xi[rs, :]
                for sft, lr, li in levels:
                    qr, qi = pltpu.roll(sr_, sft, 0), pltpu.roll(si_, sft, 0)
                    sr_, si_ = sr_ + lr * qr - li * qi, si_ + lr * qi + li * qr
                crb, cib = jnp.broadcast_to(cr, (SUBLANES, sw)), jnp.broadcast_to(ci, (SUBLANES, sw))
                sr_, si_ = sr_ + tabr * crb - tabi * cib, si_ + tabr * cib + tabi * crb
                cr, ci = sr_[SUBLANES - 1:SUBLANES, :], si_[SUBLANES - 1:SUBLANES, :]
                slabs_r.append(sr_)
                slabs_i.append(si_)
            xr, xi = jnp.concatenate(slabs_r, axis=0), jnp.concatenate(slabs_i, axis=0)
            hr_scr[:, lanes] = cr
            hi_scr[:, lanes] = ci
        else:
            h0r, h0i = h0r_ref[0, :, lanes], h0i_ref[0, :, lanes]
            xr, xi = xr + ar * h0r - ai * h0i, xi + ar * h0i + ai * h0r
            sr_ref[0, :, lanes] = xr
            si_ref[0, :, lanes] = xi
        ys.append(_dot(jnp.concatenate([xr, xi], axis=1).astype(BF16), cblk_ref[kt]))
    y = jnp.concatenate(ys, axis=1) + d_ref[...] * u
    z = _gelu_tanh(y)
    h_ref[...] = (z * _sigmoid(_dot(z.astype(BF16), wglu_ref[...]))).astype(h_ref.dtype)
    if seq:
        @pl.when(c == nc - 1)
        def _():
            sr_ref[...] = hr_scr[...].reshape(sr_ref.shape)
            si_ref[...] = hi_scr[...].reshape(si_ref.shape)


def _s5_weights(bbr, bbi, c_re, c_im):
    gn, p = bbr.shape
    g = gn // S5_N
    gpt = MXU_K // p
    n_kt = g // gpt
    eye = jnp.eye(gpt, dtype=F32)

    def b_blk(bb):
        return jnp.einsum('kgnp,gh->kgphn', bb.reshape(n_kt, gpt, S5_N, p), eye).reshape(n_kt, gpt * p, gpt * S5_N)

    def c_blk(cc):
        return jnp.einsum('kgpn,gh->kgnhp', cc.astype(F32).reshape(n_kt, gpt, p, S5_N), eye).reshape(
            n_kt, gpt * S5_N, gpt * p)

    bblk = jnp.concatenate([b_blk(bbr), b_blk(bbi)], axis=2).astype(BF16)
    cblk = jnp.concatenate([c_blk(c_re), -c_blk(c_im)], axis=1).astype(BF16)
    return bblk, cblk


def _s5_prompt(u, hbuf, ar, ai, bblk, cblk, d, wglu, bsz, seqlen, layer, depth, prev):
    width = wglu.shape[0]
    n_kt = bblk.shape[0]
    sw = bblk.shape[2] // 2
    ct = _tile(seqlen, 256)
    nc = seqlen // ct
    gn = ar.shape[1]
    full = lambda shape: pl.BlockSpec(shape, lambda b, c: (0,) * len(shape))
    st_spec = pl.BlockSpec((1, 1, 1, gn), lambda b, c: (layer, b, 0, 0))
    st_shape = jax.ShapeDtypeStruct((depth, bsz, 1, gn), F32)
    return _pcall(
        functools.partial(_s5_kernel, seq=True, nc=nc, n_kt=n_kt, sw=sw), (u, ar, ai, bblk, cblk, d, wglu),
        [pl.BlockSpec((ct, width), lambda b, c: (b * nc + c, OFF_SU // width)),
         full((1, gn)), full((1, gn)), full(bblk.shape), full(cblk.shape),
         full((1, width)), full(wglu.shape)],
        carried={0: hbuf, 1: prev[0], 2: prev[1]},
        grid=(bsz, nc),
        out_specs=[pl.BlockSpec((ct, width), lambda b, c: (b * nc + c, BR_S5)), st_spec, st_spec],
        out_shape=[jax.ShapeDtypeStruct(hbuf.shape, BF16), st_shape, st_shape],
        scratch_shapes=[pltpu.VMEM((1, gn), F32), pltpu.VMEM((1, gn), F32)],
        compiler_params=_cparams(("parallel", "arbitrary"), 56),
        name="s5_prompt",
    )


def _s5_sample(u, tp, ar, ai, bblk, cblk, d, wglu, h0r, h0i, layer, h_buf, prev):
    width = wglu.shape[0]
    n_kt = bblk.shape[0]
    sw = bblk.shape[2] // 2
    depth, bs, gn = h0r.shape
    tb = _tile(bs, 128, 8)
    r0 = tp // tb
    full = lambda shape: pl.BlockSpec(shape, lambda i: (0,) * len(shape))
    st_spec = pl.BlockSpec((1, tb, gn), lambda i: (layer, i, 0))
    st_shape = jax.ShapeDtypeStruct((depth, bs, gn), F32)
    return _pcall(
        functools.partial(_s5_kernel, seq=False, nc=1, n_kt=n_kt, sw=sw),
        (u, ar, ai, bblk, cblk, d, wglu, h0r, h0i),
        [pl.BlockSpec((tb, width), lambda i: (r0 + i, OFF_SU // width)),
         full((1, gn)), full((1, gn)), full(bblk.shape), full(cblk.shape),
         full((1, width)), full(wglu.shape), st_spec, st_spec],
        carried={0: h_buf, 1: prev[0], 2: prev[1]},
        grid=(bs // tb,),
        out_specs=[pl.BlockSpec((tb, width), lambda i: (r0 + i, BR_S5)), st_spec, st_spec],
        out_shape=[jax.ShapeDtypeStruct(h_buf.shape, BF16), st_shape, st_shape],
        compiler_params=_cparams(("parallel",), 56),
        name="s5_sample",
    )


SAMPLE_BLOCK = 8


def _state_step(s, dcol, kcol, vrow, qcol):
    s_new = dcol * s + kcol * vrow
    return s_new, jnp.sum(qcol * s_new, axis=0, keepdims=True)


def _zero_other_layers(ref, slot):
    for l in range(ref.shape[0]):
        if l != slot:
            ref[l] = jnp.zeros(ref.shape[1:], ref.dtype)


def _ret_step_kernel(gam_ref, q_ref, k_ref, v_ref, g_ref, cos_ref, sin_ref, st_ref, h_ref, so_ref, o_scr, *, slot):
    _zero_other_layers(so_ref, slot)
    cs, sn = cos_ref[...], sin_ref[...]
    for h in range(N_HEADS):
        qr = _rope(q_ref[:, h * DK:(h + 1) * DK], cs, sn)
        kr = _rope(k_ref[:, h * DK:(h + 1) * DK], cs, sn) * (DK ** -0.5)
        for i in range(SAMPLE_BLOCK):
            s_new, o = _state_step(st_ref[0, i, h], gam_ref[h], _row_to_col(kr[i:i + 1, :]),
                                   v_ref[i:i + 1, h * DV:(h + 1) * DV], _row_to_col(qr[i:i + 1, :]))
            so_ref[slot, i, h] = s_new
            o_scr[i:i + 1, h * DV:(h + 1) * DV] = o
    for h in range(N_HEADS):
        cols = slice(h * DV, (h + 1) * DV)
        h_ref[:, cols] = (_group_norm(o_scr[:, cols]) * _silu(g_ref[:, cols])).astype(h_ref.dtype)


def _mlstm_step_kernel(gb_ref, q_ref, k_ref, v_ref, mo_ref, ug_ref, nw_ref, c_ref, n_ref, m_ref,
                       h_ref, co_ref, no_ref, mo_out_ref, o_scr, *, slot):
    _zero_other_layers(co_ref, slot)
    ug = ug_ref[...]
    m_all = m_ref[0]
    m_new_cols = []
    for h in range(N_HEADS):
        ig = ug[:, h:h + 1] + gb_ref[0, h]
        lf = _log_sigmoid(ug[:, N_HEADS + h:N_HEADS + h + 1] + gb_ref[1, h])
        m_old = m_all[:, h:h + 1]
        inter = m_old + lf
        mi = jnp.maximum(inter, ig)
        w = jnp.exp(ig - mi)
        wi = jnp.exp(inter - mi)
        m_new_cols.append(mi)
        lim = jnp.exp(-mi)
        qh = q_ref[:, h * DK:(h + 1) * DK]
        kh = k_ref[:, h * DK:(h + 1) * DK] * (DK ** -0.5)
        for i in range(SAMPLE_BLOCK):
            wk = w[i:i + 1, :] * kh[i:i + 1, :]
            c_new, num = _state_step(c_ref[0, i, h], wi[i:i + 1, :], _row_to_col(wk),
                                     v_ref[i:i + 1, h * DV:(h + 1) * DV], _row_to_col(qh[i:i + 1, :]))
            n_new = wi[i:i + 1, :] * n_ref[0, i, h:h + 1, :] + wk
            nq = jnp.sum(qh[i:i + 1, :] * n_new, axis=1, keepdims=True)
            co_ref[slot, i, h] = c_new
            no_ref[0, i, h:h + 1, :] = n_new
            o_scr[i:i + 1, h * DV:(h + 1) * DV] = num / jnp.maximum(jnp.abs(nq), lim[i:i + 1, :])
    mo_out_ref[0] = jnp.concatenate(m_new_cols, axis=1)
    for h in range(N_HEADS):
        cols = slice(h * DV, (h + 1) * DV)
        h_ref[:, cols] = (_rms(o_scr[:, cols], nw_ref[:, cols]) * _sigmoid(mo_ref[:, cols])).astype(h_ref.dtype)


def _hgrn_step_kernel(lg_ref, q_ref, f_ref, i_ref, g_ref, nw_ref, st_ref, h_ref, so_ref, o_scr, *, layer, slot):
    _zero_other_layers(so_ref, slot)
    lb_all = _hgrn_lower_bound(lg_ref[...], layer)
    for h in range(N_HEADS):
        hlf, hk = _hgrn_gates(f_ref[:, h * DK:(h + 1) * DK], lb_all[:, h * DK:(h + 1) * DK])
        dec = jnp.exp(hlf)
        qh = _silu(q_ref[:, h * DK:(h + 1) * DK])
        for i in range(SAMPLE_BLOCK):
            s_new, o = _state_step(st_ref[0, i, h], _row_to_col(dec[i:i + 1, :]), _row_to_col(hk[i:i + 1, :]),
                                   i_ref[i:i + 1, h * DV:(h + 1) * DV], _row_to_col(qh[i:i + 1, :]))
            so_ref[slot, i, h] = s_new
            o_scr[i:i + 1, h * DV:(h + 1) * DV] = o
    for h in range(N_HEADS):
        cols = slice(h * DV, (h + 1) * DV)
        h_ref[:, cols] = (_rms(o_scr[:, cols], nw_ref[:, cols]) * _silu(g_ref[:, cols])).astype(h_ref.dtype)


def _seg(width, off, tp):
    return pl.BlockSpec((SAMPLE_BLOCK, width), lambda i: (tp // SAMPLE_BLOCK + i, off // width))


def _state_spec(layer, tail):
    return pl.BlockSpec((1, SAMPLE_BLOCK) + tail, lambda i: (layer, i) + (0,) * len(tail))


_MAT = (N_HEADS, DK, DV)
_SMEM = pl.BlockSpec(memory_space=pltpu.SMEM)


def _whole(shape):
    return pl.BlockSpec(shape, lambda i: (0,) * len(shape))


def _h_rows_spec(tp, branch):
    return pl.BlockSpec((SAMPLE_BLOCK, _HW), lambda i: (tp // SAMPLE_BLOCK + i, branch))


def _new_mat_state(layer, depth, prev):
    if prev is None:
        return pl.BlockSpec((depth, SAMPLE_BLOCK) + _MAT, lambda i: (0, i, 0, 0, 0)), layer
    return _state_spec(layer, _MAT), 0


def _ret_sample(u, tp, gamma, cos2, sin2, state, layer, h_buf, prev):
    depth, bs = state.shape[:2]
    spec, slot = _new_mat_state(layer, depth, prev)
    return _pcall(
        functools.partial(_ret_step_kernel, slot=slot), (gamma, u, u, u, u, cos2, sin2, state),
        [_SMEM, _seg(_KW, OFF_RQ, tp), _seg(_KW, OFF_RK, tp), _seg(_HW, OFF_RV, tp), _seg(_HW, OFF_RG, tp),
         _whole((1, DK)), _whole((1, DK)), _state_spec(layer, _MAT)],
        carried={0: h_buf, 1: prev},
        grid=(bs // SAMPLE_BLOCK,),
        out_specs=[_h_rows_spec(tp, BR_RET), spec],
        out_shape=[jax.ShapeDtypeStruct(h_buf.shape, BF16), jax.ShapeDtypeStruct(state.shape, F32)],
        scratch_shapes=[pltpu.VMEM((SAMPLE_BLOCK, _HW), F32)],
        compiler_params=_cparams(("parallel",), 48),
        name="ret_sample",
    )


def _mlstm_sample(u, ug, tp, gate_bias, norm_w, st_c, st_n, st_m, layer, h_buf, prev):
    depth, bs = st_c.shape[:2]
    spec, slot = _new_mat_state(layer, depth, prev[0])
    return _pcall(
        functools.partial(_mlstm_step_kernel, slot=slot), (gate_bias, u, u, u, u, ug, norm_w, st_c, st_n, st_m),
        [_SMEM, _seg(_KW, OFF_MQ, tp), _seg(_KW, OFF_MK, tp), _seg(_HW, OFF_MV, tp), _seg(_HW, OFF_MO, tp),
         pl.BlockSpec((SAMPLE_BLOCK, GATE_PAD), lambda i: (tp // SAMPLE_BLOCK + i, 0)), _whole((1, _HW)),
         _state_spec(layer, _MAT), _state_spec(layer, (N_HEADS, DK)), _state_spec(layer, (N_HEADS,))],
        carried={0: h_buf, 1: prev[0], 2: prev[1], 3: prev[2]},
        grid=(bs // SAMPLE_BLOCK,),
        out_specs=[_h_rows_spec(tp, BR_ML), spec,
                   _state_spec(layer, (N_HEADS, DK)), _state_spec(layer, (N_HEADS,))],
        out_shape=[jax.ShapeDtypeStruct(h_buf.shape, BF16), jax.ShapeDtypeStruct(st_c.shape, F32),
                   jax.ShapeDtypeStruct(st_n.shape, F32), jax.ShapeDtypeStruct(st_m.shape, F32)],
        scratch_shapes=[pltpu.VMEM((SAMPLE_BLOCK, _HW), F32)],
        compiler_params=_cparams(("parallel",), 48),
        name="mlstm_sample",
    )


def _hgrn_sample(u, tp, logits, norm_w, state, layer, h_buf, prev):
    depth, bs = state.shape[:2]
    spec, slot = _new_mat_state(layer, depth, prev)
    return _pcall(
        functools.partial(_hgrn_step_kernel, layer=layer, slot=slot), (logits, u, u, u, u, norm_w, state),
        [_whole(logits.shape), _seg(_KW, OFF_HQ, tp), _seg(_KW, OFF_HF, tp), _seg(_HW, OFF_HI, tp),
         _seg(_HW, OFF_HG, tp), _whole((1, _HW)), _state_spec(layer, _MAT)],
        carried={0: h_buf, 1: prev},
        grid=(bs // SAMPLE_BLOCK,),
        out_specs=[_h_rows_spec(tp, BR_HG), spec],
        out_shape=[jax.ShapeDtypeStruct(h_buf.shape, BF16), jax.ShapeDtypeStruct(state.shape, F32)],
        scratch_shapes=[pltpu.VMEM((SAMPLE_BLOCK, _HW), F32)],
        compiler_params=_cparams(("parallel",), 48),
        name="hgrn_sample",
    )


def _attn_prompt_kernel(q_ref, k_ref, v_ref, o_ref, *, scale):
    s = _dot_nt(q_ref[...].astype(BF16), k_ref[0, 0].astype(BF16)) * scale
    e = jnp.exp(s - jnp.max(s, axis=1, keepdims=True))
    p = e / jnp.sum(e, axis=1, keepdims=True)
    o_ref[...] = _dot(p.astype(BF16), v_ref[0, 0].astype(BF16)).astype(o_ref.dtype)


def _attn_prompt(q, kvbuf, obuf, bsz, seqlen, layer):
    t, d = q.shape
    dh = d // N_HEADS
    mem_len = kvbuf.shape[2] // bsz
    tq = _tile(seqlen, 2048)
    nq = seqlen // tq
    return _pcall(
        functools.partial(_attn_prompt_kernel, scale=dh ** -0.5), (q, kvbuf, kvbuf),
        [pl.BlockSpec((tq, dh), lambda b, h, i: (b * nq + i, h)),
         pl.BlockSpec((1, 1, mem_len, dh), lambda b, h, i: (layer, 0, b, h)),
         pl.BlockSpec((1, 1, mem_len, dh), lambda b, h, i: (layer, 1, b, h))],
        carried={0: obuf},
        grid=(bsz, N_HEADS, nq),
        out_specs=pl.BlockSpec((tq, dh), lambda b, h, i: (b * nq + i, h)),
        out_shape=jax.ShapeDtypeStruct((t, d), BF16),
        compiler_params=_cparams(("parallel", "parallel", "arbitrary"), 32),
        name="attn_prompt",
    )


def _attn_sample_kernel(q_ref, k_hbm, v_hbm, o_ref, kbuf, vbuf, sem, *, scale, layer, n_samples):
    i = pl.program_id(0)
    nh = q_ref.shape[1]
    slot = i % 2

    def copies(sample, to_slot):
        return [pltpu.make_async_copy(src.at[layer, sample, :, h, :], buf.at[to_slot, h], sem.at[t, to_slot, h])
                for t, (src, buf) in enumerate(((k_hbm, kbuf), (v_hbm, vbuf))) for h in range(nh)]

    @pl.when(i == 0)
    def _():
        for c in copies(0, 0):
            c.start()

    @pl.when(i + 1 < n_samples)
    def _():
        for c in copies(i + 1, 1 - slot):
            c.start()

    for c in copies(i, slot):
        c.wait()
    for h in range(nh):
        s = jnp.sum(kbuf[slot, h] * q_ref[0, h:h + 1, :], axis=1, keepdims=True) * scale
        e = jnp.exp(s - jnp.max(s, axis=0, keepdims=True))
        p = e / jnp.sum(e, axis=0, keepdims=True)
        o_ref[0, h:h + 1, :] = jnp.sum(p * vbuf[slot, h], axis=0, keepdims=True)


def _attn_sample(q4, cache_k, cache_v, layer):
    bs, nh, dh = q4.shape
    mem_len = cache_k.shape[2]
    return pl.pallas_call(
        functools.partial(_attn_sample_kernel, scale=dh ** -0.5, layer=layer, n_samples=bs),
        grid=(bs,),
        in_specs=[pl.BlockSpec((1, nh, dh), lambda i: (i, 0, 0)), _ANY, _ANY],
        out_specs=pl.BlockSpec((1, nh, dh), lambda i: (i, 0, 0)),
        out_shape=jax.ShapeDtypeStruct((bs, nh, dh), F32),
        scratch_shapes=[pltpu.VMEM((2, nh, mem_len, dh), F32), pltpu.VMEM((2, nh, mem_len, dh), F32),
                        pltpu.SemaphoreType.DMA((2, 2, nh))],
        compiler_params=_cparams(("arbitrary",), 32),
        name="attn_sample",
    )(q4, cache_k, cache_v)


def _rope_tables(pos):
    half = DK // 2
    inv = ROPE_BASE ** (-jnp.arange(half, dtype=F32) / half)
    ang = pos.astype(F32)[:, None] * inv[None, :]
    cos, sin = jnp.cos(ang), jnp.sin(ang)
    return jnp.concatenate([cos, cos], axis=1), jnp.concatenate([-sin, sin], axis=1)


def _retention_tables(c):
    lg = jnp.log1p(-jnp.exp2(-5.0 - jnp.arange(N_HEADS, dtype=F32)))
    idx = jnp.arange(c, dtype=F32)
    diff = idx[:, None] - idx[None, :]
    dmat = jnp.where(diff[None] >= 0, jnp.exp(jnp.maximum(diff, 0.0)[None] * lg[:, None, None]), 0.0)
    e1 = jnp.exp((idx[None, :] + 1.0) * lg[:, None])
    e2 = jnp.exp((c - 1.0 - idx)[None, :] * lg[:, None])
    e3 = jnp.broadcast_to(jnp.exp(c * lg)[:, None], (N_HEADS, c))
    dec = jnp.stack([e1, e2, e3] + [jnp.zeros_like(e1)] * 5, axis=-1)
    return dmat, dec


def kernel(x_prompt, x_sample, mem_prompt, state_ret, state_mlstm_c, state_mlstm_n, state_mlstm_m, state_s5_re, state_s5_im, state_hgrn, cache_mem_k, cache_mem_v, norm_w, ffn_w_in, ffn_w_out, w_in, ml_gate_bias, ml_norm_w, s5_lambda_re, s5_lambda_im, s5_log_step, s5_b_re, s5_b_im, s5_c_re, s5_c_im, s5_d, s5_w_glu, hg_lb_logits, hg_norm_w, w_branch, w_out, xa_mem_norm, xa_wq, xa_wkv, xa_wo):
    bsz, seqlen, d = x_prompt.shape
    bs = x_sample.shape[0]
    depth = norm_w.shape[0]
    mem_len = mem_prompt.shape[1]
    tp = bsz * seqlen
    gn = s5_lambda_re.shape[1] * s5_lambda_re.shape[2]
    seg_gate = OFF_MO

    x = jnp.concatenate([x_prompt.reshape(tp, d), x_sample.reshape(bs, d)], axis=0)
    mem = mem_prompt.reshape(bsz * mem_len, d)
    s5_in_re = state_s5_re.reshape(depth, bs, gn)
    s5_in_im = state_s5_im.reshape(depth, bs, gn)

    cos_p, sin_p = _rope_tables(jnp.arange(seqlen, dtype=jnp.int32))
    cos_s, sin_s = _rope_tables(PAST_LEN + jnp.arange(1, dtype=jnp.int32))
    dmat, dec = _retention_tables(CHUNK)
    _, dec1 = _retention_tables(1)
    gamma = dec1[:, 0, 0]

    zf = lambda *shape: jnp.zeros(shape, F32)
    t = tp + bs
    p_ret = zf(depth, bsz, N_HEADS, DK, DV)
    p_hg = zf(depth, bsz, N_HEADS, DK, DV)
    p_ml = [zf(depth, bsz, N_HEADS, DK, DV), zf(depth, bsz, N_HEADS, 1, DK), zf(depth, bsz, N_HEADS, 1, 1)]
    p_s5 = [zf(depth, bsz, 1, gn), zf(depth, bsz, 1, gn)]
    kvbuf = zf(depth, 2, bsz * mem_len, d)
    s_ret = s_hg = None
    s_ml = [None, zf(depth, bs, N_HEADS, DK), zf(depth, bs, N_HEADS)]
    s_s5 = [zf(depth, bs, gn), zf(depth, bs, gn)]
    hbuf = jnp.zeros((t, N_BRANCH * _HW), BF16)
    o = jnp.zeros((t, d), BF16)
    ffn_w_out_bf = ffn_w_out.astype(BF16)
    w_in_t = jnp.swapaxes(w_in, 1, 2)
    w_branch_bf, w_out_bf = w_branch.astype(BF16), w_out.astype(BF16)
    xa_wq_bf, xa_wkv_bf, xa_wo_bf = xa_wq.astype(BF16), xa_wkv.astype(BF16), xa_wo.astype(BF16)
    xn = _norm_cast(x, norm_w[0, 0:1])
    for l in range(depth):
        nw = norm_w[l]
        x, xn = _ffn(x, xn, nw[1:2], nw[2:3], ffn_w_in, l, 0, ffn_w_out_bf)
        n_gate = 2 * N_HEADS
        u, ug = _in_proj(xn, w_in_t, l, seg_gate, n_gate)
        ugt = ug[:tp, :n_gate].T
        mlw = ml_norm_w[l].reshape(1, -1)
        hgw = hg_norm_w[l].reshape(1, -1)
        ar, ai, bbr, bbi = _s5_prep(s5_lambda_re[l], s5_lambda_im[l], s5_log_step[l], s5_b_re[l], s5_b_im[l])
        ar, ai = ar.reshape(1, gn), ai.reshape(1, gn)
        bblk, cblk = _s5_weights(bbr, bbi, s5_c_re[l], s5_c_im[l])
        s5d = s5_d[l].astype(F32).reshape(1, -1)
        wglu = s5_w_glu[l].astype(BF16)
        hbuf, p_ret = _ret_prompt(u, hbuf, bsz, seqlen, cos_p, sin_p, dmat, dec, l, depth, p_ret)
        hbuf, s_ret = _ret_sample(u, tp, gamma, cos_s, sin_s, state_ret, l, hbuf, s_ret)
        hbuf, *p_ml = _mlstm_prompt(u, hbuf, ug, ugt, ml_gate_bias[l], mlw, bsz, seqlen, l, depth, p_ml)
        hbuf, *s_ml = _mlstm_sample(u, ug, tp, ml_gate_bias[l], mlw, state_mlstm_c, state_mlstm_n,
                                    state_mlstm_m, l, hbuf, s_ml)
        hbuf, *p_s5 = _s5_prompt(u, hbuf, ar, ai, bblk, cblk, s5d, wglu, bsz, seqlen, l, depth, p_s5)
        hbuf, *s_s5 = _s5_sample(u, tp, ar, ai, bblk, cblk, s5d, wglu, s5_in_re, s5_in_im, l, hbuf, s_s5)
        hbuf, p_hg = _hgrn_prompt(u, hbuf, hg_lb_logits, hgw, bsz, seqlen, l, p_hg)
        hbuf, s_hg = _hgrn_sample(u, tp, hg_lb_logits, hgw, state_hgrn, l, hbuf, s_hg)
        x, xn = _merge(hbuf, u, w_branch_bf, w_out_bf, l, x, nw[3:4], nw[4:5])
        kvbuf = _mem_kv(mem, xa_mem_norm[l].reshape(1, d), xa_wkv_bf, l, depth, kvbuf)
        q = _matmul(xn, xa_wq_bf, l, "xa_q")
        o = _attn_prompt(q, kvbuf, o, bsz, seqlen, l)
        o_s = _attn_sample(q[tp:].reshape(bs, N_HEADS, d // N_HEADS), cache_mem_k, cache_mem_v, l)
        o = lax.dynamic_update_slice(o, o_s.reshape(bs, d).astype(BF16), (tp, 0))
        x, xn = _proj_res(o, xa_wo_bf, l, x, nw[5:6], nw[6:7])
        x, xn = _ffn(x, xn, nw[7:8], norm_w[(l + 1) % depth, 0:1], ffn_w_in, l, 1, ffn_w_out_bf)

    mem_shape = (depth, bsz, mem_len, N_HEADS, d // N_HEADS)
    s5_shape = (depth, -1, gn // S5_N, S5_N)
    return (x[:tp].reshape(bsz, seqlen, d), x[tp:].reshape(bs, 1, d),
            p_ret, p_ml[0], p_ml[1].reshape(depth, bsz, N_HEADS, DK), p_ml[2].reshape(depth, bsz, N_HEADS),
            p_s5[0].reshape(s5_shape), p_s5[1].reshape(s5_shape), p_hg,
            kvbuf[:, 0].reshape(mem_shape), kvbuf[:, 1].reshape(mem_shape),
            s_ret, s_ml[0], s_ml[1], s_ml[2], s_s5[0].reshape(s5_shape), s_s5[1].reshape(s5_shape), s_hg)
```
